```python
import math
import jax, jax.numpy as jnp
from jax import lax
import numpy as np

D_MODEL = 1024
BATCH = 8
SEQ = 2048
DEPTH = 2
DEC_BATCH = 128
DEC_SEQ = 8
PAST_LEN = 16384
PAGE_SIZE = 128

N_MIXERS = 2
N_LAYERS_A = (DEPTH + 1) // 2
N_LAYERS_B = DEPTH // 2
EPS = 1e-6
D_A = 2 * D_MODEL
G_A = 8
DG_A = D_A // G_A
CHUNK_A = 128
H_B = 8
DK_B = D_MODEL // 8
DV_B = 2 * D_MODEL // H_B
D_B = H_B * DV_B
QK_B = 2 * H_B * DK_B
PROJ_B = QK_B + 2 * D_B + 2 * H_B
CONV_B = 4
CHUNK_B = 64
F_BIAS_INIT = 3.0
D_FF = 2816
CONV_F = 3

kernel_name = "hybrid_gmlp_mlstm_convffn_step"


def rmsnorm(x, g):
    x32 = x.astype(jnp.float32)
    y = x32 * lax.rsqrt(jnp.mean(x32 * x32, axis=-1, keepdims=True) + EPS)
    return (y * g.astype(jnp.float32)).astype(x.dtype)


def causal_dwconv(x, buf, w, b):
    k_w = w.shape[0]
    s = x.shape[1]
    xf = jnp.concatenate([buf.astype(x.dtype), x], axis=1)
    y = b + w[k_w - 1] * xf[:, k_w - 1:k_w - 1 + s]
    for j in range(k_w - 1):
        y = y + w[j] * xf[:, j:j + s]
    return y, xf[:, xf.shape[1] - (k_w - 1):]


def chunk_spatial_gating(x, w_in, ln_g, ln_b, w_s, b_s, w_out):
    bsz, s, _ = x.shape
    h = jax.nn.gelu(x @ w_in)
    u, v = jnp.split(h, 2, axis=-1)
    v32 = v.astype(jnp.float32)
    mu = jnp.mean(v32, axis=-1, keepdims=True)
    var = jnp.mean(jnp.square(v32 - mu), axis=-1, keepdims=True)
    v = ((v32 - mu) * lax.rsqrt(var + EPS) * ln_g + ln_b).astype(x.dtype)
    L = min(s, CHUNK_A)
    nc = s // L
    mask = jnp.tril(jnp.ones((L, L), dtype=bool))
    ws = jnp.where(mask, w_s[:, :L, :L], 0.0).astype(x.dtype)
    vc = v.reshape(bsz, nc, L, G_A, DG_A)
    mixed = jnp.einsum('gts,bnsgd->bntgd', ws, vc) + b_s[:, :L].T[:, :, None]
    y = (u * mixed.reshape(bsz, s, D_A)) @ w_out
    return y, v


def mlstm_chunkwise(q, k, v, ig, lf, C0, n0, m0):
    bsz, s, nh, dk = q.shape
    dv = v.shape[-1]
    L = math.gcd(s, CHUNK_B)
    nc = s // L

    def to_chunks(a):
        a = a.reshape((bsz, nc, L) + a.shape[2:])
        return jnp.moveaxis(jnp.moveaxis(a, 1, 0), 2, 3)

    mask = jnp.tril(jnp.ones((L, L), dtype=bool))

    def step(carry, inp):
        C, n, m = carry
        qc, kc, vc, ic, fc = inp
        b = jnp.cumsum(fc, axis=-1)
        D = b[..., :, None] - b[..., None, :] + ic[..., None, :]
        D = jnp.where(mask, D, -jnp.inf)
        inter = b + m[..., None]
        m_t = jnp.maximum(inter, jnp.max(D, axis=-1))
        w_intra = jnp.exp(D - m_t[..., None])
        w_inter = jnp.exp(inter - m_t)
        sc = jnp.einsum('bhtd,bhsd->bhts', qc, kc) * w_intra
        num = w_inter[..., None] * jnp.einsum('bhtd,bhde->bhte', qc, C) \
            + jnp.einsum('bhts,bhse->bhte', sc, vc)
        den = w_inter * jnp.einsum('bhtd,bhd->bht', qc, n) + jnp.sum(sc, axis=-1)
        h = num / jnp.maximum(jnp.abs(den), jnp.exp(-m_t))[..., None]
        bL = b[..., -1]
        m_new = m_t[..., -1]
        g = jnp.exp(bL[..., None] - b + ic - m_new[..., None])
        decay = jnp.exp(bL + m - m_new)
        kg = kc * g[..., None]
        C_new = decay[..., None, None] * C + jnp.einsum('bhsd,bhse->bhde', kg, vc)
        n_new = decay[..., None] * n + jnp.sum(kg, axis=2)
        return (C_new, n_new, m_new), h

    (C, n, m), h = lax.scan(step, (C0, n0, m0),
                            (to_chunks(q), to_chunks(k), to_chunks(v), to_chunks(ig), to_chunks(lf)))
    h = jnp.transpose(h, (1, 0, 3, 2, 4)).reshape(bsz, s, nh, dv)
    return h, C, n, m


def mlstm_mixer(x, conv_buf, C0, n0, m0, w_in, conv_w, conv_b, b_i, b_f, gn_g, w_out):
    bsz, s, _ = x.shape
    p = x @ w_in
    qk_pre = p[..., :QK_B]
    v = p[..., QK_B:QK_B + D_B]
    o_pre = p[..., QK_B + D_B:QK_B + 2 * D_B]
    i_pre = p[..., QK_B + 2 * D_B:QK_B + 2 * D_B + H_B]
    f_pre = p[..., QK_B + 2 * D_B + H_B:]
    qk, new_buf = causal_dwconv(qk_pre, conv_buf, conv_w, conv_b)
    qk = jax.nn.silu(qk).astype(jnp.float32)
    q = qk[..., :QK_B // 2].reshape(bsz, s, H_B, DK_B)
    k = qk[..., QK_B // 2:].reshape(bsz, s, H_B, DK_B) * (DK_B ** -0.5)
    v32 = v.astype(jnp.float32).reshape(bsz, s, H_B, DV_B)
    ig = (i_pre + b_i).astype(jnp.float32)
    lf = jax.nn.log_sigmoid((f_pre + b_f).astype(jnp.float32))
    h, C, n, m = mlstm_chunkwise(q, k, v32, ig, lf, C0.astype(jnp.float32),
                                 n0.astype(jnp.float32), m0.astype(jnp.float32))
    mu = jnp.mean(h, axis=-1, keepdims=True)
    var = jnp.mean(jnp.square(h - mu), axis=-1, keepdims=True)
    hn = ((h - mu) * lax.rsqrt(var + EPS)).reshape(bsz, s, D_B) * gn_g.astype(jnp.float32)
    o = jax.nn.sigmoid(o_pre.astype(jnp.float32))
    y = (o * hn).astype(x.dtype) @ w_out
    return y, new_buf, C.astype(C0.dtype), n.astype(n0.dtype), m.astype(m0.dtype)


def conv_ffn(x, buf, w_up, conv_w, conv_b, w_down):
    up = x @ w_up
    a, g = jnp.split(up, 2, axis=-1)
    a_c, new_buf = causal_dwconv(a, buf, conv_w, conv_b)
    y = (jax.nn.gelu(a_c) * g) @ w_down
    return y, new_buf


def trunk(x, ffn_buf, mC, mn, mm, mconv, norm_mix_g, norm_ffn_g, final_norm_g,
          a_w_in, a_ln_g, a_ln_b, a_w_s, a_b_s, a_w_out,
          b_w_in, b_conv_w, b_conv_b, b_bias_i, b_bias_f, b_gn_g, b_w_out,
          f_w_up, f_conv_w, f_conv_b, f_w_down):
    vs, Cs, ns, ms, mcs, fbs = [], [], [], [], [], []
    for i in range(DEPTH):
        h = rmsnorm(x, norm_mix_g[i])
        j = i // N_MIXERS
        if i % N_MIXERS == 0:
            y, v = chunk_spatial_gating(h, a_w_in[j], a_ln_g[j], a_ln_b[j], a_w_s[j], a_b_s[j], a_w_out[j])
            vs.append(v)
        else:
            y, cb, C, n, m = mlstm_mixer(h, mconv[j], mC[j], mn[j], mm[j], b_w_in[j], b_conv_w[j],
                                         b_conv_b[j], b_bias_i[j], b_bias_f[j], b_gn_g[j], b_w_out[j])
            Cs.append(C); ns.append(n); ms.append(m); mcs.append(cb)
        x = x + y
        h = rmsnorm(x, norm_ffn_g[i])
        y, fb = conv_ffn(h, ffn_buf[i], f_w_up[i], f_conv_w[i], f_conv_b[i], f_w_down[i])
        fbs.append(fb)
        x = x + y
    return (rmsnorm(x, final_norm_g), jnp.stack(vs), jnp.stack(Cs), jnp.stack(ns), jnp.stack(ms),
            jnp.stack(mcs), jnp.stack(fbs))


def setup_inputs(seed: int = 0) -> dict:
    key = jax.random.key(seed)
    ks = jax.random.split(key, 32)

    def nrm(k, shape, scale):
        return jax.random.normal(k, shape, jnp.float32) * scale

    return {
        "x_prompt": nrm(ks[0], (BATCH, SEQ, D_MODEL), 1.0),
        "x_sample": nrm(ks[1], (DEC_BATCH, DEC_SEQ, D_MODEL), 1.0),
        "state_mlstm_C": nrm(ks[2], (N_LAYERS_B, DEC_BATCH, H_B, DK_B, DV_B), 0.1),
        "state_mlstm_n": nrm(ks[3], (N_LAYERS_B, DEC_BATCH, H_B, DK_B), 0.5),
        "state_mlstm_m": nrm(ks[4], (N_LAYERS_B, DEC_BATCH, H_B), 1.0),
        "state_mlstm_conv": nrm(ks[5], (N_LAYERS_B, DEC_BATCH, CONV_B - 1, QK_B), 1.0),
        "state_ffn_conv": nrm(ks[6], (DEPTH, DEC_BATCH, CONV_F - 1, D_FF), 1.0),
        "norm_mix_g": 1.0 + nrm(ks[7], (DEPTH, D_MODEL), 0.02),
        "norm_ffn_g": 1.0 + nrm(ks[8], (DEPTH, D_MODEL), 0.02),
        "final_norm_g": 1.0 + nrm(ks[9], (D_MODEL,), 0.02),
        "a_w_in": nrm(ks[10], (N_LAYERS_A, D_MODEL, 2 * D_A), D_MODEL ** -0.5),
        "a_ln_g": 1.0 + nrm(ks[11], (N_LAYERS_A, D_A), 0.02),
        "a_ln_b": nrm(ks[12], (N_LAYERS_A, D_A), 0.02),
        "a_w_s": nrm(ks[13], (N_LAYERS_A, G_A, CHUNK_A, CHUNK_A), CHUNK_A ** -0.5),
        "a_b_s": 1.0 + nrm(ks[14], (N_LAYERS_A, G_A, CHUNK_A), 0.1),
        "a_w_out": nrm(ks[15], (N_LAYERS_A, D_A, D_MODEL), D_A ** -0.5),
        "b_w_in": nrm(ks[16], (N_LAYERS_B, D_MODEL, PROJ_B), D_MODEL ** -0.5),
        "b_conv_w": nrm(ks[17], (N_LAYERS_B, CONV_B, QK_B), CONV_B ** -0.5),
        "b_conv_b": nrm(ks[18], (N_LAYERS_B, QK_B), 0.02),
        "b_bias_i": nrm(ks[19], (N_LAYERS_B, H_B), 0.1),
        "b_bias_f": F_BIAS_INIT + nrm(ks[20], (N_LAYERS_B, H_B), 0.5),
        "b_gn_g": 1.0 + nrm(ks[21], (N_LAYERS_B, D_B), 0.02),
        "b_w_out": nrm(ks[22], (N_LAYERS_B, D_B, D_MODEL), D_B ** -0.5),
        "f_w_up": nrm(ks[23], (DEPTH, D_MODEL, 2 * D_FF), D_MODEL ** -0.5),
        "f_conv_w": nrm(ks[24], (DEPTH, CONV_F, D_FF), CONV_F ** -0.5),
        "f_conv_b": nrm(ks[25], (DEPTH, D_FF), 0.02),
        "f_w_down": nrm(ks[26], (DEPTH, D_FF, D_MODEL), D_FF ** -0.5),
    }


def reference(x_prompt, x_sample, state_mlstm_C, state_mlstm_n, state_mlstm_m, state_mlstm_conv,
              state_ffn_conv, norm_mix_g, norm_ffn_g, final_norm_g,
              a_w_in, a_ln_g, a_ln_b, a_w_s, a_b_s, a_w_out,
              b_w_in, b_conv_w, b_conv_b, b_bias_i, b_bias_f, b_gn_g, b_w_out,
              f_w_up, f_conv_w, f_conv_b, f_w_down):
    dt = x_prompt.dtype
    bp = x_prompt.shape[0]
    p_C = jnp.zeros((N_LAYERS_B, bp, H_B, DK_B, DV_B), dt)
    p_n = jnp.zeros((N_LAYERS_B, bp, H_B, DK_B), dt)
    p_m = jnp.zeros((N_LAYERS_B, bp, H_B), dt)
    p_mconv = jnp.zeros((N_LAYERS_B, bp, CONV_B - 1, QK_B), dt)
    p_fconv = jnp.zeros((DEPTH, bp, CONV_F - 1, D_FF), dt)
    params = (norm_mix_g, norm_ffn_g, final_norm_g,
              a_w_in, a_ln_g, a_ln_b, a_w_s, a_b_s, a_w_out,
              b_w_in, b_conv_w, b_conv_b, b_bias_i, b_bias_f, b_gn_g, b_w_out,
              f_w_up, f_conv_w, f_conv_b, f_w_down)
    y_prompt, _, C_p, n_p, m_p, mc_p, fc_p = trunk(x_prompt, p_fconv, p_C, p_n, p_m, p_mconv, *params)
    y_sample, v_s, C_s, n_s, m_s, mc_s, fc_s = trunk(x_sample, state_ffn_conv, state_mlstm_C,
                                                     state_mlstm_n, state_mlstm_m, state_mlstm_conv,
                                                     *params)
    return (y_prompt, y_sample, C_p, n_p, m_p, mc_p, fc_p, v_s, C_s, n_s, m_s, mc_s, fc_s)
```

```python
import functools

import jax
import jax.numpy as jnp
from jax import lax
from jax.experimental import pallas as pl
from jax.experimental.pallas import tpu as pltpu

F32 = jnp.float32
BF16 = jnp.bfloat16
HIGHEST = lax.Precision.HIGHEST

EPS = 1e-6
LANES = 128
VMEM_LIMIT_BYTES = 56 * 1024 * 1024

CHUNK_A = 128
G_A = 8
H_B = 8
CONV_B = 4
CONV_F = 3


def _const_spec(shape):
    nd = len(shape)
    return pl.BlockSpec(shape, lambda *_: (0,) * nd, pipeline_mode=pl.Buffered(1))


def _params(n_axes):
    return pltpu.CompilerParams(
        dimension_semantics=("arbitrary",) * n_axes,
        vmem_limit_bytes=VMEM_LIMIT_BYTES,
    )


def _rmsnorm(x, g):
    ms = jnp.mean(x * x, axis=-1, keepdims=True)
    return x * lax.rsqrt(ms + EPS) * g


def _expand3(c3, nb, ls):
    w = c3.shape[-1]
    if nb == 1:
        return jnp.broadcast_to(c3.reshape(1, w), (ls, w))
    return jnp.broadcast_to(c3, (nb, ls, w)).reshape(nb * ls, w)


def _expand2(c2, nb, ls):
    return _expand3(c2[:, None, :], nb, ls)


def _causal_conv(a, carry_ref, cw_ref, cb_ref, nb, ls):
    r, c = a.shape
    kw = cw_ref.shape[0]
    tpos = lax.broadcasted_iota(jnp.int32, (r, c), 0) & (ls - 1)
    y = cb_ref[...] + cw_ref[kw - 1:kw, :] * a
    rolled = a
    for k in range(1, kw):
        rolled = pltpu.roll(a, k, 0)
        sh = rolled
        for t in range(k):
            idx = kw - 1 + t - k
            prev = _expand3(carry_ref[:, idx:idx + 1, :], nb, ls)
            sh = jnp.where(tpos == t, prev, sh)
        y = y + cw_ref[kw - 1 - k:kw - k, :] * sh
    if nb == 1:
        carry_ref[0] = rolled[0:kw - 1, :]
    else:
        heads = rolled.reshape(nb, ls, c)[:, 0:kw - 1, :]
        carry_ref[0:nb - 1] = heads[1:nb]
        carry_ref[nb - 1:nb] = heads[0:1]
    return y


def _mixer_a_kernel(x_ref, g_ref, win_ref, lng_ref, lnb_ref, ws_ref, bs_ref, wout_ref,
                    y_ref, *rest, seq_chunk, emit_v):
    if emit_v:
        v_ref, gate_ref = rest
    else:
        (gate_ref,) = rest
    tm = x_ref.shape[0]
    d_a = lng_ref.shape[1]
    dg = d_a // G_A
    x = x_ref[...]
    h = _rmsnorm(x, g_ref[...]).astype(BF16)
    hh = jax.nn.gelu(jnp.dot(h, win_ref[...], preferred_element_type=F32))
    v = hh[:, d_a:]
    mu = jnp.mean(v, axis=-1, keepdims=True)
    vc = v - mu
    var = jnp.mean(vc * vc, axis=-1, keepdims=True)
    vn = vc * lax.rsqrt(var + EPS) * lng_ref[...] + lnb_ref[...]
    if emit_v:
        v_ref[...] = vn
    vb = vn.astype(BF16)
    t_i = lax.broadcasted_iota(jnp.int32, (CHUNK_A, CHUNK_A), 0)
    s_i = lax.broadcasted_iota(jnp.int32, (CHUNK_A, CHUNK_A), 1)
    mask = (s_i <= t_i) & ((t_i // seq_chunk) == (s_i // seq_chunk))
    for g in range(G_A):
        w = jnp.where(mask, ws_ref[g], jnp.zeros((), BF16))
        cols = slice(g * dg, (g + 1) * dg)
        for c in range(tm // CHUNK_A):
            rows = slice(c * CHUNK_A, (c + 1) * CHUNK_A)
            mixed = jnp.dot(w, vb[rows, cols], preferred_element_type=F32) + bs_ref[:, cols]
            gate_ref[rows, cols] = (hh[rows, cols] * mixed).astype(BF16)
    y_ref[...] = x + jnp.dot(gate_ref[...], wout_ref[...], preferred_element_type=F32)


def _mixer_a(x2, g, w_in, ln_g, ln_b, ws, bs, w_out, *, tm, seq_chunk, emit_v):
    t, d = x2.shape
    d_a = ln_g.shape[1]
    out_shape = [jax.ShapeDtypeStruct((t, d), F32)]
    out_specs = [pl.BlockSpec((tm, d), lambda i: (i, 0))]
    if emit_v:
        out_shape.append(jax.ShapeDtypeStruct((t, d_a), F32))
        out_specs.append(pl.BlockSpec((tm, d_a), lambda i: (i, 0)))
    return pl.pallas_call(
        functools.partial(_mixer_a_kernel, seq_chunk=seq_chunk, emit_v=emit_v),
        grid=(t // tm,),
        in_specs=[
            pl.BlockSpec((tm, d), lambda i: (i, 0)),
            _const_spec(g.shape), _const_spec(w_in.shape), _const_spec(ln_g.shape),
            _const_spec(ln_b.shape), _const_spec(ws.shape), _const_spec(bs.shape),
            _const_spec(w_out.shape),
        ],
        out_specs=out_specs,
        out_shape=out_shape,
        scratch_shapes=[pltpu.VMEM((tm, d_a), BF16)],
        compiler_params=_params(1),
        name="mixer_a",
    )(x2, g, w_in, ln_g, ln_b, ws, bs, w_out)


def _ffn_kernel(*refs, nb, ls, zero_init, final_norm):
    refs = list(refs)
    x_ref, g_ref, wa_ref, wg_ref, cw_ref, cb_ref, wd_ref = refs[:7]
    pos = 7
    buf_ref = None
    if not zero_init:
        buf_ref = refs[pos]
        pos += 1
    fg_ref = None
    if final_norm:
        fg_ref = refs[pos]
        pos += 1
    y_ref, nbuf_ref = refs[pos], refs[pos + 1]

    @pl.when(pl.program_id(1) == 0)
    def _():
        if zero_init:
            nbuf_ref[...] = jnp.zeros(nbuf_ref.shape, F32)
        else:
            nbuf_ref[...] = buf_ref[...]

    x = x_ref[...]
    h = _rmsnorm(x, g_ref[...]).astype(BF16)
    a = jnp.dot(h, wa_ref[...], preferred_element_type=F32)
    gv = jnp.dot(h, wg_ref[...], preferred_element_type=F32)
    a_c = _causal_conv(a, nbuf_ref, cw_ref, cb_ref, nb, ls)
    act = (jax.nn.gelu(a_c) * gv).astype(BF16)
    y = x + jnp.dot(act, wd_ref[...], preferred_element_type=F32)
    if final_norm:
        y = _rmsnorm(y, fg_ref[...])
    y_ref[...] = y


def _ffn(x2, g, w_a, w_g, cw, cb, w_d, buf, final_g, *, bsz, seq, nb, ls):
    t, d = x2.shape
    d_ff = w_a.shape[1]
    nj = seq // ls
    zero_init = buf is None
    final_norm = final_g is not None
    row_spec = pl.BlockSpec((nb * ls, d), lambda i, j: (i * nj + j, 0))
    buf_spec = pl.BlockSpec((nb, CONV_F - 1, d_ff), lambda i, j: (i, 0, 0))
    args = [x2, g, w_a, w_g, cw, cb, w_d]
    in_specs = [row_spec] + [_const_spec(a.shape) for a in args[1:]]
    if not zero_init:
        args.append(buf)
        in_specs.append(buf_spec)
    if final_norm:
        args.append(final_g)
        in_specs.append(_const_spec(final_g.shape))
    return pl.pallas_call(
        functools.partial(_ffn_kernel, nb=nb, ls=ls, zero_init=zero_init, final_norm=final_norm),
        grid=(bsz // nb, nj),
        in_specs=in_specs,
        out_specs=[row_spec, buf_spec],
        out_shape=[jax.ShapeDtypeStruct((t, d), F32),
                   jax.ShapeDtypeStruct((bsz, CONV_F - 1, d_ff), F32)],
        compiler_params=_params(2),
        name="conv_ffn",
    )(*args)


def _log_sigmoid(x):
    return jnp.minimum(x, 0.0) - jnp.log1p(jnp.exp(-jnp.abs(x)))


def _mlstm_proj_kernel(*refs, nb, ls, zero_init, k_scale):
    refs = list(refs)
    (x_ref, g_ref, wqk_ref, wv_ref, wo_ref, wi_ref, wf_ref, cw_ref, cb_ref,
     bi_ref, bf_ref) = refs[:11]
    pos = 11
    buf_ref = None
    if not zero_init:
        buf_ref = refs[pos]
        pos += 1
    q_ref, k_ref, v_ref, o_ref, ig_ref, lf_ref, nbuf_ref = refs[pos:pos + 7]

    @pl.when(pl.program_id(1) == 0)
    def _():
        if zero_init:
            nbuf_ref[...] = jnp.zeros(nbuf_ref.shape, F32)
        else:
            nbuf_ref[...] = buf_ref[...]

    h = _rmsnorm(x_ref[...], g_ref[...]).astype(BF16)
    qk_pre = jnp.dot(h, wqk_ref[...], preferred_element_type=F32)
    v_ref[...] = jnp.dot(h, wv_ref[...], preferred_element_type=F32)
    o_ref[...] = jnp.dot(h, wo_ref[...], preferred_element_type=F32)
    ig_ref[...] = jnp.dot(h, wi_ref[...], preferred_element_type=F32) + bi_ref[...]
    lf_ref[...] = _log_sigmoid(jnp.dot(h, wf_ref[...], preferred_element_type=F32) + bf_ref[...])
    qk = jax.nn.silu(_causal_conv(qk_pre, nbuf_ref, cw_ref, cb_ref, nb, ls))
    dq = q_ref.shape[1]
    q_ref[...] = qk[:, :dq]
    k_ref[...] = qk[:, dq:] * k_scale


def _mlstm_proj(x2, g, w_qk, w_v, w_o, w_i, w_f, cw, cb, b_i, b_f, buf, *, bsz, seq, nb, ls,
                k_scale):
    t, d = x2.shape
    qk_w = w_qk.shape[1]
    d_b = w_v.shape[1]
    nj = seq // ls
    zero_init = buf is None

    def rows(width):
        return pl.BlockSpec((nb * ls, width), lambda i, j: (i * nj + j, 0))

    buf_spec = pl.BlockSpec((nb, CONV_B - 1, qk_w), lambda i, j: (i, 0, 0))
    args = [x2, g, w_qk, w_v, w_o, w_i, w_f, cw, cb, b_i, b_f]
    in_specs = [rows(d)] + [_const_spec(a.shape) for a in args[1:]]
    if not zero_init:
        args.append(buf)
        in_specs.append(buf_spec)
    return pl.pallas_call(
        functools.partial(_mlstm_proj_kernel, nb=nb, ls=ls, zero_init=zero_init, k_scale=k_scale),
        grid=(bsz // nb, nj),
        in_specs=in_specs,
        out_specs=[rows(qk_w // 2), rows(qk_w // 2), rows(d_b), rows(d_b), rows(LANES),
                   rows(LANES), buf_spec],
        out_shape=[jax.ShapeDtypeStruct((t, qk_w // 2), F32),
                   jax.ShapeDtypeStruct((t, qk_w // 2), F32),
                   jax.ShapeDtypeStruct((t, d_b), F32),
                   jax.ShapeDtypeStruct((t, d_b), F32),
                   jax.ShapeDtypeStruct((t, LANES), F32),
                   jax.ShapeDtypeStruct((t, LANES), F32),
                   jax.ShapeDtypeStruct((bsz, CONV_B - 1, qk_w), F32)],
        compiler_params=_params(2),
        name="mlstm_proj",
    )(*args)


def _mlstm_cell_kernel(*refs, nb, ls, zero_init):
    refs = list(refs)
    q_ref, k_ref, v_ref, ig_ref, lf_ref = refs[:5]
    pos = 5
    if not zero_init:
        c0_ref, n0_ref, m0_ref = refs[pos:pos + 3]
        pos += 3
    h_ref, c_ref, n_ref, m_ref, qc_ref, dec_ref = refs[pos:pos + 6]

    hd = pl.program_id(0)
    i = pl.program_id(1)
    r = nb * ls
    seqs = pl.ds(pl.multiple_of(i * nb, nb), nb)

    @pl.when(pl.program_id(2) == 0)
    def _():
        if zero_init:
            c_ref[...] = jnp.zeros(c_ref.shape, F32)
            n_ref[0, seqs, :] = jnp.zeros((nb, LANES), F32)
            m_ref[0, seqs, :] = jnp.zeros((nb, LANES), F32)
        else:
            c_ref[...] = c0_ref[...]
            n_ref[0, seqs, :] = n0_ref[0, seqs, :]
            m_ref[0, seqs, :] = m0_ref[0, seqs, :]

    sel = lax.broadcasted_iota(jnp.int32, (r, LANES), 1) == hd
    ig_col = jnp.sum(jnp.where(sel, ig_ref[...], 0.0), axis=1, keepdims=True)
    lf_col = jnp.sum(jnp.where(sel, lf_ref[...], 0.0), axis=1, keepdims=True)
    r_i = lax.broadcasted_iota(jnp.int32, (r, r), 0)
    c_i = lax.broadcasted_iota(jnp.int32, (r, r), 1)
    mask = c_i <= r_i
    if nb > 1:
        mask = mask & ((r_i // ls) == (c_i // ls))
    b_all = jnp.dot(mask.astype(F32), jnp.broadcast_to(lf_col, (r, LANES)),
                    precision=HIGHEST, preferred_element_type=F32)
    b_col = b_all[:, 0:1]
    b_row = b_all.T[0:1, :]
    i_row = jnp.broadcast_to(ig_col, (r, LANES)).T[0:1, :]
    m_prev_seq = m_ref[0, seqs, :]
    m_prev = _expand2(m_prev_seq, nb, ls)[:, 0:1]
    d = jnp.where(mask, b_col - b_row + i_row, -jnp.inf)
    inter = b_col + m_prev
    m_t = jnp.maximum(inter, jnp.max(d, axis=1, keepdims=True))
    w_intra = jnp.exp(d - m_t)
    w_inter = jnp.exp(inter - m_t)

    q = q_ref[...]
    k = k_ref[...]
    vb = v_ref[...].astype(BF16)
    s = lax.dot_general(q.astype(BF16), k.astype(BF16), (((1,), (1,)), ((), ())),
                        preferred_element_type=F32)
    sc = s * w_intra
    intra = jnp.dot(sc.astype(BF16), vb, preferred_element_type=F32)
    den_intra = jnp.sum(sc, axis=1, keepdims=True)

    def qc_body(b, carry):
        rows = pl.ds(pl.multiple_of(b * ls, ls), ls)
        qc_ref[rows, :] = jnp.dot(q_ref[rows, :].astype(BF16), c_ref[b, 0].astype(BF16),
                                  preferred_element_type=F32)
        return carry

    lax.fori_loop(0, nb, qc_body, 0)
    n_seq = n_ref[0, seqs, :]
    qn = jnp.sum(q * _expand2(n_seq, nb, ls), axis=1, keepdims=True)
    num = w_inter * qc_ref[...] + intra
    den = w_inter * qn + den_intra
    h_ref[...] = num / jnp.maximum(jnp.abs(den), jnp.exp(-m_t))

    m_t_b = jnp.broadcast_to(m_t, (r, LANES))
    if nb == 1:
        bl_seq = b_all[r - 1:r, :]
        mn_seq = m_t_b[r - 1:r, :]
        bl_rows = bl_seq[:, 0:1]
        mn_rows = mn_seq[:, 0:1]
        kg_sum = None
    else:
        last = (c_i == (r_i // ls) * ls + (ls - 1)).astype(F32)
        bl_rows = jnp.dot(last, b_all, precision=HIGHEST, preferred_element_type=F32)[:, 0:1]
        mn_rows = jnp.dot(last, m_t_b, precision=HIGHEST, preferred_element_type=F32)[:, 0:1]
        sq_i = lax.broadcasted_iota(jnp.int32, (nb, r), 0)
        sr_i = lax.broadcasted_iota(jnp.int32, (nb, r), 1)
        pick = (sr_i == sq_i * ls + (ls - 1)).astype(F32)
        bl_seq = jnp.dot(pick, b_all, precision=HIGHEST, preferred_element_type=F32)
        mn_seq = jnp.dot(pick, m_t_b, precision=HIGHEST, preferred_element_type=F32)
    g = jnp.exp(bl_rows - b_col + ig_col - mn_rows)
    kg = k * g
    if nb == 1:
        kg_sum = jnp.sum(kg, axis=0, keepdims=True)
    else:
        kg_sum = jnp.sum(kg.reshape(nb, ls, LANES), axis=1)
    decay_seq = jnp.exp(bl_seq + m_prev_seq - mn_seq)
    n_ref[0, seqs, :] = decay_seq * n_seq + kg_sum
    m_ref[0, seqs, :] = mn_seq
    dec_ref[...] = jnp.broadcast_to(decay_seq[:, 0:1], dec_ref.shape)
    kg_t = kg.T
    col_seq = lax.broadcasted_iota(jnp.int32, (LANES, r), 1) // ls

    def c_body(b, carry):
        lhs = jnp.where(col_seq == b, kg_t, 0.0).astype(BF16)
        upd = jnp.dot(lhs, vb, preferred_element_type=F32)
        c_ref[b, 0] = dec_ref[pl.ds(b, 1), :] * c_ref[b, 0] + upd
        return carry

    lax.fori_loop(0, nb, c_body, 0)


def _mlstm_cell(q, k, v, ig, lf, c0, n0, m0, *, bsz, seq, nb, ls):
    t = q.shape[0]
    dk = q.shape[1] // H_B
    dv = v.shape[1] // H_B
    nj = seq // ls
    zero_init = c0 is None

    def rows(width, per_head):
        if per_head:
            return pl.BlockSpec((nb * ls, width), lambda h, i, j: (i * nj + j, h))
        return pl.BlockSpec((nb * ls, width), lambda h, i, j: (i * nj + j, 0))

    c_spec = pl.BlockSpec((nb, 1, dk, dv), lambda h, i, j: (i, h, 0, 0))
    s_spec = pl.BlockSpec((1, bsz, LANES), lambda h, i, j: (h, 0, 0))
    args = [q, k, v, ig, lf]
    in_specs = [rows(dk, True), rows(dk, True), rows(dv, True), rows(LANES, False),
                rows(LANES, False)]
    if not zero_init:
        args += [c0, n0, m0]
        in_specs += [c_spec, s_spec, s_spec]
    return pl.pallas_call(
        functools.partial(_mlstm_cell_kernel, nb=nb, ls=ls, zero_init=zero_init),
        grid=(H_B, bsz // nb, nj),
        in_specs=in_specs,
        out_specs=[rows(dv, True), c_spec, s_spec, s_spec],
        out_shape=[jax.ShapeDtypeStruct((t, H_B * dv), F32),
                   jax.ShapeDtypeStruct((bsz, H_B, dk, dv), F32),
                   jax.ShapeDtypeStruct((H_B, bsz, LANES), F32),
                   jax.ShapeDtypeStruct((H_B, bsz, LANES), F32)],
        scratch_shapes=[pltpu.VMEM((nb * ls, dv), F32), pltpu.VMEM((nb, dv), F32)],
        compiler_params=_params(3),
        name="mlstm_cell",
    )(*args)


def _mlstm_out_kernel(h_ref, o_ref, x_ref, gn_ref, wout_ref, y_ref, gate_ref):
    dv = h_ref.shape[1] // H_B
    for hd in range(H_B):
        cols = slice(hd * dv, (hd + 1) * dv)
        hh = h_ref[:, cols]
        mu = jnp.mean(hh, axis=-1, keepdims=True)
        hc = hh - mu
        var = jnp.mean(hc * hc, axis=-1, keepdims=True)
        hn = hc * lax.rsqrt(var + EPS) * gn_ref[:, cols]
        gate_ref[:, cols] = (jax.nn.sigmoid(o_ref[:, cols]) * hn).astype(BF16)
    y_ref[...] = x_ref[...] + jnp.dot(gate_ref[...], wout_ref[...], preferred_element_type=F32)


def _mlstm_out(h, o, x2, gn_g, w_out, *, tm):
    t, d = x2.shape
    d_b = h.shape[1]
    return pl.pallas_call(
        _mlstm_out_kernel,
        grid=(t // tm,),
        in_specs=[pl.BlockSpec((tm, d_b), lambda i: (i, 0)),
                  pl.BlockSpec((tm, d_b), lambda i: (i, 0)),
                  pl.BlockSpec((tm, d), lambda i: (i, 0)),
                  _const_spec(gn_g.shape), _const_spec(w_out.shape)],
        out_specs=pl.BlockSpec((tm, d), lambda i: (i, 0)),
        out_shape=jax.ShapeDtypeStruct((t, d), F32),
        scratch_shapes=[pltpu.VMEM((tm, d_b), BF16)],
        compiler_params=_params(1),
        name="mlstm_out",
    )(h, o, x2, gn_g, w_out)


def _trunk(x, state, w, *, nb, ls, tm):
    bsz, seq, d = x.shape
    x2 = x.reshape(bsz * seq, d)
    fresh = state is None
    seq_chunk = min(seq, CHUNK_A)

    ws = w["a_ws"] if seq_chunk == CHUNK_A else jnp.tile(
        w["a_ws"][:, :seq_chunk, :seq_chunk], (1, CHUNK_A // seq_chunk, CHUNK_A // seq_chunk))
    bs = jnp.tile(w["a_bs_t"][:seq_chunk], (CHUNK_A // seq_chunk, 1))
    res = _mixer_a(x2, w["norm_mix_g"][0:1], w["a_w_in"], w["a_ln_g"], w["a_ln_b"], ws, bs,
                   w["a_w_out"], tm=tm, seq_chunk=seq_chunk, emit_v=not fresh)
    x2 = res[0]
    v_rows = None if fresh else res[1]
    x2, fbuf0 = _ffn(x2, w["norm_ffn_g"][0:1], w["f_w_a"][0], w["f_w_g"][0], w["f_conv_w"][0],
                     w["f_conv_b"][0:1], w["f_w_down"][0],
                     None if fresh else state["ffn_conv"][0], None,
                     bsz=bsz, seq=seq, nb=nb, ls=ls)

    q, k, v, o, ig, lf, mconv = _mlstm_proj(
        x2, w["norm_mix_g"][1:2], w["b_w_qk"], w["b_w_v"], w["b_w_o"], w["b_w_i"], w["b_w_f"],
        w["b_conv_w"], w["b_conv_b"], w["b_bias_i"], w["b_bias_f"],
        None if fresh else state["mlstm_conv"], bsz=bsz, seq=seq, nb=nb, ls=ls,
        k_scale=w["k_scale"])
    if fresh:
        c0 = n0 = m0 = None
    else:
        c0 = state["mlstm_C"]
        n0 = jnp.transpose(state["mlstm_n"], (1, 0, 2))
        m0 = jnp.broadcast_to(jnp.transpose(state["mlstm_m"])[:, :, None], (H_B, bsz, LANES))
    h, c_new, n_new, m_new = _mlstm_cell(q, k, v, ig, lf, c0, n0, m0, bsz=bsz, seq=seq, nb=nb,
                                         ls=ls)
    x2 = _mlstm_out(h, o, x2, w["b_gn_g"], w["b_w_out"], tm=tm)
    y2, fbuf1 = _ffn(x2, w["norm_ffn_g"][1:2], w["f_w_a"][1], w["f_w_g"][1], w["f_conv_w"][1],
                     w["f_conv_b"][1:2], w["f_w_down"][1],
                     None if fresh else state["ffn_conv"][1], w["final_norm_g"],
                     bsz=bsz, seq=seq, nb=nb, ls=ls)
    return dict(
        y=y2.reshape(bsz, seq, d),
        v=None if fresh else v_rows.reshape(1, bsz, seq, -1),
        C=c_new[None],
        n=jnp.transpose(n_new, (1, 0, 2))[None],
        m=jnp.transpose(m_new[:, :, 0])[None],
        mconv=mconv[None],
        fconv=jnp.stack([fbuf0, fbuf1]),
    )


def kernel(x_prompt, x_sample, state_mlstm_C, state_mlstm_n, state_mlstm_m, state_mlstm_conv, state_ffn_conv, norm_mix_g, norm_ffn_g, final_norm_g, a_w_in, a_ln_g, a_ln_b, a_w_s, a_b_s, a_w_out, b_w_in, b_conv_w, b_conv_b, b_bias_i, b_bias_f, b_gn_g, b_w_out, f_w_up, f_conv_w, f_conv_b, f_w_down):
    d_ff = f_w_down.shape[1]
    qk_w = b_conv_w.shape[2]
    d_b = b_w_out.shape[1]
    dk = qk_w // (2 * H_B)
    d_a = a_w_out.shape[1]
    dg = d_a // G_A

    def pad_gate(cols):
        return jnp.pad(cols, ((0, 0), (0, LANES - H_B)))

    w_in_b = b_w_in[0]
    w = dict(
        norm_mix_g=norm_mix_g, norm_ffn_g=norm_ffn_g, final_norm_g=final_norm_g[None, :],
        a_w_in=a_w_in[0].astype(BF16), a_ln_g=a_ln_g, a_ln_b=a_ln_b,
        a_ws=a_w_s[0].astype(BF16),
        a_bs_t=jnp.repeat(jnp.transpose(a_b_s[0]), dg, axis=1),
        a_w_out=a_w_out[0].astype(BF16),
        b_w_qk=w_in_b[:, :qk_w].astype(BF16),
        b_w_v=w_in_b[:, qk_w:qk_w + d_b].astype(BF16),
        b_w_o=w_in_b[:, qk_w + d_b:qk_w + 2 * d_b].astype(BF16),
        b_w_i=pad_gate(w_in_b[:, qk_w + 2 * d_b:qk_w + 2 * d_b + H_B]).astype(BF16),
        b_w_f=pad_gate(w_in_b[:, qk_w + 2 * d_b + H_B:]).astype(BF16),
        b_conv_w=b_conv_w[0], b_conv_b=b_conv_b,
        b_bias_i=jnp.pad(b_bias_i, ((0, 0), (0, LANES - H_B))),
        b_bias_f=jnp.pad(b_bias_f, ((0, 0), (0, LANES - H_B))),
        b_gn_g=b_gn_g, b_w_out=b_w_out[0].astype(BF16),
        f_w_a=f_w_up[:, :, :d_ff].astype(BF16), f_w_g=f_w_up[:, :, d_ff:].astype(BF16),
        f_conv_w=f_conv_w, f_conv_b=f_conv_b, f_w_down=f_w_down.astype(BF16),
        k_scale=float(dk) ** -0.5,
    )
    p = _trunk(x_prompt, None, w, nb=1, ls=256, tm=256)
    state = dict(mlstm_C=state_mlstm_C[0], mlstm_n=state_mlstm_n[0], mlstm_m=state_mlstm_m[0],
                 mlstm_conv=state_mlstm_conv[0], ffn_conv=state_ffn_conv)
    s = _trunk(x_sample, state, w, nb=16, ls=x_sample.shape[1], tm=128)
    return (p["y"], s["y"], p["C"], p["n"], p["m"], p["mconv"], p["fconv"],
            s["v"], s["C"], s["n"], s["m"], s["mconv"], s["fconv"])
```

```python
import functools

import jax
import jax.numpy as jnp
from jax import lax
from jax.experimental import pallas as pl
from jax.experimental.pallas import tpu as pltpu

F32 = jnp.float32
BF16 = jnp.bfloat16
HIGHEST = lax.Precision.HIGHEST

EPS = 1e-6
LANES = 128
SUBLANES = 8
VMEM_LIMIT_BYTES = 56 * 1024 * 1024

CHUNK_A = 128
G_A = 8
H_B = 8
CONV_B = 4
CONV_F = 3


def _const_spec(shape):
    nd = len(shape)
    return pl.BlockSpec(shape, lambda *_: (0,) * nd, pipeline_mode=pl.Buffered(1))


def _params(n_axes):
    return pltpu.CompilerParams(
        dimension_semantics=("arbitrary",) * n_axes,
        vmem_limit_bytes=VMEM_LIMIT_BYTES,
    )


def _rmsnorm(x, g):
    ms = jnp.mean(x * x, axis=-1, keepdims=True)
    return x * lax.rsqrt(ms + EPS) * g


def _expand3(c3, nb, ls):
    w = c3.shape[-1]
    if nb == 1:
        return jnp.broadcast_to(c3.reshape(1, w), (ls, w))
    return jnp.broadcast_to(c3, (nb, ls, w)).reshape(nb * ls, w)


def _expand2(c2, nb, ls):
    return _expand3(c2[:, None, :], nb, ls)


def _causal_conv(a, carry_ref, cw_ref, cb_ref, nb, ls):
    r, c = a.shape
    kw = cw_ref.shape[0]
    rolled = [a] + [pltpu.roll(a, k, 0) for k in range(1, kw)]

    def taps(shifted):
        y = cb_ref[...] + cw_ref[kw - 1:kw, :] * shifted[0]
        for k in range(1, kw):
            y = y + cw_ref[kw - 1 - k:kw - k, :] * shifted[k]
        return y

    grp = SUBLANES if nb == 1 else r
    grp_ls = SUBLANES if nb == 1 else ls
    tpos = lax.broadcasted_iota(jnp.int32, (grp, c), 0) & (grp_ls - 1)
    fixed = [a[0:grp]]
    for k in range(1, kw):
        sh = rolled[k][0:grp]
        for t in range(k):
            idx = kw - 1 + t - k
            prev = _expand3(carry_ref[:, idx:idx + 1, :], nb, grp_ls)
            sh = jnp.where(tpos == t, prev, sh)
        fixed.append(sh)
    y = taps(fixed)
    if nb == 1:
        y = jnp.concatenate([y, taps([s[grp:] for s in rolled])], axis=0)
    last = rolled[kw - 1]
    if nb == 1:
        carry_ref[0] = last[0:kw - 1, :]
    else:
        heads = last.reshape(nb, ls, c)[:, 0:kw - 1, :]
        carry_ref[0:nb - 1] = heads[1:nb]
        carry_ref[nb - 1:nb] = heads[0:1]
    return y


def _mixer_a_kernel(x_ref, g_ref, win_ref, lng_ref, lnb_ref, ws_ref, bs_ref, wout_ref,
                    y_ref, *rest, seq_chunk, emit_v):
    if emit_v:
        v_ref, gate_ref = rest
    else:
        (gate_ref,) = rest
    tm = x_ref.shape[0]
    d_a = lng_ref.shape[1]
    dg = d_a // G_A
    x = x_ref[...]
    h = _rmsnorm(x, g_ref[...]).astype(BF16)
    hh = jax.nn.gelu(jnp.dot(h, win_ref[...], preferred_element_type=F32))
    v = hh[:, d_a:]
    mu = jnp.mean(v, axis=-1, keepdims=True)
    vc = v - mu
    var = jnp.mean(vc * vc, axis=-1, keepdims=True)
    vn = vc * lax.rsqrt(var + EPS) * lng_ref[...] + lnb_ref[...]
    if emit_v:
        v_ref[...] = vn
    vb = vn.astype(BF16)
    t_i = lax.broadcasted_iota(jnp.int32, (CHUNK_A, CHUNK_A), 0)
    s_i = lax.broadcasted_iota(jnp.int32, (CHUNK_A, CHUNK_A), 1)
    mask = (s_i <= t_i) & ((t_i // seq_chunk) == (s_i // seq_chunk))
    for g in range(G_A):
        w = jnp.where(mask, ws_ref[g], jnp.zeros((), BF16))
        cols = slice(g * dg, (g + 1) * dg)
        for c in range(tm // CHUNK_A):
            rows = slice(c * CHUNK_A, (c + 1) * CHUNK_A)
            mixed = jnp.dot(w, vb[rows, cols], preferred_element_type=F32) + bs_ref[:, cols]
            gate_ref[rows, cols] = (hh[rows, cols] * mixed).astype(BF16)
    y_ref[...] = x + jnp.dot(gate_ref[...], wout_ref[...], preferred_element_type=F32)


def _mixer_a(x2, g, w_in, ln_g, ln_b, ws, bs, w_out, *, tm, seq_chunk, emit_v):
    t, d = x2.shape
    d_a = ln_g.shape[1]
    out_shape = [jax.ShapeDtypeStruct((t, d), F32)]
    out_specs = [pl.BlockSpec((tm, d), lambda i: (i, 0))]
    if emit_v:
        out_shape.append(jax.ShapeDtypeStruct((t, d_a), F32))
        out_specs.append(pl.BlockSpec((tm, d_a), lambda i: (i, 0)))
    return pl.pallas_call(
        functools.partial(_mixer_a_kernel, seq_chunk=seq_chunk, emit_v=emit_v),
        grid=(t // tm,),
        in_specs=[
            pl.BlockSpec((tm, d), lambda i: (i, 0)),
            _const_spec(g.shape), _const_spec(w_in.shape), _const_spec(ln_g.shape),
            _const_spec(ln_b.shape), _const_spec(ws.shape), _const_spec(bs.shape),
            _const_spec(w_out.shape),
        ],
        out_specs=out_specs,
        out_shape=out_shape,
        scratch_shapes=[pltpu.VMEM((tm, d_a), BF16)],
        compiler_params=_params(1),
        name="mixer_a",
    )(x2, g, w_in, ln_g, ln_b, ws, bs, w_out)


def _ffn_kernel(*refs, nb, ls, zero_init, final_norm):
    refs = list(refs)
    x_ref, g_ref, wa_ref, wg_ref, cw_ref, cb_ref, wd_ref = refs[:7]
    pos = 7
    buf_ref = None
    if not zero_init:
        buf_ref = refs[pos]
        pos += 1
    fg_ref = None
    if final_norm:
        fg_ref = refs[pos]
        pos += 1
    y_ref, nbuf_ref = refs[pos], refs[pos + 1]

    @pl.when(pl.program_id(1) == 0)
    def _():
        if zero_init:
            nbuf_ref[...] = jnp.zeros(nbuf_ref.shape, F32)
        else:
            nbuf_ref[...] = buf_ref[...]

    x = x_ref[...]
    h = _rmsnorm(x, g_ref[...]).astype(BF16)
    a = jnp.dot(h, wa_ref[...], preferred_element_type=F32)
    gv = jnp.dot(h, wg_ref[...], preferred_element_type=F32)
    a_c = _causal_conv(a, nbuf_ref, cw_ref, cb_ref, nb, ls)
    act = (jax.nn.gelu(a_c) * gv).astype(BF16)
    y = x + jnp.dot(act, wd_ref[...], preferred_element_type=F32)
    if final_norm:
        y = _rmsnorm(y, fg_ref[...])
    y_ref[...] = y


def _ffn(x2, g, w_a, w_g, cw, cb, w_d, buf, final_g, *, bsz, seq, nb, ls):
    t, d = x2.shape
    d_ff = w_a.shape[1]
    nj = seq // ls
    zero_init = buf is None
    final_norm = final_g is not None
    row_spec = pl.BlockSpec((nb * ls, d), lambda i, j: (i * nj + j, 0))
    buf_spec = pl.BlockSpec((nb, CONV_F - 1, d_ff), lambda i, j: (i, 0, 0))
    args = [x2, g, w_a, w_g, cw, cb, w_d]
    in_specs = [row_spec] + [_const_spec(a.shape) for a in args[1:]]
    if not zero_init:
        args.append(buf)
        in_specs.append(buf_spec)
    if final_norm:
        args.append(final_g)
        in_specs.append(_const_spec(final_g.shape))
    return pl.pallas_call(
        functools.partial(_ffn_kernel, nb=nb, ls=ls, zero_init=zero_init, final_norm=final_norm),
        grid=(bsz // nb, nj),
        in_specs=in_specs,
        out_specs=[row_spec, buf_spec],
        out_shape=[jax.ShapeDtypeStruct((t, d), F32),
                   jax.ShapeDtypeStruct((bsz, CONV_F - 1, d_ff), F32)],
        compiler_params=_params(2),
        name="conv_ffn",
    )(*args)


def _log_sigmoid(x):
    return jnp.minimum(x, 0.0) - jnp.log1p(jnp.exp(-jnp.abs(x)))


def _mlstm_proj_kernel(*refs, nb, ls, zero_init, k_scale):
    refs = list(refs)
    (x_ref, g_ref, wqk_ref, wv_ref, wo_ref, wi_ref, wf_ref, cw_ref, cb_ref,
     bi_ref, bf_ref) = refs[:11]
    pos = 11
    buf_ref = None
    if not zero_init:
        buf_ref = refs[pos]
        pos += 1
    q_ref, k_ref, v_ref, o_ref, ig_ref, lf_ref, nbuf_ref = refs[pos:pos + 7]

    @pl.when(pl.program_id(1) == 0)
    def _():
        if zero_init:
            nbuf_ref[...] = jnp.zeros(nbuf_ref.shape, F32)
        else:
            nbuf_ref[...] = buf_ref[...]

    h = _rmsnorm(x_ref[...], g_ref[...]).astype(BF16)
    qk_pre = jnp.dot(h, wqk_ref[...], preferred_element_type=F32)
    v_ref[...] = jnp.dot(h, wv_ref[...], preferred_element_type=F32)
    o_ref[...] = jnp.dot(h, wo_ref[...], preferred_element_type=F32)
    ig_ref[...] = jnp.dot(h, wi_ref[...], preferred_element_type=F32) + bi_ref[...]
    lf_ref[...] = _log_sigmoid(jnp.dot(h, wf_ref[...], preferred_element_type=F32) + bf_ref[...])
    qk = jax.nn.silu(_causal_conv(qk_pre, nbuf_ref, cw_ref, cb_ref, nb, ls))
    dq = q_ref.shape[1]
    q_ref[...] = qk[:, :dq]
    k_ref[...] = qk[:, dq:] * k_scale


def _mlstm_proj(x2, g, w_qk, w_v, w_o, w_i, w_f, cw, cb, b_i, b_f, buf, *, bsz, seq, nb, ls,
                k_scale):
    t, d = x2.shape
    qk_w = w_qk.shape[1]
    d_b = w_v.shape[1]
    nj = seq // ls
    zero_init = buf is None

    def rows(width):
        return pl.BlockSpec((nb * ls, width), lambda i, j: (i * nj + j, 0))

    buf_spec = pl.BlockSpec((nb, CONV_B - 1, qk_w), lambda i, j: (i, 0, 0))
    args = [x2, g, w_qk, w_v, w_o, w_i, w_f, cw, cb, b_i, b_f]
    in_specs = [rows(d)] + [_const_spec(a.shape) for a in args[1:]]
    if not zero_init:
        args.append(buf)
        in_specs.append(buf_spec)
    return pl.pallas_call(
        functools.partial(_mlstm_proj_kernel, nb=nb, ls=ls, zero_init=zero_init, k_scale=k_scale),
        grid=(bsz // nb, nj),
        in_specs=in_specs,
        out_specs=[rows(qk_w // 2), rows(qk_w // 2), rows(d_b), rows(d_b), rows(LANES),
                   rows(LANES), buf_spec],
        out_shape=[jax.ShapeDtypeStruct((t, qk_w // 2), F32),
                   jax.ShapeDtypeStruct((t, qk_w // 2), F32),
                   jax.ShapeDtypeStruct((t, d_b), F32),
                   jax.ShapeDtypeStruct((t, d_b), F32),
                   jax.ShapeDtypeStruct((t, LANES), F32),
                   jax.ShapeDtypeStruct((t, LANES), F32),
                   jax.ShapeDtypeStruct((bsz, CONV_B - 1, qk_w), F32)],
        compiler_params=_params(2),
        name="mlstm_proj",
    )(*args)


def _mlstm_cell_kernel(*refs, nb, ls, zero_init):
    refs = list(refs)
    q_ref, k_ref, v_ref, ig_ref, lf_ref = refs[:5]
    pos = 5
    if not zero_init:
        c0_ref, n0_ref, m0_ref = refs[pos:pos + 3]
        pos += 3
    h_ref, c_ref, n_ref, m_ref, qc_ref, dec_ref = refs[pos:pos + 6]

    hd = pl.program_id(0)
    i = pl.program_id(1)
    r = nb * ls
    seqs = pl.ds(pl.multiple_of(i * nb, nb), nb)

    @pl.when(pl.program_id(2) == 0)
    def _():
        if zero_init:
            c_ref[...] = jnp.zeros(c_ref.shape, F32)
            n_ref[0, seqs, :] = jnp.zeros((nb, LANES), F32)
            m_ref[0, seqs, :] = jnp.zeros((nb, LANES), F32)
        else:
            c_ref[...] = c0_ref[...]
            n_ref[0, seqs, :] = n0_ref[0, seqs, :]
            m_ref[0, seqs, :] = m0_ref[0, seqs, :]

    sel = lax.broadcasted_iota(jnp.int32, (r, LANES), 1) == hd
    ig_col = jnp.sum(jnp.where(sel, ig_ref[...], 0.0), axis=1, keepdims=True)
    lf_col = jnp.sum(jnp.where(sel, lf_ref[...], 0.0), axis=1, keepdims=True)
    r_i = lax.broadcasted_iota(jnp.int32, (r, r), 0)
    c_i = lax.broadcasted_iota(jnp.int32, (r, r), 1)
    mask = c_i <= r_i
    if nb > 1:
        mask = mask & ((r_i // ls) == (c_i // ls))
    b_all = jnp.dot(mask.astype(F32), jnp.broadcast_to(lf_col, (r, LANES)),
                    precision=HIGHEST, preferred_element_type=F32)
    b_col = b_all[:, 0:1]
    b_row = b_all.T[0:1, :]
    i_row = jnp.broadcast_to(ig_col, (r, LANES)).T[0:1, :]
    m_prev_seq = m_ref[0, seqs, :]
    m_prev = _expand2(m_prev_seq, nb, ls)[:, 0:1]
    d = jnp.where(mask, b_col - b_row + i_row, -jnp.inf)
    inter = b_col + m_prev
    m_t = jnp.maximum(inter, jnp.max(d, axis=1, keepdims=True))
    w_intra = jnp.exp(d - m_t)
    w_inter = jnp.exp(inter - m_t)

    q = q_ref[...]
    k = k_ref[...]
    vb = v_ref[...].astype(BF16)
    s = lax.dot_general(q.astype(BF16), k.astype(BF16), (((1,), (1,)), ((), ())),
                        preferred_element_type=F32)
    sc = s * w_intra
    intra = jnp.dot(sc.astype(BF16), vb, preferred_element_type=F32)
    den_intra = jnp.sum(sc, axis=1, keepdims=True)

    def qc_body(b, carry):
        rows = pl.ds(pl.multiple_of(b * ls, ls), ls)
        qc_ref[rows, :] = jnp.dot(q_ref[rows, :].astype(BF16), c_ref[b, 0].astype(BF16),
                                  preferred_element_type=F32)
        return carry

    lax.fori_loop(0, nb, qc_body, 0)
    n_seq = n_ref[0, seqs, :]
    qn = jnp.sum(q * _expand2(n_seq, nb, ls), axis=1, keepdims=True)
    num = w_inter * qc_ref[...] + intra
    den = w_inter * qn + den_intra
    h_ref[...] = num / jnp.maximum(jnp.abs(den), jnp.exp(-m_t))

    m_t_b = jnp.broadcast_to(m_t, (r, LANES))
    if nb == 1:
        bl_seq = b_all[r - 1:r, :]
        mn_seq = m_t_b[r - 1:r, :]
        bl_rows = bl_seq[:, 0:1]
        mn_rows = mn_seq[:, 0:1]
        kg_sum = None
    else:
        last = (c_i == (r_i // ls) * ls + (ls - 1)).astype(F32)
        bl_rows = jnp.dot(last, b_all, precision=HIGHEST, preferred_element_type=F32)[:, 0:1]
        mn_rows = jnp.dot(last, m_t_b, precision=HIGHEST, preferred_element_type=F32)[:, 0:1]
        sq_i = lax.broadcasted_iota(jnp.int32, (nb, r), 0)
        sr_i = lax.broadcasted_iota(jnp.int32, (nb, r), 1)
        pick = (sr_i == sq_i * ls + (ls - 1)).astype(F32)
        bl_seq = jnp.dot(pick, b_all, precision=HIGHEST, preferred_element_type=F32)
        mn_seq = jnp.dot(pick, m_t_b, precision=HIGHEST, preferred_element_type=F32)
    g = jnp.exp(bl_rows - b_col + ig_col - mn_rows)
    kg = k * g
    if nb == 1:
        kg_sum = jnp.sum(kg, axis=0, keepdims=True)
    else:
        kg_sum = jnp.sum(kg.reshape(nb, ls, LANES), axis=1)
    decay_seq = jnp.exp(bl_seq + m_prev_seq - mn_seq)
    n_ref[0, seqs, :] = decay_seq * n_seq + kg_sum
    m_ref[0, seqs, :] = mn_seq
    dec_ref[...] = jnp.broadcast_to(decay_seq[:, 0:1], dec_ref.shape)
    kg_t = kg.T
    col_seq = lax.broadcasted_iota(jnp.int32, (LANES, r), 1) // ls

    def c_body(b, carry):
        lhs = jnp.where(col_seq == b, kg_t, 0.0).astype(BF16)
        upd = jnp.dot(lhs, vb, preferred_element_type=F32)
        c_ref[b, 0] = dec_ref[pl.ds(b, 1), :] * c_ref[b, 0] + upd
        return carry

    lax.fori_loop(0, nb, c_body, 0)


def _mlstm_cell(q, k, v, ig, lf, c0, n0, m0, *, bsz, seq, nb, ls):
    t = q.shape[0]
    dk = q.shape[1] // H_B
    dv = v.shape[1] // H_B
    nj = seq // ls
    zero_init = c0 is None

    def rows(width, per_head):
        if per_head:
            return pl.BlockSpec((nb * ls, width), lambda h, i, j: (i * nj + j, h))
        return pl.BlockSpec((nb * ls, width), lambda h, i, j: (i * nj + j, 0))

    c_spec = pl.BlockSpec((nb, 1, dk, dv), lambda h, i, j: (i, h, 0, 0))
    s_spec = pl.BlockSpec((1, bsz, LANES), lambda h, i, j: (h, 0, 0))
    args = [q, k, v, ig, lf]
    in_specs = [rows(dk, True), rows(dk, True), rows(dv, True), rows(LANES, False),
                rows(LANES, False)]
    if not zero_init:
        args += [c0, n0, m0]
        in_specs += [c_spec, s_spec, s_spec]
    return pl.pallas_call(
        functools.partial(_mlstm_cell_kernel, nb=nb, ls=ls, zero_init=zero_init),
        grid=(H_B, bsz // nb, nj),
        in_specs=in_specs,
        out_specs=[rows(dv, True), c_spec, s_spec, s_spec],
        out_shape=[jax.ShapeDtypeStruct((t, H_B * dv), F32),
                   jax.ShapeDtypeStruct((bsz, H_B, dk, dv), F32),
                   jax.ShapeDtypeStruct((H_B, bsz, LANES), F32),
                   jax.ShapeDtypeStruct((H_B, bsz, LANES), F32)],
        scratch_shapes=[pltpu.VMEM((nb * ls, dv), F32), pltpu.VMEM((nb, dv), F32)],
        compiler_params=_params(3),
        name="mlstm_cell",
    )(*args)


def _mlstm_out_kernel(h_ref, o_ref, x_ref, gn_ref, wout_ref, y_ref, gate_ref):
    dv = h_ref.shape[1] // H_B
    for hd in range(H_B):
        cols = slice(hd * dv, (hd + 1) * dv)
        hh = h_ref[:, cols]
        mu = jnp.mean(hh, axis=-1, keepdims=True)
        hc = hh - mu
        var = jnp.mean(hc * hc, axis=-1, keepdims=True)
        hn = hc * lax.rsqrt(var + EPS) * gn_ref[:, cols]
        gate_ref[:, cols] = (jax.nn.sigmoid(o_ref[:, cols]) * hn).astype(BF16)
    y_ref[...] = x_ref[...] + jnp.dot(gate_ref[...], wout_ref[...], preferred_element_type=F32)


def _mlstm_out(h, o, x2, gn_g, w_out, *, tm):
    t, d = x2.shape
    d_b = h.shape[1]
    return pl.pallas_call(
        _mlstm_out_kernel,
        grid=(t // tm,),
        in_specs=[pl.BlockSpec((tm, d_b), lambda i: (i, 0)),
                  pl.BlockSpec((tm, d_b), lambda i: (i, 0)),
                  pl.BlockSpec((tm, d), lambda i: (i, 0)),
                  _const_spec(gn_g.shape), _const_spec(w_out.shape)],
        out_specs=pl.BlockSpec((tm, d), lambda i: (i, 0)),
        out_shape=jax.ShapeDtypeStruct((t, d), F32),
        scratch_shapes=[pltpu.VMEM((tm, d_b), BF16)],
        compiler_params=_params(1),
        name="mlstm_out",
    )(h, o, x2, gn_g, w_out)


def _mlstm_fused_kernel(x_ref, g_ref, wqk_ref, wv_ref, wo_ref, wif_ref, cw_ref, cb_ref, bif_ref,
                        gn_ref, wout_ref, y_ref, c_ref, n_ref, m_ref, nbuf_ref, gate_ref, *,
                        k_scale):
    r = x_ref.shape[0]
    dk = c_ref.shape[2]
    dv = c_ref.shape[3]

    @pl.when(pl.program_id(1) == 0)
    def _():
        c_ref[...] = jnp.zeros(c_ref.shape, F32)
        n_ref[...] = jnp.zeros(n_ref.shape, F32)
        m_ref[...] = jnp.zeros(m_ref.shape, F32)
        nbuf_ref[...] = jnp.zeros(nbuf_ref.shape, F32)

    x = x_ref[...]
    h = _rmsnorm(x, g_ref[...]).astype(BF16)
    qk_pre = jnp.dot(h, wqk_ref[...], preferred_element_type=F32)
    qk = jax.nn.silu(_causal_conv(qk_pre, nbuf_ref, cw_ref, cb_ref, 1, r))
    gates = jnp.dot(h, wif_ref[...], preferred_element_type=F32) + bif_ref[...]
    lf = _log_sigmoid(gates)
    r_i = lax.broadcasted_iota(jnp.int32, (r, r), 0)
    c_i = lax.broadcasted_iota(jnp.int32, (r, r), 1)
    mask = c_i <= r_i
    b_all = jnp.dot(mask.astype(F32), lf, precision=HIGHEST, preferred_element_type=F32)
    b_t = b_all.T
    ig_t = gates.T

    for hd in range(H_B):
        ig_col = gates[:, hd:hd + 1]
        i_row = ig_t[hd:hd + 1, :]
        b_col = b_all[:, H_B + hd:H_B + hd + 1]
        b_row = b_t[H_B + hd:H_B + hd + 1, :]
        m_prev = m_ref[0, hd:hd + 1, 0:1]
        d = jnp.where(mask, b_col - b_row + i_row, -jnp.inf)
        inter = b_col + m_prev
        m_t = jnp.maximum(inter, jnp.max(d, axis=1, keepdims=True))
        w_intra = jnp.exp(d - m_t)
        w_inter = jnp.exp(inter - m_t)

        q = qk[:, hd * dk:(hd + 1) * dk]
        k = qk[:, (H_B + hd) * dk:(H_B + hd + 1) * dk] * k_scale
        vcols = slice(hd * dv, (hd + 1) * dv)
        vb = jnp.dot(h, wv_ref[:, vcols], preferred_element_type=F32).astype(BF16)
        qb = q.astype(BF16)
        s = lax.dot_general(qb, k.astype(BF16), (((1,), (1,)), ((), ())),
                            preferred_element_type=F32)
        sc = s * w_intra
        intra = jnp.dot(sc.astype(BF16), vb, preferred_element_type=F32)
        den_intra = jnp.sum(sc, axis=1, keepdims=True)
        c_old = c_ref[0, hd]
        n_old = n_ref[0, hd:hd + 1, :]
        qc = jnp.dot(qb, c_old.astype(BF16), preferred_element_type=F32)
        qn = jnp.sum(q * n_old, axis=1, keepdims=True)
        num = w_inter * qc + intra
        den = w_inter * qn + den_intra
        hout = num / jnp.maximum(jnp.abs(den), jnp.exp(-m_t))

        mu = jnp.mean(hout, axis=-1, keepdims=True)
        hc = hout - mu
        var = jnp.mean(hc * hc, axis=-1, keepdims=True)
        hn = hc * lax.rsqrt(var + EPS) * gn_ref[:, vcols]
        o = jax.nn.sigmoid(jnp.dot(h, wo_ref[:, vcols], preferred_element_type=F32))
        gate_ref[:, vcols] = (o * hn).astype(BF16)

        b_last = b_col[r - 1:r, :]
        m_new = m_t[r - 1:r, :]
        g = jnp.exp(b_last - b_col + ig_col - m_new)
        kg = k * g
        decay = jnp.exp(b_last + m_prev - m_new)
        c_ref[0, hd] = decay * c_old + lax.dot_general(
            kg.astype(BF16), vb, (((0,), (0,)), ((), ())), preferred_element_type=F32)
        n_ref[0, hd:hd + 1, :] = decay * n_old + jnp.sum(kg, axis=0, keepdims=True)
        m_ref[0, hd:hd + 1, :] = jnp.broadcast_to(m_new, (1, LANES))

    y_ref[...] = x + jnp.dot(gate_ref[...], wout_ref[...], preferred_element_type=F32)


def _mlstm_fused(x2, g, w_qk, w_v, w_o, w_if, cw, cb, b_if, gn_g, w_out, *, bsz, seq, ls, k_scale):
    t, d = x2.shape
    qk_w = w_qk.shape[1]
    d_b = w_v.shape[1]
    dk = qk_w // (2 * H_B)
    dv = d_b // H_B
    nj = seq // ls
    row_spec = pl.BlockSpec((ls, d), lambda i, j: (i * nj + j, 0))
    args = [x2, g, w_qk, w_v, w_o, w_if, cw, cb, b_if, gn_g, w_out]
    return pl.pallas_call(
        functools.partial(_mlstm_fused_kernel, k_scale=k_scale),
        grid=(bsz, nj),
        in_specs=[row_spec] + [_const_spec(a.shape) for a in args[1:]],
        out_specs=[row_spec,
                   pl.BlockSpec((1, H_B, dk, dv), lambda i, j: (i, 0, 0, 0)),
                   pl.BlockSpec((1, H_B, dk), lambda i, j: (i, 0, 0)),
                   pl.BlockSpec((1, H_B, LANES), lambda i, j: (i, 0, 0)),
                   pl.BlockSpec((1, CONV_B - 1, qk_w), lambda i, j: (i, 0, 0))],
        out_shape=[jax.ShapeDtypeStruct((t, d), F32),
                   jax.ShapeDtypeStruct((bsz, H_B, dk, dv), F32),
                   jax.ShapeDtypeStruct((bsz, H_B, dk), F32),
                   jax.ShapeDtypeStruct((bsz, H_B, LANES), F32),
                   jax.ShapeDtypeStruct((bsz, CONV_B - 1, qk_w), F32)],
        scratch_shapes=[pltpu.VMEM((ls, d_b), BF16)],
        compiler_params=_params(2),
        name="mlstm_fused",
    )(*args)


def _trunk(x, state, w, *, nb, ls, tm):
    bsz, seq, d = x.shape
    x2 = x.reshape(bsz * seq, d)
    fresh = state is None
    seq_chunk = min(seq, CHUNK_A)

    ws = w["a_ws"] if seq_chunk == CHUNK_A else jnp.tile(
        w["a_ws"][:, :seq_chunk, :seq_chunk], (1, CHUNK_A // seq_chunk, CHUNK_A // seq_chunk))
    bs = jnp.tile(w["a_bs_t"][:seq_chunk], (CHUNK_A // seq_chunk, 1))
    res = _mixer_a(x2, w["norm_mix_g"][0:1], w["a_w_in"], w["a_ln_g"], w["a_ln_b"], ws, bs,
                   w["a_w_out"], tm=tm, seq_chunk=seq_chunk, emit_v=not fresh)
    x2 = res[0]
    v_rows = None if fresh else res[1]
    x2, fbuf0 = _ffn(x2, w["norm_ffn_g"][0:1], w["f_w_a"][0], w["f_w_g"][0], w["f_conv_w"][0],
                     w["f_conv_b"][0:1], w["f_w_down"][0],
                     None if fresh else state["ffn_conv"][0], None,
                     bsz=bsz, seq=seq, nb=nb, ls=ls)

    if fresh:
        x2, c_new, n_new, m_new, mconv = _mlstm_fused(
            x2, w["norm_mix_g"][1:2], w["b_w_qk"], w["b_w_v"], w["b_w_o"], w["b_w_if"],
            w["b_conv_w"], w["b_conv_b"], w["b_bias_if"], w["b_gn_g"], w["b_w_out"],
            bsz=bsz, seq=seq, ls=ls, k_scale=w["k_scale"])
        m_new = m_new[:, :, 0]
    else:
        q, k, v, o, ig, lf, mconv = _mlstm_proj(
            x2, w["norm_mix_g"][1:2], w["b_w_qk"], w["b_w_v"], w["b_w_o"], w["b_w_i"],
            w["b_w_f"], w["b_conv_w"], w["b_conv_b"], w["b_bias_i"], w["b_bias_f"],
            state["mlstm_conv"], bsz=bsz, seq=seq, nb=nb, ls=ls, k_scale=w["k_scale"])
        n0 = jnp.transpose(state["mlstm_n"], (1, 0, 2))
        m0 = jnp.broadcast_to(jnp.transpose(state["mlstm_m"])[:, :, None], (H_B, bsz, LANES))
        h, c_new, n_new, m_new = _mlstm_cell(q, k, v, ig, lf, state["mlstm_C"], n0, m0, bsz=bsz,
                                             seq=seq, nb=nb, ls=ls)
        n_new = jnp.transpose(n_new, (1, 0, 2))
        m_new = jnp.transpose(m_new[:, :, 0])
        x2 = _mlstm_out(h, o, x2, w["b_gn_g"], w["b_w_out"], tm=tm)
    y2, fbuf1 = _ffn(x2, w["norm_ffn_g"][1:2], w["f_w_a"][1], w["f_w_g"][1], w["f_conv_w"][1],
                     w["f_conv_b"][1:2], w["f_w_down"][1],
                     None if fresh else state["ffn_conv"][1], w["final_norm_g"],
                     bsz=bsz, seq=seq, nb=nb, ls=ls)
    return dict(
        y=y2.reshape(bsz, seq, d),
        v=None if fresh else v_rows.reshape(1, bsz, seq, -1),
        C=c_new[None],
        n=n_new[None],
        m=m_new[None],
        mconv=mconv[None],
        fconv=jnp.stack([fbuf0, fbuf1]),
    )


def kernel(x_prompt, x_sample, state_mlstm_C, state_mlstm_n, state_mlstm_m, state_mlstm_conv, state_ffn_conv, norm_mix_g, norm_ffn_g, final_norm_g, a_w_in, a_ln_g, a_ln_b, a_w_s, a_b_s, a_w_out, b_w_in, b_conv_w, b_conv_b, b_bias_i, b_bias_f, b_gn_g, b_w_out, f_w_up, f_conv_w, f_conv_b, f_w_down):
    d_ff = f_w_down.shape[1]
    qk_w = b_conv_w.shape[2]
    d_b = b_w_out.shape[1]
    dk = qk_w // (2 * H_B)
    d_a = a_w_out.shape[1]
    dg = d_a // G_A

    def pad_gate(cols):
        return jnp.pad(cols, ((0, 0), (0, LANES - H_B)))

    w_in_b = b_w_in[0]
    w = dict(
        norm_mix_g=norm_mix_g, norm_ffn_g=norm_ffn_g, final_norm_g=final_norm_g[None, :],
        a_w_in=a_w_in[0].astype(BF16), a_ln_g=a_ln_g, a_ln_b=a_ln_b,
        a_ws=a_w_s[0].astype(BF16),
        a_bs_t=jnp.repeat(jnp.transpose(a_b_s[0]), dg, axis=1),
        a_w_out=a_w_out[0].astype(BF16),
        b_w_qk=w_in_b[:, :qk_w].astype(BF16),
        b_w_v=w_in_b[:, qk_w:qk_w + d_b].astype(BF16),
        b_w_o=w_in_b[:, qk_w + d_b:qk_w + 2 * d_b].astype(BF16),
        b_w_i=pad_gate(w_in_b[:, qk_w + 2 * d_b:qk_w + 2 * d_b + H_B]).astype(BF16),
        b_w_f=pad_gate(w_in_b[:, qk_w + 2 * d_b + H_B:]).astype(BF16),
        b_conv_w=b_conv_w[0], b_conv_b=b_conv_b,
        b_bias_i=jnp.pad(b_bias_i, ((0, 0), (0, LANES - H_B))),
        b_bias_f=jnp.pad(b_bias_f, ((0, 0), (0, LANES - H_B))),
        b_w_if=jnp.pad(w_in_b[:, qk_w + 2 * d_b:], ((0, 0), (0, LANES - 2 * H_B))).astype(BF16),
        b_bias_if=jnp.pad(jnp.concatenate([b_bias_i, b_bias_f], axis=1),
                          ((0, 0), (0, LANES - 2 * H_B))),
        b_gn_g=b_gn_g, b_w_out=b_w_out[0].astype(BF16),
        f_w_a=f_w_up[:, :, :d_ff].astype(BF16), f_w_g=f_w_up[:, :, d_ff:].astype(BF16),
        f_conv_w=f_conv_w, f_conv_b=f_conv_b, f_w_down=f_w_down.astype(BF16),
        k_scale=float(dk) ** -0.5,
    )
    p = _trunk(x_prompt, None, w, nb=1, ls=256, tm=256)
    state = dict(mlstm_C=state_mlstm_C[0], mlstm_n=state_mlstm_n[0], mlstm_m=state_mlstm_m[0],
                 mlstm_conv=state_mlstm_conv[0], ffn_conv=state_ffn_conv)
    s = _trunk(x_sample, state, w, nb=16, ls=x_sample.shape[1], tm=128)
    return (p["y"], s["y"], p["C"], p["n"], p["m"], p["mconv"], p["fconv"],
            s["v"], s["C"], s["n"], s["m"], s["mconv"], s["fconv"])
```

```python
import functools

import jax
import jax.numpy as jnp
from jax import lax
from jax.experimental import pallas as pl
from jax.experimental.pallas import tpu as pltpu

F32 = jnp.float32
BF16 = jnp.bfloat16
HIGHEST = lax.Precision.HIGHEST

EPS = 1e-6
LANES = 128
SUBLANES = 8
VMEM_LIMIT_BYTES = 56 * 1024 * 1024

CHUNK_A = 128
G_A = 8
H_B = 8
CONV_B = 4
CONV_F = 3


def _const_spec(shape):
    nd = len(shape)
    return pl.BlockSpec(shape, lambda *_: (0,) * nd, pipeline_mode=pl.Buffered(1))


def _params(n_axes):
    return pltpu.CompilerParams(
        dimension_semantics=("arbitrary",) * n_axes,
        vmem_limit_bytes=VMEM_LIMIT_BYTES,
    )


def _rmsnorm(x, g):
    ms = jnp.mean(x * x, axis=-1, keepdims=True)
    return x * lax.rsqrt(ms + EPS) * g


def _expand3(c3, nb, ls):
    w = c3.shape[-1]
    if nb == 1:
        return jnp.broadcast_to(c3.reshape(1, w), (ls, w))
    return jnp.broadcast_to(c3, (nb, ls, w)).reshape(nb * ls, w)


def _expand2(c2, nb, ls):
    return _expand3(c2[:, None, :], nb, ls)


def _causal_conv(a, carry_ref, cw_ref, cb_ref, nb, ls):
    r, c = a.shape
    kw = cw_ref.shape[0]
    rolled = [a] + [pltpu.roll(a, k, 0) for k in range(1, kw)]

    def taps(shifted):
        y = cb_ref[...] + cw_ref[kw - 1:kw, :] * shifted[0]
        for k in range(1, kw):
            y = y + cw_ref[kw - 1 - k:kw - k, :] * shifted[k]
        return y

    grp = SUBLANES if nb == 1 else r
    grp_ls = SUBLANES if nb == 1 else ls
    tpos = lax.broadcasted_iota(jnp.int32, (grp, c), 0) & (grp_ls - 1)
    fixed = [a[0:grp]]
    for k in range(1, kw):
        sh = rolled[k][0:grp]
        for t in range(k):
            idx = kw - 1 + t - k
            prev = _expand3(carry_ref[:, idx:idx + 1, :], nb, grp_ls)
            sh = jnp.where(tpos == t, prev, sh)
        fixed.append(sh)
    y = taps(fixed)
    if nb == 1:
        y = jnp.concatenate([y, taps([s[grp:] for s in rolled])], axis=0)
    last = rolled[kw - 1]
    if nb == 1:
        carry_ref[0] = last[0:kw - 1, :]
    else:
        heads = last.reshape(nb, ls, c)[:, 0:kw - 1, :]
        carry_ref[0:nb - 1] = heads[1:nb]
        carry_ref[nb - 1:nb] = heads[0:1]
    return y


def _mixer_a_kernel(x_ref, g_ref, win_ref, lng_ref, lnb_ref, ws_ref, bs_ref, wout_ref,
                    y_ref, *rest, seq_chunk, emit_v):
    if emit_v:
        v_ref, gate_ref = rest
    else:
        (gate_ref,) = rest
    tm = x_ref.shape[0]
    d_a = lng_ref.shape[1]
    dg = d_a // G_A
    x = x_ref[...]
    h = _rmsnorm(x, g_ref[...]).astype(BF16)
    hh = jax.nn.gelu(jnp.dot(h, win_ref[...], preferred_element_type=F32))
    v = hh[:, d_a:]
    mu = jnp.mean(v, axis=-1, keepdims=True)
    vc = v - mu
    var = jnp.mean(vc * vc, axis=-1, keepdims=True)
    vn = vc * lax.rsqrt(var + EPS) * lng_ref[...] + lnb_ref[...]
    if emit_v:
        v_ref[...] = vn
    vb = vn.astype(BF16)
    t_i = lax.broadcasted_iota(jnp.int32, (CHUNK_A, CHUNK_A), 0)
    s_i = lax.broadcasted_iota(jnp.int32, (CHUNK_A, CHUNK_A), 1)
    mask = (s_i <= t_i) & ((t_i // seq_chunk) == (s_i // seq_chunk))
    for g in range(G_A):
        w = jnp.where(mask, ws_ref[g], jnp.zeros((), BF16))
        cols = slice(g * dg, (g + 1) * dg)
        for c in range(tm // CHUNK_A):
            rows = slice(c * CHUNK_A, (c + 1) * CHUNK_A)
            mixed = jnp.dot(w, vb[rows, cols], preferred_element_type=F32) + bs_ref[:, cols]
            gate_ref[rows, cols] = (hh[rows, cols] * mixed).astype(BF16)
    y_ref[...] = x + jnp.dot(gate_ref[...], wout_ref[...], preferred_element_type=F32)


def _mixer_a(x2, g, w_in, ln_g, ln_b, ws, bs, w_out, *, tm, seq_chunk, emit_v):
    t, d = x2.shape
    d_a = ln_g.shape[1]
    out_shape = [jax.ShapeDtypeStruct((t, d), F32)]
    out_specs = [pl.BlockSpec((tm, d), lambda i: (i, 0))]
    if emit_v:
        out_shape.append(jax.ShapeDtypeStruct((t, d_a), F32))
        out_specs.append(pl.BlockSpec((tm, d_a), lambda i: (i, 0)))
    return pl.pallas_call(
        functools.partial(_mixer_a_kernel, seq_chunk=seq_chunk, emit_v=emit_v),
        grid=(t // tm,),
        in_specs=[
            pl.BlockSpec((tm, d), lambda i: (i, 0)),
            _const_spec(g.shape), _const_spec(w_in.shape), _const_spec(ln_g.shape),
            _const_spec(ln_b.shape), _const_spec(ws.shape), _const_spec(bs.shape),
            _const_spec(w_out.shape),
        ],
        out_specs=out_specs,
        out_shape=out_shape,
        scratch_shapes=[pltpu.VMEM((tm, d_a), BF16)],
        compiler_params=_params(1),
        name="mixer_a",
    )(x2, g, w_in, ln_g, ln_b, ws, bs, w_out)


def _ffn_kernel(*refs, nb, ls, zero_init, final_norm):
    refs = list(refs)
    x_ref, g_ref, wa_ref, wg_ref, cw_ref, cb_ref, wd_ref = refs[:7]
    pos = 7
    buf_ref = None
    if not zero_init:
        buf_ref = refs[pos]
        pos += 1
    fg_ref = None
    if final_norm:
        fg_ref = refs[pos]
        pos += 1
    y_ref, nbuf_ref = refs[pos], refs[pos + 1]

    @pl.when(pl.program_id(1) == 0)
    def _():
        if zero_init:
            nbuf_ref[...] = jnp.zeros(nbuf_ref.shape, F32)
        else:
            nbuf_ref[...] = buf_ref[...]

    x = x_ref[...]
    h = _rmsnorm(x, g_ref[...]).astype(BF16)
    a = jnp.dot(h, wa_ref[...], preferred_element_type=F32)
    gv = jnp.dot(h, wg_ref[...], preferred_element_type=F32)
    a_c = _causal_conv(a, nbuf_ref, cw_ref, cb_ref, nb, ls)
    act = (jax.nn.gelu(a_c) * gv).astype(BF16)
    y = x + jnp.dot(act, wd_ref[...], preferred_element_type=F32)
    if final_norm:
        y = _rmsnorm(y, fg_ref[...])
    y_ref[...] = y


def _ffn(x2, g, w_a, w_g, cw, cb, w_d, buf, final_g, *, bsz, seq, nb, ls):
    t, d = x2.shape
    d_ff = w_a.shape[1]
    nj = seq // ls
    zero_init = buf is None
    final_norm = final_g is not None
    row_spec = pl.BlockSpec((nb * ls, d), lambda i, j: (i * nj + j, 0))
    buf_spec = pl.BlockSpec((nb, CONV_F - 1, d_ff), lambda i, j: (i, 0, 0))
    args = [x2, g, w_a, w_g, cw, cb, w_d]
    in_specs = [row_spec] + [_const_spec(a.shape) for a in args[1:]]
    if not zero_init:
        args.append(buf)
        in_specs.append(buf_spec)
    if final_norm:
        args.append(final_g)
        in_specs.append(_const_spec(final_g.shape))
    return pl.pallas_call(
        functools.partial(_ffn_kernel, nb=nb, ls=ls, zero_init=zero_init, final_norm=final_norm),
        grid=(bsz // nb, nj),
        in_specs=in_specs,
        out_specs=[row_spec, buf_spec],
        out_shape=[jax.ShapeDtypeStruct((t, d), F32),
                   jax.ShapeDtypeStruct((bsz, CONV_F - 1, d_ff), F32)],
        compiler_params=_params(2),
        name="conv_ffn",
    )(*args)


def _log_sigmoid(x):
    return jnp.minimum(x, 0.0) - jnp.log1p(jnp.exp(-jnp.abs(x)))


def _mlstm_proj_kernel(*refs, nb, ls, zero_init, k_scale):
    refs = list(refs)
    (x_ref, g_ref, wqk_ref, wv_ref, wo_ref, wi_ref, wf_ref, cw_ref, cb_ref,
     bi_ref, bf_ref) = refs[:11]
    pos = 11
    buf_ref = None
    if not zero_init:
        buf_ref = refs[pos]
        pos += 1
    q_ref, k_ref, v_ref, o_ref, ig_ref, lf_ref, nbuf_ref = refs[pos:pos + 7]

    @pl.when(pl.program_id(1) == 0)
    def _():
        if zero_init:
            nbuf_ref[...] = jnp.zeros(nbuf_ref.shape, F32)
        else:
            nbuf_ref[...] = buf_ref[...]

    h = _rmsnorm(x_ref[...], g_ref[...]).astype(BF16)
    qk_pre = jnp.dot(h, wqk_ref[...], preferred_element_type=F32)
    v_ref[...] = jnp.dot(h, wv_ref[...], preferred_element_type=F32)
    o_ref[...] = jnp.dot(h, wo_ref[...], preferred_element_type=F32)
    ig_ref[...] = jnp.dot(h, wi_ref[...], preferred_element_type=F32) + bi_ref[...]
    lf_ref[...] = _log_sigmoid(jnp.dot(h, wf_ref[...], preferred_element_type=F32) + bf_ref[...])
    qk = jax.nn.silu(_causal_conv(qk_pre, nbuf_ref, cw_ref, cb_ref, nb, ls))
    dq = q_ref.shape[1]
    q_ref[...] = qk[:, :dq]
    k_ref[...] = qk[:, dq:] * k_scale


def _mlstm_proj(x2, g, w_qk, w_v, w_o, w_i, w_f, cw, cb, b_i, b_f, buf, *, bsz, seq, nb, ls,
                k_scale):
    t, d = x2.shape
    qk_w = w_qk.shape[1]
    d_b = w_v.shape[1]
    nj = seq // ls
    zero_init = buf is None

    def rows(width):
        return pl.BlockSpec((nb * ls, width), lambda i, j: (i * nj + j, 0))

    buf_spec = pl.BlockSpec((nb, CONV_B - 1, qk_w), lambda i, j: (i, 0, 0))
    args = [x2, g, w_qk, w_v, w_o, w_i, w_f, cw, cb, b_i, b_f]
    in_specs = [rows(d)] + [_const_spec(a.shape) for a in args[1:]]
    if not zero_init:
        args.append(buf)
        in_specs.append(buf_spec)
    return pl.pallas_call(
        functools.partial(_mlstm_proj_kernel, nb=nb, ls=ls, zero_init=zero_init, k_scale=k_scale),
        grid=(bsz // nb, nj),
        in_specs=in_specs,
        out_specs=[rows(qk_w // 2), rows(qk_w // 2), rows(d_b), rows(d_b), rows(LANES),
                   rows(LANES), buf_spec],
        out_shape=[jax.ShapeDtypeStruct((t, qk_w // 2), F32),
                   jax.ShapeDtypeStruct((t, qk_w // 2), F32),
                   jax.ShapeDtypeStruct((t, d_b), F32),
                   jax.ShapeDtypeStruct((t, d_b), F32),
                   jax.ShapeDtypeStruct((t, LANES), F32),
                   jax.ShapeDtypeStruct((t, LANES), F32),
                   jax.ShapeDtypeStruct((bsz, CONV_B - 1, qk_w), F32)],
        compiler_params=_params(2),
        name="mlstm_proj",
    )(*args)


def _mlstm_step_kernel(q_ref, k_ref, v_ref, ig_ref, lf_ref, c0_ref, n0_ref, m0_ref,
                       h_ref, c_ref, n_ref, m_ref, *, nb, ls):
    r = nb * ls
    rc = LANES
    dk = c0_ref.shape[2]
    dv = c0_ref.shape[3]

    def pad_rows(a):
        if r == rc:
            return a
        return jnp.concatenate([a, jnp.zeros((rc - r, a.shape[1]), a.dtype)], axis=0)

    r_i = lax.broadcasted_iota(jnp.int32, (r, rc), 0)
    c_i = lax.broadcasted_iota(jnp.int32, (r, rc), 1)
    mask = (c_i <= r_i) & ((r_i // ls) == (c_i // ls))
    ig = ig_ref[...]
    b_all = jnp.dot(mask.astype(F32), pad_rows(lf_ref[...]), precision=HIGHEST,
                    preferred_element_type=F32)
    b_t = pad_rows(b_all).T
    ig_t = pad_rows(ig).T
    m0 = m0_ref[...]
    inter_all = b_all + _expand2(m0, nb, ls)
    lane = lax.broadcasted_iota(jnp.int32, (r, LANES), 1)
    mt_all = jnp.zeros((r, LANES), F32)

    for hd in range(H_B):
        kcols = slice(hd * dk, (hd + 1) * dk)
        vcols = slice(hd * dv, (hd + 1) * dv)
        d = jnp.where(mask, b_all[:, hd:hd + 1] - b_t[hd:hd + 1, :] + ig_t[hd:hd + 1, :],
                      -jnp.inf)
        inter = inter_all[:, hd:hd + 1]
        m_t = jnp.maximum(inter, jnp.max(d, axis=1, keepdims=True))
        mt_all = jnp.where(lane == hd, m_t, mt_all)
        w_intra = jnp.exp(d - m_t)
        w_inter = jnp.exp(inter - m_t)
        q = q_ref[:, kcols]
        qb = q.astype(BF16)
        kb = pad_rows(k_ref[:, kcols].astype(BF16))
        vb = pad_rows(v_ref[:, vcols].astype(BF16))
        s = lax.dot_general(qb, kb, (((1,), (1,)), ((), ())), preferred_element_type=F32)
        sc = s * w_intra
        intra = jnp.dot(sc.astype(BF16), vb, preferred_element_type=F32)
        den_intra = jnp.sum(sc, axis=1, keepdims=True)
        qc = jnp.concatenate(
            [jnp.dot(qb[b * ls:(b + 1) * ls], c0_ref[b, hd].astype(BF16),
                     preferred_element_type=F32) for b in range(nb)], axis=0)
        qn = jnp.sum(q * _expand2(n0_ref[hd], nb, ls), axis=1, keepdims=True)
        num = w_inter * qc + intra
        den = w_inter * qn + den_intra
        h_ref[:, vcols] = num / jnp.maximum(jnp.abs(den), jnp.exp(-m_t))

    p_r = lax.broadcasted_iota(jnp.int32, (nb, rc), 0)
    p_c = lax.broadcasted_iota(jnp.int32, (nb, rc), 1)
    pick = (p_c == p_r * ls + (ls - 1)).astype(F32)
    bl_seq = jnp.dot(pick, pad_rows(b_all), precision=HIGHEST, preferred_element_type=F32)
    mn_seq = jnp.dot(pick, pad_rows(mt_all), precision=HIGHEST, preferred_element_type=F32)
    decay = jnp.exp(bl_seq + m0 - mn_seq)
    m_ref[...] = mn_seq
    g_all = jnp.exp(_expand2(bl_seq, nb, ls) - b_all + ig - _expand2(mn_seq, nb, ls))
    col_seq = lax.broadcasted_iota(jnp.int32, (dk, rc), 1) // ls

    for hd in range(H_B):
        kg = k_ref[:, hd * dk:(hd + 1) * dk] * g_all[:, hd:hd + 1]
        n_ref[hd] = decay[:, hd:hd + 1] * n0_ref[hd] + jnp.sum(kg.reshape(nb, ls, dk), axis=1)
        kg_t = pad_rows(kg).T
        vb = pad_rows(v_ref[:, hd * dv:(hd + 1) * dv].astype(BF16))
        for b in range(nb):
            lhs = jnp.where(col_seq == b, kg_t, 0.0).astype(BF16)
            c_ref[b, hd] = decay[b:b + 1, hd:hd + 1] * c0_ref[b, hd] + jnp.dot(
                lhs, vb, preferred_element_type=F32)


def _mlstm_step(q, k, v, ig, lf, c0, n0, m0, *, bsz, nb, ls):
    t = q.shape[0]
    dk = q.shape[1] // H_B
    dv = v.shape[1] // H_B

    def rows(width):
        return pl.BlockSpec((nb * ls, width), lambda i: (i, 0))

    c_spec = pl.BlockSpec((nb, H_B, dk, dv), lambda i: (i, 0, 0, 0))
    n_spec = pl.BlockSpec((H_B, nb, dk), lambda i: (0, i, 0))
    m_spec = pl.BlockSpec((nb, LANES), lambda i: (i, 0))
    return pl.pallas_call(
        functools.partial(_mlstm_step_kernel, nb=nb, ls=ls),
        grid=(bsz // nb,),
        in_specs=[rows(H_B * dk), rows(H_B * dk), rows(H_B * dv), rows(LANES), rows(LANES),
                  c_spec, n_spec, m_spec],
        out_specs=[rows(H_B * dv), c_spec, n_spec, m_spec],
        out_shape=[jax.ShapeDtypeStruct((t, H_B * dv), F32),
                   jax.ShapeDtypeStruct((bsz, H_B, dk, dv), F32),
                   jax.ShapeDtypeStruct((H_B, bsz, dk), F32),
                   jax.ShapeDtypeStruct((bsz, LANES), F32)],
        compiler_params=_params(1),
        name="mlstm_step",
    )(q, k, v, ig, lf, c0, n0, m0)


def _mlstm_out_kernel(h_ref, o_ref, x_ref, gn_ref, wout_ref, y_ref, gate_ref):
    dv = h_ref.shape[1] // H_B
    for hd in range(H_B):
        cols = slice(hd * dv, (hd + 1) * dv)
        hh = h_ref[:, cols]
        mu = jnp.mean(hh, axis=-1, keepdims=True)
        hc = hh - mu
        var = jnp.mean(hc * hc, axis=-1, keepdims=True)
        hn = hc * lax.rsqrt(var + EPS) * gn_ref[:, cols]
        gate_ref[:, cols] = (jax.nn.sigmoid(o_ref[:, cols]) * hn).astype(BF16)
    y_ref[...] = x_ref[...] + jnp.dot(gate_ref[...], wout_ref[...], preferred_element_type=F32)


def _mlstm_out(h, o, x2, gn_g, w_out, *, tm):
    t, d = x2.shape
    d_b = h.shape[1]
    return pl.pallas_call(
        _mlstm_out_kernel,
        grid=(t // tm,),
        in_specs=[pl.BlockSpec((tm, d_b), lambda i: (i, 0)),
                  pl.BlockSpec((tm, d_b), lambda i: (i, 0)),
                  pl.BlockSpec((tm, d), lambda i: (i, 0)),
                  _const_spec(gn_g.shape), _const_spec(w_out.shape)],
        out_specs=pl.BlockSpec((tm, d), lambda i: (i, 0)),
        out_shape=jax.ShapeDtypeStruct((t, d), F32),
        scratch_shapes=[pltpu.VMEM((tm, d_b), BF16)],
        compiler_params=_params(1),
        name="mlstm_out",
    )(h, o, x2, gn_g, w_out)


def _mlstm_fused_kernel(x_ref, g_ref, wqk_ref, wv_ref, wo_ref, wif_ref, cw_ref, cb_ref, bif_ref,
                        gn_ref, wout_ref, y_ref, c_ref, n_ref, m_ref, nbuf_ref, gate_ref, *,
                        k_scale):
    r = x_ref.shape[0]
    dk = c_ref.shape[2]
    dv = c_ref.shape[3]

    @pl.when(pl.program_id(1) == 0)
    def _():
        c_ref[...] = jnp.zeros(c_ref.shape, F32)
        n_ref[...] = jnp.zeros(n_ref.shape, F32)
        m_ref[...] = jnp.zeros(m_ref.shape, F32)
        nbuf_ref[...] = jnp.zeros(nbuf_ref.shape, F32)

    x = x_ref[...]
    h = _rmsnorm(x, g_ref[...]).astype(BF16)
    qk_pre = jnp.dot(h, wqk_ref[...], preferred_element_type=F32)
    qk = jax.nn.silu(_causal_conv(qk_pre, nbuf_ref, cw_ref, cb_ref, 1, r))
    gates = jnp.dot(h, wif_ref[...], preferred_element_type=F32) + bif_ref[...]
    lf = _log_sigmoid(gates)
    r_i = lax.broadcasted_iota(jnp.int32, (r, r), 0)
    c_i = lax.broadcasted_iota(jnp.int32, (r, r), 1)
    mask = c_i <= r_i
    b_all = jnp.dot(mask.astype(F32), lf, precision=HIGHEST, preferred_element_type=F32)
    b_t = b_all.T
    ig_t = gates.T

    for hd in range(H_B):
        ig_col = gates[:, hd:hd + 1]
        i_row = ig_t[hd:hd + 1, :]
        b_col = b_all[:, H_B + hd:H_B + hd + 1]
        b_row = b_t[H_B + hd:H_B + hd + 1, :]
        m_prev = m_ref[0, hd:hd + 1, 0:1]
        d = jnp.where(mask, b_col - b_row + i_row, -jnp.inf)
        inter = b_col + m_prev
        m_t = jnp.maximum(inter, jnp.max(d, axis=1, keepdims=True))
        w_intra = jnp.exp(d - m_t)
        w_inter = jnp.exp(inter - m_t)

        q = qk[:, hd * dk:(hd + 1) * dk]
        k = qk[:, (H_B + hd) * dk:(H_B + hd + 1) * dk] * k_scale
        vcols = slice(hd * dv, (hd + 1) * dv)
        vb = jnp.dot(h, wv_ref[:, vcols], preferred_element_type=F32).astype(BF16)
        qb = q.astype(BF16)
        s = lax.dot_general(qb, k.astype(BF16), (((1,), (1,)), ((), ())),
                            preferred_element_type=F32)
        sc = s * w_intra
        intra = jnp.dot(sc.astype(BF16), vb, preferred_element_type=F32)
        den_intra = jnp.sum(sc, axis=1, keepdims=True)
        c_old = c_ref[0, hd]
        n_old = n_ref[0, hd:hd + 1, :]
        qc = jnp.dot(qb, c_old.astype(BF16), preferred_element_type=F32)
        qn = jnp.sum(q * n_old, axis=1, keepdims=True)
        num = w_inter * qc + intra
        den = w_inter * qn + den_intra
        hout = num / jnp.maximum(jnp.abs(den), jnp.exp(-m_t))

        mu = jnp.mean(hout, axis=-1, keepdims=True)
        hc = hout - mu
        var = jnp.mean(hc * hc, axis=-1, keepdims=True)
        hn = hc * lax.rsqrt(var + EPS) * gn_ref[:, vcols]
        o = jax.nn.sigmoid(jnp.dot(h, wo_ref[:, vcols], preferred_element_type=F32))
        gate_ref[:, vcols] = (o * hn).astype(BF16)

        b_last = b_col[r - 1:r, :]
        m_new = m_t[r - 1:r, :]
        g = jnp.exp(b_last - b_col + ig_col - m_new)
        kg = k * g
        decay = jnp.exp(b_last + m_prev - m_new)
        c_ref[0, hd] = decay * c_old + lax.dot_general(
            kg.astype(BF16), vb, (((0,), (0,)), ((), ())), preferred_element_type=F32)
        n_ref[0, hd:hd + 1, :] = decay * n_old + jnp.sum(kg, axis=0, keepdims=True)
        m_ref[0, hd:hd + 1, :] = jnp.broadcast_to(m_new, (1, LANES))

    y_ref[...] = x + jnp.dot(gate_ref[...], wout_ref[...], preferred_element_type=F32)


def _mlstm_fused(x2, g, w_qk, w_v, w_o, w_if, cw, cb, b_if, gn_g, w_out, *, bsz, seq, ls, k_scale):
    t, d = x2.shape
    qk_w = w_qk.shape[1]
    d_b = w_v.shape[1]
    dk = qk_w // (2 * H_B)
    dv = d_b // H_B
    nj = seq // ls
    row_spec = pl.BlockSpec((ls, d), lambda i, j: (i * nj + j, 0))
    args = [x2, g, w_qk, w_v, w_o, w_if, cw, cb, b_if, gn_g, w_out]
    return pl.pallas_call(
        functools.partial(_mlstm_fused_kernel, k_scale=k_scale),
        grid=(bsz, nj),
        in_specs=[row_spec] + [_const_spec(a.shape) for a in args[1:]],
        out_specs=[row_spec,
                   pl.BlockSpec((1, H_B, dk, dv), lambda i, j: (i, 0, 0, 0)),
                   pl.BlockSpec((1, H_B, dk), lambda i, j: (i, 0, 0)),
                   pl.BlockSpec((1, H_B, LANES), lambda i, j: (i, 0, 0)),
                   pl.BlockSpec((1, CONV_B - 1, qk_w), lambda i, j: (i, 0, 0))],
        out_shape=[jax.ShapeDtypeStruct((t, d), F32),
                   jax.ShapeDtypeStruct((bsz, H_B, dk, dv), F32),
                   jax.ShapeDtypeStruct((bsz, H_B, dk), F32),
                   jax.ShapeDtypeStruct((bsz, H_B, LANES), F32),
                   jax.ShapeDtypeStruct((bsz, CONV_B - 1, qk_w), F32)],
        scratch_shapes=[pltpu.VMEM((ls, d_b), BF16)],
        compiler_params=_params(2),
        name="mlstm_fused",
    )(*args)


def _trunk(x, state, w, *, nb, ls, tm, nb_step=None):
    bsz, seq, d = x.shape
    x2 = x.reshape(bsz * seq, d)
    fresh = state is None
    seq_chunk = min(seq, CHUNK_A)

    ws = w["a_ws"] if seq_chunk == CHUNK_A else jnp.tile(
        w["a_ws"][:, :seq_chunk, :seq_chunk], (1, CHUNK_A // seq_chunk, CHUNK_A // seq_chunk))
    bs = jnp.tile(w["a_bs_t"][:seq_chunk], (CHUNK_A // seq_chunk, 1))
    res = _mixer_a(x2, w["norm_mix_g"][0:1], w["a_w_in"], w["a_ln_g"], w["a_ln_b"], ws, bs,
                   w["a_w_out"], tm=tm, seq_chunk=seq_chunk, emit_v=not fresh)
    x2 = res[0]
    v_rows = None if fresh else res[1]
    x2, fbuf0 = _ffn(x2, w["norm_ffn_g"][0:1], w["f_w_a"][0], w["f_w_g"][0], w["f_conv_w"][0],
                     w["f_conv_b"][0:1], w["f_w_down"][0],
                     None if fresh else state["ffn_conv"][0], None,
                     bsz=bsz, seq=seq, nb=nb, ls=ls)

    if fresh:
        x2, c_new, n_new, m_new, mconv = _mlstm_fused(
            x2, w["norm_mix_g"][1:2], w["b_w_qk"], w["b_w_v"], w["b_w_o"], w["b_w_if"],
            w["b_conv_w"], w["b_conv_b"], w["b_bias_if"], w["b_gn_g"], w["b_w_out"],
            bsz=bsz, seq=seq, ls=ls, k_scale=w["k_scale"])
        m_new = m_new[:, :, 0]
    else:
        q, k, v, o, ig, lf, mconv = _mlstm_proj(
            x2, w["norm_mix_g"][1:2], w["b_w_qk"], w["b_w_v"], w["b_w_o"], w["b_w_i"],
            w["b_w_f"], w["b_conv_w"], w["b_conv_b"], w["b_bias_i"], w["b_bias_f"],
            state["mlstm_conv"], bsz=bsz, seq=seq, nb=nb, ls=ls, k_scale=w["k_scale"])
        n0 = jnp.transpose(state["mlstm_n"], (1, 0, 2))
        m0 = jnp.pad(state["mlstm_m"], ((0, 0), (0, LANES - H_B)))
        h, c_new, n_new, m_new = _mlstm_step(q, k, v, ig, lf, state["mlstm_C"], n0, m0, bsz=bsz,
                                             nb=nb_step, ls=seq)
        n_new = jnp.transpose(n_new, (1, 0, 2))
        m_new = m_new[:, :H_B]
        x2 = _mlstm_out(h, o, x2, w["b_gn_g"], w["b_w_out"], tm=tm)
    y2, fbuf1 = _ffn(x2, w["norm_ffn_g"][1:2], w["f_w_a"][1], w["f_w_g"][1], w["f_conv_w"][1],
                     w["f_conv_b"][1:2], w["f_w_down"][1],
                     None if fresh else state["ffn_conv"][1], w["final_norm_g"],
                     bsz=bsz, seq=seq, nb=nb, ls=ls)
    return dict(
        y=y2.reshape(bsz, seq, d),
        v=None if fresh else v_rows.reshape(1, bsz, seq, -1),
        C=c_new[None],
        n=n_new[None],
        m=m_new[None],
        mconv=mconv[None],
        fconv=jnp.stack([fbuf0, fbuf1]),
    )


def kernel(x_prompt, x_sample, state_mlstm_C, state_mlstm_n, state_mlstm_m, state_mlstm_conv, state_ffn_conv, norm_mix_g, norm_ffn_g, final_norm_g, a_w_in, a_ln_g, a_ln_b, a_w_s, a_b_s, a_w_out, b_w_in, b_conv_w, b_conv_b, b_bias_i, b_bias_f, b_gn_g, b_w_out, f_w_up, f_conv_w, f_conv_b, f_w_down):
    d_ff = f_w_down.shape[1]
    qk_w = b_conv_w.shape[2]
    d_b = b_w_out.shape[1]
    dk = qk_w // (2 * H_B)
    d_a = a_w_out.shape[1]
    dg = d_a // G_A

    def pad_gate(cols):
        return jnp.pad(cols, ((0, 0), (0, LANES - H_B)))

    w_in_b = b_w_in[0]
    w = dict(
        norm_mix_g=norm_mix_g, norm_ffn_g=norm_ffn_g, final_norm_g=final_norm_g[None, :],
        a_w_in=a_w_in[0].astype(BF16), a_ln_g=a_ln_g, a_ln_b=a_ln_b,
        a_ws=a_w_s[0].astype(BF16),
        a_bs_t=jnp.repeat(jnp.transpose(a_b_s[0]), dg, axis=1),
        a_w_out=a_w_out[0].astype(BF16),
        b_w_qk=w_in_b[:, :qk_w].astype(BF16),
        b_w_v=w_in_b[:, qk_w:qk_w + d_b].astype(BF16),
        b_w_o=w_in_b[:, qk_w + d_b:qk_w + 2 * d_b].astype(BF16),
        b_w_i=pad_gate(w_in_b[:, qk_w + 2 * d_b:qk_w + 2 * d_b + H_B]).astype(BF16),
        b_w_f=pad_gate(w_in_b[:, qk_w + 2 * d_b + H_B:]).astype(BF16),
        b_conv_w=b_conv_w[0], b_conv_b=b_conv_b,
        b_bias_i=jnp.pad(b_bias_i, ((0, 0), (0, LANES - H_B))),
        b_bias_f=jnp.pad(b_bias_f, ((0, 0), (0, LANES - H_B))),
        b_w_if=jnp.pad(w_in_b[:, qk_w + 2 * d_b:], ((0, 0), (0, LANES - 2 * H_B))).astype(BF16),
        b_bias_if=jnp.pad(jnp.concatenate([b_bias_i, b_bias_f], axis=1),
                          ((0, 0), (0, LANES - 2 * H_B))),
        b_gn_g=b_gn_g, b_w_out=b_w_out[0].astype(BF16),
        f_w_a=f_w_up[:, :, :d_ff].astype(BF16), f_w_g=f_w_up[:, :, d_ff:].astype(BF16),
        f_conv_w=f_conv_w, f_conv_b=f_conv_b, f_w_down=f_w_down.astype(BF16),
        k_scale=float(dk) ** -0.5,
    )
    p = _trunk(x_prompt, None, w, nb=1, ls=256, tm=256)
    state = dict(mlstm_C=state_mlstm_C[0], mlstm_n=state_mlstm_n[0], mlstm_m=state_mlstm_m[0],
                 mlstm_conv=state_mlstm_conv[0], ffn_conv=state_ffn_conv)
    s = _trunk(x_sample, state, w, nb=32, ls=x_sample.shape[1], tm=256, nb_step=8)
    return (p["y"], s["y"], p["C"], p["n"], p["m"], p["mconv"], p["fconv"],
            s["v"], s["C"], s["n"], s["m"], s["mconv"], s["fconv"])
```

```python
import functools

import jax
import jax.numpy as jnp
from jax import lax
from jax.experimental import pallas as pl
from jax.experimental.pallas import tpu as pltpu

F32 = jnp.float32
BF16 = jnp.bfloat16
HIGHEST = lax.Precision.HIGHEST

EPS = 1e-6
LANES = 128
SUBLANES = 8
VMEM_LIMIT_BYTES = 56 * 1024 * 1024

CHUNK_A = 128
G_A = 8
H_B = 8
CONV_B = 4
CONV_F = 3


def _const_spec(shape):
    nd = len(shape)
    return pl.BlockSpec(shape, lambda *_: (0,) * nd, pipeline_mode=pl.Buffered(1))


def _params(n_axes):
    return pltpu.CompilerParams(
        dimension_semantics=("arbitrary",) * n_axes,
        vmem_limit_bytes=VMEM_LIMIT_BYTES,
    )


def _rmsnorm(x, g):
    ms = jnp.mean(x * x, axis=-1, keepdims=True)
    return x * lax.rsqrt(ms + EPS) * g


def _expand3(c3, nb, ls):
    w = c3.shape[-1]
    if nb == 1:
        return jnp.broadcast_to(c3.reshape(1, w), (ls, w))
    return jnp.broadcast_to(c3, (nb, ls, w)).reshape(nb * ls, w)


def _expand2(c2, nb, ls):
    return _expand3(c2[:, None, :], nb, ls)


def _causal_conv(a, carry_ref, cw_ref, cb_ref, nb, ls, cols=slice(None)):
    r, c = a.shape
    kw = cw_ref.shape[0]
    rolled = [a] + [pltpu.roll(a, k, 0) for k in range(1, kw)]

    def taps(shifted):
        y = cb_ref[:, cols] + cw_ref[kw - 1:kw, cols] * shifted[0]
        for k in range(1, kw):
            y = y + cw_ref[kw - 1 - k:kw - k, cols] * shifted[k]
        return y

    grp = SUBLANES if nb == 1 else r
    grp_ls = SUBLANES if nb == 1 else ls
    tpos = lax.broadcasted_iota(jnp.int32, (grp, c), 0) & (grp_ls - 1)
    fixed = [a[0:grp]]
    for k in range(1, kw):
        sh = rolled[k][0:grp]
        for t in range(k):
            idx = kw - 1 + t - k
            prev = _expand3(carry_ref[:, idx:idx + 1, cols], nb, grp_ls)
            sh = jnp.where(tpos == t, prev, sh)
        fixed.append(sh)
    y = taps(fixed)
    if nb == 1:
        y = jnp.concatenate([y, taps([s[grp:] for s in rolled])], axis=0)
    last = rolled[kw - 1]
    if nb == 1:
        carry_ref[0, :, cols] = last[0:kw - 1, :]
    else:
        heads = last.reshape(nb, ls, c)[:, 0:kw - 1, :]
        carry_ref[0:nb - 1, :, cols] = heads[1:nb]
        carry_ref[nb - 1:nb, :, cols] = heads[0:1]
    return y


def _mixer_a_kernel(x_ref, g_ref, win_ref, lng_ref, lnb_ref, ws_ref, bs_ref, wout_ref,
                    y_ref, *rest, seq_chunk, emit_v):
    if emit_v:
        v_ref, gate_ref = rest
    else:
        (gate_ref,) = rest
    tm = x_ref.shape[0]
    d_a = lng_ref.shape[1]
    dg = d_a // G_A
    x = x_ref[...]
    h = _rmsnorm(x, g_ref[...]).astype(BF16)
    v = jax.nn.gelu(jnp.dot(h, win_ref[:, d_a:], preferred_element_type=F32))
    mu = jnp.mean(v, axis=-1, keepdims=True)
    vc = v - mu
    var = jnp.mean(vc * vc, axis=-1, keepdims=True)
    vn = vc * lax.rsqrt(var + EPS) * lng_ref[...] + lnb_ref[...]
    if emit_v:
        v_ref[...] = vn
    vb = vn.astype(BF16)
    t_i = lax.broadcasted_iota(jnp.int32, (CHUNK_A, CHUNK_A), 0)
    s_i = lax.broadcasted_iota(jnp.int32, (CHUNK_A, CHUNK_A), 1)
    mask = (s_i <= t_i) & ((t_i // seq_chunk) == (s_i // seq_chunk))
    for g in range(G_A):
        w = jnp.where(mask, ws_ref[g], jnp.zeros((), BF16))
        cols = slice(g * dg, (g + 1) * dg)
        u = jax.nn.gelu(jnp.dot(h, win_ref[:, cols], preferred_element_type=F32))
        for c in range(tm // CHUNK_A):
            rows = slice(c * CHUNK_A, (c + 1) * CHUNK_A)
            mixed = jnp.dot(w, vb[rows, cols], preferred_element_type=F32) + bs_ref[:, cols]
            gate_ref[rows, cols] = (u[rows] * mixed).astype(BF16)
    y_ref[...] = x + jnp.dot(gate_ref[...], wout_ref[...], preferred_element_type=F32)


def _mixer_a(x2, g, w_in, ln_g, ln_b, ws, bs, w_out, *, tm, seq_chunk, emit_v):
    t, d = x2.shape
    d_a = ln_g.shape[1]
    out_shape = [jax.ShapeDtypeStruct((t, d), F32)]
    out_specs = [pl.BlockSpec((tm, d), lambda i: (i, 0))]
    if emit_v:
        out_shape.append(jax.ShapeDtypeStruct((t, d_a), F32))
        out_specs.append(pl.BlockSpec((tm, d_a), lambda i: (i, 0)))
    return pl.pallas_call(
        functools.partial(_mixer_a_kernel, seq_chunk=seq_chunk, emit_v=emit_v),
        grid=(t // tm,),
        in_specs=[
            pl.BlockSpec((tm, d), lambda i: (i, 0)),
            _const_spec(g.shape), _const_spec(w_in.shape), _const_spec(ln_g.shape),
            _const_spec(ln_b.shape), _const_spec(ws.shape), _const_spec(bs.shape),
            _const_spec(w_out.shape),
        ],
        out_specs=out_specs,
        out_shape=out_shape,
        scratch_shapes=[pltpu.VMEM((tm, d_a), BF16)],
        compiler_params=_params(1),
        name="mixer_a",
    )(x2, g, w_in, ln_g, ln_b, ws, bs, w_out)


def _ffn_kernel(*refs, nb, ls, zero_init, final_norm):
    refs = list(refs)
    x_ref, g_ref, wa_ref, wg_ref, cw_ref, cb_ref, wd_ref = refs[:7]
    pos = 7
    buf_ref = None
    if not zero_init:
        buf_ref = refs[pos]
        pos += 1
    fg_ref = None
    if final_norm:
        fg_ref = refs[pos]
        pos += 1
    y_ref, nbuf_ref = refs[pos], refs[pos + 1]

    @pl.when(pl.program_id(1) == 0)
    def _():
        if zero_init:
            nbuf_ref[...] = jnp.zeros(nbuf_ref.shape, F32)
        else:
            nbuf_ref[...] = buf_ref[...]

    x = x_ref[...]
    h = _rmsnorm(x, g_ref[...]).astype(BF16)
    a = jnp.dot(h, wa_ref[...], preferred_element_type=F32)
    gv = jnp.dot(h, wg_ref[...], preferred_element_type=F32)
    a_c = _causal_conv(a, nbuf_ref, cw_ref, cb_ref, nb, ls)
    act = (jax.nn.gelu(a_c) * gv).astype(BF16)
    y = x + jnp.dot(act, wd_ref[...], preferred_element_type=F32)
    if final_norm:
        y = _rmsnorm(y, fg_ref[...])
    y_ref[...] = y


def _ffn(x2, g, w_a, w_g, cw, cb, w_d, buf, final_g, *, bsz, seq, nb, ls):
    t, d = x2.shape
    d_ff = w_a.shape[1]
    nj = seq // ls
    zero_init = buf is None
    final_norm = final_g is not None
    row_spec = pl.BlockSpec((nb * ls, d), lambda i, j: (i * nj + j, 0))
    buf_spec = pl.BlockSpec((nb, CONV_F - 1, d_ff), lambda i, j: (i, 0, 0))
    args = [x2, g, w_a, w_g, cw, cb, w_d]
    in_specs = [row_spec] + [_const_spec(a.shape) for a in args[1:]]
    if not zero_init:
        args.append(buf)
        in_specs.append(buf_spec)
    if final_norm:
        args.append(final_g)
        in_specs.append(_const_spec(final_g.shape))
    return pl.pallas_call(
        functools.partial(_ffn_kernel, nb=nb, ls=ls, zero_init=zero_init, final_norm=final_norm),
        grid=(bsz // nb, nj),
        in_specs=in_specs,
        out_specs=[row_spec, buf_spec],
        out_shape=[jax.ShapeDtypeStruct((t, d), F32),
                   jax.ShapeDtypeStruct((bsz, CONV_F - 1, d_ff), F32)],
        compiler_params=_params(2),
        name="conv_ffn",
    )(*args)


def _log_sigmoid(x):
    return jnp.minimum(x, 0.0) - jnp.log1p(jnp.exp(-jnp.abs(x)))


def _mlstm_proj_kernel(*refs, nb, ls, zero_init, k_scale):
    refs = list(refs)
    (x_ref, g_ref, wqk_ref, wv_ref, wo_ref, wi_ref, wf_ref, cw_ref, cb_ref,
     bi_ref, bf_ref) = refs[:11]
    pos = 11
    buf_ref = None
    if not zero_init:
        buf_ref = refs[pos]
        pos += 1
    q_ref, k_ref, v_ref, o_ref, ig_ref, lf_ref, nbuf_ref = refs[pos:pos + 7]

    @pl.when(pl.program_id(1) == 0)
    def _():
        if zero_init:
            nbuf_ref[...] = jnp.zeros(nbuf_ref.shape, F32)
        else:
            nbuf_ref[...] = buf_ref[...]

    h = _rmsnorm(x_ref[...], g_ref[...]).astype(BF16)
    qk_pre = jnp.dot(h, wqk_ref[...], preferred_element_type=F32)
    v_ref[...] = jnp.dot(h, wv_ref[...], preferred_element_type=F32)
    o_ref[...] = jnp.dot(h, wo_ref[...], preferred_element_type=F32)
    ig_ref[...] = jnp.dot(h, wi_ref[...], preferred_element_type=F32) + bi_ref[...]
    lf_ref[...] = _log_sigmoid(jnp.dot(h, wf_ref[...], preferred_element_type=F32) + bf_ref[...])
    qk = jax.nn.silu(_causal_conv(qk_pre, nbuf_ref, cw_ref, cb_ref, nb, ls))
    dq = q_ref.shape[1]
    q_ref[...] = qk[:, :dq]
    k_ref[...] = qk[:, dq:] * k_scale


def _mlstm_proj(x2, g, w_qk, w_v, w_o, w_i, w_f, cw, cb, b_i, b_f, buf, *, bsz, seq, nb, ls,
                k_scale):
    t, d = x2.shape
    qk_w = w_qk.shape[1]
    d_b = w_v.shape[1]
    nj = seq // ls
    zero_init = buf is None

    def rows(width):
        return pl.BlockSpec((nb * ls, width), lambda i, j: (i * nj + j, 0))

    buf_spec = pl.BlockSpec((nb, CONV_B - 1, qk_w), lambda i, j: (i, 0, 0))
    args = [x2, g, w_qk, w_v, w_o, w_i, w_f, cw, cb, b_i, b_f]
    in_specs = [rows(d)] + [_const_spec(a.shape) for a in args[1:]]
    if not zero_init:
        args.append(buf)
        in_specs.append(buf_spec)
    return pl.pallas_call(
        functools.partial(_mlstm_proj_kernel, nb=nb, ls=ls, zero_init=zero_init, k_scale=k_scale),
        grid=(bsz // nb, nj),
        in_specs=in_specs,
        out_specs=[rows(qk_w // 2), rows(qk_w // 2), rows(d_b), rows(d_b), rows(LANES),
                   rows(LANES), buf_spec],
        out_shape=[jax.ShapeDtypeStruct((t, qk_w // 2), F32),
                   jax.ShapeDtypeStruct((t, qk_w // 2), F32),
                   jax.ShapeDtypeStruct((t, d_b), F32),
                   jax.ShapeDtypeStruct((t, d_b), F32),
                   jax.ShapeDtypeStruct((t, LANES), F32),
                   jax.ShapeDtypeStruct((t, LANES), F32),
                   jax.ShapeDtypeStruct((bsz, CONV_B - 1, qk_w), F32)],
        compiler_params=_params(2),
        name="mlstm_proj",
    )(*args)


def _mlstm_step_kernel(q_ref, k_ref, v_ref, ig_ref, lf_ref, c0_ref, n0_ref, m0_ref,
                       h_ref, c_ref, n_ref, m_ref, *, nb, ls):
    r = nb * ls
    rc = LANES
    dk = c0_ref.shape[2]
    dv = c0_ref.shape[3]

    def pad_rows(a):
        if r == rc:
            return a
        return jnp.concatenate([a, jnp.zeros((rc - r, a.shape[1]), a.dtype)], axis=0)

    r_i = lax.broadcasted_iota(jnp.int32, (r, rc), 0)
    c_i = lax.broadcasted_iota(jnp.int32, (r, rc), 1)
    mask = (c_i <= r_i) & ((r_i // ls) == (c_i // ls))
    ig = ig_ref[...]
    b_all = jnp.dot(mask.astype(F32), pad_rows(lf_ref[...]), precision=HIGHEST,
                    preferred_element_type=F32)
    b_t = pad_rows(b_all).T
    ig_t = pad_rows(ig).T
    m0 = m0_ref[...]
    inter_all = b_all + _expand2(m0, nb, ls)
    lane = lax.broadcasted_iota(jnp.int32, (r, LANES), 1)
    mt_all = jnp.zeros((r, LANES), F32)

    for hd in range(H_B):
        kcols = slice(hd * dk, (hd + 1) * dk)
        vcols = slice(hd * dv, (hd + 1) * dv)
        d = jnp.where(mask, b_all[:, hd:hd + 1] - b_t[hd:hd + 1, :] + ig_t[hd:hd + 1, :],
                      -jnp.inf)
        inter = inter_all[:, hd:hd + 1]
        m_t = jnp.maximum(inter, jnp.max(d, axis=1, keepdims=True))
        mt_all = jnp.where(lane == hd, m_t, mt_all)
        w_intra = jnp.exp(d - m_t)
        w_inter = jnp.exp(inter - m_t)
        q = q_ref[:, kcols]
        qb = q.astype(BF16)
        kb = pad_rows(k_ref[:, kcols].astype(BF16))
        vb = pad_rows(v_ref[:, vcols].astype(BF16))
        s = lax.dot_general(qb, kb, (((1,), (1,)), ((), ())), preferred_element_type=F32)
        sc = s * w_intra
        intra = jnp.dot(sc.astype(BF16), vb, preferred_element_type=F32)
        den_intra = jnp.sum(sc, axis=1, keepdims=True)
        qc = jnp.concatenate(
            [jnp.dot(qb[b * ls:(b + 1) * ls], c0_ref[b, hd].astype(BF16),
                     preferred_element_type=F32) for b in range(nb)], axis=0)
        qn = jnp.sum(q * _expand2(n0_ref[hd], nb, ls), axis=1, keepdims=True)
        num = w_inter * qc + intra
        den = w_inter * qn + den_intra
        h_ref[:, vcols] = num / jnp.maximum(jnp.abs(den), jnp.exp(-m_t))

    p_r = lax.broadcasted_iota(jnp.int32, (nb, rc), 0)
    p_c = lax.broadcasted_iota(jnp.int32, (nb, rc), 1)
    pick = (p_c == p_r * ls + (ls - 1)).astype(F32)
    bl_seq = jnp.dot(pick, pad_rows(b_all), precision=HIGHEST, preferred_element_type=F32)
    mn_seq = jnp.dot(pick, pad_rows(mt_all), precision=HIGHEST, preferred_element_type=F32)
    decay = jnp.exp(bl_seq + m0 - mn_seq)
    m_ref[...] = mn_seq
    g_all = jnp.exp(_expand2(bl_seq, nb, ls) - b_all + ig - _expand2(mn_seq, nb, ls))
    col_seq = lax.broadcasted_iota(jnp.int32, (dk, rc), 1) // ls

    for hd in range(H_B):
        kg = k_ref[:, hd * dk:(hd + 1) * dk] * g_all[:, hd:hd + 1]
        n_ref[hd] = decay[:, hd:hd + 1] * n0_ref[hd] + jnp.sum(kg.reshape(nb, ls, dk), axis=1)
        kg_t = pad_rows(kg).T
        vb = pad_rows(v_ref[:, hd * dv:(hd + 1) * dv].astype(BF16))
        for b in range(nb):
            lhs = jnp.where(col_seq == b, kg_t, 0.0).astype(BF16)
            c_ref[b, hd] = decay[b:b + 1, hd:hd + 1] * c0_ref[b, hd] + jnp.dot(
                lhs, vb, preferred_element_type=F32)


def _mlstm_step(q, k, v, ig, lf, c0, n0, m0, *, bsz, nb, ls):
    t = q.shape[0]
    dk = q.shape[1] // H_B
    dv = v.shape[1] // H_B

    def rows(width):
        return pl.BlockSpec((nb * ls, width), lambda i: (i, 0))

    c_spec = pl.BlockSpec((nb, H_B, dk, dv), lambda i: (i, 0, 0, 0))
    n_spec = pl.BlockSpec((H_B, nb, dk), lambda i: (0, i, 0))
    m_spec = pl.BlockSpec((nb, LANES), lambda i: (i, 0))
    return pl.pallas_call(
        functools.partial(_mlstm_step_kernel, nb=nb, ls=ls),
        grid=(bsz // nb,),
        in_specs=[rows(H_B * dk), rows(H_B * dk), rows(H_B * dv), rows(LANES), rows(LANES),
                  c_spec, n_spec, m_spec],
        out_specs=[rows(H_B * dv), c_spec, n_spec, m_spec],
        out_shape=[jax.ShapeDtypeStruct((t, H_B * dv), F32),
                   jax.ShapeDtypeStruct((bsz, H_B, dk, dv), F32),
                   jax.ShapeDtypeStruct((H_B, bsz, dk), F32),
                   jax.ShapeDtypeStruct((bsz, LANES), F32)],
        compiler_params=_params(1),
        name="mlstm_step",
    )(q, k, v, ig, lf, c0, n0, m0)


def _mlstm_out_kernel(h_ref, o_ref, x_ref, gn_ref, wout_ref, y_ref, gate_ref):
    dv = h_ref.shape[1] // H_B
    for hd in range(H_B):
        cols = slice(hd * dv, (hd + 1) * dv)
        hh = h_ref[:, cols]
        mu = jnp.mean(hh, axis=-1, keepdims=True)
        hc = hh - mu
        var = jnp.mean(hc * hc, axis=-1, keepdims=True)
        hn = hc * lax.rsqrt(var + EPS) * gn_ref[:, cols]
        gate_ref[:, cols] = (jax.nn.sigmoid(o_ref[:, cols]) * hn).astype(BF16)
    y_ref[...] = x_ref[...] + jnp.dot(gate_ref[...], wout_ref[...], preferred_element_type=F32)


def _mlstm_out(h, o, x2, gn_g, w_out, *, tm):
    t, d = x2.shape
    d_b = h.shape[1]
    return pl.pallas_call(
        _mlstm_out_kernel,
        grid=(t // tm,),
        in_specs=[pl.BlockSpec((tm, d_b), lambda i: (i, 0)),
                  pl.BlockSpec((tm, d_b), lambda i: (i, 0)),
                  pl.BlockSpec((tm, d), lambda i: (i, 0)),
                  _const_spec(gn_g.shape), _const_spec(w_out.shape)],
        out_specs=pl.BlockSpec((tm, d), lambda i: (i, 0)),
        out_shape=jax.ShapeDtypeStruct((t, d), F32),
        scratch_shapes=[pltpu.VMEM((tm, d_b), BF16)],
        compiler_params=_params(1),
        name="mlstm_out",
    )(h, o, x2, gn_g, w_out)


def _mlstm_fused_kernel(x_ref, g_ref, wqk_ref, wv_ref, wo_ref, wif_ref, cw_ref, cb_ref, bif_ref,
                        gn_ref, wout_ref, y_ref, c_ref, n_ref, m_ref, nbuf_ref, *, k_scale):
    @pl.when(pl.program_id(1) == 0)
    def _():
        c_ref[...] = jnp.zeros(c_ref.shape, F32)
        n_ref[...] = jnp.zeros(n_ref.shape, F32)
        m_ref[...] = jnp.zeros(m_ref.shape, F32)
        nbuf_ref[...] = jnp.zeros(nbuf_ref.shape, F32)

    for s in range(x_ref.shape[0]):
        _mlstm_fused_seq(x_ref.at[s], g_ref, wqk_ref, wv_ref, wo_ref, wif_ref, cw_ref, cb_ref,
                         bif_ref, gn_ref, wout_ref, y_ref.at[s], c_ref.at[s], n_ref.at[s],
                         m_ref.at[s], nbuf_ref.at[s], k_scale)


def _mlstm_fused_seq(x_ref, g_ref, wqk_ref, wv_ref, wo_ref, wif_ref, cw_ref, cb_ref, bif_ref,
                     gn_ref, wout_ref, y_ref, c_ref, n_ref, m_ref, nbuf_ref, k_scale):
    r = x_ref.shape[0]
    dk = c_ref.shape[2]
    dv = c_ref.shape[3]

    x = x_ref[...]
    h = _rmsnorm(x, g_ref[...]).astype(BF16)
    gates = jnp.dot(h, wif_ref[...], preferred_element_type=F32) + bif_ref[...]
    lf = _log_sigmoid(gates)
    r_i = lax.broadcasted_iota(jnp.int32, (r, r), 0)
    c_i = lax.broadcasted_iota(jnp.int32, (r, r), 1)
    mask = c_i <= r_i
    b_all = jnp.dot(mask.astype(F32), lf, precision=HIGHEST, preferred_element_type=F32)
    b_t = b_all.T
    ig_t = gates.T

    pair = 2 * dk
    n_pairs = H_B // 2
    q_chunks = {}
    k_chunks = {}
    gated = {}

    def qk_chunk(c):
        cols = slice(c * pair, (c + 1) * pair)
        pre = jnp.dot(h, wqk_ref[:, cols], preferred_element_type=F32)
        return jax.nn.silu(_causal_conv(pre, nbuf_ref, cw_ref, cb_ref, 1, r, cols))

    def head_stages(hd):
        p, half = divmod(hd, 2)
        ig_col = gates[:, hd:hd + 1]
        i_row = ig_t[hd:hd + 1, :]
        b_col = b_all[:, H_B + hd:H_B + hd + 1]
        b_row = b_t[H_B + hd:H_B + hd + 1, :]
        m_prev = m_ref[0, hd:hd + 1, 0:1]
        vcols = slice(hd * dv, (hd + 1) * dv)
        vb = jnp.dot(h, wv_ref[:, vcols], preferred_element_type=F32).astype(BF16)
        o_pre = jnp.dot(h, wo_ref[:, vcols], preferred_element_type=F32)
        d = jnp.where(mask, b_col - b_row + i_row, -jnp.inf)
        yield
        inter = b_col + m_prev
        m_t = jnp.maximum(inter, jnp.max(d, axis=1, keepdims=True))
        w_intra = jnp.exp(d - m_t)
        w_inter = jnp.exp(inter - m_t)
        yield
        q = q_chunks[p][:, half * dk:(half + 1) * dk]
        k = k_chunks[p][:, half * dk:(half + 1) * dk] * k_scale
        qb = q.astype(BF16)
        s = lax.dot_general(qb, k.astype(BF16), (((1,), (1,)), ((), ())),
                            preferred_element_type=F32)
        sc = s * w_intra
        intra = jnp.dot(sc.astype(BF16), vb, preferred_element_type=F32)
        den_intra = jnp.sum(sc, axis=1, keepdims=True)
        c_old = c_ref[0, hd]
        n_old = n_ref[0, hd:hd + 1, :]
        qc = jnp.dot(qb, c_old.astype(BF16), preferred_element_type=F32)
        qn = jnp.sum(q * n_old, axis=1, keepdims=True)
        yield
        num = w_inter * qc + intra
        den = w_inter * qn + den_intra
        hout = num / jnp.maximum(jnp.abs(den), jnp.exp(-m_t))
        mu = jnp.mean(hout, axis=-1, keepdims=True)
        hc = hout - mu
        var = jnp.mean(hc * hc, axis=-1, keepdims=True)
        hn = hc * lax.rsqrt(var + EPS) * gn_ref[:, vcols]
        yield
        gated[hd] = (jax.nn.sigmoid(o_pre) * hn).astype(BF16)
        b_last = b_col[r - 1:r, :]
        m_new = m_t[r - 1:r, :]
        g = jnp.exp(b_last - b_col + ig_col - m_new)
        g_row = jnp.exp(b_last - b_row + i_row - m_new)
        decay = jnp.exp(b_last + m_prev - m_new)
        yield
        kg_t = (k.T * g_row).astype(BF16)
        c_ref[0, hd] = decay * c_old + jnp.dot(kg_t, vb, preferred_element_type=F32)
        n_ref[0, hd:hd + 1, :] = decay * n_old + jnp.sum(k * g, axis=0, keepdims=True)
        m_ref[0, hd:hd + 1, :] = jnp.broadcast_to(m_new, (1, LANES))
        yield

    n_stages = 6
    gate_stage = 4
    q_chunks[0] = qk_chunk(0)
    k_chunks[0] = qk_chunk(n_pairs)
    heads = [head_stages(hd) for hd in range(H_B)]
    y = x
    for t in range(H_B + n_stages - 1):
        if t % 2 == 0 and t // 2 + 1 < n_pairs:
            q_chunks[t // 2 + 1] = qk_chunk(t // 2 + 1)
            k_chunks[t // 2 + 1] = qk_chunk(n_pairs + t // 2 + 1)
        for hd in range(H_B):
            if hd <= t < hd + n_stages:
                next(heads[hd])
        p, odd = divmod(t - gate_stage, 2)
        if odd and 0 <= p < n_pairs:
            lhs = jnp.concatenate([gated[2 * p], gated[2 * p + 1]], axis=1)
            y = y + jnp.dot(lhs, wout_ref[2 * p * dv:(2 * p + 2) * dv, :],
                            preferred_element_type=F32)
    y_ref[...] = y


def _mlstm_fused(x2, g, w_qk, w_v, w_o, w_if, cw, cb, b_if, gn_g, w_out, *, bsz, seq, ls, k_scale,
                 n_streams):
    t, d = x2.shape
    qk_w = w_qk.shape[1]
    d_b = w_v.shape[1]
    dk = qk_w // (2 * H_B)
    dv = d_b // H_B
    nj = seq // ls
    ns = n_streams
    bpg = bsz // ns
    row_spec = pl.BlockSpec((ns, ls, d), lambda i, j: (0, i * nj + j, 0))

    def state_spec(*tail):
        return pl.BlockSpec((ns, 1) + tail, lambda i, j: (0, i) + (0,) * len(tail))

    args = [x2.reshape(ns, t // ns, d), g, w_qk, w_v, w_o, w_if, cw, cb, b_if, gn_g, w_out]
    y, c, n, m, nbuf = pl.pallas_call(
        functools.partial(_mlstm_fused_kernel, k_scale=k_scale),
        grid=(bpg, nj),
        in_specs=[row_spec] + [_const_spec(a.shape) for a in args[1:]],
        out_specs=[row_spec, state_spec(H_B, dk, dv), state_spec(H_B, dk),
                   state_spec(H_B, LANES), state_spec(CONV_B - 1, qk_w)],
        out_shape=[jax.ShapeDtypeStruct((ns, t // ns, d), F32),
                   jax.ShapeDtypeStruct((ns, bpg, H_B, dk, dv), F32),
                   jax.ShapeDtypeStruct((ns, bpg, H_B, dk), F32),
                   jax.ShapeDtypeStruct((ns, bpg, H_B, LANES), F32),
                   jax.ShapeDtypeStruct((ns, bpg, CONV_B - 1, qk_w), F32)],
        compiler_params=_params(2),
        name="mlstm_fused",
    )(*args)
    return (y.reshape(t, d), c.reshape(bsz, H_B, dk, dv), n.reshape(bsz, H_B, dk),
            m.reshape(bsz, H_B, LANES), nbuf.reshape(bsz, CONV_B - 1, qk_w))


def _trunk(x, state, w, *, nb, ls, tm, nb_step=None):
    bsz, seq, d = x.shape
    x2 = x.reshape(bsz * seq, d)
    fresh = state is None
    seq_chunk = min(seq, CHUNK_A)

    ws = w["a_ws"] if seq_chunk == CHUNK_A else jnp.tile(
        w["a_ws"][:, :seq_chunk, :seq_chunk], (1, CHUNK_A // seq_chunk, CHUNK_A // seq_chunk))
    bs = jnp.tile(w["a_bs_t"][:seq_chunk], (CHUNK_A // seq_chunk, 1))
    res = _mixer_a(x2, w["norm_mix_g"][0:1], w["a_w_in"], w["a_ln_g"], w["a_ln_b"], ws, bs,
                   w["a_w_out"], tm=tm, seq_chunk=seq_chunk, emit_v=not fresh)
    x2 = res[0]
    v_rows = None if fresh else res[1]
    x2, fbuf0 = _ffn(x2, w["norm_ffn_g"][0:1], w["f_w_a"][0], w["f_w_g"][0], w["f_conv_w"][0],
                     w["f_conv_b"][0:1], w["f_w_down"][0],
                     None if fresh else state["ffn_conv"][0], None,
                     bsz=bsz, seq=seq, nb=nb, ls=ls)

    if fresh:
        x2, c_new, n_new, m_new, mconv = _mlstm_fused(
            x2, w["norm_mix_g"][1:2], w["b_w_qk"], w["b_w_v"], w["b_w_o"], w["b_w_if"],
            w["b_conv_w"], w["b_conv_b"], w["b_bias_if"], w["b_gn_g"], w["b_w_out"],
            bsz=bsz, seq=seq, ls=ls, k_scale=w["k_scale"], n_streams=1)
        m_new = m_new[:, :, 0]
    else:
        q, k, v, o, ig, lf, mconv = _mlstm_proj(
            x2, w["norm_mix_g"][1:2], w["b_w_qk"], w["b_w_v"], w["b_w_o"], w["b_w_i"],
            w["b_w_f"], w["b_conv_w"], w["b_conv_b"], w["b_bias_i"], w["b_bias_f"],
            state["mlstm_conv"], bsz=bsz, seq=seq, nb=nb, ls=ls, k_scale=w["k_scale"])
        n0 = jnp.transpose(state["mlstm_n"], (1, 0, 2))
        m0 = jnp.pad(state["mlstm_m"], ((0, 0), (0, LANES - H_B)))
        h, c_new, n_new, m_new = _mlstm_step(q, k, v, ig, lf, state["mlstm_C"], n0, m0, bsz=bsz,
                                             nb=nb_step, ls=seq)
        n_new = jnp.transpose(n_new, (1, 0, 2))
        m_new = m_new[:, :H_B]
        x2 = _mlstm_out(h, o, x2, w["b_gn_g"], w["b_w_out"], tm=tm)
    y2, fbuf1 = _ffn(x2, w["norm_ffn_g"][1:2], w["f_w_a"][1], w["f_w_g"][1], w["f_conv_w"][1],
                     w["f_conv_b"][1:2], w["f_w_down"][1],
                     None if fresh else state["ffn_conv"][1], w["final_norm_g"],
                     bsz=bsz, seq=seq, nb=nb, ls=ls)
    return dict(
        y=y2.reshape(bsz, seq, d),
        v=None if fresh else v_rows.reshape(1, bsz, seq, -1),
        C=c_new[None],
        n=n_new[None],
        m=m_new[None],
        mconv=mconv[None],
        fconv=jnp.stack([fbuf0, fbuf1]),
    )


def kernel(x_prompt, x_sample, state_mlstm_C, state_mlstm_n, state_mlstm_m, state_mlstm_conv, state_ffn_conv, norm_mix_g, norm_ffn_g, final_norm_g, a_w_in, a_ln_g, a_ln_b, a_w_s, a_b_s, a_w_out, b_w_in, b_conv_w, b_conv_b, b_bias_i, b_bias_f, b_gn_g, b_w_out, f_w_up, f_conv_w, f_conv_b, f_w_down):
    d_ff = f_w_down.shape[1]
    qk_w = b_conv_w.shape[2]
    d_b = b_w_out.shape[1]
    dk = qk_w // (2 * H_B)
    d_a = a_w_out.shape[1]
    dg = d_a // G_A

    def pad_gate(cols):
        return jnp.pad(cols, ((0, 0), (0, LANES - H_B)))

    w_in_b = b_w_in[0]
    w = dict(
        norm_mix_g=norm_mix_g, norm_ffn_g=norm_ffn_g, final_norm_g=final_norm_g[None, :],
        a_w_in=a_w_in[0].astype(BF16), a_ln_g=a_ln_g, a_ln_b=a_ln_b,
        a_ws=a_w_s[0].astype(BF16),
        a_bs_t=jnp.repeat(jnp.transpose(a_b_s[0]), dg, axis=1),
        a_w_out=a_w_out[0].astype(BF16),
        b_w_qk=w_in_b[:, :qk_w].astype(BF16),
        b_w_v=w_in_b[:, qk_w:qk_w + d_b].astype(BF16),
        b_w_o=w_in_b[:, qk_w + d_b:qk_w + 2 * d_b].astype(BF16),
        b_w_i=pad_gate(w_in_b[:, qk_w + 2 * d_b:qk_w + 2 * d_b + H_B]).astype(BF16),
        b_w_f=pad_gate(w_in_b[:, qk_w + 2 * d_b + H_B:]).astype(BF16),
        b_conv_w=b_conv_w[0], b_conv_b=b_conv_b,
        b_bias_i=jnp.pad(b_bias_i, ((0, 0), (0, LANES - H_B))),
        b_bias_f=jnp.pad(b_bias_f, ((0, 0), (0, LANES - H_B))),
        b_w_if=jnp.pad(w_in_b[:, qk_w + 2 * d_b:], ((0, 0), (0, LANES - 2 * H_B))).astype(BF16),
        b_bias_if=jnp.pad(jnp.concatenate([b_bias_i, b_bias_f], axis=1),
                          ((0, 0), (0, LANES - 2 * H_B))),
        b_gn_g=b_gn_g, b_w_out=b_w_out[0].astype(BF16),
        f_w_a=f_w_up[:, :, :d_ff].astype(BF16), f_w_g=f_w_up[:, :, d_ff:].astype(BF16),
        f_conv_w=f_conv_w, f_conv_b=f_conv_b, f_w_down=f_w_down.astype(BF16),
        k_scale=float(dk) ** -0.5,
    )
    p = _trunk(x_prompt, None, w, nb=1, ls=256, tm=256)
    state = dict(mlstm_C=state_mlstm_C[0], mlstm_n=state_mlstm_n[0], mlstm_m=state_mlstm_m[0],
                 mlstm_conv=state_mlstm_conv[0], ffn_conv=state_ffn_conv)
    s = _trunk(x_sample, state, w, nb=32, ls=x_sample.shape[1], tm=256, nb_step=8)
    return (p["y"], s["y"], p["C"], p["n"], p["m"], p["mconv"], p["fconv"],
            s["v"], s["C"], s["n"], s["m"], s["mconv"], s["fconv"])
```

```python
import functools

import jax
import jax.numpy as jnp
from jax import lax
from jax.experimental import pallas as pl
from jax.experimental.pallas import tpu as pltpu

F32 = jnp.float32
BF16 = jnp.bfloat16
HIGHEST = lax.Precision.HIGHEST

EPS = 1e-6
LANES = 128
SUBLANES = 8
VMEM_LIMIT_BYTES = 56 * 1024 * 1024

CHUNK_A = 128
G_A = 8
H_B = 8
CONV_B = 4
CONV_F = 3


def _const_spec(shape):
    nd = len(shape)
    return pl.BlockSpec(shape, lambda *_: (0,) * nd, pipeline_mode=pl.Buffered(1))


def _layer_spec(shape, layer):
    return pl.BlockSpec((1,) + tuple(shape[1:]), lambda *_: (layer, 0, 0),
                        pipeline_mode=pl.Buffered(1))


def _params(n_axes):
    return pltpu.CompilerParams(
        dimension_semantics=("arbitrary",) * n_axes,
        vmem_limit_bytes=VMEM_LIMIT_BYTES,
    )


def _rmsnorm(x, g):
    ms = jnp.mean(x * x, axis=-1, keepdims=True)
    return x * lax.rsqrt(ms + EPS) * g


def _expand3(c3, nb, ls):
    w = c3.shape[-1]
    if nb == 1:
        return jnp.broadcast_to(c3.reshape(1, w), (ls, w))
    return jnp.broadcast_to(c3, (nb, ls, w)).reshape(nb * ls, w)


def _expand2(c2, nb, ls):
    return _expand3(c2[:, None, :], nb, ls)


def _causal_conv(a, carry_ref, cw_ref, cb_ref, nb, ls, cols=slice(None)):
    r, c = a.shape
    kw = cw_ref.shape[0]
    rolled = [a] + [pltpu.roll(a, k, 0) for k in range(1, kw)]

    def taps(shifted):
        y = cb_ref[:, cols] + cw_ref[kw - 1:kw, cols] * shifted[0]
        for k in range(1, kw):
            y = y + cw_ref[kw - 1 - k:kw - k, cols] * shifted[k]
        return y

    grp = SUBLANES if nb == 1 else r
    grp_ls = SUBLANES if nb == 1 else ls
    tpos = lax.broadcasted_iota(jnp.int32, (grp, c), 0) & (grp_ls - 1)
    fixed = [a[0:grp]]
    for k in range(1, kw):
        sh = rolled[k][0:grp]
        for t in range(k):
            idx = kw - 1 + t - k
            prev = _expand3(carry_ref[:, idx:idx + 1, cols], nb, grp_ls)
            sh = jnp.where(tpos == t, prev, sh)
        fixed.append(sh)
    y = taps(fixed)
    if nb == 1:
        y = jnp.concatenate([y, taps([s[grp:] for s in rolled])], axis=0)
    last = rolled[kw - 1]
    if nb == 1:
        carry_ref[0, :, cols] = last[0:kw - 1, :]
    else:
        heads = last.reshape(nb, ls, c)[:, 0:kw - 1, :]
        carry_ref[0:nb - 1, :, cols] = heads[1:nb]
        carry_ref[nb - 1:nb, :, cols] = heads[0:1]
    return y


def _mixer_a_kernel(x_ref, g_ref, win_ref, lng_ref, lnb_ref, ws_ref, bs_ref, wout_ref,
                    y_ref, *rest, seq_chunk, emit_v):
    if emit_v:
        (v_ref,) = rest
    tm = x_ref.shape[0]
    d_a = lng_ref.shape[1]
    dg = d_a // G_A
    x = x_ref[...]
    h = _rmsnorm(x, g_ref[...]).astype(BF16)
    v_pre = {}
    v_chunks = []
    for g in range(G_A + 1):
        if g < G_A:
            v_pre[g] = jnp.dot(h, win_ref[:, d_a + g * dg:d_a + (g + 1) * dg],
                               preferred_element_type=F32)
        if g > 0:
            v_chunks.append(jax.nn.gelu(v_pre.pop(g - 1)))
    t_i = lax.broadcasted_iota(jnp.int32, (CHUNK_A, CHUNK_A), 0)
    s_i = lax.broadcasted_iota(jnp.int32, (CHUNK_A, CHUNK_A), 1)
    mask = (s_i <= t_i) & ((t_i // seq_chunk) == (s_i // seq_chunk))
    stats = {}
    gated = {}

    def group_stages(g):
        cols = slice(g * dg, (g + 1) * dg)
        u_pre = jnp.dot(h, win_ref[:, cols], preferred_element_type=F32)
        yield
        vn = ((v_chunks[g] - stats["mu"]) * stats["rs"]) * lng_ref[:, cols] + lnb_ref[:, cols]
        if emit_v:
            v_ref[:, cols] = vn
        vb = vn.astype(BF16)
        w = jnp.where(mask, ws_ref[g], jnp.zeros((), BF16))
        mixed = [jnp.dot(w, vb[c * CHUNK_A:(c + 1) * CHUNK_A], preferred_element_type=F32)
                 + bs_ref[:, cols] for c in range(tm // CHUNK_A)]
        yield
        u = jax.nn.gelu(u_pre)
        gated[g] = jnp.concatenate(
            [(u[c * CHUNK_A:(c + 1) * CHUNK_A] * mixed[c]).astype(BF16)
             for c in range(tm // CHUNK_A)], axis=0)
        yield

    lead = 3
    times = (0, lead, lead + 1)
    groups = [group_stages(g) for g in range(G_A)]
    y = x
    for t in range(G_A + times[-1]):
        if t == lead:
            mu = sum(jnp.sum(v, axis=-1, keepdims=True) for v in v_chunks) / d_a
            var = sum(jnp.sum((v - mu) * (v - mu), axis=-1, keepdims=True)
                      for v in v_chunks) / d_a
            stats["mu"] = mu
            stats["rs"] = lax.rsqrt(var + EPS)
        for g in reversed(range(G_A)):
            if t - g in times:
                next(groups[g])
        p, odd = divmod(t - times[-1], 2)
        if odd and 0 <= p < G_A // 2:
            lhs = jnp.concatenate([gated[2 * p], gated[2 * p + 1]], axis=1)
            y = y + jnp.dot(lhs, wout_ref[2 * p * dg:(2 * p + 2) * dg, :],
                            preferred_element_type=F32)
    y_ref[...] = y


def _mixer_a(x2, g, w_in, ln_g, ln_b, ws, bs, w_out, *, tm, seq_chunk, emit_v):
    t, d = x2.shape
    d_a = ln_g.shape[1]
    out_shape = [jax.ShapeDtypeStruct((t, d), F32)]
    out_specs = [pl.BlockSpec((tm, d), lambda i: (i, 0))]
    if emit_v:
        out_shape.append(jax.ShapeDtypeStruct((t, d_a), F32))
        out_specs.append(pl.BlockSpec((tm, d_a), lambda i: (i, 0)))
    return pl.pallas_call(
        functools.partial(_mixer_a_kernel, seq_chunk=seq_chunk, emit_v=emit_v),
        grid=(t // tm,),
        in_specs=[
            pl.BlockSpec((tm, d), lambda i: (i, 0)),
            _const_spec(g.shape), _const_spec(w_in.shape), _const_spec(ln_g.shape),
            _const_spec(ln_b.shape), _const_spec(ws.shape), _const_spec(bs.shape),
            _const_spec(w_out.shape),
        ],
        out_specs=out_specs,
        out_shape=out_shape,
        compiler_params=_params(1),
        name="mixer_a",
    )(x2, g, w_in, ln_g, ln_b, ws, bs, w_out)


def _ffn_kernel(*refs, nb, ls, zero_init, final_norm):
    refs = list(refs)
    x_ref, g_ref, wup_ref, cw_ref, cb_ref, wd_ref = refs[:6]
    pos = 6
    buf_ref = None
    if not zero_init:
        buf_ref = refs[pos]
        pos += 1
    fg_ref = None
    if final_norm:
        fg_ref = refs[pos]
        pos += 1
    y_ref, nbuf_ref = refs[pos], refs[pos + 1]

    @pl.when(pl.program_id(1) == 0)
    def _():
        if zero_init:
            nbuf_ref[...] = jnp.zeros(nbuf_ref.shape, F32)
        else:
            nbuf_ref[...] = buf_ref[...]

    x = x_ref[...]
    h = _rmsnorm(x, g_ref[...]).astype(BF16)
    d_ff = wd_ref.shape[1]
    a = jnp.dot(h, wup_ref[0, :, :d_ff], preferred_element_type=F32)
    gv = jnp.dot(h, wup_ref[0, :, d_ff:], preferred_element_type=F32)
    a_c = _causal_conv(a, nbuf_ref, cw_ref, cb_ref, nb, ls)
    act = (jax.nn.gelu(a_c) * gv).astype(BF16)
    y = x + jnp.dot(act, wd_ref[0], preferred_element_type=F32)
    if final_norm:
        y = _rmsnorm(y, fg_ref[...])
    y_ref[...] = y


def _ffn(x2, g, w_up, cw, cb, w_d, buf, final_g, *, layer, bsz, seq, nb, ls):
    t, d = x2.shape
    d_ff = w_d.shape[1]
    nj = seq // ls
    zero_init = buf is None
    final_norm = final_g is not None
    row_spec = pl.BlockSpec((nb * ls, d), lambda i, j: (i * nj + j, 0))
    buf_spec = pl.BlockSpec((nb, CONV_F - 1, d_ff), lambda i, j: (i, 0, 0))
    args = [x2, g, w_up, cw, cb, w_d]
    in_specs = [row_spec, _const_spec(g.shape), _layer_spec(w_up.shape, layer),
                _const_spec(cw.shape), _const_spec(cb.shape), _layer_spec(w_d.shape, layer)]
    if not zero_init:
        args.append(buf)
        in_specs.append(buf_spec)
    if final_norm:
        args.append(final_g)
        in_specs.append(_const_spec(final_g.shape))
    return pl.pallas_call(
        functools.partial(_ffn_kernel, nb=nb, ls=ls, zero_init=zero_init, final_norm=final_norm),
        grid=(bsz // nb, nj),
        in_specs=in_specs,
        out_specs=[row_spec, buf_spec],
        out_shape=[jax.ShapeDtypeStruct((t, d), F32),
                   jax.ShapeDtypeStruct((bsz, CONV_F - 1, d_ff), F32)],
        compiler_params=_params(2),
        name="conv_ffn",
    )(*args)


def _log_sigmoid(x):
    return jnp.minimum(x, 0.0) - jnp.log1p(jnp.exp(-jnp.abs(x)))


def _mlstm_proj_kernel(*refs, nb, ls, zero_init, k_scale):
    refs = list(refs)
    x_ref, g_ref, wall_ref, wi_ref, wf_ref, cw_ref, cb_ref, bi_ref, bf_ref = refs[:9]
    pos = 9
    buf_ref = None
    if not zero_init:
        buf_ref = refs[pos]
        pos += 1
    q_ref, k_ref, v_ref, o_ref, ig_ref, lf_ref, nbuf_ref = refs[pos:pos + 7]

    @pl.when(pl.program_id(1) == 0)
    def _():
        if zero_init:
            nbuf_ref[...] = jnp.zeros(nbuf_ref.shape, F32)
        else:
            nbuf_ref[...] = buf_ref[...]

    h = _rmsnorm(x_ref[...], g_ref[...]).astype(BF16)
    qk_w = cw_ref.shape[1]
    d_b = v_ref.shape[1]
    qk_pre = jnp.dot(h, wall_ref[0, :, 0:qk_w], preferred_element_type=F32)
    v_ref[...] = jnp.dot(h, wall_ref[0, :, qk_w:qk_w + d_b], preferred_element_type=F32)
    o_ref[...] = jnp.dot(h, wall_ref[0, :, qk_w + d_b:qk_w + 2 * d_b],
                         preferred_element_type=F32)
    ig_ref[...] = jnp.dot(h, wi_ref[...], preferred_element_type=F32) + bi_ref[...]
    lf_ref[...] = _log_sigmoid(jnp.dot(h, wf_ref[...], preferred_element_type=F32) + bf_ref[...])
    qk = jax.nn.silu(_causal_conv(qk_pre, nbuf_ref, cw_ref, cb_ref, nb, ls))
    dq = q_ref.shape[1]
    q_ref[...] = qk[:, :dq]
    k_ref[...] = qk[:, dq:] * k_scale


def _mlstm_proj(x2, g, w_all, w_i, w_f, cw, cb, b_i, b_f, buf, *, d_b, bsz, seq, nb, ls,
                k_scale):
    t, d = x2.shape
    qk_w = cw.shape[1]
    nj = seq // ls
    zero_init = buf is None

    def rows(width):
        return pl.BlockSpec((nb * ls, width), lambda i, j: (i * nj + j, 0))

    buf_spec = pl.BlockSpec((nb, CONV_B - 1, qk_w), lambda i, j: (i, 0, 0))
    args = [x2, g, w_all, w_i, w_f, cw, cb, b_i, b_f]
    in_specs = ([rows(d), _const_spec(g.shape), _layer_spec(w_all.shape, 0)]
                + [_const_spec(a.shape) for a in args[3:]])
    if not zero_init:
        args.append(buf)
        in_specs.append(buf_spec)
    return pl.pallas_call(
        functools.partial(_mlstm_proj_kernel, nb=nb, ls=ls, zero_init=zero_init, k_scale=k_scale),
        grid=(bsz // nb, nj),
        in_specs=in_specs,
        out_specs=[rows(qk_w // 2), rows(qk_w // 2), rows(d_b), rows(d_b), rows(LANES),
                   rows(LANES), buf_spec],
        out_shape=[jax.ShapeDtypeStruct((t, qk_w // 2), F32),
                   jax.ShapeDtypeStruct((t, qk_w // 2), F32),
                   jax.ShapeDtypeStruct((t, d_b), F32),
                   jax.ShapeDtypeStruct((t, d_b), F32),
                   jax.ShapeDtypeStruct((t, LANES), F32),
                   jax.ShapeDtypeStruct((t, LANES), F32),
                   jax.ShapeDtypeStruct((bsz, CONV_B - 1, qk_w), F32)],
        compiler_params=_params(2),
        name="mlstm_proj",
    )(*args)


def _mlstm_step_kernel(q_ref, k_ref, v_ref, ig_ref, lf_ref, c0_ref, n0_ref, m0_ref,
                       h_ref, c_ref, n_ref, m_ref, *, nb, ls):
    r = nb * ls
    rc = LANES
    dk = c0_ref.shape[2]
    dv = c0_ref.shape[3]

    def pad_rows(a):
        if r == rc:
            return a
        return jnp.concatenate([a, jnp.zeros((rc - r, a.shape[1]), a.dtype)], axis=0)

    r_i = lax.broadcasted_iota(jnp.int32, (r, rc), 0)
    c_i = lax.broadcasted_iota(jnp.int32, (r, rc), 1)
    mask = (c_i <= r_i) & ((r_i // ls) == (c_i // ls))
    ig = ig_ref[...]
    b_all = jnp.dot(mask.astype(F32), pad_rows(lf_ref[...]), precision=HIGHEST,
                    preferred_element_type=F32)
    b_t = pad_rows(b_all).T
    ig_t = pad_rows(ig).T
    m0 = m0_ref[...]
    inter_all = b_all + _expand2(m0, nb, ls)
    lane = lax.broadcasted_iota(jnp.int32, (r, LANES), 1)
    mt_all = jnp.zeros((r, LANES), F32)

    for hd in range(H_B):
        kcols = slice(hd * dk, (hd + 1) * dk)
        vcols = slice(hd * dv, (hd + 1) * dv)
        d = jnp.where(mask, b_all[:, hd:hd + 1] - b_t[hd:hd + 1, :] + ig_t[hd:hd + 1, :],
                      -jnp.inf)
        inter = inter_all[:, hd:hd + 1]
        m_t = jnp.maximum(inter, jnp.max(d, axis=1, keepdims=True))
        mt_all = jnp.where(lane == hd, m_t, mt_all)
        w_intra = jnp.exp(d - m_t)
        w_inter = jnp.exp(inter - m_t)
        q = q_ref[:, kcols]
        qb = q.astype(BF16)
        kb = pad_rows(k_ref[:, kcols].astype(BF16))
        vb = pad_rows(v_ref[:, vcols].astype(BF16))
        s = lax.dot_general(qb, kb, (((1,), (1,)), ((), ())), preferred_element_type=F32)
        sc = s * w_intra
        intra = jnp.dot(sc.astype(BF16), vb, preferred_element_type=F32)
        den_intra = jnp.sum(sc, axis=1, keepdims=True)
        qc = jnp.concatenate(
            [jnp.dot(qb[b * ls:(b + 1) * ls], c0_ref[b, hd].astype(BF16),
                     preferred_element_type=F32) for b in range(nb)], axis=0)
        qn = jnp.sum(q * _expand2(n0_ref[hd], nb, ls), axis=1, keepdims=True)
        num = w_inter * qc + intra
        den = w_inter * qn + den_intra
        h_ref[:, vcols] = num / jnp.maximum(jnp.abs(den), jnp.exp(-m_t))

    p_r = lax.broadcasted_iota(jnp.int32, (nb, rc), 0)
    p_c = lax.broadcasted_iota(jnp.int32, (nb, rc), 1)
    pick = (p_c == p_r * ls + (ls - 1)).astype(F32)
    bl_seq = jnp.dot(pick, pad_rows(b_all), precision=HIGHEST, preferred_element_type=F32)
    mn_seq = jnp.dot(pick, pad_rows(mt_all), precision=HIGHEST, preferred_element_type=F32)
    decay = jnp.exp(bl_seq + m0 - mn_seq)
    m_ref[...] = mn_seq
    g_all = jnp.exp(_expand2(bl_seq, nb, ls) - b_all + ig - _expand2(mn_seq, nb, ls))
    col_seq = lax.broadcasted_iota(jnp.int32, (dk, rc), 1) // ls

    for hd in range(H_B):
        kg = k_ref[:, hd * dk:(hd + 1) * dk] * g_all[:, hd:hd + 1]
        n_ref[hd] = decay[:, hd:hd + 1] * n0_ref[hd] + jnp.sum(kg.reshape(nb, ls, dk), axis=1)
        kg_t = pad_rows(kg).T
        vb = pad_rows(v_ref[:, hd * dv:(hd + 1) * dv].astype(BF16))
        for b in range(nb):
            lhs = jnp.where(col_seq == b, kg_t, 0.0).astype(BF16)
            c_ref[b, hd] = decay[b:b + 1, hd:hd + 1] * c0_ref[b, hd] + jnp.dot(
                lhs, vb, preferred_element_type=F32)


def _mlstm_step(q, k, v, ig, lf, c0, n0, m0, *, bsz, nb, ls):
    t = q.shape[0]
    dk = q.shape[1] // H_B
    dv = v.shape[1] // H_B

    def rows(width):
        return pl.BlockSpec((nb * ls, width), lambda i: (i, 0))

    c_spec = pl.BlockSpec((nb, H_B, dk, dv), lambda i: (i, 0, 0, 0))
    n_spec = pl.BlockSpec((H_B, nb, dk), lambda i: (0, i, 0))
    m_spec = pl.BlockSpec((nb, LANES), lambda i: (i, 0))
    return pl.pallas_call(
        functools.partial(_mlstm_step_kernel, nb=nb, ls=ls),
        grid=(bsz // nb,),
        in_specs=[rows(H_B * dk), rows(H_B * dk), rows(H_B * dv), rows(LANES), rows(LANES),
                  c_spec, n_spec, m_spec],
        out_specs=[rows(H_B * dv), c_spec, n_spec, m_spec],
        out_shape=[jax.ShapeDtypeStruct((t, H_B * dv), F32),
                   jax.ShapeDtypeStruct((bsz, H_B, dk, dv), F32),
                   jax.ShapeDtypeStruct((H_B, bsz, dk), F32),
                   jax.ShapeDtypeStruct((bsz, LANES), F32)],
        compiler_params=_params(1),
        name="mlstm_step",
    )(q, k, v, ig, lf, c0, n0, m0)


def _mlstm_out_kernel(h_ref, o_ref, x_ref, gn_ref, wout_ref, y_ref, gate_ref):
    dv = h_ref.shape[1] // H_B
    for hd in range(H_B):
        cols = slice(hd * dv, (hd + 1) * dv)
        hh = h_ref[:, cols]
        mu = jnp.mean(hh, axis=-1, keepdims=True)
        hc = hh - mu
        var = jnp.mean(hc * hc, axis=-1, keepdims=True)
        hn = hc * lax.rsqrt(var + EPS) * gn_ref[:, cols]
        gate_ref[:, cols] = (jax.nn.sigmoid(o_ref[:, cols]) * hn).astype(BF16)
    y_ref[...] = x_ref[...] + jnp.dot(gate_ref[...], wout_ref[...], preferred_element_type=F32)


def _mlstm_out(h, o, x2, gn_g, w_out, *, tm):
    t, d = x2.shape
    d_b = h.shape[1]
    return pl.pallas_call(
        _mlstm_out_kernel,
        grid=(t // tm,),
        in_specs=[pl.BlockSpec((tm, d_b), lambda i: (i, 0)),
                  pl.BlockSpec((tm, d_b), lambda i: (i, 0)),
                  pl.BlockSpec((tm, d), lambda i: (i, 0)),
                  _const_spec(gn_g.shape), _const_spec(w_out.shape)],
        out_specs=pl.BlockSpec((tm, d), lambda i: (i, 0)),
        out_shape=jax.ShapeDtypeStruct((t, d), F32),
        scratch_shapes=[pltpu.VMEM((tm, d_b), BF16)],
        compiler_params=_params(1),
        name="mlstm_out",
    )(h, o, x2, gn_g, w_out)


def _mlstm_fused_kernel(x_ref, g_ref, wall_ref, wif_ref, cw_ref, cb_ref, bif_ref,
                        gn_ref, wout_ref, y_ref, c_ref, n_ref, m_ref, nbuf_ref, *, k_scale):
    @pl.when(pl.program_id(1) == 0)
    def _():
        c_ref[...] = jnp.zeros(c_ref.shape, F32)
        n_ref[...] = jnp.zeros(n_ref.shape, F32)
        m_ref[...] = jnp.zeros(m_ref.shape, F32)
        nbuf_ref[...] = jnp.zeros(nbuf_ref.shape, F32)

    for s in range(x_ref.shape[0]):
        _mlstm_fused_seq(x_ref.at[s], g_ref, wall_ref, wif_ref, cw_ref, cb_ref,
                         bif_ref, gn_ref, wout_ref, y_ref.at[s], c_ref.at[s], n_ref.at[s],
                         m_ref.at[s], nbuf_ref.at[s], k_scale)


def _mlstm_fused_seq(x_ref, g_ref, wall_ref, wif_ref, cw_ref, cb_ref, bif_ref,
                     gn_ref, wout_ref, y_ref, c_ref, n_ref, m_ref, nbuf_ref, k_scale):
    r = x_ref.shape[0]
    dk = c_ref.shape[2]
    dv = c_ref.shape[3]

    x = x_ref[...]
    h = _rmsnorm(x, g_ref[...]).astype(BF16)
    gates = jnp.dot(h, wif_ref[...], preferred_element_type=F32) + bif_ref[...]
    lf = _log_sigmoid(gates)
    r_i = lax.broadcasted_iota(jnp.int32, (r, r), 0)
    c_i = lax.broadcasted_iota(jnp.int32, (r, r), 1)
    mask = c_i <= r_i
    b_all = jnp.dot(mask.astype(F32), lf, precision=HIGHEST, preferred_element_type=F32)
    b_t = b_all.T
    ig_t = gates.T

    pair = 2 * dk
    n_pairs = H_B // 2
    qk_w = cw_ref.shape[1]

    def w_cols(base, cols):
        return wall_ref[0, :, base + cols.start:base + cols.stop]

    q_chunks = {}
    k_chunks = {}
    gated = {}

    def qk_chunk(c):
        cols = slice(c * pair, (c + 1) * pair)
        pre = jnp.dot(h, w_cols(0, cols), preferred_element_type=F32)
        return jax.nn.silu(_causal_conv(pre, nbuf_ref, cw_ref, cb_ref, 1, r, cols))

    def head_stages(hd):
        p, half = divmod(hd, 2)
        ig_col = gates[:, hd:hd + 1]
        i_row = ig_t[hd:hd + 1, :]
        b_col = b_all[:, H_B + hd:H_B + hd + 1]
        b_row = b_t[H_B + hd:H_B + hd + 1, :]
        m_prev = m_ref[0, hd:hd + 1, 0:1]
        vcols = slice(hd * dv, (hd + 1) * dv)
        vb = jnp.dot(h, w_cols(qk_w, vcols), preferred_element_type=F32).astype(BF16)
        o_pre = jnp.dot(h, w_cols(qk_w + H_B * dv, vcols), preferred_element_type=F32)
        d = jnp.where(mask, b_col - b_row + i_row, -jnp.inf)
        yield
        inter = b_col + m_prev
        m_t = jnp.maximum(inter, jnp.max(d, axis=1, keepdims=True))
        w_intra = jnp.exp(d - m_t)
        w_inter = jnp.exp(inter - m_t)
        yield
        q = q_chunks[p][:, half * dk:(half + 1) * dk]
        k = k_chunks[p][:, half * dk:(half + 1) * dk] * k_scale
        qb = q.astype(BF16)
        s = lax.dot_general(qb, k.astype(BF16), (((1,), (1,)), ((), ())),
                            preferred_element_type=F32)
        sc = s * w_intra
        intra = jnp.dot(sc.astype(BF16), vb, preferred_element_type=F32)
        den_intra = jnp.sum(sc, axis=1, keepdims=True)
        c_old = c_ref[0, hd]
        n_old = n_ref[0, hd:hd + 1, :]
        qc = jnp.dot(qb, c_old.astype(BF16), preferred_element_type=F32)
        qn = jnp.sum(q * n_old, axis=1, keepdims=True)
        yield
        num = w_inter * qc + intra
        den = w_inter * qn + den_intra
        hout = num / jnp.maximum(jnp.abs(den), jnp.exp(-m_t))
        mu = jnp.mean(hout, axis=-1, keepdims=True)
        hc = hout - mu
        var = jnp.mean(hc * hc, axis=-1, keepdims=True)
        hn = hc * lax.rsqrt(var + EPS) * gn_ref[:, vcols]
        yield
        gated[hd] = (jax.nn.sigmoid(o_pre) * hn).astype(BF16)
        b_last = b_col[r - 1:r, :]
        m_new = m_t[r - 1:r, :]
        g = jnp.exp(b_last - b_col + ig_col - m_new)
        g_row = jnp.exp(b_last - b_row + i_row - m_new)
        decay = jnp.exp(b_last + m_prev - m_new)
        yield
        kg_t = (k.T * g_row).astype(BF16)
        c_ref[0, hd] = decay * c_old + jnp.dot(kg_t, vb, preferred_element_type=F32)
        n_ref[0, hd:hd + 1, :] = decay * n_old + jnp.sum(k * g, axis=0, keepdims=True)
        m_ref[0, hd:hd + 1, :] = jnp.broadcast_to(m_new, (1, LANES))
        yield

    n_stages = 6
    gate_stage = 4
    q_chunks[0] = qk_chunk(0)
    k_chunks[0] = qk_chunk(n_pairs)
    heads = [head_stages(hd) for hd in range(H_B)]
    y = x
    for t in range(H_B + n_stages - 1):
        if t % 2 == 0 and t // 2 + 1 < n_pairs:
            q_chunks[t // 2 + 1] = qk_chunk(t // 2 + 1)
            k_chunks[t // 2 + 1] = qk_chunk(n_pairs + t // 2 + 1)
        for hd in reversed(range(H_B)):
            if hd <= t < hd + n_stages:
                next(heads[hd])
        p, odd = divmod(t - gate_stage, 2)
        if odd and 0 <= p < n_pairs:
            lhs = jnp.concatenate([gated[2 * p], gated[2 * p + 1]], axis=1)
            y = y + jnp.dot(lhs, wout_ref[2 * p * dv:(2 * p + 2) * dv, :],
                            preferred_element_type=F32)
    y_ref[...] = y


def _mlstm_fused(x2, g, w_all, w_if, cw, cb, b_if, gn_g, w_out, *, bsz, seq, ls, k_scale,
                 n_streams):
    t, d = x2.shape
    qk_w = cw.shape[1]
    d_b = w_out.shape[0]
    dk = qk_w // (2 * H_B)
    dv = d_b // H_B
    nj = seq // ls
    ns = n_streams
    bpg = bsz // ns
    row_spec = pl.BlockSpec((ns, ls, d), lambda i, j: (0, i * nj + j, 0))

    def state_spec(*tail):
        return pl.BlockSpec((ns, 1) + tail, lambda i, j: (0, i) + (0,) * len(tail))

    args = [x2.reshape(ns, t // ns, d), g, w_all, w_if, cw, cb, b_if, gn_g, w_out]
    y, c, n, m, nbuf = pl.pallas_call(
        functools.partial(_mlstm_fused_kernel, k_scale=k_scale),
        grid=(bpg, nj),
        in_specs=([row_spec, _const_spec(g.shape), _layer_spec(w_all.shape, 0)]
                  + [_const_spec(a.shape) for a in args[3:]]),
        out_specs=[row_spec, state_spec(H_B, dk, dv), state_spec(H_B, dk),
                   state_spec(H_B, LANES), state_spec(CONV_B - 1, qk_w)],
        out_shape=[jax.ShapeDtypeStruct((ns, t // ns, d), F32),
                   jax.ShapeDtypeStruct((ns, bpg, H_B, dk, dv), F32),
                   jax.ShapeDtypeStruct((ns, bpg, H_B, dk), F32),
                   jax.ShapeDtypeStruct((ns, bpg, H_B, LANES), F32),
                   jax.ShapeDtypeStruct((ns, bpg, CONV_B - 1, qk_w), F32)],
        compiler_params=_params(2),
        name="mlstm_fused",
    )(*args)
    return (y.reshape(t, d), c.reshape(bsz, H_B, dk, dv), n.reshape(bsz, H_B, dk),
            m.reshape(bsz, H_B, LANES), nbuf.reshape(bsz, CONV_B - 1, qk_w))


def _trunk(x, state, w, *, nb, ls, tm, nb_step=None):
    bsz, seq, d = x.shape
    x2 = x.reshape(bsz * seq, d)
    fresh = state is None
    seq_chunk = min(seq, CHUNK_A)

    ws = w["a_ws"] if seq_chunk == CHUNK_A else jnp.tile(
        w["a_ws"][:, :seq_chunk, :seq_chunk], (1, CHUNK_A // seq_chunk, CHUNK_A // seq_chunk))
    bs = jnp.tile(w["a_bs_t"][:seq_chunk], (CHUNK_A // seq_chunk, 1))
    res = _mixer_a(x2, w["norm_mix_g"][0:1], w["a_w_in"], w["a_ln_g"], w["a_ln_b"], ws, bs,
                   w["a_w_out"], tm=tm, seq_chunk=seq_chunk, emit_v=not fresh)
    x2 = res[0]
    v_rows = None if fresh else res[1]
    x2, fbuf0 = _ffn(x2, w["norm_ffn_g"][0:1], w["f_w_up"], w["f_conv_w"][0],
                     w["f_conv_b"][0:1], w["f_w_down"],
                     None if fresh else state["ffn_conv"][0], None,
                     layer=0, bsz=bsz, seq=seq, nb=nb, ls=ls)

    if fresh:
        x2, c_new, n_new, m_new, mconv = _mlstm_fused(
            x2, w["norm_mix_g"][1:2], w["b_w_all"], w["b_w_if"],
            w["b_conv_w"], w["b_conv_b"], w["b_bias_if"], w["b_gn_g"], w["b_w_out"],
            bsz=bsz, seq=seq, ls=ls, k_scale=w["k_scale"], n_streams=1)
        m_new = m_new[:, :, 0]
    else:
        q, k, v, o, ig, lf, mconv = _mlstm_proj(
            x2, w["norm_mix_g"][1:2], w["b_w_all"], w["b_w_i"],
            w["b_w_f"], w["b_conv_w"], w["b_conv_b"], w["b_bias_i"], w["b_bias_f"],
            state["mlstm_conv"], d_b=w["b_w_out"].shape[0], bsz=bsz, seq=seq, nb=nb, ls=ls, k_scale=w["k_scale"])
        n0 = jnp.transpose(state["mlstm_n"], (1, 0, 2))
        m0 = jnp.pad(state["mlstm_m"], ((0, 0), (0, LANES - H_B)))
        h, c_new, n_new, m_new = _mlstm_step(q, k, v, ig, lf, state["mlstm_C"], n0, m0, bsz=bsz,
                                             nb=nb_step, ls=seq)
        n_new = jnp.transpose(n_new, (1, 0, 2))
        m_new = m_new[:, :H_B]
        x2 = _mlstm_out(h, o, x2, w["b_gn_g"], w["b_w_out"], tm=tm)
    y2, fbuf1 = _ffn(x2, w["norm_ffn_g"][1:2], w["f_w_up"], w["f_conv_w"][1],
                     w["f_conv_b"][1:2], w["f_w_down"],
                     None if fresh else state["ffn_conv"][1], w["final_norm_g"],
                     layer=1, bsz=bsz, seq=seq, nb=nb, ls=ls)
    return dict(
        y=y2.reshape(bsz, seq, d),
        v=None if fresh else v_rows.reshape(1, bsz, seq, -1),
        C=c_new[None],
        n=n_new[None],
        m=m_new[None],
        mconv=mconv[None],
        fconv=jnp.stack([fbuf0, fbuf1]),
    )


def kernel(x_prompt, x_sample, state_mlstm_C, state_mlstm_n, state_mlstm_m, state_mlstm_conv, state_ffn_conv, norm_mix_g, norm_ffn_g, final_norm_g, a_w_in, a_ln_g, a_ln_b, a_w_s, a_b_s, a_w_out, b_w_in, b_conv_w, b_conv_b, b_bias_i, b_bias_f, b_gn_g, b_w_out, f_w_up, f_conv_w, f_conv_b, f_w_down):
    d_ff = f_w_down.shape[1]
    qk_w = b_conv_w.shape[2]
    d_b = b_w_out.shape[1]
    dk = qk_w // (2 * H_B)
    d_a = a_w_out.shape[1]
    dg = d_a // G_A

    def pad_gate(cols):
        return jnp.pad(cols, ((0, 0), (0, LANES - H_B)))

    w_in_b = b_w_in[0]
    w = dict(
        norm_mix_g=norm_mix_g, norm_ffn_g=norm_ffn_g, final_norm_g=final_norm_g[None, :],
        a_w_in=a_w_in[0].astype(BF16), a_ln_g=a_ln_g, a_ln_b=a_ln_b,
        a_ws=a_w_s[0].astype(BF16),
        a_bs_t=jnp.repeat(jnp.transpose(a_b_s[0]), dg, axis=1),
        a_w_out=a_w_out[0].astype(BF16),
        b_w_all=b_w_in.astype(BF16),
        b_w_i=pad_gate(w_in_b[:, qk_w + 2 * d_b:qk_w + 2 * d_b + H_B]).astype(BF16),
        b_w_f=pad_gate(w_in_b[:, qk_w + 2 * d_b + H_B:]).astype(BF16),
        b_conv_w=b_conv_w[0], b_conv_b=b_conv_b,
        b_bias_i=jnp.pad(b_bias_i, ((0, 0), (0, LANES - H_B))),
        b_bias_f=jnp.pad(b_bias_f, ((0, 0), (0, LANES - H_B))),
        b_w_if=jnp.pad(w_in_b[:, qk_w + 2 * d_b:], ((0, 0), (0, LANES - 2 * H_B))).astype(BF16),
        b_bias_if=jnp.pad(jnp.concatenate([b_bias_i, b_bias_f], axis=1),
                          ((0, 0), (0, LANES - 2 * H_B))),
        b_gn_g=b_gn_g, b_w_out=b_w_out[0].astype(BF16),
        f_w_up=f_w_up.astype(BF16),
        f_conv_w=f_conv_w, f_conv_b=f_conv_b, f_w_down=f_w_down.astype(BF16),
        k_scale=float(dk) ** -0.5,
    )
    p = _trunk(x_prompt, None, w, nb=1, ls=256, tm=256)
    state = dict(mlstm_C=state_mlstm_C[0], mlstm_n=state_mlstm_n[0], mlstm_m=state_mlstm_m[0],
                 mlstm_conv=state_mlstm_conv[0], ffn_conv=state_ffn_conv)
    s = _trunk(x_sample, state, w, nb=32, ls=x_sample.shape[1], tm=256, nb_step=8)
    return (p["y"], s["y"], p["C"], p["n"], p["m"], p["mconv"], p["fconv"],
            s["v"], s["C"], s["n"], s["m"], s["mconv"], s["fconv"])
```

```python
import functools

import jax
import jax.numpy as jnp
from jax import lax
from jax.experimental import pallas as pl
from jax.experimental.pallas import tpu as pltpu

F32 = jnp.float32
BF16 = jnp.bfloat16

EPS = 1e-6
LANES = 128
SUBLANES = 8
VMEM_LIMIT_BYTES = 56 * 1024 * 1024

CHUNK_A = 128
G_A = 8
H_B = 8
CONV_B = 4
CONV_F = 3


def _const_spec(shape):
    nd = len(shape)
    return pl.BlockSpec(shape, lambda *_: (0,) * nd, pipeline_mode=pl.Buffered(1))


def _layer_spec(shape, layer):
    return pl.BlockSpec((1,) + tuple(shape[1:]), lambda *_: (layer, 0, 0),
                        pipeline_mode=pl.Buffered(1))


def _params(n_axes):
    return pltpu.CompilerParams(
        dimension_semantics=("arbitrary",) * n_axes,
        vmem_limit_bytes=VMEM_LIMIT_BYTES,
    )


def _rmsnorm(x, g):
    ms = jnp.mean(x * x, axis=-1, keepdims=True)
    return x * lax.rsqrt(ms + EPS) * g


def _expand3(c3, nb, ls):
    w = c3.shape[-1]
    if nb == 1:
        return jnp.broadcast_to(c3.reshape(1, w), (ls, w))
    return jnp.broadcast_to(c3, (nb, ls, w)).reshape(nb * ls, w)


def _expand2(c2, nb, ls):
    return _expand3(c2[:, None, :], nb, ls)


def _causal_conv(a, carry_ref, cw_ref, cb_ref, nb, ls, cols=slice(None)):
    r, c = a.shape
    kw = cw_ref.shape[0]
    rolled = [a] + [pltpu.roll(a, k, 0) for k in range(1, kw)]

    def taps(shifted):
        y = cb_ref[:, cols] + cw_ref[kw - 1:kw, cols] * shifted[0]
        for k in range(1, kw):
            y = y + cw_ref[kw - 1 - k:kw - k, cols] * shifted[k]
        return y

    grp = SUBLANES if nb == 1 else r
    grp_ls = SUBLANES if nb == 1 else ls
    tpos = lax.broadcasted_iota(jnp.int32, (grp, c), 0) & (grp_ls - 1)
    fixed = [a[0:grp]]
    for k in range(1, kw):
        sh = rolled[k][0:grp]
        for t in range(k):
            idx = kw - 1 + t - k
            prev = _expand3(carry_ref[:, idx:idx + 1, cols], nb, grp_ls)
            sh = jnp.where(tpos == t, prev, sh)
        fixed.append(sh)
    y = taps(fixed)
    if nb == 1:
        y = jnp.concatenate([y, taps([s[grp:] for s in rolled])], axis=0)
    last = rolled[kw - 1]
    if nb == 1:
        carry_ref[0, :, cols] = last[0:kw - 1, :]
    else:
        heads = last.reshape(nb, ls, c)[:, 0:kw - 1, :]
        carry_ref[0:nb - 1, :, cols] = heads[1:nb]
        carry_ref[nb - 1:nb, :, cols] = heads[0:1]
    return y


def _mixer_a_kernel(x_ref, g_ref, win_ref, lng_ref, lnb_ref, ws_ref, bs_ref, wout_ref,
                    y_ref, *rest, seq_chunk, emit_v):
    if emit_v:
        (v_ref,) = rest
    tm = x_ref.shape[0]
    d_a = lng_ref.shape[1]
    dg = d_a // G_A
    x = x_ref[...]
    h = _rmsnorm(x, g_ref[...]).astype(BF16)
    v_pre = {}
    v_chunks = []
    for g in range(G_A + 1):
        if g < G_A:
            v_pre[g] = jnp.dot(h, win_ref[:, d_a + g * dg:d_a + (g + 1) * dg],
                               preferred_element_type=F32)
        if g > 0:
            v_chunks.append(jax.nn.gelu(v_pre.pop(g - 1)))
    t_i = lax.broadcasted_iota(jnp.int32, (CHUNK_A, CHUNK_A), 0)
    s_i = lax.broadcasted_iota(jnp.int32, (CHUNK_A, CHUNK_A), 1)
    mask = (s_i <= t_i) & ((t_i // seq_chunk) == (s_i // seq_chunk))
    stats = {}
    gated = {}

    def group_stages(g):
        cols = slice(g * dg, (g + 1) * dg)
        u_pre = jnp.dot(h, win_ref[:, cols], preferred_element_type=F32)
        yield
        vn = ((v_chunks[g] - stats["mu"]) * stats["rs"]) * lng_ref[:, cols] + lnb_ref[:, cols]
        if emit_v:
            v_ref[:, cols] = vn
        vb = vn.astype(BF16)
        w = jnp.where(mask, ws_ref[g], jnp.zeros((), BF16))
        mixed = [jnp.dot(w, vb[c * CHUNK_A:(c + 1) * CHUNK_A], preferred_element_type=F32)
                 + bs_ref[:, cols] for c in range(tm // CHUNK_A)]
        yield
        u = jax.nn.gelu(u_pre)
        gated[g] = jnp.concatenate(
            [(u[c * CHUNK_A:(c + 1) * CHUNK_A] * mixed[c]).astype(BF16)
             for c in range(tm // CHUNK_A)], axis=0)
        yield

    lead = 3
    times = (0, lead, lead + 1)
    groups = [group_stages(g) for g in range(G_A)]
    y = x
    for t in range(G_A + times[-1]):
        if t == lead:
            mu = sum(jnp.sum(v, axis=-1, keepdims=True) for v in v_chunks) / d_a
            var = sum(jnp.sum((v - mu) * (v - mu), axis=-1, keepdims=True)
                      for v in v_chunks) / d_a
            stats["mu"] = mu
            stats["rs"] = lax.rsqrt(var + EPS)
        for g in reversed(range(G_A)):
            if t - g in times:
                next(groups[g])
        p, odd = divmod(t - times[-1], 2)
        if odd and 0 <= p < G_A // 2:
            lhs = jnp.concatenate([gated[2 * p], gated[2 * p + 1]], axis=1)
            y = y + jnp.dot(lhs, wout_ref[2 * p * dg:(2 * p + 2) * dg, :],
                            preferred_element_type=F32)
    y_ref[...] = y


def _mixer_a(x2, g, w_in, ln_g, ln_b, ws, bs, w_out, *, tm, seq_chunk, emit_v):
    t, d = x2.shape
    d_a = ln_g.shape[1]
    out_shape = [jax.ShapeDtypeStruct((t, d), F32)]
    out_specs = [pl.BlockSpec((tm, d), lambda i: (i, 0))]
    if emit_v:
        out_shape.append(jax.ShapeDtypeStruct((t, d_a), F32))
        out_specs.append(pl.BlockSpec((tm, d_a), lambda i: (i, 0)))
    return pl.pallas_call(
        functools.partial(_mixer_a_kernel, seq_chunk=seq_chunk, emit_v=emit_v),
        grid=(t // tm,),
        in_specs=[
            pl.BlockSpec((tm, d), lambda i: (i, 0)),
            _const_spec(g.shape), _const_spec(w_in.shape), _const_spec(ln_g.shape),
            _const_spec(ln_b.shape), _const_spec(ws.shape), _const_spec(bs.shape),
            _const_spec(w_out.shape),
        ],
        out_specs=out_specs,
        out_shape=out_shape,
        compiler_params=_params(1),
        name="mixer_a",
    )(x2, g, w_in, ln_g, ln_b, ws, bs, w_out)


def _ffn_kernel(*refs, nb, ls, zero_init, final_norm):
    refs = list(refs)
    x_ref, g_ref, wup_ref, cw_ref, cb_ref, wd_ref = refs[:6]
    pos = 6
    buf_ref = None
    if not zero_init:
        buf_ref = refs[pos]
        pos += 1
    fg_ref = None
    if final_norm:
        fg_ref = refs[pos]
        pos += 1
    y_ref, nbuf_ref = refs[pos], refs[pos + 1]

    @pl.when(pl.program_id(1) == 0)
    def _():
        if zero_init:
            nbuf_ref[...] = jnp.zeros(nbuf_ref.shape, F32)
        else:
            nbuf_ref[...] = buf_ref[...]

    x = x_ref[...]
    h = _rmsnorm(x, g_ref[...]).astype(BF16)
    d_ff = wd_ref.shape[1]
    a = jnp.dot(h, wup_ref[0, :, :d_ff], preferred_element_type=F32)
    gv = jnp.dot(h, wup_ref[0, :, d_ff:], preferred_element_type=F32)
    a_c = _causal_conv(a, nbuf_ref, cw_ref, cb_ref, nb, ls)
    act = (jax.nn.gelu(a_c) * gv).astype(BF16)
    y = x + jnp.dot(act, wd_ref[0], preferred_element_type=F32)
    if final_norm:
        y = _rmsnorm(y, fg_ref[...])
    y_ref[...] = y


def _ffn(x2, g, w_up, cw, cb, w_d, buf, final_g, *, layer, bsz, seq, nb, ls):
    t, d = x2.shape
    d_ff = w_d.shape[1]
    nj = seq // ls
    zero_init = buf is None
    final_norm = final_g is not None
    row_spec = pl.BlockSpec((nb * ls, d), lambda i, j: (i * nj + j, 0))
    buf_spec = pl.BlockSpec((nb, CONV_F - 1, d_ff), lambda i, j: (i, 0, 0))
    args = [x2, g, w_up, cw, cb, w_d]
    in_specs = [row_spec, _const_spec(g.shape), _layer_spec(w_up.shape, layer),
                _const_spec(cw.shape), _const_spec(cb.shape), _layer_spec(w_d.shape, layer)]
    if not zero_init:
        args.append(buf)
        in_specs.append(buf_spec)
    if final_norm:
        args.append(final_g)
        in_specs.append(_const_spec(final_g.shape))
    return pl.pallas_call(
        functools.partial(_ffn_kernel, nb=nb, ls=ls, zero_init=zero_init, final_norm=final_norm),
        grid=(bsz // nb, nj),
        in_specs=in_specs,
        out_specs=[row_spec, buf_spec],
        out_shape=[jax.ShapeDtypeStruct((t, d), F32),
                   jax.ShapeDtypeStruct((bsz, CONV_F - 1, d_ff), F32)],
        compiler_params=_params(2),
        name="conv_ffn",
    )(*args)


def _select_sum(sel, x):
    hi = x.astype(BF16)
    r1 = x - hi.astype(F32)
    mid = r1.astype(BF16)
    lo = (r1 - mid.astype(F32)).astype(BF16)
    n = x.shape[1]
    out = jnp.dot(sel.astype(BF16), jnp.concatenate([hi, mid, lo], axis=1),
                  preferred_element_type=F32)
    return out[:, 0:n] + out[:, n:2 * n] + out[:, 2 * n:3 * n]


def _log_sigmoid(x):
    return jnp.minimum(x, 0.0) - jnp.log1p(jnp.exp(-jnp.abs(x)))


def _mlstm_proj_kernel(*refs, nb, ls, zero_init, k_scale):
    refs = list(refs)
    x_ref, g_ref, wall_ref, wi_ref, wf_ref, cw_ref, cb_ref, bi_ref, bf_ref = refs[:9]
    pos = 9
    buf_ref = None
    if not zero_init:
        buf_ref = refs[pos]
        pos += 1
    q_ref, k_ref, v_ref, o_ref, ig_ref, lf_ref, nbuf_ref = refs[pos:pos + 7]

    @pl.when(pl.program_id(1) == 0)
    def _():
        if zero_init:
            nbuf_ref[...] = jnp.zeros(nbuf_ref.shape, F32)
        else:
            nbuf_ref[...] = buf_ref[...]

    h = _rmsnorm(x_ref[...], g_ref[...]).astype(BF16)
    qk_w = cw_ref.shape[1]
    d_b = v_ref.shape[1]
    qk_pre = jnp.dot(h, wall_ref[0, :, 0:qk_w], preferred_element_type=F32)
    v_ref[...] = jnp.dot(h, wall_ref[0, :, qk_w:qk_w + d_b], preferred_element_type=F32)
    o_ref[...] = jnp.dot(h, wall_ref[0, :, qk_w + d_b:qk_w + 2 * d_b],
                         preferred_element_type=F32)
    ig_ref[...] = jnp.dot(h, wi_ref[...], preferred_element_type=F32) + bi_ref[...]
    lf_ref[...] = _log_sigmoid(jnp.dot(h, wf_ref[...], preferred_element_type=F32) + bf_ref[...])
    qk = jax.nn.silu(_causal_conv(qk_pre, nbuf_ref, cw_ref, cb_ref, nb, ls))
    dq = q_ref.shape[1]
    q_ref[...] = qk[:, :dq]
    k_ref[...] = qk[:, dq:] * k_scale


def _mlstm_proj(x2, g, w_all, w_i, w_f, cw, cb, b_i, b_f, buf, *, d_b, bsz, seq, nb, ls,
                k_scale):
    t, d = x2.shape
    qk_w = cw.shape[1]
    nj = seq // ls
    zero_init = buf is None

    def rows(width):
        return pl.BlockSpec((nb * ls, width), lambda i, j: (i * nj + j, 0))

    buf_spec = pl.BlockSpec((nb, CONV_B - 1, qk_w), lambda i, j: (i, 0, 0))
    args = [x2, g, w_all, w_i, w_f, cw, cb, b_i, b_f]
    in_specs = ([rows(d), _const_spec(g.shape), _layer_spec(w_all.shape, 0)]
                + [_const_spec(a.shape) for a in args[3:]])
    if not zero_init:
        args.append(buf)
        in_specs.append(buf_spec)
    return pl.pallas_call(
        functools.partial(_mlstm_proj_kernel, nb=nb, ls=ls, zero_init=zero_init, k_scale=k_scale),
        grid=(bsz // nb, nj),
        in_specs=in_specs,
        out_specs=[rows(qk_w // 2), rows(qk_w // 2), rows(d_b), rows(d_b), rows(LANES),
                   rows(LANES), buf_spec],
        out_shape=[jax.ShapeDtypeStruct((t, qk_w // 2), F32),
                   jax.ShapeDtypeStruct((t, qk_w // 2), F32),
                   jax.ShapeDtypeStruct((t, d_b), F32),
                   jax.ShapeDtypeStruct((t, d_b), F32),
                   jax.ShapeDtypeStruct((t, LANES), F32),
                   jax.ShapeDtypeStruct((t, LANES), F32),
                   jax.ShapeDtypeStruct((bsz, CONV_B - 1, qk_w), F32)],
        compiler_params=_params(2),
        name="mlstm_proj",
    )(*args)


def _mlstm_step_kernel(q_ref, k_ref, v_ref, ig_ref, lf_ref, c0_ref, n0_ref, m0_ref,
                       h_ref, c_ref, n_ref, m_ref, *, nb, ls):
    r = nb * ls
    rc = LANES
    dk = c0_ref.shape[2]
    dv = c0_ref.shape[3]

    def pad_rows(a):
        if r == rc:
            return a
        return jnp.concatenate([a, jnp.zeros((rc - r, a.shape[1]), a.dtype)], axis=0)

    r_i = lax.broadcasted_iota(jnp.int32, (r, rc), 0)
    c_i = lax.broadcasted_iota(jnp.int32, (r, rc), 1)
    mask = (c_i <= r_i) & ((r_i // ls) == (c_i // ls))
    ig = ig_ref[...]
    b_all = _select_sum(mask, pad_rows(lf_ref[...]))
    b_t = pad_rows(b_all).T
    ig_t = pad_rows(ig).T
    m0 = m0_ref[...]
    inter_all = b_all + _expand2(m0, nb, ls)
    lane = lax.broadcasted_iota(jnp.int32, (r, LANES), 1)
    mt_all = jnp.zeros((r, LANES), F32)

    for hd in range(H_B):
        kcols = slice(hd * dk, (hd + 1) * dk)
        vcols = slice(hd * dv, (hd + 1) * dv)
        d = jnp.where(mask, b_all[:, hd:hd + 1] - b_t[hd:hd + 1, :] + ig_t[hd:hd + 1, :],
                      -jnp.inf)
        inter = inter_all[:, hd:hd + 1]
        m_t = jnp.maximum(inter, jnp.max(d, axis=1, keepdims=True))
        mt_all = jnp.where(lane == hd, m_t, mt_all)
        w_intra = jnp.exp(d - m_t)
        w_inter = jnp.exp(inter - m_t)
        q = q_ref[:, kcols]
        qb = q.astype(BF16)
        kb = pad_rows(k_ref[:, kcols].astype(BF16))
        vb = pad_rows(v_ref[:, vcols].astype(BF16))
        s = lax.dot_general(qb, kb, (((1,), (1,)), ((), ())), preferred_element_type=F32)
        sc = s * w_intra
        intra = jnp.dot(sc.astype(BF16), vb, preferred_element_type=F32)
        den_intra = jnp.sum(sc, axis=1, keepdims=True)
        qc = jnp.concatenate(
            [jnp.dot(qb[b * ls:(b + 1) * ls], c0_ref[b, hd].astype(BF16),
                     preferred_element_type=F32) for b in range(nb)], axis=0)
        qn = jnp.sum(q * _expand2(n0_ref[hd], nb, ls), axis=1, keepdims=True)
        num = w_inter * qc + intra
        den = w_inter * qn + den_intra
        h_ref[:, vcols] = num / jnp.maximum(jnp.abs(den), jnp.exp(-m_t))

    p_r = lax.broadcasted_iota(jnp.int32, (nb, rc), 0)
    p_c = lax.broadcasted_iota(jnp.int32, (nb, rc), 1)
    pick = p_c == p_r * ls + (ls - 1)
    bl_seq = _select_sum(pick, pad_rows(b_all))
    mn_seq = _select_sum(pick, pad_rows(mt_all))
    decay = jnp.exp(bl_seq + m0 - mn_seq)
    m_ref[...] = mn_seq
    g_all = jnp.exp(_expand2(bl_seq, nb, ls) - b_all + ig - _expand2(mn_seq, nb, ls))
    col_seq = lax.broadcasted_iota(jnp.int32, (dk, rc), 1) // ls

    for hd in range(H_B):
        kg = k_ref[:, hd * dk:(hd + 1) * dk] * g_all[:, hd:hd + 1]
        n_ref[hd] = decay[:, hd:hd + 1] * n0_ref[hd] + jnp.sum(kg.reshape(nb, ls, dk), axis=1)
        kg_t = pad_rows(kg).T
        vb = pad_rows(v_ref[:, hd * dv:(hd + 1) * dv].astype(BF16))
        for b in range(nb):
            lhs = jnp.where(col_seq == b, kg_t, 0.0).astype(BF16)
            c_ref[b, hd] = decay[b:b + 1, hd:hd + 1] * c0_ref[b, hd] + jnp.dot(
                lhs, vb, preferred_element_type=F32)


def _mlstm_step(q, k, v, ig, lf, c0, n0, m0, *, bsz, nb, ls):
    t = q.shape[0]
    dk = q.shape[1] // H_B
    dv = v.shape[1] // H_B

    def rows(width):
        return pl.BlockSpec((nb * ls, width), lambda i: (i, 0))

    c_spec = pl.BlockSpec((nb, H_B, dk, dv), lambda i: (i, 0, 0, 0))
    n_spec = pl.BlockSpec((H_B, nb, dk), lambda i: (0, i, 0))
    m_spec = pl.BlockSpec((nb, LANES), lambda i: (i, 0))
    return pl.pallas_call(
        functools.partial(_mlstm_step_kernel, nb=nb, ls=ls),
        grid=(bsz // nb,),
        in_specs=[rows(H_B * dk), rows(H_B * dk), rows(H_B * dv), rows(LANES), rows(LANES),
                  c_spec, n_spec, m_spec],
        out_specs=[rows(H_B * dv), c_spec, n_spec, m_spec],
        out_shape=[jax.ShapeDtypeStruct((t, H_B * dv), F32),
                   jax.ShapeDtypeStruct((bsz, H_B, dk, dv), F32),
                   jax.ShapeDtypeStruct((H_B, bsz, dk), F32),
                   jax.ShapeDtypeStruct((bsz, LANES), F32)],
        compiler_params=_params(1),
        name="mlstm_step",
    )(q, k, v, ig, lf, c0, n0, m0)


def _mlstm_out_kernel(h_ref, o_ref, x_ref, gn_ref, wout_ref, y_ref, gate_ref):
    dv = h_ref.shape[1] // H_B
    for hd in range(H_B):
        cols = slice(hd * dv, (hd + 1) * dv)
        hh = h_ref[:, cols]
        mu = jnp.mean(hh, axis=-1, keepdims=True)
        hc = hh - mu
        var = jnp.mean(hc * hc, axis=-1, keepdims=True)
        hn = hc * lax.rsqrt(var + EPS) * gn_ref[:, cols]
        gate_ref[:, cols] = (jax.nn.sigmoid(o_ref[:, cols]) * hn).astype(BF16)
    y_ref[...] = x_ref[...] + jnp.dot(gate_ref[...], wout_ref[...], preferred_element_type=F32)


def _mlstm_out(h, o, x2, gn_g, w_out, *, tm):
    t, d = x2.shape
    d_b = h.shape[1]
    return pl.pallas_call(
        _mlstm_out_kernel,
        grid=(t // tm,),
        in_specs=[pl.BlockSpec((tm, d_b), lambda i: (i, 0)),
                  pl.BlockSpec((tm, d_b), lambda i: (i, 0)),
                  pl.BlockSpec((tm, d), lambda i: (i, 0)),
                  _const_spec(gn_g.shape), _const_spec(w_out.shape)],
        out_specs=pl.BlockSpec((tm, d), lambda i: (i, 0)),
        out_shape=jax.ShapeDtypeStruct((t, d), F32),
        scratch_shapes=[pltpu.VMEM((tm, d_b), BF16)],
        compiler_params=_params(1),
        name="mlstm_out",
    )(h, o, x2, gn_g, w_out)


def _mlstm_fused_kernel(x_ref, g_ref, wall_ref, wif_ref, cw_ref, cb_ref, bif_ref,
                        gn_ref, wout_ref, y_ref, c_ref, n_ref, m_ref, nbuf_ref, *, k_scale):
    @pl.when(pl.program_id(1) == 0)
    def _():
        c_ref[...] = jnp.zeros(c_ref.shape, F32)
        n_ref[...] = jnp.zeros(n_ref.shape, F32)
        m_ref[...] = jnp.zeros(m_ref.shape, F32)
        nbuf_ref[...] = jnp.zeros(nbuf_ref.shape, F32)

    for s in range(x_ref.shape[0]):
        _mlstm_fused_seq(x_ref.at[s], g_ref, wall_ref, wif_ref, cw_ref, cb_ref,
                         bif_ref, gn_ref, wout_ref, y_ref.at[s], c_ref.at[s], n_ref.at[s],
                         m_ref.at[s], nbuf_ref.at[s], k_scale)


def _mlstm_fused_seq(x_ref, g_ref, wall_ref, wif_ref, cw_ref, cb_ref, bif_ref,
                     gn_ref, wout_ref, y_ref, c_ref, n_ref, m_ref, nbuf_ref, k_scale):
    r = x_ref.shape[0]
    dk = c_ref.shape[2]
    dv = c_ref.shape[3]

    x = x_ref[...]
    h = _rmsnorm(x, g_ref[...]).astype(BF16)
    gates = jnp.dot(h, wif_ref[...], preferred_element_type=F32) + bif_ref[...]
    lf = _log_sigmoid(gates)
    r_i = lax.broadcasted_iota(jnp.int32, (r, r), 0)
    c_i = lax.broadcasted_iota(jnp.int32, (r, r), 1)
    mask = c_i <= r_i
    b_all = _select_sum(mask, lf)
    b_t = b_all.T
    ig_t = gates.T

    pair = 2 * dk
    n_pairs = H_B // 2
    qk_w = cw_ref.shape[1]

    def w_cols(base, cols):
        return wall_ref[0, :, base + cols.start:base + cols.stop]

    q_chunks = {}
    k_chunks = {}
    gated = {}

    def qk_chunk(c):
        cols = slice(c * pair, (c + 1) * pair)
        pre = jnp.dot(h, w_cols(0, cols), preferred_element_type=F32)
        return jax.nn.silu(_causal_conv(pre, nbuf_ref, cw_ref, cb_ref, 1, r, cols))

    def head_stages(hd):
        p, half = divmod(hd, 2)
        ig_col = gates[:, hd:hd + 1]
        i_row = ig_t[hd:hd + 1, :]
        b_col = b_all[:, H_B + hd:H_B + hd + 1]
        b_row = b_t[H_B + hd:H_B + hd + 1, :]
        m_prev = m_ref[0, hd:hd + 1, 0:1]
        vcols = slice(hd * dv, (hd + 1) * dv)
        vb = jnp.dot(h, w_cols(qk_w, vcols), preferred_element_type=F32).astype(BF16)
        o_pre = jnp.dot(h, w_cols(qk_w + H_B * dv, vcols), preferred_element_type=F32)
        d = jnp.where(mask, b_col - b_row + i_row, -jnp.inf)
        yield
        inter = b_col + m_prev
        m_t = jnp.maximum(inter, jnp.max(d, axis=1, keepdims=True))
        w_intra = jnp.exp(d - m_t)
        w_inter = jnp.exp(inter - m_t)
        yield
        q = q_chunks[p][:, half * dk:(half + 1) * dk]
        k = k_chunks[p][:, half * dk:(half + 1) * dk] * k_scale
        qb = q.astype(BF16)
        s = lax.dot_general(qb, k.astype(BF16), (((1,), (1,)), ((), ())),
                            preferred_element_type=F32)
        sc = s * w_intra
        intra = jnp.dot(sc.astype(BF16), vb, preferred_element_type=F32)
        den_intra = jnp.sum(sc, axis=1, keepdims=True)
        c_old = c_ref[0, hd]
        n_old = n_ref[0, hd:hd + 1, :]
        qc = jnp.dot(qb, c_old.astype(BF16), preferred_element_type=F32)
        qn = jnp.sum(q * n_old, axis=1, keepdims=True)
        yield
        num = w_inter * qc + intra
        den = w_inter * qn + den_intra
        hout = num / jnp.maximum(jnp.abs(den), jnp.exp(-m_t))
        mu = jnp.mean(hout, axis=-1, keepdims=True)
        hc = hout - mu
        var = jnp.mean(hc * hc, axis=-1, keepdims=True)
        hn = hc * lax.rsqrt(var + EPS) * gn_ref[:, vcols]
        yield
        gated[hd] = (jax.nn.sigmoid(o_pre) * hn).astype(BF16)
        b_last = b_col[r - 1:r, :]
        m_new = m_t[r - 1:r, :]
        g = jnp.exp(b_last - b_col + ig_col - m_new)
        g_row = jnp.exp(b_last - b_row + i_row - m_new)
        decay = jnp.exp(b_last + m_prev - m_new)
        yield
        kg_t = (k.T * g_row).astype(BF16)
        c_ref[0, hd] = decay * c_old + jnp.dot(kg_t, vb, preferred_element_type=F32)
        n_ref[0, hd:hd + 1, :] = decay * n_old + jnp.sum(k * g, axis=0, keepdims=True)
        m_ref[0, hd:hd + 1, :] = jnp.broadcast_to(m_new, (1, LANES))
        yield

    n_stages = 6
    gate_stage = 4
    q_chunks[0] = qk_chunk(0)
    k_chunks[0] = qk_chunk(n_pairs)
    heads = [head_stages(hd) for hd in range(H_B)]
    y = x
    for t in range(H_B + n_stages - 1):
        if t % 2 == 0 and t // 2 + 1 < n_pairs:
            q_chunks[t // 2 + 1] = qk_chunk(t // 2 + 1)
            k_chunks[t // 2 + 1] = qk_chunk(n_pairs + t // 2 + 1)
        for hd in reversed(range(H_B)):
            if hd <= t < hd + n_stages:
                next(heads[hd])
        p, odd = divmod(t - gate_stage, 2)
        if odd and 0 <= p < n_pairs:
            lhs = jnp.concatenate([gated[2 * p], gated[2 * p + 1]], axis=1)
            y = y + jnp.dot(lhs, wout_ref[2 * p * dv:(2 * p + 2) * dv, :],
                            preferred_element_type=F32)
    y_ref[...] = y


def _mlstm_fused(x2, g, w_all, w_if, cw, cb, b_if, gn_g, w_out, *, bsz, seq, ls, k_scale,
                 n_streams):
    t, d = x2.shape
    qk_w = cw.shape[1]
    d_b = w_out.shape[0]
    dk = qk_w // (2 * H_B)
    dv = d_b // H_B
    nj = seq // ls
    ns = n_streams
    bpg = bsz // ns
    row_spec = pl.BlockSpec((ns, ls, d), lambda i, j: (0, i * nj + j, 0))

    def state_spec(*tail):
        return pl.BlockSpec((ns, 1) + tail, lambda i, j: (0, i) + (0,) * len(tail))

    args = [x2.reshape(ns, t // ns, d), g, w_all, w_if, cw, cb, b_if, gn_g, w_out]
    y, c, n, m, nbuf = pl.pallas_call(
        functools.partial(_mlstm_fused_kernel, k_scale=k_scale),
        grid=(bpg, nj),
        in_specs=([row_spec, _const_spec(g.shape), _layer_spec(w_all.shape, 0)]
                  + [_const_spec(a.shape) for a in args[3:]]),
        out_specs=[row_spec, state_spec(H_B, dk, dv), state_spec(H_B, dk),
                   state_spec(H_B, LANES), state_spec(CONV_B - 1, qk_w)],
        out_shape=[jax.ShapeDtypeStruct((ns, t // ns, d), F32),
                   jax.ShapeDtypeStruct((ns, bpg, H_B, dk, dv), F32),
                   jax.ShapeDtypeStruct((ns, bpg, H_B, dk), F32),
                   jax.ShapeDtypeStruct((ns, bpg, H_B, LANES), F32),
                   jax.ShapeDtypeStruct((ns, bpg, CONV_B - 1, qk_w), F32)],
        compiler_params=_params(2),
        name="mlstm_fused",
    )(*args)
    return (y.reshape(t, d), c.reshape(bsz, H_B, dk, dv), n.reshape(bsz, H_B, dk),
            m.reshape(bsz, H_B, LANES), nbuf.reshape(bsz, CONV_B - 1, qk_w))


def _trunk(x, state, w, *, nb, ls, ls_ffn, tm, nb_step=None):
    bsz, seq, d = x.shape
    x2 = x.reshape(bsz * seq, d)
    fresh = state is None
    seq_chunk = min(seq, CHUNK_A)

    ws = w["a_ws"] if seq_chunk == CHUNK_A else jnp.tile(
        w["a_ws"][:, :seq_chunk, :seq_chunk], (1, CHUNK_A // seq_chunk, CHUNK_A // seq_chunk))
    bs = jnp.tile(w["a_bs_t"][:seq_chunk], (CHUNK_A // seq_chunk, 1))
    res = _mixer_a(x2, w["norm_mix_g"][0:1], w["a_w_in"], w["a_ln_g"], w["a_ln_b"], ws, bs,
                   w["a_w_out"], tm=tm, seq_chunk=seq_chunk, emit_v=not fresh)
    x2 = res[0]
    v_rows = None if fresh else res[1]
    x2, fbuf0 = _ffn(x2, w["norm_ffn_g"][0:1], w["f_w_up"], w["f_conv_w"][0],
                     w["f_conv_b"][0:1], w["f_w_down"],
                     None if fresh else state["ffn_conv"][0], None,
                     layer=0, bsz=bsz, seq=seq, nb=nb, ls=ls_ffn)

    if fresh:
        x2, c_new, n_new, m_new, mconv = _mlstm_fused(
            x2, w["norm_mix_g"][1:2], w["b_w_all"], w["b_w_if"],
            w["b_conv_w"], w["b_conv_b"], w["b_bias_if"], w["b_gn_g"], w["b_w_out"],
            bsz=bsz, seq=seq, ls=ls, k_scale=w["k_scale"], n_streams=1)
        m_new = m_new[:, :, 0]
    else:
        q, k, v, o, ig, lf, mconv = _mlstm_proj(
            x2, w["norm_mix_g"][1:2], w["b_w_all"], w["b_w_i"],
            w["b_w_f"], w["b_conv_w"], w["b_conv_b"], w["b_bias_i"], w["b_bias_f"],
            state["mlstm_conv"], d_b=w["b_w_out"].shape[0], bsz=bsz, seq=seq, nb=nb, ls=ls, k_scale=w["k_scale"])
        n0 = jnp.transpose(state["mlstm_n"], (1, 0, 2))
        m0 = jnp.pad(state["mlstm_m"], ((0, 0), (0, LANES - H_B)))
        h, c_new, n_new, m_new = _mlstm_step(q, k, v, ig, lf, state["mlstm_C"], n0, m0, bsz=bsz,
                                             nb=nb_step, ls=seq)
        n_new = jnp.transpose(n_new, (1, 0, 2))
        m_new = m_new[:, :H_B]
        x2 = _mlstm_out(h, o, x2, w["b_gn_g"], w["b_w_out"], tm=tm)
    y2, fbuf1 = _ffn(x2, w["norm_ffn_g"][1:2], w["f_w_up"], w["f_conv_w"][1],
                     w["f_conv_b"][1:2], w["f_w_down"],
                     None if fresh else state["ffn_conv"][1], w["final_norm_g"],
                     layer=1, bsz=bsz, seq=seq, nb=nb, ls=ls_ffn)
    return dict(
        y=y2.reshape(bsz, seq, d),
        v=None if fresh else v_rows.reshape(1, bsz, seq, -1),
        C=c_new[None],
        n=n_new[None],
        m=m_new[None],
        mconv=mconv[None],
        fconv=jnp.stack([fbuf0, fbuf1]),
    )


def kernel(x_prompt, x_sample, state_mlstm_C, state_mlstm_n, state_mlstm_m, state_mlstm_conv, state_ffn_conv, norm_mix_g, norm_ffn_g, final_norm_g, a_w_in, a_ln_g, a_ln_b, a_w_s, a_b_s, a_w_out, b_w_in, b_conv_w, b_conv_b, b_bias_i, b_bias_f, b_gn_g, b_w_out, f_w_up, f_conv_w, f_conv_b, f_w_down):
    d_ff = f_w_down.shape[1]
    qk_w = b_conv_w.shape[2]
    d_b = b_w_out.shape[1]
    dk = qk_w // (2 * H_B)
    d_a = a_w_out.shape[1]
    dg = d_a // G_A

    def pad_gate(cols):
        return jnp.pad(cols, ((0, 0), (0, LANES - H_B)))

    w_in_b = b_w_in[0]
    w = dict(
        norm_mix_g=norm_mix_g, norm_ffn_g=norm_ffn_g, final_norm_g=final_norm_g[None, :],
        a_w_in=a_w_in[0].astype(BF16), a_ln_g=a_ln_g, a_ln_b=a_ln_b,
        a_ws=a_w_s[0].astype(BF16),
        a_bs_t=jnp.repeat(jnp.transpose(a_b_s[0]), dg, axis=1),
        a_w_out=a_w_out[0].astype(BF16),
        b_w_all=b_w_in.astype(BF16),
        b_w_i=pad_gate(w_in_b[:, qk_w + 2 * d_b:qk_w + 2 * d_b + H_B]).astype(BF16),
        b_w_f=pad_gate(w_in_b[:, qk_w + 2 * d_b + H_B:]).astype(BF16),
        b_conv_w=b_conv_w[0], b_conv_b=b_conv_b,
        b_bias_i=jnp.pad(b_bias_i, ((0, 0), (0, LANES - H_B))),
        b_bias_f=jnp.pad(b_bias_f, ((0, 0), (0, LANES - H_B))),
        b_w_if=jnp.pad(w_in_b[:, qk_w + 2 * d_b:], ((0, 0), (0, LANES - 2 * H_B))).astype(BF16),
        b_bias_if=jnp.pad(jnp.concatenate([b_bias_i, b_bias_f], axis=1),
                          ((0, 0), (0, LANES - 2 * H_B))),
        b_gn_g=b_gn_g, b_w_out=b_w_out[0].astype(BF16),
        f_w_up=f_w_up.astype(BF16),
        f_conv_w=f_conv_w, f_conv_b=f_conv_b, f_w_down=f_w_down.astype(BF16),
        k_scale=float(dk) ** -0.5,
    )
    p = _trunk(x_prompt, None, w, nb=1, ls=256, ls_ffn=512, tm=512)
    state = dict(mlstm_C=state_mlstm_C[0], mlstm_n=state_mlstm_n[0], mlstm_m=state_mlstm_m[0],
                 mlstm_conv=state_mlstm_conv[0], ffn_conv=state_ffn_conv)
    s = _trunk(x_sample, state, w, nb=32, ls=x_sample.shape[1], ls_ffn=x_sample.shape[1], tm=256,
               nb_step=8)
    return (p["y"], s["y"], p["C"], p["n"], p["m"], p["mconv"], p["fconv"],
            s["v"], s["C"], s["n"], s["m"], s["mconv"], s["fconv"])
```

```python
import functools

import jax
import jax.numpy as jnp
from jax import lax
from jax.experimental import pallas as pl
from jax.experimental.pallas import tpu as pltpu

F32 = jnp.float32
BF16 = jnp.bfloat16

EPS = 1e-6
LANES = 128
SUBLANES = 8
VMEM_LIMIT_BYTES = 56 * 1024 * 1024

CHUNK_A = 128
G_A = 8
H_B = 8
CONV_B = 4
CONV_F = 3


def _const_spec(shape):
    nd = len(shape)
    return pl.BlockSpec(shape, lambda *_: (0,) * nd, pipeline_mode=pl.Buffered(1))


def _layer_spec(shape, layer):
    return pl.BlockSpec((1,) + tuple(shape[1:]), lambda *_: (layer, 0, 0),
                        pipeline_mode=pl.Buffered(1))


def _params(n_axes):
    return pltpu.CompilerParams(
        dimension_semantics=("arbitrary",) * n_axes,
        vmem_limit_bytes=VMEM_LIMIT_BYTES,
    )


def _rmsnorm(x, g):
    ms = jnp.mean(x * x, axis=-1, keepdims=True)
    return x * lax.rsqrt(ms + EPS) * g


def _expand3(c3, nb, ls):
    w = c3.shape[-1]
    if nb == 1:
        return jnp.broadcast_to(c3.reshape(1, w), (ls, w))
    return jnp.broadcast_to(c3, (nb, ls, w)).reshape(nb * ls, w)


def _expand2(c2, nb, ls):
    return _expand3(c2[:, None, :], nb, ls)


def _causal_conv(a, carry_ref, cw_ref, cb_ref, nb, ls, cols=slice(None)):
    r, c = a.shape
    kw = cw_ref.shape[0]
    rolled = [a] + [pltpu.roll(a, k, 0) for k in range(1, kw)]

    def taps(shifted):
        y = cb_ref[:, cols] + cw_ref[kw - 1:kw, cols] * shifted[0]
        for k in range(1, kw):
            y = y + cw_ref[kw - 1 - k:kw - k, cols] * shifted[k]
        return y

    grp = SUBLANES if nb == 1 else r
    grp_ls = SUBLANES if nb == 1 else ls
    tpos = lax.broadcasted_iota(jnp.int32, (grp, c), 0) & (grp_ls - 1)
    fixed = [a[0:grp]]
    for k in range(1, kw):
        sh = rolled[k][0:grp]
        for t in range(k):
            idx = kw - 1 + t - k
            prev = _expand3(carry_ref[:, idx:idx + 1, cols], nb, grp_ls)
            sh = jnp.where(tpos == t, prev, sh)
        fixed.append(sh)
    y = taps(fixed)
    if nb == 1:
        y = jnp.concatenate([y, taps([s[grp:] for s in rolled])], axis=0)
    last = rolled[kw - 1]
    if nb == 1:
        carry_ref[0, :, cols] = last[0:kw - 1, :]
    else:
        heads = last.reshape(nb, ls, c)[:, 0:kw - 1, :]
        carry_ref[0:nb - 1, :, cols] = heads[1:nb]
        carry_ref[nb - 1:nb, :, cols] = heads[0:1]
    return y


def _causal_conv_permuted(a, carry_ref, cw_ref, cb_ref, cols):
    r, c = a.shape
    kw = cw_ref.shape[0]
    grp = r // SUBLANES
    sub = lax.broadcasted_iota(jnp.int32, (SUBLANES, c), 0)
    y = cb_ref[:, cols] + cw_ref[kw - 1:kw, cols] * a
    for k in range(1, kw):
        heads = []
        for s in range(k):
            src = a[(s - k + grp) * SUBLANES:(s - k + grp + 1) * SUBLANES]
            prev = carry_ref[0, kw - 1 + s - k:kw + s - k, cols]
            heads.append(jnp.where(sub == 0, jnp.broadcast_to(prev, (SUBLANES, c)),
                                   pltpu.roll(src, 1, 0)))
        shifted = jnp.concatenate(heads + [a[0:r - k * SUBLANES]], axis=0)
        y = y + cw_ref[kw - 1 - k:kw - k, cols] * shifted
    for j in range(kw - 1):
        row = (grp - (kw - 1) + j) * SUBLANES + SUBLANES - 1
        carry_ref[0, j:j + 1, cols] = a[row:row + 1]
    return y


def _mixer_a_kernel(x_ref, g_ref, win_ref, lng_ref, lnb_ref, ws_ref, bs_ref, wout_ref,
                    y_ref, *rest, seq_chunk, emit_v):
    if emit_v:
        (v_ref,) = rest
    tm = x_ref.shape[0]
    d_a = lng_ref.shape[1]
    dg = d_a // G_A
    x = x_ref[...]
    h = _rmsnorm(x, g_ref[...]).astype(BF16)
    v_pre = {}
    v_chunks = []
    for g in range(G_A + 1):
        if g < G_A:
            v_pre[g] = jnp.dot(h, win_ref[:, d_a + g * dg:d_a + (g + 1) * dg],
                               preferred_element_type=F32)
        if g > 0:
            v_chunks.append(jax.nn.gelu(v_pre.pop(g - 1)))
    t_i = lax.broadcasted_iota(jnp.int32, (CHUNK_A, CHUNK_A), 0)
    s_i = lax.broadcasted_iota(jnp.int32, (CHUNK_A, CHUNK_A), 1)
    mask = (s_i <= t_i) & ((t_i // seq_chunk) == (s_i // seq_chunk))
    stats = {}
    gated = {}

    def group_stages(g):
        cols = slice(g * dg, (g + 1) * dg)
        u_pre = jnp.dot(h, win_ref[:, cols], preferred_element_type=F32)
        yield
        vn = ((v_chunks[g] - stats["mu"]) * stats["rs"]) * lng_ref[:, cols] + lnb_ref[:, cols]
        if emit_v:
            v_ref[:, cols] = vn
        vb = vn.astype(BF16)
        w = jnp.where(mask, ws_ref[g], jnp.zeros((), BF16))
        mixed = [jnp.dot(w, vb[c * CHUNK_A:(c + 1) * CHUNK_A], preferred_element_type=F32)
                 + bs_ref[:, cols] for c in range(tm // CHUNK_A)]
        yield
        u = jax.nn.gelu(u_pre)
        gated[g] = jnp.concatenate(
            [(u[c * CHUNK_A:(c + 1) * CHUNK_A] * mixed[c]).astype(BF16)
             for c in range(tm // CHUNK_A)], axis=0)
        yield

    lead = 3
    times = (0, lead, lead + 1)
    groups = [group_stages(g) for g in range(G_A)]
    y = x
    for t in range(G_A + times[-1]):
        if t == lead:
            mu = sum(jnp.sum(v, axis=-1, keepdims=True) for v in v_chunks) / d_a
            var = sum(jnp.sum((v - mu) * (v - mu), axis=-1, keepdims=True)
                      for v in v_chunks) / d_a
            stats["mu"] = mu
            stats["rs"] = lax.rsqrt(var + EPS)
        for g in reversed(range(G_A)):
            if t - g in times:
                next(groups[g])
        p, odd = divmod(t - times[-1], 2)
        if odd and 0 <= p < G_A // 2:
            lhs = jnp.concatenate([gated[2 * p], gated[2 * p + 1]], axis=1)
            y = y + jnp.dot(lhs, wout_ref[2 * p * dg:(2 * p + 2) * dg, :],
                            preferred_element_type=F32)
    y_ref[...] = y


def _mixer_a(x2, g, w_in, ln_g, ln_b, ws, bs, w_out, *, tm, seq_chunk, emit_v):
    t, d = x2.shape
    d_a = ln_g.shape[1]
    out_shape = [jax.ShapeDtypeStruct((t, d), F32)]
    out_specs = [pl.BlockSpec((tm, d), lambda i: (i, 0))]
    if emit_v:
        out_shape.append(jax.ShapeDtypeStruct((t, d_a), F32))
        out_specs.append(pl.BlockSpec((tm, d_a), lambda i: (i, 0)))
    return pl.pallas_call(
        functools.partial(_mixer_a_kernel, seq_chunk=seq_chunk, emit_v=emit_v),
        grid=(t // tm,),
        in_specs=[
            pl.BlockSpec((tm, d), lambda i: (i, 0)),
            _const_spec(g.shape), _const_spec(w_in.shape), _const_spec(ln_g.shape),
            _const_spec(ln_b.shape), _const_spec(ws.shape), _const_spec(bs.shape),
            _const_spec(w_out.shape),
        ],
        out_specs=out_specs,
        out_shape=out_shape,
        compiler_params=_params(1),
        name="mixer_a",
    )(x2, g, w_in, ln_g, ln_b, ws, bs, w_out)


def _ffn_kernel(*refs, nb, ls, zero_init, final_norm):
    refs = list(refs)
    x_ref, g_ref, wup_ref, cw_ref, cb_ref, wd_ref = refs[:6]
    pos = 6
    buf_ref = None
    if not zero_init:
        buf_ref = refs[pos]
        pos += 1
    fg_ref = None
    if final_norm:
        fg_ref = refs[pos]
        pos += 1
    y_ref, nbuf_ref = refs[pos], refs[pos + 1]

    @pl.when(pl.program_id(1) == 0)
    def _():
        if zero_init:
            nbuf_ref[...] = jnp.zeros(nbuf_ref.shape, F32)
        else:
            nbuf_ref[...] = buf_ref[...]

    x = x_ref[...]
    h = _rmsnorm(x, g_ref[...]).astype(BF16)
    d_ff = wd_ref.shape[1]
    a = jnp.dot(h, wup_ref[0, :, :d_ff], preferred_element_type=F32)
    gv = jnp.dot(h, wup_ref[0, :, d_ff:], preferred_element_type=F32)
    a_c = _causal_conv(a, nbuf_ref, cw_ref, cb_ref, nb, ls)
    act = (jax.nn.gelu(a_c) * gv).astype(BF16)
    y = x + jnp.dot(act, wd_ref[0], preferred_element_type=F32)
    if final_norm:
        y = _rmsnorm(y, fg_ref[...])
    y_ref[...] = y


def _ffn(x2, g, w_up, cw, cb, w_d, buf, final_g, *, layer, bsz, seq, nb, ls):
    t, d = x2.shape
    d_ff = w_d.shape[1]
    nj = seq // ls
    zero_init = buf is None
    final_norm = final_g is not None
    row_spec = pl.BlockSpec((nb * ls, d), lambda i, j: (i * nj + j, 0))
    buf_spec = pl.BlockSpec((nb, CONV_F - 1, d_ff), lambda i, j: (i, 0, 0))
    args = [x2, g, w_up, cw, cb, w_d]
    in_specs = [row_spec, _const_spec(g.shape), _layer_spec(w_up.shape, layer),
                _const_spec(cw.shape), _const_spec(cb.shape), _layer_spec(w_d.shape, layer)]
    if not zero_init:
        args.append(buf)
        in_specs.append(buf_spec)
    if final_norm:
        args.append(final_g)
        in_specs.append(_const_spec(final_g.shape))
    return pl.pallas_call(
        functools.partial(_ffn_kernel, nb=nb, ls=ls, zero_init=zero_init, final_norm=final_norm),
        grid=(bsz // nb, nj),
        in_specs=in_specs,
        out_specs=[row_spec, buf_spec],
        out_shape=[jax.ShapeDtypeStruct((t, d), F32),
                   jax.ShapeDtypeStruct((bsz, CONV_F - 1, d_ff), F32)],
        compiler_params=_params(2),
        name="conv_ffn",
    )(*args)


def _select_sum(sel, x):
    return jnp.dot(sel.astype(F32), x, precision=lax.Precision.HIGHEST,
                   preferred_element_type=F32)


def _log_sigmoid(x):
    return jnp.minimum(x, 0.0) - jnp.log1p(jnp.exp(-jnp.abs(x)))


def _mlstm_proj_kernel(*refs, nb, ls, zero_init, k_scale):
    refs = list(refs)
    x_ref, g_ref, wall_ref, wi_ref, wf_ref, cw_ref, cb_ref, bi_ref, bf_ref = refs[:9]
    pos = 9
    buf_ref = None
    if not zero_init:
        buf_ref = refs[pos]
        pos += 1
    q_ref, k_ref, v_ref, o_ref, ig_ref, lf_ref, nbuf_ref = refs[pos:pos + 7]

    @pl.when(pl.program_id(1) == 0)
    def _():
        if zero_init:
            nbuf_ref[...] = jnp.zeros(nbuf_ref.shape, F32)
        else:
            nbuf_ref[...] = buf_ref[...]

    h = _rmsnorm(x_ref[...], g_ref[...]).astype(BF16)
    qk_w = cw_ref.shape[1]
    d_b = v_ref.shape[1]
    qk_pre = jnp.dot(h, wall_ref[0, :, 0:qk_w], preferred_element_type=F32)
    v_ref[...] = jnp.dot(h, wall_ref[0, :, qk_w:qk_w + d_b], preferred_element_type=F32)
    o_ref[...] = jnp.dot(h, wall_ref[0, :, qk_w + d_b:qk_w + 2 * d_b],
                         preferred_element_type=F32)
    ig_ref[...] = jnp.dot(h, wi_ref[...], preferred_element_type=F32) + bi_ref[...]
    lf_ref[...] = _log_sigmoid(jnp.dot(h, wf_ref[...], preferred_element_type=F32) + bf_ref[...])
    qk = jax.nn.silu(_causal_conv(qk_pre, nbuf_ref, cw_ref, cb_ref, nb, ls))
    dq = q_ref.shape[1]
    q_ref[...] = qk[:, :dq]
    k_ref[...] = qk[:, dq:] * k_scale


def _mlstm_proj(x2, g, w_all, w_i, w_f, cw, cb, b_i, b_f, buf, *, d_b, bsz, seq, nb, ls,
                k_scale):
    t, d = x2.shape
    qk_w = cw.shape[1]
    nj = seq // ls
    zero_init = buf is None

    def rows(width):
        return pl.BlockSpec((nb * ls, width), lambda i, j: (i * nj + j, 0))

    buf_spec = pl.BlockSpec((nb, CONV_B - 1, qk_w), lambda i, j: (i, 0, 0))
    args = [x2, g, w_all, w_i, w_f, cw, cb, b_i, b_f]
    in_specs = ([rows(d), _const_spec(g.shape), _layer_spec(w_all.shape, 0)]
                + [_const_spec(a.shape) for a in args[3:]])
    if not zero_init:
        args.append(buf)
        in_specs.append(buf_spec)
    return pl.pallas_call(
        functools.partial(_mlstm_proj_kernel, nb=nb, ls=ls, zero_init=zero_init, k_scale=k_scale),
        grid=(bsz // nb, nj),
        in_specs=in_specs,
        out_specs=[rows(qk_w // 2), rows(qk_w // 2), rows(d_b), rows(d_b), rows(LANES),
                   rows(LANES), buf_spec],
        out_shape=[jax.ShapeDtypeStruct((t, qk_w // 2), F32),
                   jax.ShapeDtypeStruct((t, qk_w // 2), F32),
                   jax.ShapeDtypeStruct((t, d_b), F32),
                   jax.ShapeDtypeStruct((t, d_b), F32),
                   jax.ShapeDtypeStruct((t, LANES), F32),
                   jax.ShapeDtypeStruct((t, LANES), F32),
                   jax.ShapeDtypeStruct((bsz, CONV_B - 1, qk_w), F32)],
        compiler_params=_params(2),
        name="mlstm_proj",
    )(*args)


def _mlstm_step_kernel(q_ref, k_ref, v_ref, ig_ref, lf_ref, c0_ref, n0_ref, m0_ref,
                       h_ref, c_ref, n_ref, m_ref, *, nb, ls):
    r = nb * ls
    rc = LANES
    dk = c0_ref.shape[2]
    dv = c0_ref.shape[3]

    def pad_rows(a):
        if r == rc:
            return a
        return jnp.concatenate([a, jnp.zeros((rc - r, a.shape[1]), a.dtype)], axis=0)

    r_i = lax.broadcasted_iota(jnp.int32, (r, rc), 0)
    c_i = lax.broadcasted_iota(jnp.int32, (r, rc), 1)
    mask = (c_i <= r_i) & ((r_i // ls) == (c_i // ls))
    ig = ig_ref[...]
    b_all = _select_sum(mask, pad_rows(lf_ref[...]))
    b_t = pad_rows(b_all).T
    ig_t = pad_rows(ig).T
    m0 = m0_ref[...]
    inter_all = b_all + _expand2(m0, nb, ls)
    lane = lax.broadcasted_iota(jnp.int32, (r, LANES), 1)
    mt_all = jnp.zeros((r, LANES), F32)

    for hd in range(H_B):
        kcols = slice(hd * dk, (hd + 1) * dk)
        vcols = slice(hd * dv, (hd + 1) * dv)
        d = jnp.where(mask, b_all[:, hd:hd + 1] - b_t[hd:hd + 1, :] + ig_t[hd:hd + 1, :],
                      -jnp.inf)
        inter = inter_all[:, hd:hd + 1]
        m_t = jnp.maximum(inter, jnp.max(d, axis=1, keepdims=True))
        mt_all = jnp.where(lane == hd, m_t, mt_all)
        w_intra = jnp.exp(d - m_t)
        w_inter = jnp.exp(inter - m_t)
        q = q_ref[:, kcols]
        qb = q.astype(BF16)
        kb = pad_rows(k_ref[:, kcols].astype(BF16))
        vb = pad_rows(v_ref[:, vcols].astype(BF16))
        s = lax.dot_general(qb, kb, (((1,), (1,)), ((), ())), preferred_element_type=F32)
        sc = s * w_intra
        intra = jnp.dot(sc.astype(BF16), vb, preferred_element_type=F32)
        den_intra = jnp.sum(sc, axis=1, keepdims=True)
        qc = jnp.concatenate(
            [jnp.dot(qb[b * ls:(b + 1) * ls], c0_ref[b, hd].astype(BF16),
                     preferred_element_type=F32) for b in range(nb)], axis=0)
        qn = jnp.sum(q * _expand2(n0_ref[hd], nb, ls), axis=1, keepdims=True)
        num = w_inter * qc + intra
        den = w_inter * qn + den_intra
        h_ref[:, vcols] = num / jnp.maximum(jnp.abs(den), jnp.exp(-m_t))

    p_r = lax.broadcasted_iota(jnp.int32, (nb, rc), 0)
    p_c = lax.broadcasted_iota(jnp.int32, (nb, rc), 1)
    pick = p_c == p_r * ls + (ls - 1)
    bl_seq = _select_sum(pick, pad_rows(b_all))
    mn_seq = _select_sum(pick, pad_rows(mt_all))
    decay = jnp.exp(bl_seq + m0 - mn_seq)
    m_ref[...] = mn_seq
    g_all = jnp.exp(_expand2(bl_seq, nb, ls) - b_all + ig - _expand2(mn_seq, nb, ls))
    col_seq = lax.broadcasted_iota(jnp.int32, (dk, rc), 1) // ls

    for hd in range(H_B):
        kg = k_ref[:, hd * dk:(hd + 1) * dk] * g_all[:, hd:hd + 1]
        n_ref[hd] = decay[:, hd:hd + 1] * n0_ref[hd] + jnp.sum(kg.reshape(nb, ls, dk), axis=1)
        kg_t = pad_rows(kg).T
        vb = pad_rows(v_ref[:, hd * dv:(hd + 1) * dv].astype(BF16))
        for b in range(nb):
            lhs = jnp.where(col_seq == b, kg_t, 0.0).astype(BF16)
            c_ref[b, hd] = decay[b:b + 1, hd:hd + 1] * c0_ref[b, hd] + jnp.dot(
                lhs, vb, preferred_element_type=F32)


def _mlstm_step(q, k, v, ig, lf, c0, n0, m0, *, bsz, nb, ls):
    t = q.shape[0]
    dk = q.shape[1] // H_B
    dv = v.shape[1] // H_B

    def rows(width):
        return pl.BlockSpec((nb * ls, width), lambda i: (i, 0))

    c_spec = pl.BlockSpec((nb, H_B, dk, dv), lambda i: (i, 0, 0, 0))
    n_spec = pl.BlockSpec((H_B, nb, dk), lambda i: (0, i, 0))
    m_spec = pl.BlockSpec((nb, LANES), lambda i: (i, 0))
    return pl.pallas_call(
        functools.partial(_mlstm_step_kernel, nb=nb, ls=ls),
        grid=(bsz // nb,),
        in_specs=[rows(H_B * dk), rows(H_B * dk), rows(H_B * dv), rows(LANES), rows(LANES),
                  c_spec, n_spec, m_spec],
        out_specs=[rows(H_B * dv), c_spec, n_spec, m_spec],
        out_shape=[jax.ShapeDtypeStruct((t, H_B * dv), F32),
                   jax.ShapeDtypeStruct((bsz, H_B, dk, dv), F32),
                   jax.ShapeDtypeStruct((H_B, bsz, dk), F32),
                   jax.ShapeDtypeStruct((bsz, LANES), F32)],
        compiler_params=_params(1),
        name="mlstm_step",
    )(q, k, v, ig, lf, c0, n0, m0)


def _mlstm_out_kernel(h_ref, o_ref, x_ref, gn_ref, wout_ref, y_ref, gate_ref):
    dv = h_ref.shape[1] // H_B
    for hd in range(H_B):
        cols = slice(hd * dv, (hd + 1) * dv)
        hh = h_ref[:, cols]
        mu = jnp.mean(hh, axis=-1, keepdims=True)
        hc = hh - mu
        var = jnp.mean(hc * hc, axis=-1, keepdims=True)
        hn = hc * lax.rsqrt(var + EPS) * gn_ref[:, cols]
        gate_ref[:, cols] = (jax.nn.sigmoid(o_ref[:, cols]) * hn).astype(BF16)
    y_ref[...] = x_ref[...] + jnp.dot(gate_ref[...], wout_ref[...], preferred_element_type=F32)


def _mlstm_out(h, o, x2, gn_g, w_out, *, tm):
    t, d = x2.shape
    d_b = h.shape[1]
    return pl.pallas_call(
        _mlstm_out_kernel,
        grid=(t // tm,),
        in_specs=[pl.BlockSpec((tm, d_b), lambda i: (i, 0)),
                  pl.BlockSpec((tm, d_b), lambda i: (i, 0)),
                  pl.BlockSpec((tm, d), lambda i: (i, 0)),
                  _const_spec(gn_g.shape), _const_spec(w_out.shape)],
        out_specs=pl.BlockSpec((tm, d), lambda i: (i, 0)),
        out_shape=jax.ShapeDtypeStruct((t, d), F32),
        scratch_shapes=[pltpu.VMEM((tm, d_b), BF16)],
        compiler_params=_params(1),
        name="mlstm_out",
    )(h, o, x2, gn_g, w_out)


def _mlstm_fused_kernel(x_ref, g_ref, wall_ref, wif_ref, cw_ref, cb_ref, bif_ref,
                        gn_ref, wout_ref, y_ref, c_ref, n_ref, m_ref, nbuf_ref, *, k_scale):
    @pl.when(pl.program_id(1) == 0)
    def _():
        c_ref[...] = jnp.zeros(c_ref.shape, F32)
        n_ref[...] = jnp.zeros(n_ref.shape, F32)
        m_ref[...] = jnp.zeros(m_ref.shape, F32)
        nbuf_ref[...] = jnp.zeros(nbuf_ref.shape, F32)

    r = x_ref.shape[0]
    dk = c_ref.shape[2]
    dv = c_ref.shape[3]

    grp = r // SUBLANES
    x = jnp.swapaxes(x_ref[...].reshape(SUBLANES, grp, x_ref.shape[1]), 0, 1).reshape(x_ref.shape)
    h = _rmsnorm(x, g_ref[...]).astype(BF16)
    gates = jnp.dot(h, wif_ref[...], preferred_element_type=F32) + bif_ref[...]
    lf = _log_sigmoid(gates)
    r_i = lax.broadcasted_iota(jnp.int32, (r, r), 0)
    c_i = lax.broadcasted_iota(jnp.int32, (r, r), 1)
    mask = ((c_i % SUBLANES) * grp + c_i // SUBLANES) <= ((r_i % SUBLANES) * grp + r_i // SUBLANES)
    b_all = _select_sum(mask, lf)
    b_t = b_all.T
    ig_t = gates.T

    pair = 2 * dk
    n_pairs = H_B // 2
    qk_w = cw_ref.shape[1]

    def w_cols(base, cols):
        return wall_ref[0, :, base + cols.start:base + cols.stop]

    q_chunks = {}
    k_chunks = {}
    gated = {}

    def qk_chunk(c):
        cols = slice(c * pair, (c + 1) * pair)
        pre = jnp.dot(h, w_cols(0, cols), preferred_element_type=F32)
        return jax.nn.silu(_causal_conv_permuted(pre, nbuf_ref, cw_ref, cb_ref, cols))

    def head_stages(hd):
        p, half = divmod(hd, 2)
        ig_col = gates[:, hd:hd + 1]
        i_row = ig_t[hd:hd + 1, :]
        b_col = b_all[:, H_B + hd:H_B + hd + 1]
        b_row = b_t[H_B + hd:H_B + hd + 1, :]
        m_prev = m_ref[0, hd:hd + 1, 0:1]
        vcols = slice(hd * dv, (hd + 1) * dv)
        vb = jnp.dot(h, w_cols(qk_w, vcols), preferred_element_type=F32).astype(BF16)
        o_pre = jnp.dot(h, w_cols(qk_w + H_B * dv, vcols), preferred_element_type=F32)
        d = jnp.where(mask, b_col - b_row + i_row, -jnp.inf)
        yield
        inter = b_col + m_prev
        m_t = jnp.maximum(inter, jnp.max(d, axis=1, keepdims=True))
        w_intra = jnp.exp(d - m_t)
        w_inter = jnp.exp(inter - m_t)
        yield
        q = q_chunks[p][:, half * dk:(half + 1) * dk]
        k = k_chunks[p][:, half * dk:(half + 1) * dk] * k_scale
        qb = q.astype(BF16)
        s = lax.dot_general(qb, k.astype(BF16), (((1,), (1,)), ((), ())),
                            preferred_element_type=F32)
        sc = s * w_intra
        intra = jnp.dot(sc.astype(BF16), vb, preferred_element_type=F32)
        den_intra = jnp.sum(sc, axis=1, keepdims=True)
        c_old = c_ref[0, hd]
        n_old = n_ref[0, hd:hd + 1, :]
        qc = jnp.dot(qb, c_old.astype(BF16), preferred_element_type=F32)
        qn = jnp.sum(q * n_old, axis=1, keepdims=True)
        yield
        num = w_inter * qc + intra
        den = w_inter * qn + den_intra
        hout = num / jnp.maximum(jnp.abs(den), jnp.exp(-m_t))
        mu = jnp.mean(hout, axis=-1, keepdims=True)
        hc = hout - mu
        var = jnp.mean(hc * hc, axis=-1, keepdims=True)
        hn = hc * lax.rsqrt(var + EPS) * gn_ref[:, vcols]
        yield
        gated[hd] = (jax.nn.sigmoid(o_pre) * hn).astype(BF16)
        b_last = b_col[r - 1:r, :]
        m_new = m_t[r - 1:r, :]
        g = jnp.exp(b_last - b_col + ig_col - m_new)
        g_row = jnp.exp(b_last - b_row + i_row - m_new)
        decay = jnp.exp(b_last + m_prev - m_new)
        yield
        kg_t = (k.T * g_row).astype(BF16)
        c_ref[0, hd] = decay * c_old + jnp.dot(kg_t, vb, preferred_element_type=F32)
        n_ref[0, hd:hd + 1, :] = decay * n_old + jnp.sum(k * g, axis=0, keepdims=True)
        m_ref[0, hd:hd + 1, :] = jnp.broadcast_to(m_new, (1, LANES))
        yield

    n_stages = 6
    gate_stage = 4
    q_chunks[0] = qk_chunk(0)
    k_chunks[0] = qk_chunk(n_pairs)
    heads = [head_stages(hd) for hd in range(H_B)]
    y = x
    for t in range(H_B + n_stages - 1):
        if t % 2 == 0 and t // 2 + 1 < n_pairs:
            q_chunks[t // 2 + 1] = qk_chunk(t // 2 + 1)
            k_chunks[t // 2 + 1] = qk_chunk(n_pairs + t // 2 + 1)
        for hd in reversed(range(H_B)):
            if hd <= t < hd + n_stages:
                next(heads[hd])
        p, odd = divmod(t - gate_stage, 2)
        if odd and 0 <= p < n_pairs:
            lhs = jnp.concatenate([gated[2 * p], gated[2 * p + 1]], axis=1)
            y = y + jnp.dot(lhs, wout_ref[2 * p * dv:(2 * p + 2) * dv, :],
                            preferred_element_type=F32)
    y_ref[...] = jnp.swapaxes(y.reshape(grp, SUBLANES, y.shape[1]), 0, 1).reshape(y.shape)


def _mlstm_fused(x2, g, w_all, w_if, cw, cb, b_if, gn_g, w_out, *, bsz, seq, ls, k_scale):
    t, d = x2.shape
    qk_w = cw.shape[1]
    d_b = w_out.shape[0]
    dk = qk_w // (2 * H_B)
    dv = d_b // H_B
    nj = seq // ls
    row_spec = pl.BlockSpec((ls, d), lambda i, j: (i * nj + j, 0))

    def state_spec(*tail):
        return pl.BlockSpec((1,) + tail, lambda i, j: (i,) + (0,) * len(tail))

    args = [x2, g, w_all, w_if, cw, cb, b_if, gn_g, w_out]
    return pl.pallas_call(
        functools.partial(_mlstm_fused_kernel, k_scale=k_scale),
        grid=(bsz, nj),
        in_specs=([row_spec, _const_spec(g.shape), _layer_spec(w_all.shape, 0)]
                  + [_const_spec(a.shape) for a in args[3:]]),
        out_specs=[row_spec, state_spec(H_B, dk, dv), state_spec(H_B, dk),
                   state_spec(H_B, LANES), state_spec(CONV_B - 1, qk_w)],
        out_shape=[jax.ShapeDtypeStruct((t, d), F32),
                   jax.ShapeDtypeStruct((bsz, H_B, dk, dv), F32),
                   jax.ShapeDtypeStruct((bsz, H_B, dk), F32),
                   jax.ShapeDtypeStruct((bsz, H_B, LANES), F32),
                   jax.ShapeDtypeStruct((bsz, CONV_B - 1, qk_w), F32)],
        compiler_params=_params(2),
        name="mlstm_fused",
    )(*args)


def _trunk(x, state, w, *, nb, ls, ls_ffn, tm, nb_step=None):
    bsz, seq, d = x.shape
    x2 = x.reshape(bsz * seq, d)
    fresh = state is None
    seq_chunk = min(seq, CHUNK_A)

    ws = w["a_ws"] if seq_chunk == CHUNK_A else jnp.tile(
        w["a_ws"][:, :seq_chunk, :seq_chunk], (1, CHUNK_A // seq_chunk, CHUNK_A // seq_chunk))
    bs = jnp.tile(w["a_bs_t"][:seq_chunk], (CHUNK_A // seq_chunk, 1))
    res = _mixer_a(x2, w["norm_mix_g"][0:1], w["a_w_in"], w["a_ln_g"], w["a_ln_b"], ws, bs,
                   w["a_w_out"], tm=tm, seq_chunk=seq_chunk, emit_v=not fresh)
    x2 = res[0]
    v_rows = None if fresh else res[1]
    x2, fbuf0 = _ffn(x2, w["norm_ffn_g"][0:1], w["f_w_up"], w["f_conv_w"][0],
                     w["f_conv_b"][0:1], w["f_w_down"],
                     None if fresh else state["ffn_conv"][0], None,
                     layer=0, bsz=bsz, seq=seq, nb=nb, ls=ls_ffn)

    if fresh:
        x2, c_new, n_new, m_new, mconv = _mlstm_fused(
            x2, w["norm_mix_g"][1:2], w["b_w_all"], w["b_w_if"],
            w["b_conv_w"], w["b_conv_b"], w["b_bias_if"], w["b_gn_g"], w["b_w_out"],
            bsz=bsz, seq=seq, ls=ls, k_scale=w["k_scale"])
        m_new = m_new[:, :, 0]
    else:
        q, k, v, o, ig, lf, mconv = _mlstm_proj(
            x2, w["norm_mix_g"][1:2], w["b_w_all"], w["b_w_i"],
            w["b_w_f"], w["b_conv_w"], w["b_conv_b"], w["b_bias_i"], w["b_bias_f"],
            state["mlstm_conv"], d_b=w["b_w_out"].shape[0], bsz=bsz, seq=seq, nb=nb, ls=ls, k_scale=w["k_scale"])
        n0 = jnp.transpose(state["mlstm_n"], (1, 0, 2))
        m0 = jnp.pad(state["mlstm_m"], ((0, 0), (0, LANES - H_B)))
        h, c_new, n_new, m_new = _mlstm_step(q, k, v, ig, lf, state["mlstm_C"], n0, m0, bsz=bsz,
                                             nb=nb_step, ls=seq)
        n_new = jnp.transpose(n_new, (1, 0, 2))
        m_new = m_new[:, :H_B]
        x2 = _mlstm_out(h, o, x2, w["b_gn_g"], w["b_w_out"], tm=tm)
    y2, fbuf1 = _ffn(x2, w["norm_ffn_g"][1:2], w["f_w_up"], w["f_conv_w"][1],
                     w["f_conv_b"][1:2], w["f_w_down"],
                     None if fresh else state["ffn_conv"][1], w["final_norm_g"],
                     layer=1, bsz=bsz, seq=seq, nb=nb, ls=ls_ffn)
    return dict(
        y=y2.reshape(bsz, seq, d),
        v=None if fresh else v_rows.reshape(1, bsz, seq, -1),
        C=c_new[None],
        n=n_new[None],
        m=m_new[None],
        mconv=mconv[None],
        fconv=jnp.stack([fbuf0, fbuf1]),
    )


def kernel(x_prompt, x_sample, state_mlstm_C, state_mlstm_n, state_mlstm_m, state_mlstm_conv, state_ffn_conv, norm_mix_g, norm_ffn_g, final_norm_g, a_w_in, a_ln_g, a_ln_b, a_w_s, a_b_s, a_w_out, b_w_in, b_conv_w, b_conv_b, b_bias_i, b_bias_f, b_gn_g, b_w_out, f_w_up, f_conv_w, f_conv_b, f_w_down):
    d_ff = f_w_down.shape[1]
    qk_w = b_conv_w.shape[2]
    d_b = b_w_out.shape[1]
    dk = qk_w // (2 * H_B)
    d_a = a_w_out.shape[1]
    dg = d_a // G_A

    def pad_gate(cols):
        return jnp.pad(cols, ((0, 0), (0, LANES - H_B)))

    w_in_b = b_w_in[0]
    w = dict(
        norm_mix_g=norm_mix_g, norm_ffn_g=norm_ffn_g, final_norm_g=final_norm_g[None, :],
        a_w_in=a_w_in[0].astype(BF16), a_ln_g=a_ln_g, a_ln_b=a_ln_b,
        a_ws=a_w_s[0].astype(BF16),
        a_bs_t=jnp.repeat(jnp.transpose(a_b_s[0]), dg, axis=1),
        a_w_out=a_w_out[0].astype(BF16),
        b_w_all=b_w_in.astype(BF16),
        b_w_i=pad_gate(w_in_b[:, qk_w + 2 * d_b:qk_w + 2 * d_b + H_B]).astype(BF16),
        b_w_f=pad_gate(w_in_b[:, qk_w + 2 * d_b + H_B:]).astype(BF16),
        b_conv_w=b_conv_w[0], b_conv_b=b_conv_b,
        b_bias_i=jnp.pad(b_bias_i, ((0, 0), (0, LANES - H_B))),
        b_bias_f=jnp.pad(b_bias_f, ((0, 0), (0, LANES - H_B))),
        b_w_if=jnp.pad(w_in_b[:, qk_w + 2 * d_b:], ((0, 0), (0, LANES - 2 * H_B))).astype(BF16),
        b_bias_if=jnp.pad(jnp.concatenate([b_bias_i, b_bias_f], axis=1),
                          ((0, 0), (0, LANES - 2 * H_B))),
        b_gn_g=b_gn_g, b_w_out=b_w_out[0].astype(BF16),
        f_w_up=f_w_up.astype(BF16),
        f_conv_w=f_conv_w, f_conv_b=f_conv_b, f_w_down=f_w_down.astype(BF16),
        k_scale=float(dk) ** -0.5,
    )
    p = _trunk(x_prompt, None, w, nb=1, ls=256, ls_ffn=512, tm=512)
    state = dict(mlstm_C=state_mlstm_C[0], mlstm_n=state_mlstm_n[0], mlstm_m=state_mlstm_m[0],
                 mlstm_conv=state_mlstm_conv[0], ffn_conv=state_ffn_conv)
    s = _trunk(x_sample, state, w, nb=32, ls=x_sample.shape[1], ls_ffn=x_sample.shape[1], tm=256,
               nb_step=8)
    return (p["y"], s["y"], p["C"], p["n"], p["m"], p["mconv"], p["fconv"],
            s["v"], s["C"], s["n"], s["m"], s["mconv"], s["fconv"])
```

```python
import functools

import jax
import jax.numpy as jnp
from jax import lax
from jax.experimental import pallas as pl
from jax.experimental.pallas import tpu as pltpu

F32 = jnp.float32
BF16 = jnp.bfloat16

EPS = 1e-6
LANES = 128
SUBLANES = 8
VMEM_LIMIT_BYTES = 56 * 1024 * 1024

CHUNK_A = 128
G_A = 8
H_B = 8
CONV_B = 4
CONV_F = 3


def _const_spec(shape):
    nd = len(shape)
    return pl.BlockSpec(shape, lambda *_: (0,) * nd, pipeline_mode=pl.Buffered(1))


def _layer_spec(shape, layer):
    return pl.BlockSpec((1,) + tuple(shape[1:]), lambda *_: (layer, 0, 0),
                        pipeline_mode=pl.Buffered(1))


def _params(n_axes):
    return pltpu.CompilerParams(
        dimension_semantics=("arbitrary",) * n_axes,
        vmem_limit_bytes=VMEM_LIMIT_BYTES,
    )


def _software_pipeline(chains, n_stages):
    for t in range(len(chains) + n_stages - 1):
        for i in reversed(range(len(chains))):
            if i <= t < i + n_stages:
                next(chains[i])


def _rmsnorm(x, g):
    ms = jnp.mean(x * x, axis=-1, keepdims=True)
    return x * lax.rsqrt(ms + EPS) * g


def _expand3(c3, nb, ls):
    w = c3.shape[-1]
    if nb == 1:
        return jnp.broadcast_to(c3.reshape(1, w), (ls, w))
    return jnp.broadcast_to(c3, (nb, ls, w)).reshape(nb * ls, w)


def _expand2(c2, nb, ls):
    return _expand3(c2[:, None, :], nb, ls)


def _causal_conv(a, carry_ref, cw_ref, cb_ref, nb, ls, cols=slice(None)):
    r, c = a.shape
    kw = cw_ref.shape[0]
    rolled = [a] + [pltpu.roll(a, k, 0) for k in range(1, kw)]

    def taps(shifted):
        y = cb_ref[:, cols] + cw_ref[kw - 1:kw, cols] * shifted[0]
        for k in range(1, kw):
            y = y + cw_ref[kw - 1 - k:kw - k, cols] * shifted[k]
        return y

    grp = SUBLANES if nb == 1 else r
    grp_ls = SUBLANES if nb == 1 else ls
    tpos = lax.broadcasted_iota(jnp.int32, (grp, c), 0) & (grp_ls - 1)
    fixed = [a[0:grp]]
    for k in range(1, kw):
        sh = rolled[k][0:grp]
        for t in range(k):
            idx = kw - 1 + t - k
            prev = _expand3(carry_ref[:, idx:idx + 1, cols], nb, grp_ls)
            sh = jnp.where(tpos == t, prev, sh)
        fixed.append(sh)
    y = taps(fixed)
    if nb == 1:
        y = jnp.concatenate([y, taps([s[grp:] for s in rolled])], axis=0)
    last = rolled[kw - 1]
    if nb == 1:
        carry_ref[0, :, cols] = last[0:kw - 1, :]
    else:
        heads = last.reshape(nb, ls, c)[:, 0:kw - 1, :]
        carry_ref[0:nb - 1, :, cols] = heads[1:nb]
        carry_ref[nb - 1:nb, :, cols] = heads[0:1]
    return y


def _mixer_a_kernel(x_ref, g_ref, win_ref, lng_ref, lnb_ref, ws_ref, bs_ref, wout_ref,
                    y_ref, *rest, seq_chunk, emit_v):
    if emit_v:
        (v_ref,) = rest
    tm = x_ref.shape[0]
    d_a = lng_ref.shape[1]
    dg = d_a // G_A
    x = x_ref[...]
    h = _rmsnorm(x, g_ref[...]).astype(BF16)
    v_pre = {}
    v_chunks = []
    for g in range(G_A + 1):
        if g < G_A:
            v_pre[g] = jnp.dot(h, win_ref[:, d_a + g * dg:d_a + (g + 1) * dg],
                               preferred_element_type=F32)
        if g > 0:
            v_chunks.append(jax.nn.gelu(v_pre.pop(g - 1)))
    t_i = lax.broadcasted_iota(jnp.int32, (CHUNK_A, CHUNK_A), 0)
    s_i = lax.broadcasted_iota(jnp.int32, (CHUNK_A, CHUNK_A), 1)
    mask = (s_i <= t_i) & ((t_i // seq_chunk) == (s_i // seq_chunk))
    stats = {}
    gated = {}

    def group_stages(g):
        cols = slice(g * dg, (g + 1) * dg)
        u_pre = jnp.dot(h, win_ref[:, cols], preferred_element_type=F32)
        yield
        vn = ((v_chunks[g] - stats["mu"]) * stats["rs"]) * lng_ref[:, cols] + lnb_ref[:, cols]
        if emit_v:
            v_ref[:, cols] = vn
        vb = vn.astype(BF16)
        w = jnp.where(mask, ws_ref[g], jnp.zeros((), BF16))
        mixed = [jnp.dot(w, vb[c * CHUNK_A:(c + 1) * CHUNK_A], preferred_element_type=F32)
                 + bs_ref[:, cols] for c in range(tm // CHUNK_A)]
        yield
        u = jax.nn.gelu(u_pre)
        gated[g] = jnp.concatenate(
            [(u[c * CHUNK_A:(c + 1) * CHUNK_A] * mixed[c]).astype(BF16)
             for c in range(tm // CHUNK_A)], axis=0)
        yield

    lead = 3
    times = (0, lead, lead + 1)
    groups = [group_stages(g) for g in range(G_A)]
    y = x
    for t in range(G_A + times[-1]):
        if t == lead:
            mu = sum(jnp.sum(v, axis=-1, keepdims=True) for v in v_chunks) / d_a
            var = sum(jnp.sum((v - mu) * (v - mu), axis=-1, keepdims=True)
                      for v in v_chunks) / d_a
            stats["mu"] = mu
            stats["rs"] = lax.rsqrt(var + EPS)
        for g in reversed(range(G_A)):
            if t - g in times:
                next(groups[g])
        p, odd = divmod(t - times[-1], 2)
        if odd and 0 <= p < G_A // 2:
            lhs = jnp.concatenate([gated[2 * p], gated[2 * p + 1]], axis=1)
            y = y + jnp.dot(lhs, wout_ref[2 * p * dg:(2 * p + 2) * dg, :],
                            preferred_element_type=F32)
    y_ref[...] = y


def _mixer_a(x2, g, w_in, ln_g, ln_b, ws, bs, w_out, *, tm, seq_chunk, emit_v):
    t, d = x2.shape
    d_a = ln_g.shape[1]
    out_shape = [jax.ShapeDtypeStruct((t, d), F32)]
    out_specs = [pl.BlockSpec((tm, d), lambda i: (i, 0))]
    if emit_v:
        out_shape.append(jax.ShapeDtypeStruct((t, d_a), F32))
        out_specs.append(pl.BlockSpec((tm, d_a), lambda i: (i, 0)))
    return pl.pallas_call(
        functools.partial(_mixer_a_kernel, seq_chunk=seq_chunk, emit_v=emit_v),
        grid=(t // tm,),
        in_specs=[
            pl.BlockSpec((tm, d), lambda i: (i, 0)),
            _const_spec(g.shape), _const_spec(w_in.shape), _const_spec(ln_g.shape),
            _const_spec(ln_b.shape), _const_spec(ws.shape), _const_spec(bs.shape),
            _const_spec(w_out.shape),
        ],
        out_specs=out_specs,
        out_shape=out_shape,
        compiler_params=_params(1),
        name="mixer_a",
    )(x2, g, w_in, ln_g, ln_b, ws, bs, w_out)


def _ffn_kernel(*refs, nb, ls, zero_init, final_norm):
    refs = list(refs)
    x_ref, g_ref, wup_ref, cw_ref, cb_ref, wd_ref = refs[:6]
    pos = 6
    buf_ref = None
    if not zero_init:
        buf_ref = refs[pos]
        pos += 1
    fg_ref = None
    if final_norm:
        fg_ref = refs[pos]
        pos += 1
    y_ref, nbuf_ref = refs[pos], refs[pos + 1]

    @pl.when(pl.program_id(1) == 0)
    def _():
        if zero_init:
            nbuf_ref[...] = jnp.zeros(nbuf_ref.shape, F32)
        else:
            nbuf_ref[...] = buf_ref[...]

    x = x_ref[...]
    h = _rmsnorm(x, g_ref[...]).astype(BF16)
    d_ff = wd_ref.shape[1]
    a = jnp.dot(h, wup_ref[0, :, :d_ff], preferred_element_type=F32)
    gv = jnp.dot(h, wup_ref[0, :, d_ff:], preferred_element_type=F32)
    a_c = _causal_conv(a, nbuf_ref, cw_ref, cb_ref, nb, ls)
    act = (jax.nn.gelu(a_c) * gv).astype(BF16)
    y = x + jnp.dot(act, wd_ref[0], preferred_element_type=F32)
    if final_norm:
        y = _rmsnorm(y, fg_ref[...])
    y_ref[...] = y


def _ffn(x2, g, w_up, cw, cb, w_d, buf, final_g, *, layer, bsz, seq, nb, ls):
    t, d = x2.shape
    d_ff = w_d.shape[1]
    nj = seq // ls
    zero_init = buf is None
    final_norm = final_g is not None
    row_spec = pl.BlockSpec((nb * ls, d), lambda i, j: (i * nj + j, 0))
    buf_spec = pl.BlockSpec((nb, CONV_F - 1, d_ff), lambda i, j: (i, 0, 0))
    args = [x2, g, w_up, cw, cb, w_d]
    in_specs = [row_spec, _const_spec(g.shape), _layer_spec(w_up.shape, layer),
                _const_spec(cw.shape), _const_spec(cb.shape), _layer_spec(w_d.shape, layer)]
    if not zero_init:
        args.append(buf)
        in_specs.append(buf_spec)
    if final_norm:
        args.append(final_g)
        in_specs.append(_const_spec(final_g.shape))
    return pl.pallas_call(
        functools.partial(_ffn_kernel, nb=nb, ls=ls, zero_init=zero_init, final_norm=final_norm),
        grid=(bsz // nb, nj),
        in_specs=in_specs,
        out_specs=[row_spec, buf_spec],
        out_shape=[jax.ShapeDtypeStruct((t, d), F32),
                   jax.ShapeDtypeStruct((bsz, CONV_F - 1, d_ff), F32)],
        compiler_params=_params(2),
        name="conv_ffn",
    )(*args)


def _select_sum(sel, x):
    return jnp.dot(sel.astype(F32), x, precision=lax.Precision.HIGHEST,
                   preferred_element_type=F32)


def _log_sigmoid(x):
    return jnp.minimum(x, 0.0) - jnp.log1p(jnp.exp(-jnp.abs(x)))


def _mlstm_proj_kernel(*refs, nb, ls, zero_init, k_scale):
    refs = list(refs)
    x_ref, g_ref, wall_ref, wi_ref, wf_ref, cw_ref, cb_ref, bi_ref, bf_ref = refs[:9]
    pos = 9
    buf_ref = None
    if not zero_init:
        buf_ref = refs[pos]
        pos += 1
    q_ref, k_ref, v_ref, o_ref, ig_ref, lf_ref, nbuf_ref = refs[pos:pos + 7]

    @pl.when(pl.program_id(1) == 0)
    def _():
        if zero_init:
            nbuf_ref[...] = jnp.zeros(nbuf_ref.shape, F32)
        else:
            nbuf_ref[...] = buf_ref[...]

    h = _rmsnorm(x_ref[...], g_ref[...]).astype(BF16)
    qk_w = cw_ref.shape[1]
    d_b = v_ref.shape[1]
    qk_pre = jnp.dot(h, wall_ref[0, :, 0:qk_w], preferred_element_type=F32)
    v_ref[...] = jnp.dot(h, wall_ref[0, :, qk_w:qk_w + d_b], preferred_element_type=F32)
    o_ref[...] = jnp.dot(h, wall_ref[0, :, qk_w + d_b:qk_w + 2 * d_b],
                         preferred_element_type=F32)
    ig_ref[...] = jnp.dot(h, wi_ref[...], preferred_element_type=F32) + bi_ref[...]
    lf_ref[...] = _log_sigmoid(jnp.dot(h, wf_ref[...], preferred_element_type=F32) + bf_ref[...])
    qk = jax.nn.silu(_causal_conv(qk_pre, nbuf_ref, cw_ref, cb_ref, nb, ls))
    dq = q_ref.shape[1]
    q_ref[...] = qk[:, :dq]
    k_ref[...] = qk[:, dq:] * k_scale


def _mlstm_proj(x2, g, w_all, w_i, w_f, cw, cb, b_i, b_f, buf, *, d_b, bsz, seq, nb, ls,
                k_scale):
    t, d = x2.shape
    qk_w = cw.shape[1]
    nj = seq // ls
    zero_init = buf is None

    def rows(width):
        return pl.BlockSpec((nb * ls, width), lambda i, j: (i * nj + j, 0))

    buf_spec = pl.BlockSpec((nb, CONV_B - 1, qk_w), lambda i, j: (i, 0, 0))
    args = [x2, g, w_all, w_i, w_f, cw, cb, b_i, b_f]
    in_specs = ([rows(d), _const_spec(g.shape), _layer_spec(w_all.shape, 0)]
                + [_const_spec(a.shape) for a in args[3:]])
    if not zero_init:
        args.append(buf)
        in_specs.append(buf_spec)
    return pl.pallas_call(
        functools.partial(_mlstm_proj_kernel, nb=nb, ls=ls, zero_init=zero_init, k_scale=k_scale),
        grid=(bsz // nb, nj),
        in_specs=in_specs,
        out_specs=[rows(qk_w // 2), rows(qk_w // 2), rows(d_b), rows(d_b), rows(LANES),
                   rows(LANES), buf_spec],
        out_shape=[jax.ShapeDtypeStruct((t, qk_w // 2), F32),
                   jax.ShapeDtypeStruct((t, qk_w // 2), F32),
                   jax.ShapeDtypeStruct((t, d_b), F32),
                   jax.ShapeDtypeStruct((t, d_b), F32),
                   jax.ShapeDtypeStruct((t, LANES), F32),
                   jax.ShapeDtypeStruct((t, LANES), F32),
                   jax.ShapeDtypeStruct((bsz, CONV_B - 1, qk_w), F32)],
        compiler_params=_params(2),
        name="mlstm_proj",
    )(*args)


def _mlstm_step_kernel(q_ref, k_ref, v_ref, ig_ref, lf_ref, c0_ref, n0_ref, m0_ref,
                       h_ref, c_ref, n_ref, m_ref, *, nb, ls):
    r = nb * ls
    rc = LANES
    dk = c0_ref.shape[2]
    dv = c0_ref.shape[3]

    def pad_rows(a):
        if r == rc:
            return a
        return jnp.concatenate([a, jnp.zeros((rc - r, a.shape[1]), a.dtype)], axis=0)

    r_i = lax.broadcasted_iota(jnp.int32, (r, rc), 0)
    c_i = lax.broadcasted_iota(jnp.int32, (r, rc), 1)
    mask = (c_i <= r_i) & ((r_i // ls) == (c_i // ls))
    ig = ig_ref[...]
    b_all = _select_sum(mask, pad_rows(lf_ref[...]))
    b_t = pad_rows(b_all).T
    ig_t = pad_rows(ig).T
    m0 = m0_ref[...]
    inter_all = b_all + _expand2(m0, nb, ls)
    lane = lax.broadcasted_iota(jnp.int32, (r, LANES), 1)
    m_t_of = {}

    def output_stages(hd):
        kcols = slice(hd * dk, (hd + 1) * dk)
        vcols = slice(hd * dv, (hd + 1) * dv)
        q = q_ref[:, kcols]
        qb = q.astype(BF16)
        qc = jnp.concatenate(
            [jnp.dot(qb[b * ls:(b + 1) * ls], c0_ref[b, hd].astype(BF16),
                     preferred_element_type=F32) for b in range(nb)], axis=0)
        d = jnp.where(mask, b_all[:, hd:hd + 1] - b_t[hd:hd + 1, :] + ig_t[hd:hd + 1, :],
                      -jnp.inf)
        inter = inter_all[:, hd:hd + 1]
        m_t = jnp.maximum(inter, jnp.max(d, axis=1, keepdims=True))
        m_t_of[hd] = m_t
        w_intra = jnp.exp(d - m_t)
        w_inter = jnp.exp(inter - m_t)
        yield
        kb = pad_rows(k_ref[:, kcols].astype(BF16))
        vb = pad_rows(v_ref[:, vcols].astype(BF16))
        s = lax.dot_general(qb, kb, (((1,), (1,)), ((), ())), preferred_element_type=F32)
        sc = s * w_intra
        intra = jnp.dot(sc.astype(BF16), vb, preferred_element_type=F32)
        den_intra = jnp.sum(sc, axis=1, keepdims=True)
        qn = jnp.sum(q * _expand2(n0_ref[hd], nb, ls), axis=1, keepdims=True)
        yield
        num = w_inter * qc + intra
        den = w_inter * qn + den_intra
        h_ref[:, vcols] = num / jnp.maximum(jnp.abs(den), jnp.exp(-m_t))
        yield

    _software_pipeline([output_stages(hd) for hd in range(H_B)], 3)
    mt_all = jnp.zeros((r, LANES), F32)
    for hd in range(H_B):
        mt_all = jnp.where(lane == hd, m_t_of[hd], mt_all)

    p_r = lax.broadcasted_iota(jnp.int32, (nb, rc), 0)
    p_c = lax.broadcasted_iota(jnp.int32, (nb, rc), 1)
    pick = p_c == p_r * ls + (ls - 1)
    bl_seq = _select_sum(pick, pad_rows(b_all))
    mn_seq = _select_sum(pick, pad_rows(mt_all))
    decay = jnp.exp(bl_seq + m0 - mn_seq)
    m_ref[...] = mn_seq
    g_all = jnp.exp(_expand2(bl_seq, nb, ls) - b_all + ig - _expand2(mn_seq, nb, ls))
    col_seq = lax.broadcasted_iota(jnp.int32, (dk, rc), 1) // ls

    def state_stages(hd):
        kg = k_ref[:, hd * dk:(hd + 1) * dk] * g_all[:, hd:hd + 1]
        n_ref[hd] = decay[:, hd:hd + 1] * n0_ref[hd] + jnp.sum(kg.reshape(nb, ls, dk), axis=1)
        kg_t = pad_rows(kg).T
        vb = pad_rows(v_ref[:, hd * dv:(hd + 1) * dv].astype(BF16))
        yield
        for b in range(nb):
            lhs = jnp.where(col_seq == b, kg_t, 0.0).astype(BF16)
            c_ref[b, hd] = decay[b:b + 1, hd:hd + 1] * c0_ref[b, hd] + jnp.dot(
                lhs, vb, preferred_element_type=F32)
        yield

    _software_pipeline([state_stages(hd) for hd in range(H_B)], 2)


def _mlstm_step(q, k, v, ig, lf, c0, n0, m0, *, bsz, nb, ls):
    t = q.shape[0]
    dk = q.shape[1] // H_B
    dv = v.shape[1] // H_B

    def rows(width):
        return pl.BlockSpec((nb * ls, width), lambda i: (i, 0))

    c_spec = pl.BlockSpec((nb, H_B, dk, dv), lambda i: (i, 0, 0, 0))
    n_spec = pl.BlockSpec((H_B, nb, dk), lambda i: (0, i, 0))
    m_spec = pl.BlockSpec((nb, LANES), lambda i: (i, 0))
    return pl.pallas_call(
        functools.partial(_mlstm_step_kernel, nb=nb, ls=ls),
        grid=(bsz // nb,),
        in_specs=[rows(H_B * dk), rows(H_B * dk), rows(H_B * dv), rows(LANES), rows(LANES),
                  c_spec, n_spec, m_spec],
        out_specs=[rows(H_B * dv), c_spec, n_spec, m_spec],
        out_shape=[jax.ShapeDtypeStruct((t, H_B * dv), F32),
                   jax.ShapeDtypeStruct((bsz, H_B, dk, dv), F32),
                   jax.ShapeDtypeStruct((H_B, bsz, dk), F32),
                   jax.ShapeDtypeStruct((bsz, LANES), F32)],
        compiler_params=_params(1),
        name="mlstm_step",
    )(q, k, v, ig, lf, c0, n0, m0)


def _mlstm_out_kernel(h_ref, o_ref, x_ref, gn_ref, wout_ref, y_ref, gate_ref):
    dv = h_ref.shape[1] // H_B
    for hd in range(H_B):
        cols = slice(hd * dv, (hd + 1) * dv)
        hh = h_ref[:, cols]
        mu = jnp.mean(hh, axis=-1, keepdims=True)
        hc = hh - mu
        var = jnp.mean(hc * hc, axis=-1, keepdims=True)
        hn = hc * lax.rsqrt(var + EPS) * gn_ref[:, cols]
        gate_ref[:, cols] = (jax.nn.sigmoid(o_ref[:, cols]) * hn).astype(BF16)
    y_ref[...] = x_ref[...] + jnp.dot(gate_ref[...], wout_ref[...], preferred_element_type=F32)


def _mlstm_out(h, o, x2, gn_g, w_out, *, tm):
    t, d = x2.shape
    d_b = h.shape[1]
    return pl.pallas_call(
        _mlstm_out_kernel,
        grid=(t // tm,),
        in_specs=[pl.BlockSpec((tm, d_b), lambda i: (i, 0)),
                  pl.BlockSpec((tm, d_b), lambda i: (i, 0)),
                  pl.BlockSpec((tm, d), lambda i: (i, 0)),
                  _const_spec(gn_g.shape), _const_spec(w_out.shape)],
        out_specs=pl.BlockSpec((tm, d), lambda i: (i, 0)),
        out_shape=jax.ShapeDtypeStruct((t, d), F32),
        scratch_shapes=[pltpu.VMEM((tm, d_b), BF16)],
        compiler_params=_params(1),
        name="mlstm_out",
    )(h, o, x2, gn_g, w_out)


def _mlstm_fused_kernel(x_ref, g_ref, wall_ref, wif_ref, cw_ref, cb_ref, bif_ref,
                        gn_ref, wout_ref, y_ref, c_ref, n_ref, m_ref, nbuf_ref, *, k_scale):
    @pl.when(pl.program_id(1) == 0)
    def _():
        c_ref[...] = jnp.zeros(c_ref.shape, F32)
        n_ref[...] = jnp.zeros(n_ref.shape, F32)
        m_ref[...] = jnp.zeros(m_ref.shape, F32)
        nbuf_ref[...] = jnp.zeros(nbuf_ref.shape, F32)

    r = x_ref.shape[0]
    dk = c_ref.shape[2]
    dv = c_ref.shape[3]

    x = x_ref[...]
    h = _rmsnorm(x, g_ref[...]).astype(BF16)
    gates = jnp.dot(h, wif_ref[...], preferred_element_type=F32) + bif_ref[...]
    lf = _log_sigmoid(gates)
    r_i = lax.broadcasted_iota(jnp.int32, (r, r), 0)
    c_i = lax.broadcasted_iota(jnp.int32, (r, r), 1)
    mask = c_i <= r_i
    b_all = _select_sum(mask, lf)
    b_t = b_all.T
    ig_t = gates.T

    pair = 2 * dk
    n_pairs = H_B // 2
    qk_w = cw_ref.shape[1]

    def w_cols(base, cols):
        return wall_ref[0, :, base + cols.start:base + cols.stop]

    q_chunks = {}
    k_chunks = {}
    gated = {}

    def qk_chunk(c):
        cols = slice(c * pair, (c + 1) * pair)
        pre = jnp.dot(h, w_cols(0, cols), preferred_element_type=F32)
        return jax.nn.silu(_causal_conv(pre, nbuf_ref, cw_ref, cb_ref, 1, r, cols))

    def head_stages(hd):
        p, half = divmod(hd, 2)
        ig_col = gates[:, hd:hd + 1]
        i_row = ig_t[hd:hd + 1, :]
        b_col = b_all[:, H_B + hd:H_B + hd + 1]
        b_row = b_t[H_B + hd:H_B + hd + 1, :]
        m_prev = m_ref[0, hd:hd + 1, 0:1]
        vcols = slice(hd * dv, (hd + 1) * dv)
        vb = jnp.dot(h, w_cols(qk_w, vcols), preferred_element_type=F32).astype(BF16)
        o_pre = jnp.dot(h, w_cols(qk_w + H_B * dv, vcols), preferred_element_type=F32)
        d = jnp.where(mask, b_col - b_row + i_row, -jnp.inf)
        yield
        inter = b_col + m_prev
        m_t = jnp.maximum(inter, jnp.max(d, axis=1, keepdims=True))
        w_intra = jnp.exp(d - m_t)
        w_inter = jnp.exp(inter - m_t)
        yield
        q = q_chunks[p][:, half * dk:(half + 1) * dk]
        k = k_chunks[p][:, half * dk:(half + 1) * dk] * k_scale
        qb = q.astype(BF16)
        s = lax.dot_general(qb, k.astype(BF16), (((1,), (1,)), ((), ())),
                            preferred_element_type=F32)
        sc = s * w_intra
        intra = jnp.dot(sc.astype(BF16), vb, preferred_element_type=F32)
        den_intra = jnp.sum(sc, axis=1, keepdims=True)
        c_old = c_ref[0, hd]
        n_old = n_ref[0, hd:hd + 1, :]
        qc = jnp.dot(qb, c_old.astype(BF16), preferred_element_type=F32)
        qn = jnp.sum(q * n_old, axis=1, keepdims=True)
        yield
        num = w_inter * qc + intra
        den = w_inter * qn + den_intra
        hout = num / jnp.maximum(jnp.abs(den), jnp.exp(-m_t))
        mu = jnp.mean(hout, axis=-1, keepdims=True)
        hc = hout - mu
        var = jnp.mean(hc * hc, axis=-1, keepdims=True)
        hn = hc * lax.rsqrt(var + EPS) * gn_ref[:, vcols]
        yield
        gated[hd] = (jax.nn.sigmoid(o_pre) * hn).astype(BF16)
        b_last = b_col[r - 1:r, :]
        m_new = m_t[r - 1:r, :]
        g = jnp.exp(b_last - b_col + ig_col - m_new)
        g_row = jnp.exp(b_last - b_row + i_row - m_new)
        decay = jnp.exp(b_last + m_prev - m_new)
        yield
        kg_t = (k.T * g_row).astype(BF16)
        c_ref[0, hd] = decay * c_old + jnp.dot(kg_t, vb, preferred_element_type=F32)
        n_ref[0, hd:hd + 1, :] = decay * n_old + jnp.sum(k * g, axis=0, keepdims=True)
        m_ref[0, hd:hd + 1, :] = jnp.broadcast_to(m_new, (1, LANES))
        yield

    n_stages = 6
    gate_stage = 4
    q_chunks[0] = qk_chunk(0)
    k_chunks[0] = qk_chunk(n_pairs)
    heads = [head_stages(hd) for hd in range(H_B)]
    y = x
    for t in range(H_B + n_stages - 1):
        if t % 2 == 0 and t // 2 + 1 < n_pairs:
            q_chunks[t // 2 + 1] = qk_chunk(t // 2 + 1)
            k_chunks[t // 2 + 1] = qk_chunk(n_pairs + t // 2 + 1)
        for hd in reversed(range(H_B)):
            if hd <= t < hd + n_stages:
                next(heads[hd])
        p, odd = divmod(t - gate_stage, 2)
        if odd and 0 <= p < n_pairs:
            lhs = jnp.concatenate([gated[2 * p], gated[2 * p + 1]], axis=1)
            y = y + jnp.dot(lhs, wout_ref[2 * p * dv:(2 * p + 2) * dv, :],
                            preferred_element_type=F32)
    y_ref[...] = y


def _mlstm_fused(x2, g, w_all, w_if, cw, cb, b_if, gn_g, w_out, *, bsz, seq, ls, k_scale):
    t, d = x2.shape
    qk_w = cw.shape[1]
    d_b = w_out.shape[0]
    dk = qk_w // (2 * H_B)
    dv = d_b // H_B
    nj = seq // ls
    row_spec = pl.BlockSpec((ls, d), lambda i, j: (i * nj + j, 0))

    def state_spec(*tail):
        return pl.BlockSpec((1,) + tail, lambda i, j: (i,) + (0,) * len(tail))

    args = [x2, g, w_all, w_if, cw, cb, b_if, gn_g, w_out]
    return pl.pallas_call(
        functools.partial(_mlstm_fused_kernel, k_scale=k_scale),
        grid=(bsz, nj),
        in_specs=([row_spec, _const_spec(g.shape), _layer_spec(w_all.shape, 0)]
                  + [_const_spec(a.shape) for a in args[3:]]),
        out_specs=[row_spec, state_spec(H_B, dk, dv), state_spec(H_B, dk),
                   state_spec(H_B, LANES), state_spec(CONV_B - 1, qk_w)],
        out_shape=[jax.ShapeDtypeStruct((t, d), F32),
                   jax.ShapeDtypeStruct((bsz, H_B, dk, dv), F32),
                   jax.ShapeDtypeStruct((bsz, H_B, dk), F32),
                   jax.ShapeDtypeStruct((bsz, H_B, LANES), F32),
                   jax.ShapeDtypeStruct((bsz, CONV_B - 1, qk_w), F32)],
        compiler_params=_params(2),
        name="mlstm_fused",
    )(*args)


def _trunk(x, state, w, *, nb, ls, nb_ffn, ls_ffn, tm, nb_step=None):
    bsz, seq, d = x.shape
    x2 = x.reshape(bsz * seq, d)
    fresh = state is None
    seq_chunk = min(seq, CHUNK_A)

    ws = w["a_ws"] if seq_chunk == CHUNK_A else jnp.tile(
        w["a_ws"][:, :seq_chunk, :seq_chunk], (1, CHUNK_A // seq_chunk, CHUNK_A // seq_chunk))
    bs = jnp.tile(w["a_bs_t"][:seq_chunk], (CHUNK_A // seq_chunk, 1))
    res = _mixer_a(x2, w["norm_mix_g"][0:1], w["a_w_in"], w["a_ln_g"], w["a_ln_b"], ws, bs,
                   w["a_w_out"], tm=tm, seq_chunk=seq_chunk, emit_v=not fresh)
    x2 = res[0]
    v_rows = None if fresh else res[1]
    x2, fbuf0 = _ffn(x2, w["norm_ffn_g"][0:1], w["f_w_up"], w["f_conv_w"][0],
                     w["f_conv_b"][0:1], w["f_w_down"],
                     None if fresh else state["ffn_conv"][0], None,
                     layer=0, bsz=bsz, seq=seq, nb=nb_ffn, ls=ls_ffn)

    if fresh:
        x2, c_new, n_new, m_new, mconv = _mlstm_fused(
            x2, w["norm_mix_g"][1:2], w["b_w_all"], w["b_w_if"],
            w["b_conv_w"], w["b_conv_b"], w["b_bias_if"], w["b_gn_g"], w["b_w_out"],
            bsz=bsz, seq=seq, ls=ls, k_scale=w["k_scale"])
        m_new = m_new[:, :, 0]
    else:
        q, k, v, o, ig, lf, mconv = _mlstm_proj(
            x2, w["norm_mix_g"][1:2], w["b_w_all"], w["b_w_i"],
            w["b_w_f"], w["b_conv_w"], w["b_conv_b"], w["b_bias_i"], w["b_bias_f"],
            state["mlstm_conv"], d_b=w["b_w_out"].shape[0], bsz=bsz, seq=seq, nb=nb, ls=ls, k_scale=w["k_scale"])
        n0 = jnp.transpose(state["mlstm_n"], (1, 0, 2))
        m0 = jnp.pad(state["mlstm_m"], ((0, 0), (0, LANES - H_B)))
        h, c_new, n_new, m_new = _mlstm_step(q, k, v, ig, lf, state["mlstm_C"], n0, m0, bsz=bsz,
                                             nb=nb_step, ls=seq)
        n_new = jnp.transpose(n_new, (1, 0, 2))
        m_new = m_new[:, :H_B]
        x2 = _mlstm_out(h, o, x2, w["b_gn_g"], w["b_w_out"], tm=tm)
    y2, fbuf1 = _ffn(x2, w["norm_ffn_g"][1:2], w["f_w_up"], w["f_conv_w"][1],
                     w["f_conv_b"][1:2], w["f_w_down"],
                     None if fresh else state["ffn_conv"][1], w["final_norm_g"],
                     layer=1, bsz=bsz, seq=seq, nb=nb_ffn, ls=ls_ffn)
    return dict(
        y=y2.reshape(bsz, seq, d),
        v=None if fresh else v_rows.reshape(1, bsz, seq, -1),
        C=c_new[None],
        n=n_new[None],
        m=m_new[None],
        mconv=mconv[None],
        fconv=jnp.stack([fbuf0, fbuf1]),
    )


def kernel(x_prompt, x_sample, state_mlstm_C, state_mlstm_n, state_mlstm_m, state_mlstm_conv, state_ffn_conv, norm_mix_g, norm_ffn_g, final_norm_g, a_w_in, a_ln_g, a_ln_b, a_w_s, a_b_s, a_w_out, b_w_in, b_conv_w, b_conv_b, b_bias_i, b_bias_f, b_gn_g, b_w_out, f_w_up, f_conv_w, f_conv_b, f_w_down):
    d_ff = f_w_down.shape[1]
    qk_w = b_conv_w.shape[2]
    d_b = b_w_out.shape[1]
    dk = qk_w // (2 * H_B)
    d_a = a_w_out.shape[1]
    dg = d_a // G_A

    def pad_gate(cols):
        return jnp.pad(cols, ((0, 0), (0, LANES - H_B)))

    w_in_b = b_w_in[0]
    w = dict(
        norm_mix_g=norm_mix_g, norm_ffn_g=norm_ffn_g, final_norm_g=final_norm_g[None, :],
        a_w_in=a_w_in[0].astype(BF16), a_ln_g=a_ln_g, a_ln_b=a_ln_b,
        a_ws=a_w_s[0].astype(BF16),
        a_bs_t=jnp.repeat(jnp.transpose(a_b_s[0]), dg, axis=1),
        a_w_out=a_w_out[0].astype(BF16),
        b_w_all=b_w_in.astype(BF16),
        b_w_i=pad_gate(w_in_b[:, qk_w + 2 * d_b:qk_w + 2 * d_b + H_B]).astype(BF16),
        b_w_f=pad_gate(w_in_b[:, qk_w + 2 * d_b + H_B:]).astype(BF16),
        b_conv_w=b_conv_w[0], b_conv_b=b_conv_b,
        b_bias_i=jnp.pad(b_bias_i, ((0, 0), (0, LANES - H_B))),
        b_bias_f=jnp.pad(b_bias_f, ((0, 0), (0, LANES - H_B))),
        b_w_if=jnp.pad(w_in_b[:, qk_w + 2 * d_b:], ((0, 0), (0, LANES - 2 * H_B))).astype(BF16),
        b_bias_if=jnp.pad(jnp.concatenate([b_bias_i, b_bias_f], axis=1),
                          ((0, 0), (0, LANES - 2 * H_B))),
        b_gn_g=b_gn_g, b_w_out=b_w_out[0].astype(BF16),
        f_w_up=f_w_up.astype(BF16),
        f_conv_w=f_conv_w, f_conv_b=f_conv_b, f_w_down=f_w_down.astype(BF16),
        k_scale=float(dk) ** -0.5,
    )
    p = _trunk(x_prompt, None, w, nb=1, ls=256, nb_ffn=1, ls_ffn=512, tm=512)
    state = dict(mlstm_C=state_mlstm_C[0], mlstm_n=state_mlstm_n[0], mlstm_m=state_mlstm_m[0],
                 mlstm_conv=state_mlstm_conv[0], ffn_conv=state_ffn_conv)
    s = _trunk(x_sample, state, w, nb=32, ls=x_sample.shape[1], nb_ffn=64,
               ls_ffn=x_sample.shape[1], tm=512, nb_step=8)
    return (p["y"], s["y"], p["C"], p["n"], p["m"], p["mconv"], p["fconv"],
            s["v"], s["C"], s["n"], s["m"], s["mconv"], s["fconv"])
```

```python
import functools

import jax
import jax.numpy as jnp
from jax import lax
from jax.experimental import pallas as pl
from jax.experimental.pallas import tpu as pltpu

F32 = jnp.float32
BF16 = jnp.bfloat16

EPS = 1e-6
LANES = 128
SUBLANES = 8
VMEM_LIMIT_BYTES = 56 * 1024 * 1024

CHUNK_A = 128
G_A = 8
H_B = 8
CONV_B = 4
CONV_F = 3

PROMPT_ROWS = 512
PROMPT_MLSTM_ROWS = 256
SAMPLE_ROWS = 256
STEP_SEQS = 8


def _const_spec(shape):
    nd = len(shape)
    return pl.BlockSpec(shape, lambda *_: (0,) * nd, pipeline_mode=pl.Buffered(1))


def _layer_spec(shape, layer):
    return pl.BlockSpec((1,) + tuple(shape[1:]), lambda *_: (layer, 0, 0),
                        pipeline_mode=pl.Buffered(1))


def _params(n_axes):
    return pltpu.CompilerParams(
        dimension_semantics=("arbitrary",) * n_axes,
        vmem_limit_bytes=VMEM_LIMIT_BYTES,
    )


def _software_pipeline(chains, n_stages):
    for t in range(len(chains) + n_stages - 1):
        for i in reversed(range(len(chains))):
            if i <= t < i + n_stages:
                next(chains[i])


def _rmsnorm(x, g):
    ms = jnp.mean(x * x, axis=-1, keepdims=True)
    return x * lax.rsqrt(ms + EPS) * g


def _expand3(c3, nb, ls):
    w = c3.shape[-1]
    if nb == 1:
        return jnp.broadcast_to(c3.reshape(1, w), (ls, w))
    return jnp.broadcast_to(c3, (nb, ls, w)).reshape(nb * ls, w)


def _expand2(c2, nb, ls):
    return _expand3(c2[:, None, :], nb, ls)


def _causal_conv(a, carry_ref, cw_ref, cb_ref, nb, ls, cols=slice(None)):
    r, c = a.shape
    kw = cw_ref.shape[0]
    rolled = [a] + [pltpu.roll(a, k, 0) for k in range(1, kw)]

    def taps(shifted):
        y = cb_ref[:, cols] + cw_ref[kw - 1:kw, cols] * shifted[0]
        for k in range(1, kw):
            y = y + cw_ref[kw - 1 - k:kw - k, cols] * shifted[k]
        return y

    grp = SUBLANES if nb == 1 else r
    grp_ls = SUBLANES if nb == 1 else ls
    tpos = lax.broadcasted_iota(jnp.int32, (grp, c), 0) & (grp_ls - 1)
    fixed = [a[0:grp]]
    for k in range(1, kw):
        sh = rolled[k][0:grp]
        for t in range(k):
            idx = kw - 1 + t - k
            prev = _expand3(carry_ref[:, idx:idx + 1, cols], nb, grp_ls)
            sh = jnp.where(tpos == t, prev, sh)
        fixed.append(sh)
    y = taps(fixed)
    if nb == 1:
        y = jnp.concatenate([y, taps([s[grp:] for s in rolled])], axis=0)
    last = rolled[kw - 1]
    if nb == 1:
        carry_ref[0, :, cols] = last[0:kw - 1, :]
    else:
        heads = last.reshape(nb, ls, c)[:, 0:kw - 1, :]
        carry_ref[0:nb - 1, :, cols] = heads[1:nb]
        carry_ref[nb - 1:nb, :, cols] = heads[0:1]
    return y


def _mixer_a_kernel(x_ref, g_ref, win_ref, lng_ref, lnb_ref, ws_ref, bs_ref, wout_ref,
                    y_ref, *rest, seq_chunk, emit_v):
    if emit_v:
        (v_ref,) = rest
    tm = x_ref.shape[0]
    d_a = lng_ref.shape[1]
    dg = d_a // G_A
    x = x_ref[...]
    h = _rmsnorm(x, g_ref[...]).astype(BF16)
    v_pre = {}
    v_chunks = []
    ahead = 1
    for g in range(G_A + ahead):
        if g < G_A:
            v_pre[g] = jnp.dot(h, win_ref[:, d_a + g * dg:d_a + (g + 1) * dg],
                               preferred_element_type=F32)
        if g >= ahead:
            v_chunks.append(jax.nn.gelu(v_pre.pop(g - ahead)))
    t_i = lax.broadcasted_iota(jnp.int32, (CHUNK_A, CHUNK_A), 0)
    s_i = lax.broadcasted_iota(jnp.int32, (CHUNK_A, CHUNK_A), 1)
    mask = (s_i <= t_i) & ((t_i // seq_chunk) == (s_i // seq_chunk))
    stats = {}
    gated = {}

    def group_stages(g):
        cols = slice(g * dg, (g + 1) * dg)
        u_pre = jnp.dot(h, win_ref[:, cols], preferred_element_type=F32)
        yield
        vn = ((v_chunks[g] - stats["mu"]) * stats["rs"]) * lng_ref[:, cols] + lnb_ref[:, cols]
        if emit_v:
            v_ref[:, cols] = vn
        vb = vn.astype(BF16)
        w = jnp.where(mask, ws_ref[g], jnp.zeros((), BF16))
        mixed = [jnp.dot(w, vb[c * CHUNK_A:(c + 1) * CHUNK_A], preferred_element_type=F32)
                 + bs_ref[:, cols] for c in range(tm // CHUNK_A)]
        yield
        u = jax.nn.gelu(u_pre)
        gated[g] = jnp.concatenate(
            [(u[c * CHUNK_A:(c + 1) * CHUNK_A] * mixed[c]).astype(BF16)
             for c in range(tm // CHUNK_A)], axis=0)
        yield

    lead = 3
    times = (0, lead, lead + 1)
    groups = [group_stages(g) for g in range(G_A)]
    y = x
    for t in range(G_A + times[-1]):
        if t == lead:
            mu = sum(jnp.sum(v, axis=-1, keepdims=True) for v in v_chunks) / d_a
            var = sum(jnp.sum((v - mu) * (v - mu), axis=-1, keepdims=True)
                      for v in v_chunks) / d_a
            stats["mu"] = mu
            stats["rs"] = lax.rsqrt(var + EPS)
        for g in reversed(range(G_A)):
            if t - g in times:
                next(groups[g])
        p, odd = divmod(t - times[-1], 2)
        if odd and 0 <= p < G_A // 2:
            lhs = jnp.concatenate([gated[2 * p], gated[2 * p + 1]], axis=1)
            y = y + jnp.dot(lhs, wout_ref[2 * p * dg:(2 * p + 2) * dg, :],
                            preferred_element_type=F32)
    y_ref[...] = y


def _mixer_a(x2, g, w_in, ln_g, ln_b, ws, bs, w_out, *, tm, seq_chunk, emit_v):
    t, d = x2.shape
    d_a = ln_g.shape[1]
    out_shape = [jax.ShapeDtypeStruct((t, d), F32)]
    out_specs = [pl.BlockSpec((tm, d), lambda i: (i, 0))]
    if emit_v:
        out_shape.append(jax.ShapeDtypeStruct((t, d_a), F32))
        out_specs.append(pl.BlockSpec((tm, d_a), lambda i: (i, 0)))
    return pl.pallas_call(
        functools.partial(_mixer_a_kernel, seq_chunk=seq_chunk, emit_v=emit_v),
        grid=(t // tm,),
        in_specs=[
            pl.BlockSpec((tm, d), lambda i: (i, 0)),
            _const_spec(g.shape), _const_spec(w_in.shape), _const_spec(ln_g.shape),
            _const_spec(ln_b.shape), _const_spec(ws.shape), _const_spec(bs.shape),
            _const_spec(w_out.shape),
        ],
        out_specs=out_specs,
        out_shape=out_shape,
        compiler_params=_params(1),
        name="mixer_a",
    )(x2, g, w_in, ln_g, ln_b, ws, bs, w_out)


def _ffn_tile(x_ref, y_ref, nbuf_ref, g_ref, wup_ref, cw_ref, cb_ref, wd_ref, fg_ref, nb, ls):
    x = x_ref[...]
    h = _rmsnorm(x, g_ref[...]).astype(BF16)
    d_ff = wd_ref.shape[1]
    a = jnp.dot(h, wup_ref[0, :, :d_ff], preferred_element_type=F32)
    gv = jnp.dot(h, wup_ref[0, :, d_ff:], preferred_element_type=F32)
    a_c = _causal_conv(a, nbuf_ref, cw_ref, cb_ref, nb, ls)
    act = (jax.nn.gelu(a_c) * gv).astype(BF16)
    y = x + jnp.dot(act, wd_ref[0], preferred_element_type=F32)
    if fg_ref is not None:
        y = _rmsnorm(y, fg_ref[...])
    y_ref[...] = y


def _ffn_kernel(*refs, fresh_steps, fresh_nj, fresh_ls, carried_nb, carried_ls, final_norm):
    refs = list(refs)
    xf_ref, xc_ref, g_ref, wup_ref, cw_ref, cb_ref, wd_ref, buf_ref = refs[:8]
    pos = 8
    fg_ref = None
    if final_norm:
        fg_ref = refs[pos]
        pos += 1
    yf_ref, yc_ref, nbuff_ref, nbufc_ref = refs[pos:pos + 4]
    s = pl.program_id(0)
    shared = (g_ref, wup_ref, cw_ref, cb_ref, wd_ref, fg_ref)

    @pl.when(s < fresh_steps)
    def _():
        @pl.when(s % fresh_nj == 0)
        def _():
            nbuff_ref[...] = jnp.zeros(nbuff_ref.shape, F32)

        _ffn_tile(xf_ref, yf_ref, nbuff_ref, *shared, 1, fresh_ls)

    @pl.when(s >= fresh_steps)
    def _():
        nbufc_ref[...] = buf_ref[...]
        _ffn_tile(xc_ref, yc_ref, nbufc_ref, *shared, carried_nb, carried_ls)


def _ffn(xf2, xc2, g, w_up, cw, cb, w_d, buf, final_g, *, layer, fresh_bsz, fresh_seq, fresh_ls,
         carried_nb, carried_ls):
    tf, d = xf2.shape
    tc = xc2.shape[0]
    d_ff = w_d.shape[1]
    nj = fresh_seq // fresh_ls
    fs = fresh_bsz * nj
    cs = tc // (carried_nb * carried_ls)
    final_norm = final_g is not None
    f_rows = pl.BlockSpec((fresh_ls, d), lambda s: (jnp.minimum(s, fs - 1), 0))
    c_rows = pl.BlockSpec((carried_nb * carried_ls, d), lambda s: (jnp.maximum(s - fs, 0), 0))
    f_buf = pl.BlockSpec((1, CONV_F - 1, d_ff), lambda s: (jnp.minimum(s, fs - 1) // nj, 0, 0))
    c_buf = pl.BlockSpec((carried_nb, CONV_F - 1, d_ff), lambda s: (jnp.maximum(s - fs, 0), 0, 0))
    args = [xf2, xc2, g, w_up, cw, cb, w_d, buf]
    in_specs = [f_rows, c_rows, _const_spec(g.shape), _layer_spec(w_up.shape, layer),
                _const_spec(cw.shape), _const_spec(cb.shape), _layer_spec(w_d.shape, layer), c_buf]
    if final_norm:
        args.append(final_g)
        in_specs.append(_const_spec(final_g.shape))
    return pl.pallas_call(
        functools.partial(_ffn_kernel, fresh_steps=fs, fresh_nj=nj, fresh_ls=fresh_ls,
                          carried_nb=carried_nb, carried_ls=carried_ls, final_norm=final_norm),
        grid=(fs + cs,),
        in_specs=in_specs,
        out_specs=[f_rows, c_rows, f_buf, c_buf],
        out_shape=[jax.ShapeDtypeStruct((tf, d), F32),
                   jax.ShapeDtypeStruct((tc, d), F32),
                   jax.ShapeDtypeStruct((fresh_bsz, CONV_F - 1, d_ff), F32),
                   jax.ShapeDtypeStruct((tc // carried_ls, CONV_F - 1, d_ff), F32)],
        compiler_params=_params(1),
        name="conv_ffn",
    )(*args)


def _select_sum(sel, x):
    return jnp.dot(sel.astype(F32), x, precision=lax.Precision.HIGHEST,
                   preferred_element_type=F32)


def _log_sigmoid(x):
    return jnp.minimum(x, 0.0) - jnp.log1p(jnp.exp(-jnp.abs(x)))


def _mlstm_proj_kernel(*refs, nb, ls, zero_init, k_scale):
    refs = list(refs)
    x_ref, g_ref, wall_ref, wi_ref, wf_ref, cw_ref, cb_ref, bi_ref, bf_ref = refs[:9]
    pos = 9
    buf_ref = None
    if not zero_init:
        buf_ref = refs[pos]
        pos += 1
    q_ref, k_ref, v_ref, o_ref, ig_ref, lf_ref, nbuf_ref = refs[pos:pos + 7]

    @pl.when(pl.program_id(1) == 0)
    def _():
        if zero_init:
            nbuf_ref[...] = jnp.zeros(nbuf_ref.shape, F32)
        else:
            nbuf_ref[...] = buf_ref[...]

    h = _rmsnorm(x_ref[...], g_ref[...]).astype(BF16)
    qk_w = cw_ref.shape[1]
    d_b = v_ref.shape[1]
    qk_pre = jnp.dot(h, wall_ref[0, :, 0:qk_w], preferred_element_type=F32)
    v_ref[...] = jnp.dot(h, wall_ref[0, :, qk_w:qk_w + d_b], preferred_element_type=F32)
    o_ref[...] = jnp.dot(h, wall_ref[0, :, qk_w + d_b:qk_w + 2 * d_b],
                         preferred_element_type=F32)
    ig_ref[...] = jnp.dot(h, wi_ref[...], preferred_element_type=F32) + bi_ref[...]
    lf_ref[...] = _log_sigmoid(jnp.dot(h, wf_ref[...], preferred_element_type=F32) + bf_ref[...])
    qk = jax.nn.silu(_causal_conv(qk_pre, nbuf_ref, cw_ref, cb_ref, nb, ls))
    dq = q_ref.shape[1]
    q_ref[...] = qk[:, :dq]
    k_ref[...] = qk[:, dq:] * k_scale


def _mlstm_proj(x2, g, w_all, w_i, w_f, cw, cb, b_i, b_f, buf, *, d_b, bsz, seq, nb, ls,
                k_scale):
    t, d = x2.shape
    qk_w = cw.shape[1]
    nj = seq // ls
    zero_init = buf is None

    def rows(width):
        return pl.BlockSpec((nb * ls, width), lambda i, j: (i * nj + j, 0))

    buf_spec = pl.BlockSpec((nb, CONV_B - 1, qk_w), lambda i, j: (i, 0, 0))
    args = [x2, g, w_all, w_i, w_f, cw, cb, b_i, b_f]
    in_specs = ([rows(d), _const_spec(g.shape), _layer_spec(w_all.shape, 0)]
                + [_const_spec(a.shape) for a in args[3:]])
    if not zero_init:
        args.append(buf)
        in_specs.append(buf_spec)
    return pl.pallas_call(
        functools.partial(_mlstm_proj_kernel, nb=nb, ls=ls, zero_init=zero_init, k_scale=k_scale),
        grid=(bsz // nb, nj),
        in_specs=in_specs,
        out_specs=[rows(qk_w // 2), rows(qk_w // 2), rows(d_b), rows(d_b), rows(LANES),
                   rows(LANES), buf_spec],
        out_shape=[jax.ShapeDtypeStruct((t, qk_w // 2), F32),
                   jax.ShapeDtypeStruct((t, qk_w // 2), F32),
                   jax.ShapeDtypeStruct((t, d_b), F32),
                   jax.ShapeDtypeStruct((t, d_b), F32),
                   jax.ShapeDtypeStruct((t, LANES), F32),
                   jax.ShapeDtypeStruct((t, LANES), F32),
                   jax.ShapeDtypeStruct((bsz, CONV_B - 1, qk_w), F32)],
        compiler_params=_params(2),
        name="mlstm_proj",
    )(*args)


def _mlstm_step_kernel(q_ref, k_ref, v_ref, ig_ref, lf_ref, c0_ref, n0_ref, m0_ref,
                       h_ref, c_ref, n_ref, m_ref, *, nb, ls):
    r = nb * ls
    rc = LANES
    dk = c0_ref.shape[2]
    dv = c0_ref.shape[3]

    def pad_rows(a):
        if r == rc:
            return a
        return jnp.concatenate([a, jnp.zeros((rc - r, a.shape[1]), a.dtype)], axis=0)

    r_i = lax.broadcasted_iota(jnp.int32, (r, rc), 0)
    c_i = lax.broadcasted_iota(jnp.int32, (r, rc), 1)
    mask = (c_i <= r_i) & ((r_i // ls) == (c_i // ls))
    ig = ig_ref[...]
    b_all = _select_sum(mask, pad_rows(lf_ref[...]))
    b_t = pad_rows(b_all).T
    ig_t = pad_rows(ig).T
    m0 = m0_ref[...]
    inter_all = b_all + _expand2(m0, nb, ls)
    lane = lax.broadcasted_iota(jnp.int32, (r, LANES), 1)
    m_t_of = {}

    def output_stages(hd):
        kcols = slice(hd * dk, (hd + 1) * dk)
        vcols = slice(hd * dv, (hd + 1) * dv)
        q = q_ref[:, kcols]
        qb = q.astype(BF16)
        qc = jnp.concatenate(
            [jnp.dot(qb[b * ls:(b + 1) * ls], c0_ref[b, hd].astype(BF16),
                     preferred_element_type=F32) for b in range(nb)], axis=0)
        d = jnp.where(mask, b_all[:, hd:hd + 1] - b_t[hd:hd + 1, :] + ig_t[hd:hd + 1, :],
                      -jnp.inf)
        inter = inter_all[:, hd:hd + 1]
        m_t = jnp.maximum(inter, jnp.max(d, axis=1, keepdims=True))
        m_t_of[hd] = m_t
        w_intra = jnp.exp(d - m_t)
        w_inter = jnp.exp(inter - m_t)
        yield
        kb = pad_rows(k_ref[:, kcols].astype(BF16))
        vb = pad_rows(v_ref[:, vcols].astype(BF16))
        s = lax.dot_general(qb, kb, (((1,), (1,)), ((), ())), preferred_element_type=F32)
        sc = s * w_intra
        intra = jnp.dot(sc.astype(BF16), vb, preferred_element_type=F32)
        den_intra = jnp.sum(sc, axis=1, keepdims=True)
        qn = jnp.sum(q * _expand2(n0_ref[hd], nb, ls), axis=1, keepdims=True)
        yield
        num = w_inter * qc + intra
        den = w_inter * qn + den_intra
        h_ref[:, vcols] = num / jnp.maximum(jnp.abs(den), jnp.exp(-m_t))
        yield

    _software_pipeline([output_stages(hd) for hd in range(H_B)], 3)
    mt_all = jnp.zeros((r, LANES), F32)
    for hd in range(H_B):
        mt_all = jnp.where(lane == hd, m_t_of[hd], mt_all)

    p_r = lax.broadcasted_iota(jnp.int32, (nb, rc), 0)
    p_c = lax.broadcasted_iota(jnp.int32, (nb, rc), 1)
    pick = p_c == p_r * ls + (ls - 1)
    bl_seq = _select_sum(pick, pad_rows(b_all))
    mn_seq = _select_sum(pick, pad_rows(mt_all))
    decay = jnp.exp(bl_seq + m0 - mn_seq)
    m_ref[...] = mn_seq
    g_all = jnp.exp(_expand2(bl_seq, nb, ls) - b_all + ig - _expand2(mn_seq, nb, ls))
    col_seq = lax.broadcasted_iota(jnp.int32, (dk, rc), 1) // ls

    def state_stages(hd):
        kg = k_ref[:, hd * dk:(hd + 1) * dk] * g_all[:, hd:hd + 1]
        n_ref[hd] = decay[:, hd:hd + 1] * n0_ref[hd] + jnp.sum(kg.reshape(nb, ls, dk), axis=1)
        kg_t = pad_rows(kg).T
        vb = pad_rows(v_ref[:, hd * dv:(hd + 1) * dv].astype(BF16))
        yield
        for b in range(nb):
            lhs = jnp.where(col_seq == b, kg_t, 0.0).astype(BF16)
            c_ref[b, hd] = decay[b:b + 1, hd:hd + 1] * c0_ref[b, hd] + jnp.dot(
                lhs, vb, preferred_element_type=F32)
        yield

    _software_pipeline([state_stages(hd) for hd in range(H_B)], 2)


def _mlstm_step(q, k, v, ig, lf, c0, n0, m0, *, bsz, nb, ls):
    t = q.shape[0]
    dk = q.shape[1] // H_B
    dv = v.shape[1] // H_B

    def rows(width):
        return pl.BlockSpec((nb * ls, width), lambda i: (i, 0))

    c_spec = pl.BlockSpec((nb, H_B, dk, dv), lambda i: (i, 0, 0, 0))
    n_spec = pl.BlockSpec((H_B, nb, dk), lambda i: (0, i, 0))
    m_spec = pl.BlockSpec((nb, LANES), lambda i: (i, 0))
    return pl.pallas_call(
        functools.partial(_mlstm_step_kernel, nb=nb, ls=ls),
        grid=(bsz // nb,),
        in_specs=[rows(H_B * dk), rows(H_B * dk), rows(H_B * dv), rows(LANES), rows(LANES),
                  c_spec, n_spec, m_spec],
        out_specs=[rows(H_B * dv), c_spec, n_spec, m_spec],
        out_shape=[jax.ShapeDtypeStruct((t, H_B * dv), F32),
                   jax.ShapeDtypeStruct((bsz, H_B, dk, dv), F32),
                   jax.ShapeDtypeStruct((H_B, bsz, dk), F32),
                   jax.ShapeDtypeStruct((bsz, LANES), F32)],
        compiler_params=_params(1),
        name="mlstm_step",
    )(q, k, v, ig, lf, c0, n0, m0)


def _mlstm_out_kernel(h_ref, o_ref, x_ref, gn_ref, wout_ref, y_ref, gate_ref):
    dv = h_ref.shape[1] // H_B
    for hd in range(H_B):
        cols = slice(hd * dv, (hd + 1) * dv)
        hh = h_ref[:, cols]
        mu = jnp.mean(hh, axis=-1, keepdims=True)
        hc = hh - mu
        var = jnp.mean(hc * hc, axis=-1, keepdims=True)
        hn = hc * lax.rsqrt(var + EPS) * gn_ref[:, cols]
        gate_ref[:, cols] = (jax.nn.sigmoid(o_ref[:, cols]) * hn).astype(BF16)
    y_ref[...] = x_ref[...] + jnp.dot(gate_ref[...], wout_ref[...], preferred_element_type=F32)


def _mlstm_out(h, o, x2, gn_g, w_out, *, tm):
    t, d = x2.shape
    d_b = h.shape[1]
    return pl.pallas_call(
        _mlstm_out_kernel,
        grid=(t // tm,),
        in_specs=[pl.BlockSpec((tm, d_b), lambda i: (i, 0)),
                  pl.BlockSpec((tm, d_b), lambda i: (i, 0)),
                  pl.BlockSpec((tm, d), lambda i: (i, 0)),
                  _const_spec(gn_g.shape), _const_spec(w_out.shape)],
        out_specs=pl.BlockSpec((tm, d), lambda i: (i, 0)),
        out_shape=jax.ShapeDtypeStruct((t, d), F32),
        scratch_shapes=[pltpu.VMEM((tm, d_b), BF16)],
        compiler_params=_params(1),
        name="mlstm_out",
    )(h, o, x2, gn_g, w_out)


def _mlstm_fused_kernel(x_ref, g_ref, wall_ref, wif_ref, cw_ref, cb_ref, bif_ref,
                        gn_ref, wout_ref, y_ref, c_ref, n_ref, m_ref, nbuf_ref, *, k_scale):
    @pl.when(pl.program_id(1) == 0)
    def _():
        c_ref[...] = jnp.zeros(c_ref.shape, F32)
        n_ref[...] = jnp.zeros(n_ref.shape, F32)
        m_ref[...] = jnp.zeros(m_ref.shape, F32)
        nbuf_ref[...] = jnp.zeros(nbuf_ref.shape, F32)

    r = x_ref.shape[0]
    dk = c_ref.shape[2]
    dv = c_ref.shape[3]

    x = x_ref[...]
    h = _rmsnorm(x, g_ref[...]).astype(BF16)
    gates = jnp.dot(h, wif_ref[...], preferred_element_type=F32) + bif_ref[...]
    r_i = lax.broadcasted_iota(jnp.int32, (r, r), 0)
    c_i = lax.broadcasted_iota(jnp.int32, (r, r), 1)
    mask = c_i <= r_i
    ig_t = gates.T
    b_t = jnp.dot(_log_sigmoid(ig_t), (r_i <= c_i).astype(F32),
                  precision=lax.Precision.HIGHEST, preferred_element_type=F32)
    b_all = b_t.T

    pair = 2 * dk
    n_pairs = H_B // 2
    qk_w = cw_ref.shape[1]

    def w_cols(base, cols):
        return wall_ref[0, :, base + cols.start:base + cols.stop]

    q_chunks = {}
    k_chunks = {}
    gated = {}

    def qk_chunk(c):
        cols = slice(c * pair, (c + 1) * pair)
        pre = jnp.dot(h, w_cols(0, cols), preferred_element_type=F32)
        return jax.nn.silu(_causal_conv(pre, nbuf_ref, cw_ref, cb_ref, 1, r, cols))

    def head_stages(hd):
        p, half = divmod(hd, 2)
        ig_col = gates[:, hd:hd + 1]
        i_row = ig_t[hd:hd + 1, :]
        b_col = b_all[:, H_B + hd:H_B + hd + 1]
        b_row = b_t[H_B + hd:H_B + hd + 1, :]
        m_prev = m_ref[0, hd:hd + 1, 0:1]
        vcols = slice(hd * dv, (hd + 1) * dv)
        vb = jnp.dot(h, w_cols(qk_w, vcols), preferred_element_type=F32).astype(BF16)
        o_pre = jnp.dot(h, w_cols(qk_w + H_B * dv, vcols), preferred_element_type=F32)
        d = jnp.where(mask, b_col - b_row + i_row, -jnp.inf)
        yield
        inter = b_col + m_prev
        m_t = jnp.maximum(inter, jnp.max(d, axis=1, keepdims=True))
        w_intra = jnp.exp(d - m_t)
        w_inter = jnp.exp(inter - m_t)
        yield
        q = q_chunks[p][:, half * dk:(half + 1) * dk]
        k = k_chunks[p][:, half * dk:(half + 1) * dk] * k_scale
        qb = q.astype(BF16)
        s = lax.dot_general(qb, k.astype(BF16), (((1,), (1,)), ((), ())),
                            preferred_element_type=F32)
        sc = s * w_intra
        intra = jnp.dot(sc.astype(BF16), vb, preferred_element_type=F32)
        den_intra = jnp.sum(sc, axis=1, keepdims=True)
        c_old = c_ref[0, hd]
        n_old = n_ref[0, hd:hd + 1, :]
        qc = jnp.dot(qb, c_old.astype(BF16), preferred_element_type=F32)
        qn = jnp.sum(q * n_old, axis=1, keepdims=True)
        yield
        num = w_inter * qc + intra
        den = w_inter * qn + den_intra
        hout = num / jnp.maximum(jnp.abs(den), jnp.exp(-m_t))
        mu = jnp.mean(hout, axis=-1, keepdims=True)
        hc = hout - mu
        var = jnp.mean(hc * hc, axis=-1, keepdims=True)
        hn = hc * lax.rsqrt(var + EPS) * gn_ref[:, vcols]
        yield
        gated[hd] = (jax.nn.sigmoid(o_pre) * hn).astype(BF16)
        b_last = b_col[r - 1:r, :]
        m_new = m_t[r - 1:r, :]
        g = jnp.exp(b_last - b_col + ig_col - m_new)
        g_row = jnp.exp(b_last - b_row + i_row - m_new)
        decay = jnp.exp(b_last + m_prev - m_new)
        yield
        kg_t = (k.T * g_row).astype(BF16)
        c_ref[0, hd] = decay * c_old + jnp.dot(kg_t, vb, preferred_element_type=F32)
        n_ref[0, hd:hd + 1, :] = decay * n_old + jnp.sum(k * g, axis=0, keepdims=True)
        m_ref[0, hd:hd + 1, :] = jnp.broadcast_to(m_new, (1, LANES))
        yield

    n_stages = 6
    gate_stage = 4
    q_chunks[0] = qk_chunk(0)
    k_chunks[0] = qk_chunk(n_pairs)
    heads = [head_stages(hd) for hd in range(H_B)]
    y = x
    for t in range(H_B + n_stages - 1):
        if t % 2 == 0 and t // 2 + 1 < n_pairs:
            q_chunks[t // 2 + 1] = qk_chunk(t // 2 + 1)
            k_chunks[t // 2 + 1] = qk_chunk(n_pairs + t // 2 + 1)
        for hd in reversed(range(H_B)):
            if hd <= t < hd + n_stages:
                next(heads[hd])
        p, odd = divmod(t - gate_stage, 2)
        if odd and 0 <= p < n_pairs:
            lhs = jnp.concatenate([gated[2 * p], gated[2 * p + 1]], axis=1)
            y = y + jnp.dot(lhs, wout_ref[2 * p * dv:(2 * p + 2) * dv, :],
                            preferred_element_type=F32)
    y_ref[...] = y


def _mlstm_fused(x2, g, w_all, w_if, cw, cb, b_if, gn_g, w_out, *, bsz, seq, ls, k_scale):
    t, d = x2.shape
    qk_w = cw.shape[1]
    d_b = w_out.shape[0]
    dk = qk_w // (2 * H_B)
    dv = d_b // H_B
    nj = seq // ls
    row_spec = pl.BlockSpec((ls, d), lambda i, j: (i * nj + j, 0))

    def state_spec(*tail):
        return pl.BlockSpec((1,) + tail, lambda i, j: (i,) + (0,) * len(tail))

    args = [x2, g, w_all, w_if, cw, cb, b_if, gn_g, w_out]
    return pl.pallas_call(
        functools.partial(_mlstm_fused_kernel, k_scale=k_scale),
        grid=(bsz, nj),
        in_specs=([row_spec, _const_spec(g.shape), _layer_spec(w_all.shape, 0)]
                  + [_const_spec(a.shape) for a in args[3:]]),
        out_specs=[row_spec, state_spec(H_B, dk, dv), state_spec(H_B, dk),
                   state_spec(H_B, LANES), state_spec(CONV_B - 1, qk_w)],
        out_shape=[jax.ShapeDtypeStruct((t, d), F32),
                   jax.ShapeDtypeStruct((bsz, H_B, dk, dv), F32),
                   jax.ShapeDtypeStruct((bsz, H_B, dk), F32),
                   jax.ShapeDtypeStruct((bsz, H_B, LANES), F32),
                   jax.ShapeDtypeStruct((bsz, CONV_B - 1, qk_w), F32)],
        compiler_params=_params(2),
        name="mlstm_fused",
    )(*args)


def _trunks(xp, xs, state, w):
    bp, sp, d = xp.shape
    bs_, ss, _ = xs.shape
    xp2 = xp.reshape(bp * sp, d)
    xs2 = xs.reshape(bs_ * ss, d)
    sample_nb = SAMPLE_ROWS // ss

    (xp2,) = _mixer_a(xp2, w["norm_mix_g"][0:1], w["a_w_in"], w["a_ln_g"], w["a_ln_b"],
                      w["a_ws"], w["a_bs_t"], w["a_w_out"], tm=PROMPT_ROWS, seq_chunk=CHUNK_A,
                      emit_v=False)
    ws_s = jnp.tile(w["a_ws"][:, :ss, :ss], (1, CHUNK_A // ss, CHUNK_A // ss))
    bs_s = jnp.tile(w["a_bs_t"][:ss], (CHUNK_A // ss, 1))
    xs2, v_rows = _mixer_a(xs2, w["norm_mix_g"][0:1], w["a_w_in"], w["a_ln_g"], w["a_ln_b"],
                           ws_s, bs_s, w["a_w_out"], tm=SAMPLE_ROWS, seq_chunk=ss, emit_v=True)
    ffn_tiles = dict(fresh_bsz=bp, fresh_seq=sp, fresh_ls=PROMPT_ROWS, carried_nb=sample_nb,
                     carried_ls=ss)
    xp2, xs2, fbuf0_p, fbuf0_s = _ffn(
        xp2, xs2, w["norm_ffn_g"][0:1], w["f_w_up"], w["f_conv_w"][0], w["f_conv_b"][0:1],
        w["f_w_down"], state["ffn_conv"][0], None, layer=0, **ffn_tiles)

    xp2, c_p, n_p, m_p, mconv_p = _mlstm_fused(
        xp2, w["norm_mix_g"][1:2], w["b_w_all"], w["b_w_if"], w["b_conv_w"], w["b_conv_b"],
        w["b_bias_if"], w["b_gn_g"], w["b_w_out"], bsz=bp, seq=sp, ls=PROMPT_MLSTM_ROWS,
        k_scale=w["k_scale"])
    q, k, v, o, ig, lf, mconv_s = _mlstm_proj(
        xs2, w["norm_mix_g"][1:2], w["b_w_all"], w["b_w_i"], w["b_w_f"], w["b_conv_w"],
        w["b_conv_b"], w["b_bias_i"], w["b_bias_f"], state["mlstm_conv"],
        d_b=w["b_w_out"].shape[0], bsz=bs_, seq=ss, nb=sample_nb, ls=ss, k_scale=w["k_scale"])
    n0 = jnp.transpose(state["mlstm_n"], (1, 0, 2))
    m0 = jnp.pad(state["mlstm_m"], ((0, 0), (0, LANES - H_B)))
    h, c_s, n_s, m_s = _mlstm_step(q, k, v, ig, lf, state["mlstm_C"], n0, m0, bsz=bs_,
                                   nb=STEP_SEQS, ls=ss)
    xs2 = _mlstm_out(h, o, xs2, w["b_gn_g"], w["b_w_out"], tm=SAMPLE_ROWS)
    yp2, ys2, fbuf1_p, fbuf1_s = _ffn(
        xp2, xs2, w["norm_ffn_g"][1:2], w["f_w_up"], w["f_conv_w"][1], w["f_conv_b"][1:2],
        w["f_w_down"], state["ffn_conv"][1], w["final_norm_g"], layer=1, **ffn_tiles)
    prompt = dict(y=yp2.reshape(bp, sp, d), C=c_p[None], n=n_p[None], m=m_p[:, :, 0][None],
                  mconv=mconv_p[None], fconv=jnp.stack([fbuf0_p, fbuf1_p]))
    sample = dict(y=ys2.reshape(bs_, ss, d), v=v_rows.reshape(1, bs_, ss, -1), C=c_s[None],
                  n=jnp.transpose(n_s, (1, 0, 2))[None], m=m_s[:, :H_B][None],
                  mconv=mconv_s[None], fconv=jnp.stack([fbuf0_s, fbuf1_s]))
    return prompt, sample


def kernel(x_prompt, x_sample, state_mlstm_C, state_mlstm_n, state_mlstm_m, state_mlstm_conv, state_ffn_conv, norm_mix_g, norm_ffn_g, final_norm_g, a_w_in, a_ln_g, a_ln_b, a_w_s, a_b_s, a_w_out, b_w_in, b_conv_w, b_conv_b, b_bias_i, b_bias_f, b_gn_g, b_w_out, f_w_up, f_conv_w, f_conv_b, f_w_down):
    d_ff = f_w_down.shape[1]
    qk_w = b_conv_w.shape[2]
    d_b = b_w_out.shape[1]
    dk = qk_w // (2 * H_B)
    d_a = a_w_out.shape[1]
    dg = d_a // G_A

    def pad_gate(cols):
        return jnp.pad(cols, ((0, 0), (0, LANES - H_B)))

    w_in_b = b_w_in[0]
    w = dict(
        norm_mix_g=norm_mix_g, norm_ffn_g=norm_ffn_g, final_norm_g=final_norm_g[None, :],
        a_w_in=a_w_in[0].astype(BF16), a_ln_g=a_ln_g, a_ln_b=a_ln_b,
        a_ws=a_w_s[0].astype(BF16),
        a_bs_t=jnp.repeat(jnp.transpose(a_b_s[0]), dg, axis=1),
        a_w_out=a_w_out[0].astype(BF16),
        b_w_all=b_w_in.astype(BF16),
        b_w_i=pad_gate(w_in_b[:, qk_w + 2 * d_b:qk_w + 2 * d_b + H_B]).astype(BF16),
        b_w_f=pad_gate(w_in_b[:, qk_w + 2 * d_b + H_B:]).astype(BF16),
        b_conv_w=b_conv_w[0], b_conv_b=b_conv_b,
        b_bias_i=jnp.pad(b_bias_i, ((0, 0), (0, LANES - H_B))),
        b_bias_f=jnp.pad(b_bias_f, ((0, 0), (0, LANES - H_B))),
        b_w_if=jnp.pad(w_in_b[:, qk_w + 2 * d_b:], ((0, 0), (0, LANES - 2 * H_B))).astype(BF16),
        b_bias_if=jnp.pad(jnp.concatenate([b_bias_i, b_bias_f], axis=1),
                          ((0, 0), (0, LANES - 2 * H_B))),
        b_gn_g=b_gn_g, b_w_out=b_w_out[0].astype(BF16),
        f_w_up=f_w_up.astype(BF16),
        f_conv_w=f_conv_w, f_conv_b=f_conv_b, f_w_down=f_w_down.astype(BF16),
        k_scale=float(dk) ** -0.5,
    )
    state = dict(mlstm_C=state_mlstm_C[0], mlstm_n=state_mlstm_n[0], mlstm_m=state_mlstm_m[0],
                 mlstm_conv=state_mlstm_conv[0], ffn_conv=state_ffn_conv)
    p, s = _trunks(x_prompt, x_sample, state, w)
    return (p["y"], s["y"], p["C"], p["n"], p["m"], p["mconv"], p["fconv"],
            s["v"], s["C"], s["n"], s["m"], s["mconv"], s["fconv"])
```

```python
import functools

import jax
import jax.numpy as jnp
from jax import lax
from jax.experimental import pallas as pl
from jax.experimental.pallas import tpu as pltpu

F32 = jnp.float32
BF16 = jnp.bfloat16

EPS = 1e-6
LANES = 128
SUBLANES = 8
VMEM_LIMIT_BYTES = 56 * 1024 * 1024

CHUNK_A = 128
G_A = 8
H_B = 8
CONV_B = 4
CONV_F = 3

PROMPT_ROWS = 512
PROMPT_MLSTM_ROWS = 256
SAMPLE_ROWS = 256
STEP_SEQS = 8


def _const_spec(shape):
    nd = len(shape)
    return pl.BlockSpec(shape, lambda *_: (0,) * nd, pipeline_mode=pl.Buffered(1))


def _layer_spec(shape, layer):
    return pl.BlockSpec((1,) + tuple(shape[1:]), lambda *_: (layer, 0, 0),
                        pipeline_mode=pl.Buffered(1))


def _params(n_axes):
    return pltpu.CompilerParams(
        dimension_semantics=("arbitrary",) * n_axes,
        vmem_limit_bytes=VMEM_LIMIT_BYTES,
    )


def _software_pipeline(chains, n_stages):
    for t in range(len(chains) + n_stages - 1):
        for i in reversed(range(len(chains))):
            if i <= t < i + n_stages:
                next(chains[i])


def _rmsnorm(x, g):
    ms = jnp.mean(x * x, axis=-1, keepdims=True)
    return x * lax.rsqrt(ms + EPS) * g


def _expand3(c3, nb, ls):
    w = c3.shape[-1]
    if nb == 1:
        return jnp.broadcast_to(c3.reshape(1, w), (ls, w))
    return jnp.broadcast_to(c3, (nb, ls, w)).reshape(nb * ls, w)


def _expand2(c2, nb, ls):
    return _expand3(c2[:, None, :], nb, ls)


def _causal_conv(a, carry_ref, cw_ref, cb_ref, nb, ls, cols=slice(None)):
    r, c = a.shape
    kw = cw_ref.shape[0]
    rolled = [a] + [pltpu.roll(a, k, 0) for k in range(1, kw)]

    def taps(shifted):
        y = cb_ref[:, cols] + cw_ref[kw - 1:kw, cols] * shifted[0]
        for k in range(1, kw):
            y = y + cw_ref[kw - 1 - k:kw - k, cols] * shifted[k]
        return y

    grp = SUBLANES if nb == 1 else r
    grp_ls = SUBLANES if nb == 1 else ls
    tpos = lax.broadcasted_iota(jnp.int32, (grp, c), 0) & (grp_ls - 1)
    fixed = [a[0:grp]]
    for k in range(1, kw):
        sh = rolled[k][0:grp]
        for t in range(k):
            idx = kw - 1 + t - k
            prev = _expand3(carry_ref[:, idx:idx + 1, cols], nb, grp_ls)
            sh = jnp.where(tpos == t, prev, sh)
        fixed.append(sh)
    y = taps(fixed)
    if nb == 1:
        y = jnp.concatenate([y, taps([s[grp:] for s in rolled])], axis=0)
    last = rolled[kw - 1]
    if nb == 1:
        carry_ref[0, :, cols] = last[0:kw - 1, :]
    else:
        heads = last.reshape(nb, ls, c)[:, 0:kw - 1, :]
        carry_ref[0:nb - 1, :, cols] = heads[1:nb]
        carry_ref[nb - 1:nb, :, cols] = heads[0:1]
    return y


def _mixer_a_kernel(xl_ref, xs_ref, g_ref, win_ref, lng_ref, lnb_ref, wsl_ref, bsl_ref, wss_ref,
                    bss_ref, wout_ref, yl_ref, ys_ref, v_ref, *, long_steps, short_chunk):
    s = pl.program_id(0)
    shared = (g_ref, win_ref, lng_ref, lnb_ref, wout_ref)

    @pl.when(s < long_steps)
    def _():
        _mixer_a_tile(xl_ref, yl_ref, None, wsl_ref, bsl_ref, *shared, CHUNK_A)

    @pl.when(s >= long_steps)
    def _():
        _mixer_a_tile(xs_ref, ys_ref, v_ref, wss_ref, bss_ref, *shared, short_chunk)


def _mixer_a_tile(x_ref, y_ref, v_ref, ws_ref, bs_ref, g_ref, win_ref, lng_ref, lnb_ref,
                  wout_ref, seq_chunk):
    emit_v = v_ref is not None
    tm = x_ref.shape[0]
    d_a = lng_ref.shape[1]
    dg = d_a // G_A
    x = x_ref[...]
    h = _rmsnorm(x, g_ref[...]).astype(BF16)
    v_pre = {}
    v_chunks = []
    ahead = 1
    for g in range(G_A + ahead):
        if g < G_A:
            v_pre[g] = jnp.dot(h, win_ref[:, d_a + g * dg:d_a + (g + 1) * dg],
                               preferred_element_type=F32)
        if g >= ahead:
            v_chunks.append(jax.nn.gelu(v_pre.pop(g - ahead)))
    t_i = lax.broadcasted_iota(jnp.int32, (CHUNK_A, CHUNK_A), 0)
    s_i = lax.broadcasted_iota(jnp.int32, (CHUNK_A, CHUNK_A), 1)
    mask = (s_i <= t_i) & ((t_i // seq_chunk) == (s_i // seq_chunk))
    stats = {}
    gated = {}

    def group_stages(g):
        cols = slice(g * dg, (g + 1) * dg)
        u_pre = jnp.dot(h, win_ref[:, cols], preferred_element_type=F32)
        yield
        vn = ((v_chunks[g] - stats["mu"]) * stats["rs"]) * lng_ref[:, cols] + lnb_ref[:, cols]
        if emit_v:
            v_ref[:, cols] = vn
        vb = vn.astype(BF16)
        w = jnp.where(mask, ws_ref[g], jnp.zeros((), BF16))
        mixed = [jnp.dot(w, vb[c * CHUNK_A:(c + 1) * CHUNK_A], preferred_element_type=F32)
                 + bs_ref[:, cols] for c in range(tm // CHUNK_A)]
        yield
        u = jax.nn.gelu(u_pre)
        gated[g] = jnp.concatenate(
            [(u[c * CHUNK_A:(c + 1) * CHUNK_A] * mixed[c]).astype(BF16)
             for c in range(tm // CHUNK_A)], axis=0)
        yield

    lead = 3
    times = (0, lead, lead + 1)
    groups = [group_stages(g) for g in range(G_A)]
    y = x
    for t in range(G_A + times[-1]):
        if t == lead:
            mu = sum(jnp.sum(v, axis=-1, keepdims=True) for v in v_chunks) / d_a
            var = sum(jnp.sum((v - mu) * (v - mu), axis=-1, keepdims=True)
                      for v in v_chunks) / d_a
            stats["mu"] = mu
            stats["rs"] = lax.rsqrt(var + EPS)
        for g in reversed(range(G_A)):
            if t - g in times:
                next(groups[g])
        p, odd = divmod(t - times[-1], 2)
        if odd and 0 <= p < G_A // 2:
            lhs = jnp.concatenate([gated[2 * p], gated[2 * p + 1]], axis=1)
            y = y + jnp.dot(lhs, wout_ref[2 * p * dg:(2 * p + 2) * dg, :],
                            preferred_element_type=F32)
    y_ref[...] = y


def _mixer_a(xl2, xs2, g, w_in, ln_g, ln_b, ws_l, bs_l, ws_s, bs_s, w_out, *, long_tm, short_tm,
             short_chunk):
    tl, d = xl2.shape
    ts = xs2.shape[0]
    d_a = ln_g.shape[1]
    n_l = tl // long_tm

    def l_rows(width):
        return pl.BlockSpec((long_tm, width), lambda s: (jnp.minimum(s, n_l - 1), 0))

    def s_rows(width):
        return pl.BlockSpec((short_tm, width), lambda s: (jnp.maximum(s - n_l, 0), 0))

    consts = [g, w_in, ln_g, ln_b, ws_l, bs_l, ws_s, bs_s, w_out]
    return pl.pallas_call(
        functools.partial(_mixer_a_kernel, long_steps=n_l, short_chunk=short_chunk),
        grid=(n_l + ts // short_tm,),
        in_specs=[l_rows(d), s_rows(d)] + [_const_spec(a.shape) for a in consts],
        out_specs=[l_rows(d), s_rows(d), s_rows(d_a)],
        out_shape=[jax.ShapeDtypeStruct((tl, d), F32),
                   jax.ShapeDtypeStruct((ts, d), F32),
                   jax.ShapeDtypeStruct((ts, d_a), F32)],
        compiler_params=_params(1),
        name="mixer_a",
    )(xl2, xs2, *consts)


def _ffn_tile(x_ref, y_ref, nbuf_ref, g_ref, wup_ref, cw_ref, cb_ref, wd_ref, fg_ref, nb, ls):
    x = x_ref[...]
    h = _rmsnorm(x, g_ref[...]).astype(BF16)
    d_ff = wd_ref.shape[1]
    a = jnp.dot(h, wup_ref[0, :, :d_ff], preferred_element_type=F32)
    gv = jnp.dot(h, wup_ref[0, :, d_ff:], preferred_element_type=F32)
    a_c = _causal_conv(a, nbuf_ref, cw_ref, cb_ref, nb, ls)
    act = (jax.nn.gelu(a_c) * gv).astype(BF16)
    y = x + jnp.dot(act, wd_ref[0], preferred_element_type=F32)
    if fg_ref is not None:
        y = _rmsnorm(y, fg_ref[...])
    y_ref[...] = y


def _ffn_kernel(*refs, fresh_steps, fresh_nj, fresh_ls, carried_nb, carried_ls, final_norm):
    refs = list(refs)
    xf_ref, xc_ref, g_ref, wup_ref, cw_ref, cb_ref, wd_ref, buf_ref = refs[:8]
    pos = 8
    fg_ref = None
    if final_norm:
        fg_ref = refs[pos]
        pos += 1
    yf_ref, yc_ref, nbuff_ref, nbufc_ref = refs[pos:pos + 4]
    s = pl.program_id(0)
    shared = (g_ref, wup_ref, cw_ref, cb_ref, wd_ref, fg_ref)

    @pl.when(s < fresh_steps)
    def _():
        @pl.when(s % fresh_nj == 0)
        def _():
            nbuff_ref[...] = jnp.zeros(nbuff_ref.shape, F32)

        _ffn_tile(xf_ref, yf_ref, nbuff_ref, *shared, 1, fresh_ls)

    @pl.when(s >= fresh_steps)
    def _():
        nbufc_ref[...] = buf_ref[...]
        _ffn_tile(xc_ref, yc_ref, nbufc_ref, *shared, carried_nb, carried_ls)


def _ffn(xf2, xc2, g, w_up, cw, cb, w_d, buf, final_g, *, layer, fresh_bsz, fresh_seq, fresh_ls,
         carried_nb, carried_ls):
    tf, d = xf2.shape
    tc = xc2.shape[0]
    d_ff = w_d.shape[1]
    nj = fresh_seq // fresh_ls
    fs = fresh_bsz * nj
    cs = tc // (carried_nb * carried_ls)
    final_norm = final_g is not None
    f_rows = pl.BlockSpec((fresh_ls, d), lambda s: (jnp.minimum(s, fs - 1), 0))
    c_rows = pl.BlockSpec((carried_nb * carried_ls, d), lambda s: (jnp.maximum(s - fs, 0), 0))
    f_buf = pl.BlockSpec((1, CONV_F - 1, d_ff), lambda s: (jnp.minimum(s, fs - 1) // nj, 0, 0))
    c_buf = pl.BlockSpec((carried_nb, CONV_F - 1, d_ff), lambda s: (jnp.maximum(s - fs, 0), 0, 0))
    args = [xf2, xc2, g, w_up, cw, cb, w_d, buf]
    in_specs = [f_rows, c_rows, _const_spec(g.shape), _layer_spec(w_up.shape, layer),
                _const_spec(cw.shape), _const_spec(cb.shape), _layer_spec(w_d.shape, layer), c_buf]
    if final_norm:
        args.append(final_g)
        in_specs.append(_const_spec(final_g.shape))
    return pl.pallas_call(
        functools.partial(_ffn_kernel, fresh_steps=fs, fresh_nj=nj, fresh_ls=fresh_ls,
                          carried_nb=carried_nb, carried_ls=carried_ls, final_norm=final_norm),
        grid=(fs + cs,),
        in_specs=in_specs,
        out_specs=[f_rows, c_rows, f_buf, c_buf],
        out_shape=[jax.ShapeDtypeStruct((tf, d), F32),
                   jax.ShapeDtypeStruct((tc, d), F32),
                   jax.ShapeDtypeStruct((fresh_bsz, CONV_F - 1, d_ff), F32),
                   jax.ShapeDtypeStruct((tc // carried_ls, CONV_F - 1, d_ff), F32)],
        compiler_params=_params(1),
        name="conv_ffn",
    )(*args)


def _select_sum(sel, x):
    return jnp.dot(sel.astype(F32), x, precision=lax.Precision.HIGHEST,
                   preferred_element_type=F32)


def _log_sigmoid(x):
    return jnp.minimum(x, 0.0) - jnp.log1p(jnp.exp(-jnp.abs(x)))


def _mlstm_proj_kernel(*refs, nb, ls, zero_init, k_scale):
    refs = list(refs)
    x_ref, g_ref, wall_ref, wi_ref, wf_ref, cw_ref, cb_ref, bi_ref, bf_ref = refs[:9]
    pos = 9
    buf_ref = None
    if not zero_init:
        buf_ref = refs[pos]
        pos += 1
    q_ref, k_ref, v_ref, o_ref, ig_ref, lf_ref, nbuf_ref = refs[pos:pos + 7]

    @pl.when(pl.program_id(1) == 0)
    def _():
        if zero_init:
            nbuf_ref[...] = jnp.zeros(nbuf_ref.shape, F32)
        else:
            nbuf_ref[...] = buf_ref[...]

    h = _rmsnorm(x_ref[...], g_ref[...]).astype(BF16)
    qk_w = cw_ref.shape[1]
    d_b = v_ref.shape[1]
    qk_pre = jnp.dot(h, wall_ref[0, :, 0:qk_w], preferred_element_type=F32)
    v_ref[...] = jnp.dot(h, wall_ref[0, :, qk_w:qk_w + d_b], preferred_element_type=F32)
    o_ref[...] = jnp.dot(h, wall_ref[0, :, qk_w + d_b:qk_w + 2 * d_b],
                         preferred_element_type=F32)
    ig_ref[...] = jnp.dot(h, wi_ref[...], preferred_element_type=F32) + bi_ref[...]
    lf_ref[...] = _log_sigmoid(jnp.dot(h, wf_ref[...], preferred_element_type=F32) + bf_ref[...])
    qk = jax.nn.silu(_causal_conv(qk_pre, nbuf_ref, cw_ref, cb_ref, nb, ls))
    dq = q_ref.shape[1]
    q_ref[...] = qk[:, :dq]
    k_ref[...] = qk[:, dq:] * k_scale


def _mlstm_proj(x2, g, w_all, w_i, w_f, cw, cb, b_i, b_f, buf, *, d_b, bsz, seq, nb, ls,
                k_scale):
    t, d = x2.shape
    qk_w = cw.shape[1]
    nj = seq // ls
    zero_init = buf is None

    def rows(width):
        return pl.BlockSpec((nb * ls, width), lambda i, j: (i * nj + j, 0))

    buf_spec = pl.BlockSpec((nb, CONV_B - 1, qk_w), lambda i, j: (i, 0, 0))
    args = [x2, g, w_all, w_i, w_f, cw, cb, b_i, b_f]
    in_specs = ([rows(d), _const_spec(g.shape), _layer_spec(w_all.shape, 0)]
                + [_const_spec(a.shape) for a in args[3:]])
    if not zero_init:
        args.append(buf)
        in_specs.append(buf_spec)
    return pl.pallas_call(
        functools.partial(_mlstm_proj_kernel, nb=nb, ls=ls, zero_init=zero_init, k_scale=k_scale),
        grid=(bsz // nb, nj),
        in_specs=in_specs,
        out_specs=[rows(qk_w // 2), rows(qk_w // 2), rows(d_b), rows(d_b), rows(LANES),
                   rows(LANES), buf_spec],
        out_shape=[jax.ShapeDtypeStruct((t, qk_w // 2), F32),
                   jax.ShapeDtypeStruct((t, qk_w // 2), F32),
                   jax.ShapeDtypeStruct((t, d_b), F32),
                   jax.ShapeDtypeStruct((t, d_b), F32),
                   jax.ShapeDtypeStruct((t, LANES), F32),
                   jax.ShapeDtypeStruct((t, LANES), F32),
                   jax.ShapeDtypeStruct((bsz, CONV_B - 1, qk_w), F32)],
        compiler_params=_params(2),
        name="mlstm_proj",
    )(*args)


def _mlstm_step_kernel(q_ref, k_ref, v_ref, ig_ref, lf_ref, c0_ref, n0_ref, m0_ref,
                       h_ref, c_ref, n_ref, m_ref, *, nb, ls):
    r = nb * ls
    rc = LANES
    dk = c0_ref.shape[2]
    dv = c0_ref.shape[3]

    def pad_rows(a):
        if r == rc:
            return a
        return jnp.concatenate([a, jnp.zeros((rc - r, a.shape[1]), a.dtype)], axis=0)

    r_i = lax.broadcasted_iota(jnp.int32, (r, rc), 0)
    c_i = lax.broadcasted_iota(jnp.int32, (r, rc), 1)
    mask = (c_i <= r_i) & ((r_i // ls) == (c_i // ls))
    ig = ig_ref[...]
    b_all = _select_sum(mask, pad_rows(lf_ref[...]))
    b_t = pad_rows(b_all).T
    ig_t = pad_rows(ig).T
    m0 = m0_ref[...]
    inter_all = b_all + _expand2(m0, nb, ls)
    lane = lax.broadcasted_iota(jnp.int32, (r, LANES), 1)
    m_t_of = {}

    def output_stages(hd):
        kcols = slice(hd * dk, (hd + 1) * dk)
        vcols = slice(hd * dv, (hd + 1) * dv)
        q = q_ref[:, kcols]
        qb = q.astype(BF16)
        qc = jnp.concatenate(
            [jnp.dot(qb[b * ls:(b + 1) * ls], c0_ref[b, hd].astype(BF16),
                     preferred_element_type=F32) for b in range(nb)], axis=0)
        d = jnp.where(mask, b_all[:, hd:hd + 1] - b_t[hd:hd + 1, :] + ig_t[hd:hd + 1, :],
                      -jnp.inf)
        inter = inter_all[:, hd:hd + 1]
        m_t = jnp.maximum(inter, jnp.max(d, axis=1, keepdims=True))
        m_t_of[hd] = m_t
        w_intra = jnp.exp(d - m_t)
        w_inter = jnp.exp(inter - m_t)
        yield
        kb = pad_rows(k_ref[:, kcols].astype(BF16))
        vb = pad_rows(v_ref[:, vcols].astype(BF16))
        s = lax.dot_general(qb, kb, (((1,), (1,)), ((), ())), preferred_element_type=F32)
        sc = s * w_intra
        intra = jnp.dot(sc.astype(BF16), vb, preferred_element_type=F32)
        den_intra = jnp.sum(sc, axis=1, keepdims=True)
        qn = jnp.sum(q * _expand2(n0_ref[hd], nb, ls), axis=1, keepdims=True)
        yield
        num = w_inter * qc + intra
        den = w_inter * qn + den_intra
        h_ref[:, vcols] = num / jnp.maximum(jnp.abs(den), jnp.exp(-m_t))
        yield

    _software_pipeline([output_stages(hd) for hd in range(H_B)], 3)
    mt_all = jnp.zeros((r, LANES), F32)
    for hd in range(H_B):
        mt_all = jnp.where(lane == hd, m_t_of[hd], mt_all)

    p_r = lax.broadcasted_iota(jnp.int32, (nb, rc), 0)
    p_c = lax.broadcasted_iota(jnp.int32, (nb, rc), 1)
    pick = p_c == p_r * ls + (ls - 1)
    bl_seq = _select_sum(pick, pad_rows(b_all))
    mn_seq = _select_sum(pick, pad_rows(mt_all))
    decay = jnp.exp(bl_seq + m0 - mn_seq)
    m_ref[...] = mn_seq
    g_all = jnp.exp(_expand2(bl_seq, nb, ls) - b_all + ig - _expand2(mn_seq, nb, ls))
    col_seq = lax.broadcasted_iota(jnp.int32, (dk, rc), 1) // ls

    def state_stages(hd):
        kg = k_ref[:, hd * dk:(hd + 1) * dk] * g_all[:, hd:hd + 1]
        n_ref[hd] = decay[:, hd:hd + 1] * n0_ref[hd] + jnp.sum(kg.reshape(nb, ls, dk), axis=1)
        kg_t = pad_rows(kg).T
        vb = pad_rows(v_ref[:, hd * dv:(hd + 1) * dv].astype(BF16))
        yield
        for b in range(nb):
            lhs = jnp.where(col_seq == b, kg_t, 0.0).astype(BF16)
            c_ref[b, hd] = decay[b:b + 1, hd:hd + 1] * c0_ref[b, hd] + jnp.dot(
                lhs, vb, preferred_element_type=F32)
        yield

    _software_pipeline([state_stages(hd) for hd in range(H_B)], 2)


def _mlstm_step(q, k, v, ig, lf, c0, n0, m0, *, bsz, nb, ls):
    t = q.shape[0]
    dk = q.shape[1] // H_B
    dv = v.shape[1] // H_B

    def rows(width):
        return pl.BlockSpec((nb * ls, width), lambda i: (i, 0))

    c_spec = pl.BlockSpec((nb, H_B, dk, dv), lambda i: (i, 0, 0, 0))
    n_spec = pl.BlockSpec((H_B, nb, dk), lambda i: (0, i, 0))
    m_spec = pl.BlockSpec((nb, LANES), lambda i: (i, 0))
    return pl.pallas_call(
        functools.partial(_mlstm_step_kernel, nb=nb, ls=ls),
        grid=(bsz // nb,),
        in_specs=[rows(H_B * dk), rows(H_B * dk), rows(H_B * dv), rows(LANES), rows(LANES),
                  c_spec, n_spec, m_spec],
        out_specs=[rows(H_B * dv), c_spec, n_spec, m_spec],
        out_shape=[jax.ShapeDtypeStruct((t, H_B * dv), F32),
                   jax.ShapeDtypeStruct((bsz, H_B, dk, dv), F32),
                   jax.ShapeDtypeStruct((H_B, bsz, dk), F32),
                   jax.ShapeDtypeStruct((bsz, LANES), F32)],
        compiler_params=_params(1),
        name="mlstm_step",
    )(q, k, v, ig, lf, c0, n0, m0)


def _mlstm_out_kernel(h_ref, o_ref, x_ref, gn_ref, wout_ref, y_ref, gate_ref):
    dv = h_ref.shape[1] // H_B
    for hd in range(H_B):
        cols = slice(hd * dv, (hd + 1) * dv)
        hh = h_ref[:, cols]
        mu = jnp.mean(hh, axis=-1, keepdims=True)
        hc = hh - mu
        var = jnp.mean(hc * hc, axis=-1, keepdims=True)
        hn = hc * lax.rsqrt(var + EPS) * gn_ref[:, cols]
        gate_ref[:, cols] = (jax.nn.sigmoid(o_ref[:, cols]) * hn).astype(BF16)
    y_ref[...] = x_ref[...] + jnp.dot(gate_ref[...], wout_ref[...], preferred_element_type=F32)


def _mlstm_out(h, o, x2, gn_g, w_out, *, tm):
    t, d = x2.shape
    d_b = h.shape[1]
    return pl.pallas_call(
        _mlstm_out_kernel,
        grid=(t // tm,),
        in_specs=[pl.BlockSpec((tm, d_b), lambda i: (i, 0)),
                  pl.BlockSpec((tm, d_b), lambda i: (i, 0)),
                  pl.BlockSpec((tm, d), lambda i: (i, 0)),
                  _const_spec(gn_g.shape), _const_spec(w_out.shape)],
        out_specs=pl.BlockSpec((tm, d), lambda i: (i, 0)),
        out_shape=jax.ShapeDtypeStruct((t, d), F32),
        scratch_shapes=[pltpu.VMEM((tm, d_b), BF16)],
        compiler_params=_params(1),
        name="mlstm_out",
    )(h, o, x2, gn_g, w_out)


def _mlstm_fused_kernel(x_ref, g_ref, wall_ref, wif_ref, cw_ref, cb_ref, bif_ref,
                        gn_ref, wout_ref, y_ref, c_ref, n_ref, m_ref, nbuf_ref, *, k_scale):
    @pl.when(pl.program_id(1) == 0)
    def _():
        c_ref[...] = jnp.zeros(c_ref.shape, F32)
        n_ref[...] = jnp.zeros(n_ref.shape, F32)
        m_ref[...] = jnp.zeros(m_ref.shape, F32)
        nbuf_ref[...] = jnp.zeros(nbuf_ref.shape, F32)

    r = x_ref.shape[0]
    dk = c_ref.shape[2]
    dv = c_ref.shape[3]

    x = x_ref[...]
    h = _rmsnorm(x, g_ref[...]).astype(BF16)
    gates = jnp.dot(h, wif_ref[...], preferred_element_type=F32) + bif_ref[...]
    r_i = lax.broadcasted_iota(jnp.int32, (r, r), 0)
    c_i = lax.broadcasted_iota(jnp.int32, (r, r), 1)
    mask = c_i <= r_i
    b_all = _select_sum(mask, _log_sigmoid(gates))
    b_t = b_all.T
    ig_t = gates.T

    pair = 2 * dk
    n_pairs = H_B // 2
    qk_w = cw_ref.shape[1]

    def w_cols(base, cols):
        return wall_ref[0, :, base + cols.start:base + cols.stop]

    q_chunks = {}
    k_chunks = {}
    gated = {}

    def qk_chunk(c):
        cols = slice(c * pair, (c + 1) * pair)
        pre = jnp.dot(h, w_cols(0, cols), preferred_element_type=F32)
        return jax.nn.silu(_causal_conv(pre, nbuf_ref, cw_ref, cb_ref, 1, r, cols))

    def head_stages(hd):
        p, half = divmod(hd, 2)
        ig_col = gates[:, hd:hd + 1]
        i_row = ig_t[hd:hd + 1, :]
        b_col = b_all[:, H_B + hd:H_B + hd + 1]
        b_row = b_t[H_B + hd:H_B + hd + 1, :]
        m_prev = m_ref[0, hd:hd + 1, 0:1]
        vcols = slice(hd * dv, (hd + 1) * dv)
        vb = jnp.dot(h, w_cols(qk_w, vcols), preferred_element_type=F32).astype(BF16)
        o_pre = jnp.dot(h, w_cols(qk_w + H_B * dv, vcols), preferred_element_type=F32)
        d = jnp.where(mask, b_col - b_row + i_row, -jnp.inf)
        yield
        inter = b_col + m_prev
        m_t = jnp.maximum(inter, jnp.max(d, axis=1, keepdims=True))
        w_intra = jnp.exp(d - m_t)
        w_inter = jnp.exp(inter - m_t)
        yield
        q = q_chunks[p][:, half * dk:(half + 1) * dk]
        k = k_chunks[p][:, half * dk:(half + 1) * dk] * k_scale
        qb = q.astype(BF16)
        s = lax.dot_general(qb, k.astype(BF16), (((1,), (1,)), ((), ())),
                            preferred_element_type=F32)
        sc = s * w_intra
        intra = jnp.dot(sc.astype(BF16), vb, preferred_element_type=F32)
        den_intra = jnp.sum(sc, axis=1, keepdims=True)
        c_old = c_ref[0, hd]
        n_old = n_ref[0, hd:hd + 1, :]
        qc = jnp.dot(qb, c_old.astype(BF16), preferred_element_type=F32)
        qn = jnp.sum(q * n_old, axis=1, keepdims=True)
        yield
        num = w_inter * qc + intra
        den = w_inter * qn + den_intra
        hout = num / jnp.maximum(jnp.abs(den), jnp.exp(-m_t))
        mu = jnp.mean(hout, axis=-1, keepdims=True)
        hc = hout - mu
        var = jnp.mean(hc * hc, axis=-1, keepdims=True)
        hn = hc * lax.rsqrt(var + EPS) * gn_ref[:, vcols]
        yield
        gated[hd] = (jax.nn.sigmoid(o_pre) * hn).astype(BF16)
        b_last = b_col[r - 1:r, :]
        m_new = m_t[r - 1:r, :]
        g = jnp.exp(b_last - b_col + ig_col - m_new)
        g_row = jnp.exp(b_last - b_row + i_row - m_new)
        decay = jnp.exp(b_last + m_prev - m_new)
        yield
        kg_t = (k.T * g_row).astype(BF16)
        c_ref[0, hd] = decay * c_old + jnp.dot(kg_t, vb, preferred_element_type=F32)
        n_ref[0, hd:hd + 1, :] = decay * n_old + jnp.sum(k * g, axis=0, keepdims=True)
        m_ref[0, hd:hd + 1, :] = jnp.broadcast_to(m_new, (1, LANES))
        yield

    n_stages = 6
    gate_stage = 4
    q_chunks[0] = qk_chunk(0)
    k_chunks[0] = qk_chunk(n_pairs)
    heads = [head_stages(hd) for hd in range(H_B)]
    y = x
    for t in range(H_B + n_stages - 1):
        if t % 2 == 0 and t // 2 + 1 < n_pairs:
            q_chunks[t // 2 + 1] = qk_chunk(t // 2 + 1)
            k_chunks[t // 2 + 1] = qk_chunk(n_pairs + t // 2 + 1)
        for hd in reversed(range(H_B)):
            if hd <= t < hd + n_stages:
                next(heads[hd])
        p, odd = divmod(t - gate_stage, 2)
        if odd and 0 <= p < n_pairs:
            lhs = jnp.concatenate([gated[2 * p], gated[2 * p + 1]], axis=1)
            y = y + jnp.dot(lhs, wout_ref[2 * p * dv:(2 * p + 2) * dv, :],
                            preferred_element_type=F32)
    y_ref[...] = y


def _mlstm_fused(x2, g, w_all, w_if, cw, cb, b_if, gn_g, w_out, *, bsz, seq, ls, k_scale):
    t, d = x2.shape
    qk_w = cw.shape[1]
    d_b = w_out.shape[0]
    dk = qk_w // (2 * H_B)
    dv = d_b // H_B
    nj = seq // ls
    row_spec = pl.BlockSpec((ls, d), lambda i, j: (i * nj + j, 0))

    def state_spec(*tail):
        return pl.BlockSpec((1,) + tail, lambda i, j: (i,) + (0,) * len(tail))

    args = [x2, g, w_all, w_if, cw, cb, b_if, gn_g, w_out]
    return pl.pallas_call(
        functools.partial(_mlstm_fused_kernel, k_scale=k_scale),
        grid=(bsz, nj),
        in_specs=([row_spec, _const_spec(g.shape), _layer_spec(w_all.shape, 0)]
                  + [_const_spec(a.shape) for a in args[3:]]),
        out_specs=[row_spec, state_spec(H_B, dk, dv), state_spec(H_B, dk),
                   state_spec(H_B, LANES), state_spec(CONV_B - 1, qk_w)],
        out_shape=[jax.ShapeDtypeStruct((t, d), F32),
                   jax.ShapeDtypeStruct((bsz, H_B, dk, dv), F32),
                   jax.ShapeDtypeStruct((bsz, H_B, dk), F32),
                   jax.ShapeDtypeStruct((bsz, H_B, LANES), F32),
                   jax.ShapeDtypeStruct((bsz, CONV_B - 1, qk_w), F32)],
        compiler_params=_params(2),
        name="mlstm_fused",
    )(*args)


def _trunks(xp, xs, state, w):
    bp, sp, d = xp.shape
    bs_, ss, _ = xs.shape
    xp2 = xp.reshape(bp * sp, d)
    xs2 = xs.reshape(bs_ * ss, d)
    sample_nb = SAMPLE_ROWS // ss

    ws_s = jnp.tile(w["a_ws"][:, :ss, :ss], (1, CHUNK_A // ss, CHUNK_A // ss))
    bs_s = jnp.tile(w["a_bs_t"][:ss], (CHUNK_A // ss, 1))
    xp2, xs2, v_rows = _mixer_a(
        xp2, xs2, w["norm_mix_g"][0:1], w["a_w_in"], w["a_ln_g"], w["a_ln_b"], w["a_ws"],
        w["a_bs_t"], ws_s, bs_s, w["a_w_out"], long_tm=PROMPT_ROWS, short_tm=SAMPLE_ROWS,
        short_chunk=ss)
    ffn_tiles = dict(fresh_bsz=bp, fresh_seq=sp, fresh_ls=PROMPT_ROWS, carried_nb=sample_nb,
                     carried_ls=ss)
    xp2, xs2, fbuf0_p, fbuf0_s = _ffn(
        xp2, xs2, w["norm_ffn_g"][0:1], w["f_w_up"], w["f_conv_w"][0], w["f_conv_b"][0:1],
        w["f_w_down"], state["ffn_conv"][0], None, layer=0, **ffn_tiles)

    xp2, c_p, n_p, m_p, mconv_p = _mlstm_fused(
        xp2, w["norm_mix_g"][1:2], w["b_w_all"], w["b_w_if"], w["b_conv_w"], w["b_conv_b"],
        w["b_bias_if"], w["b_gn_g"], w["b_w_out"], bsz=bp, seq=sp, ls=PROMPT_MLSTM_ROWS,
        k_scale=w["k_scale"])
    q, k, v, o, ig, lf, mconv_s = _mlstm_proj(
        xs2, w["norm_mix_g"][1:2], w["b_w_all"], w["b_w_i"], w["b_w_f"], w["b_conv_w"],
        w["b_conv_b"], w["b_bias_i"], w["b_bias_f"], state["mlstm_conv"],
        d_b=w["b_w_out"].shape[0], bsz=bs_, seq=ss, nb=sample_nb, ls=ss, k_scale=w["k_scale"])
    n0 = jnp.transpose(state["mlstm_n"], (1, 0, 2))
    m0 = jnp.pad(state["mlstm_m"], ((0, 0), (0, LANES - H_B)))
    h, c_s, n_s, m_s = _mlstm_step(q, k, v, ig, lf, state["mlstm_C"], n0, m0, bsz=bs_,
                                   nb=STEP_SEQS, ls=ss)
    xs2 = _mlstm_out(h, o, xs2, w["b_gn_g"], w["b_w_out"], tm=SAMPLE_ROWS)
    yp2, ys2, fbuf1_p, fbuf1_s = _ffn(
        xp2, xs2, w["norm_ffn_g"][1:2], w["f_w_up"], w["f_conv_w"][1], w["f_conv_b"][1:2],
        w["f_w_down"], state["ffn_conv"][1], w["final_norm_g"], layer=1, **ffn_tiles)
    prompt = dict(y=yp2.reshape(bp, sp, d), C=c_p[None], n=n_p[None], m=m_p[:, :, 0][None],
                  mconv=mconv_p[None], fconv=jnp.stack([fbuf0_p, fbuf1_p]))
    sample = dict(y=ys2.reshape(bs_, ss, d), v=v_rows.reshape(1, bs_, ss, -1), C=c_s[None],
                  n=jnp.transpose(n_s, (1, 0, 2))[None], m=m_s[:, :H_B][None],
                  mconv=mconv_s[None], fconv=jnp.stack([fbuf0_s, fbuf1_s]))
    return prompt, sample


def kernel(x_prompt, x_sample, state_mlstm_C, state_mlstm_n, state_mlstm_m, state_mlstm_conv, state_ffn_conv, norm_mix_g, norm_ffn_g, final_norm_g, a_w_in, a_ln_g, a_ln_b, a_w_s, a_b_s, a_w_out, b_w_in, b_conv_w, b_conv_b, b_bias_i, b_bias_f, b_gn_g, b_w_out, f_w_up, f_conv_w, f_conv_b, f_w_down):
    d_ff = f_w_down.shape[1]
    qk_w = b_conv_w.shape[2]
    d_b = b_w_out.shape[1]
    dk = qk_w // (2 * H_B)
    d_a = a_w_out.shape[1]
    dg = d_a // G_A

    def pad_gate(cols):
        return jnp.pad(cols, ((0, 0), (0, LANES - H_B)))

    w_in_b = b_w_in[0]
    w = dict(
        norm_mix_g=norm_mix_g, norm_ffn_g=norm_ffn_g, final_norm_g=final_norm_g[None, :],
        a_w_in=a_w_in[0].astype(BF16), a_ln_g=a_ln_g, a_ln_b=a_ln_b,
        a_ws=a_w_s[0].astype(BF16),
        a_bs_t=jnp.repeat(jnp.transpose(a_b_s[0]), dg, axis=1),
        a_w_out=a_w_out[0].astype(BF16),
        b_w_all=b_w_in.astype(BF16),
        b_w_i=pad_gate(w_in_b[:, qk_w + 2 * d_b:qk_w + 2 * d_b + H_B]).astype(BF16),
        b_w_f=pad_gate(w_in_b[:, qk_w + 2 * d_b + H_B:]).astype(BF16),
        b_conv_w=b_conv_w[0], b_conv_b=b_conv_b,
        b_bias_i=jnp.pad(b_bias_i, ((0, 0), (0, LANES - H_B))),
        b_bias_f=jnp.pad(b_bias_f, ((0, 0), (0, LANES - H_B))),
        b_w_if=jnp.pad(w_in_b[:, qk_w + 2 * d_b:], ((0, 0), (0, LANES - 2 * H_B))).astype(BF16),
        b_bias_if=jnp.pad(jnp.concatenate([b_bias_i, b_bias_f], axis=1),
                          ((0, 0), (0, LANES - 2 * H_B))),
        b_gn_g=b_gn_g, b_w_out=b_w_out[0].astype(BF16),
        f_w_up=f_w_up.astype(BF16),
        f_conv_w=f_conv_w, f_conv_b=f_conv_b, f_w_down=f_w_down.astype(BF16),
        k_scale=float(dk) ** -0.5,
    )
    state = dict(mlstm_C=state_mlstm_C[0], mlstm_n=state_mlstm_n[0], mlstm_m=state_mlstm_m[0],
                 mlstm_conv=state_mlstm_conv[0], ffn_conv=state_ffn_conv)
    p, s = _trunks(x_prompt, x_sample, state, w)
    return (p["y"], s["y"], p["C"], p["n"], p["m"], p["mconv"], p["fconv"],
            s["v"], s["C"], s["n"], s["m"], s["mconv"], s["fconv"])
```

```python
import functools

import jax
import jax.numpy as jnp
from jax import lax
from jax.experimental import pallas as pl
from jax.experimental.pallas import tpu as pltpu

F32 = jnp.float32
BF16 = jnp.bfloat16

EPS = 1e-6
LANES = 128
SUBLANES = 8
VMEM_LIMIT_BYTES = 56 * 1024 * 1024

CHUNK_A = 128
G_A = 8
H_B = 8
CONV_B = 4
CONV_F = 3
LAYER_A = 0
LAYER_B = 1

PROMPT_ROWS = 512
PROMPT_MLSTM_ROWS = 256
SAMPLE_ROWS = 256
STEP_SEQS = 8


def _const_spec(shape):
    nd = len(shape)
    return pl.BlockSpec(shape, lambda *_: (0,) * nd, pipeline_mode=pl.Buffered(1))


def _layer_spec(shape, layer):
    return pl.BlockSpec((1,) + tuple(shape[1:]), lambda *_: (layer, 0, 0),
                        pipeline_mode=pl.Buffered(1))


def _params(n_axes):
    return pltpu.CompilerParams(
        dimension_semantics=("arbitrary",) * n_axes,
        vmem_limit_bytes=VMEM_LIMIT_BYTES,
    )


def _software_pipeline(chains, n_stages):
    for t in range(len(chains) + n_stages - 1):
        for i in reversed(range(len(chains))):
            if i <= t < i + n_stages:
                next(chains[i])


def _rmsnorm(x, g):
    ms = jnp.mean(x * x, axis=-1, keepdims=True)
    return x * lax.rsqrt(ms + EPS) * g


def _expand3(c3, nb, ls):
    w = c3.shape[-1]
    if nb == 1:
        return jnp.broadcast_to(c3.reshape(1, w), (ls, w))
    return jnp.broadcast_to(c3, (nb, ls, w)).reshape(nb * ls, w)


def _expand2(c2, nb, ls):
    return _expand3(c2[:, None, :], nb, ls)


def _causal_conv(a, carry_ref, cw_ref, cb_ref, nb, ls, cols=slice(None)):
    r, c = a.shape
    kw = cw_ref.shape[0]
    rolled = [a] + [pltpu.roll(a, k, 0) for k in range(1, kw)]

    def taps(shifted):
        y = cb_ref[:, cols] + cw_ref[kw - 1:kw, cols] * shifted[0]
        for k in range(1, kw):
            y = y + cw_ref[kw - 1 - k:kw - k, cols] * shifted[k]
        return y

    grp = SUBLANES if nb == 1 else r
    grp_ls = SUBLANES if nb == 1 else ls
    tpos = lax.broadcasted_iota(jnp.int32, (grp, c), 0) & (grp_ls - 1)
    fixed = [a[0:grp]]
    for k in range(1, kw):
        sh = rolled[k][0:grp]
        for t in range(k):
            idx = kw - 1 + t - k
            prev = _expand3(carry_ref[:, idx:idx + 1, cols], nb, grp_ls)
            sh = jnp.where(tpos == t, prev, sh)
        fixed.append(sh)
    y = taps(fixed)
    if nb == 1:
        y = jnp.concatenate([y, taps([s[grp:] for s in rolled])], axis=0)
    last = rolled[kw - 1]
    if nb == 1:
        carry_ref[0, :, cols] = last[0:kw - 1, :]
    else:
        heads = last.reshape(nb, ls, c)[:, 0:kw - 1, :]
        carry_ref[0:nb - 1, :, cols] = heads[1:nb]
        carry_ref[nb - 1:nb, :, cols] = heads[0:1]
    return y


def _mixer_a_kernel(xl_ref, xs_ref, g_ref, win_ref, lng_ref, lnb_ref, wsl_ref, bsl_ref, wss_ref,
                    bss_ref, wout_ref, yl_ref, ys_ref, v_ref, *, long_steps, short_chunk):
    s = pl.program_id(0)
    shared = (g_ref.at[pl.ds(LAYER_A, 1)], win_ref, lng_ref, lnb_ref, wout_ref)

    @pl.when(s < long_steps)
    def _():
        _mixer_a_tile(xl_ref, yl_ref, None, wsl_ref, bsl_ref, *shared, CHUNK_A)

    @pl.when(s >= long_steps)
    def _():
        _mixer_a_tile(xs_ref, ys_ref, v_ref, wss_ref, bss_ref, *shared, short_chunk)


def _mixer_a_tile(x_ref, y_ref, v_ref, ws_ref, bs_ref, g_ref, win_ref, lng_ref, lnb_ref,
                  wout_ref, seq_chunk):
    emit_v = v_ref is not None
    tm = x_ref.shape[0]
    d_a = lng_ref.shape[1]
    dg = d_a // G_A
    x = x_ref[...]
    h = _rmsnorm(x, g_ref[...]).astype(BF16)
    v_pre = {}
    v_chunks = []
    ahead = 1
    for g in range(G_A + ahead):
        if g < G_A:
            v_pre[g] = jnp.dot(h, win_ref[:, d_a + g * dg:d_a + (g + 1) * dg],
                               preferred_element_type=F32)
        if g >= ahead:
            v_chunks.append(jax.nn.gelu(v_pre.pop(g - ahead)))
    t_i = lax.broadcasted_iota(jnp.int32, (CHUNK_A, CHUNK_A), 0)
    s_i = lax.broadcasted_iota(jnp.int32, (CHUNK_A, CHUNK_A), 1)
    mask = (s_i <= t_i) & ((t_i // seq_chunk) == (s_i // seq_chunk))
    stats = {}
    gated = {}

    def group_stages(g):
        cols = slice(g * dg, (g + 1) * dg)
        u_pre = jnp.dot(h, win_ref[:, cols], preferred_element_type=F32)
        yield
        vn = ((v_chunks[g] - stats["mu"]) * stats["rs"]) * lng_ref[:, cols] + lnb_ref[:, cols]
        if emit_v:
            v_ref[:, cols] = vn
        vb = vn.astype(BF16)
        w = jnp.where(mask, ws_ref[g], jnp.zeros((), BF16))
        mixed = [jnp.dot(w, vb[c * CHUNK_A:(c + 1) * CHUNK_A], preferred_element_type=F32)
                 + bs_ref[:, cols] for c in range(tm // CHUNK_A)]
        yield
        u = jax.nn.gelu(u_pre)
        gated[g] = jnp.concatenate(
            [(u[c * CHUNK_A:(c + 1) * CHUNK_A] * mixed[c]).astype(BF16)
             for c in range(tm // CHUNK_A)], axis=0)
        yield

    lead = 3
    times = (0, lead, lead + 1)
    groups = [group_stages(g) for g in range(G_A)]
    y = x
    for t in range(G_A + times[-1]):
        if t == lead:
            mu = sum(jnp.sum(v, axis=-1, keepdims=True) for v in v_chunks) / d_a
            var = sum(jnp.sum((v - mu) * (v - mu), axis=-1, keepdims=True)
                      for v in v_chunks) / d_a
            stats["mu"] = mu
            stats["rs"] = lax.rsqrt(var + EPS)
        for g in reversed(range(G_A)):
            if t - g in times:
                next(groups[g])
        p, odd = divmod(t - times[-1], 2)
        if odd and 0 <= p < G_A // 2:
            lhs = jnp.concatenate([gated[2 * p], gated[2 * p + 1]], axis=1)
            y = y + jnp.dot(lhs, wout_ref[2 * p * dg:(2 * p + 2) * dg, :],
                            preferred_element_type=F32)
    y_ref[...] = y


def _mixer_a(xl2, xs2, g, w_in, ln_g, ln_b, ws_l, bs_l, ws_s, bs_s, w_out, *, long_tm, short_tm,
             short_chunk):
    tl, d = xl2.shape
    ts = xs2.shape[0]
    d_a = ln_g.shape[1]
    n_l = tl // long_tm

    def l_rows(width):
        return pl.BlockSpec((long_tm, width), lambda s: (jnp.minimum(s, n_l - 1), 0))

    def s_rows(width):
        return pl.BlockSpec((short_tm, width), lambda s: (jnp.maximum(s - n_l, 0), 0))

    consts = [g, w_in, ln_g, ln_b, ws_l, bs_l, ws_s, bs_s, w_out]
    return pl.pallas_call(
        functools.partial(_mixer_a_kernel, long_steps=n_l, short_chunk=short_chunk),
        grid=(n_l + ts // short_tm,),
        in_specs=[l_rows(d), s_rows(d)] + [_const_spec(a.shape) for a in consts],
        out_specs=[l_rows(d), s_rows(d), s_rows(d_a)],
        out_shape=[jax.ShapeDtypeStruct((tl, d), F32),
                   jax.ShapeDtypeStruct((ts, d), F32),
                   jax.ShapeDtypeStruct((ts, d_a), F32)],
        compiler_params=_params(1),
        name="mixer_a",
    )(xl2, xs2, *consts)


def _ffn_tile(x_ref, y_ref, nbuf_ref, g_ref, wup_ref, cw_ref, cb_ref, wd_ref, fg_ref, nb, ls):
    x = x_ref[...]
    h = _rmsnorm(x, g_ref[...]).astype(BF16)
    d_ff = wd_ref.shape[1]
    a = jnp.dot(h, wup_ref[0, :, :d_ff], preferred_element_type=F32)
    gv = jnp.dot(h, wup_ref[0, :, d_ff:], preferred_element_type=F32)
    a_c = _causal_conv(a, nbuf_ref, cw_ref, cb_ref, nb, ls)
    act = (jax.nn.gelu(a_c) * gv).astype(BF16)
    y = x + jnp.dot(act, wd_ref[0], preferred_element_type=F32)
    if fg_ref is not None:
        y = _rmsnorm(y, fg_ref[...])
    y_ref[...] = y


def _ffn_kernel(*refs, layer, fresh_steps, fresh_nj, fresh_ls, carried_nb, carried_ls,
                final_norm):
    refs = list(refs)
    xf_ref, xc_ref, g_ref, wup_ref, cw_ref, cb_ref, wd_ref, buf_ref = refs[:8]
    pos = 8
    fg_ref = None
    if final_norm:
        fg_ref = refs[pos]
        pos += 1
    yf_ref, yc_ref, nbuff_ref, nbufc_ref = refs[pos:pos + 4]
    s = pl.program_id(0)
    shared = (g_ref.at[pl.ds(layer, 1)], wup_ref, cw_ref.at[layer], cb_ref.at[pl.ds(layer, 1)],
              wd_ref, fg_ref)

    @pl.when(s < fresh_steps)
    def _():
        @pl.when(s % fresh_nj == 0)
        def _():
            nbuff_ref[...] = jnp.zeros(nbuff_ref.shape, F32)

        _ffn_tile(xf_ref, yf_ref, nbuff_ref, *shared, 1, fresh_ls)

    @pl.when(s >= fresh_steps)
    def _():
        nbufc_ref[...] = buf_ref[0]
        _ffn_tile(xc_ref, yc_ref, nbufc_ref, *shared, carried_nb, carried_ls)


def _ffn(xf2, xc2, g, w_up, cw, cb, w_d, buf, final_g, *, layer, fresh_bsz, fresh_seq, fresh_ls,
         carried_nb, carried_ls):
    tf, d = xf2.shape
    tc = xc2.shape[0]
    d_ff = w_d.shape[1]
    nj = fresh_seq // fresh_ls
    fs = fresh_bsz * nj
    cs = tc // (carried_nb * carried_ls)
    final_norm = final_g is not None
    f_rows = pl.BlockSpec((fresh_ls, d), lambda s: (jnp.minimum(s, fs - 1), 0))
    c_rows = pl.BlockSpec((carried_nb * carried_ls, d), lambda s: (jnp.maximum(s - fs, 0), 0))
    f_buf = pl.BlockSpec((1, CONV_F - 1, d_ff), lambda s: (jnp.minimum(s, fs - 1) // nj, 0, 0))
    c_buf = pl.BlockSpec((carried_nb, CONV_F - 1, d_ff), lambda s: (jnp.maximum(s - fs, 0), 0, 0))
    c_buf_in = pl.BlockSpec((1, carried_nb, CONV_F - 1, d_ff),
                            lambda s: (layer, jnp.maximum(s - fs, 0), 0, 0))
    args = [xf2, xc2, g, w_up, cw, cb, w_d, buf]
    in_specs = [f_rows, c_rows, _const_spec(g.shape), _layer_spec(w_up.shape, layer),
                _const_spec(cw.shape), _const_spec(cb.shape), _layer_spec(w_d.shape, layer),
                c_buf_in]
    if final_norm:
        args.append(final_g)
        in_specs.append(_const_spec(final_g.shape))
    return pl.pallas_call(
        functools.partial(_ffn_kernel, layer=layer, fresh_steps=fs, fresh_nj=nj, fresh_ls=fresh_ls,
                          carried_nb=carried_nb, carried_ls=carried_ls, final_norm=final_norm),
        grid=(fs + cs,),
        in_specs=in_specs,
        out_specs=[f_rows, c_rows, f_buf, c_buf],
        out_shape=[jax.ShapeDtypeStruct((tf, d), F32),
                   jax.ShapeDtypeStruct((tc, d), F32),
                   jax.ShapeDtypeStruct((fresh_bsz, CONV_F - 1, d_ff), F32),
                   jax.ShapeDtypeStruct((tc // carried_ls, CONV_F - 1, d_ff), F32)],
        compiler_params=_params(1),
        name="conv_ffn",
    )(*args)


def _select_sum(sel, x):
    return jnp.dot(sel.astype(F32), x, precision=lax.Precision.HIGHEST,
                   preferred_element_type=F32)


def _log_sigmoid(x):
    return jnp.minimum(x, 0.0) - jnp.log1p(jnp.exp(-jnp.abs(x)))


def _mlstm_proj_kernel(*refs, nb, ls, zero_init, k_scale):
    refs = list(refs)
    x_ref, g_ref, wall_ref, wif_ref, cw_ref, cb_ref, bif_ref = refs[:7]
    pos = 7
    buf_ref = None
    if not zero_init:
        buf_ref = refs[pos]
        pos += 1
    q_ref, k_ref, v_ref, o_ref, ig_ref, lf_ref, nbuf_ref = refs[pos:pos + 7]

    @pl.when(pl.program_id(1) == 0)
    def _():
        if zero_init:
            nbuf_ref[...] = jnp.zeros(nbuf_ref.shape, F32)
        else:
            nbuf_ref[...] = buf_ref[...]

    h = _rmsnorm(x_ref[...], g_ref[LAYER_B:LAYER_B + 1, :]).astype(BF16)
    qk_w = cw_ref.shape[1]
    d_b = v_ref.shape[1]
    qk_pre = jnp.dot(h, wall_ref[0, :, 0:qk_w], preferred_element_type=F32)
    v_ref[...] = jnp.dot(h, wall_ref[0, :, qk_w:qk_w + d_b], preferred_element_type=F32)
    o_ref[...] = jnp.dot(h, wall_ref[0, :, qk_w + d_b:qk_w + 2 * d_b],
                         preferred_element_type=F32)
    gates = jnp.dot(h, wif_ref[...], preferred_element_type=F32) + bif_ref[...]
    ig_ref[...] = gates
    lf_ref[...] = pltpu.roll(_log_sigmoid(gates), LANES - H_B, 1)
    qk = jax.nn.silu(_causal_conv(qk_pre, nbuf_ref, cw_ref, cb_ref, nb, ls))
    dq = q_ref.shape[1]
    q_ref[...] = qk[:, :dq]
    k_ref[...] = qk[:, dq:] * k_scale


def _mlstm_proj(x2, g, w_all, w_if, cw, cb, b_if, buf, *, d_b, bsz, seq, nb, ls, k_scale):
    t, d = x2.shape
    qk_w = cw.shape[1]
    nj = seq // ls
    zero_init = buf is None

    def rows(width):
        return pl.BlockSpec((nb * ls, width), lambda i, j: (i * nj + j, 0))

    buf_spec = pl.BlockSpec((nb, CONV_B - 1, qk_w), lambda i, j: (i, 0, 0))
    args = [x2, g, w_all, w_if, cw, cb, b_if]
    in_specs = ([rows(d), _const_spec(g.shape), _layer_spec(w_all.shape, 0)]
                + [_const_spec(a.shape) for a in args[3:]])
    if not zero_init:
        args.append(buf)
        in_specs.append(buf_spec)
    return pl.pallas_call(
        functools.partial(_mlstm_proj_kernel, nb=nb, ls=ls, zero_init=zero_init, k_scale=k_scale),
        grid=(bsz // nb, nj),
        in_specs=in_specs,
        out_specs=[rows(qk_w // 2), rows(qk_w // 2), rows(d_b), rows(d_b), rows(LANES),
                   rows(LANES), buf_spec],
        out_shape=[jax.ShapeDtypeStruct((t, qk_w // 2), F32),
                   jax.ShapeDtypeStruct((t, qk_w // 2), F32),
                   jax.ShapeDtypeStruct((t, d_b), F32),
                   jax.ShapeDtypeStruct((t, d_b), F32),
                   jax.ShapeDtypeStruct((t, LANES), F32),
                   jax.ShapeDtypeStruct((t, LANES), F32),
                   jax.ShapeDtypeStruct((bsz, CONV_B - 1, qk_w), F32)],
        compiler_params=_params(2),
        name="mlstm_proj",
    )(*args)


def _mlstm_step_kernel(q_ref, k_ref, v_ref, ig_ref, lf_ref, c0_ref, n0_ref, m0_ref,
                       h_ref, c_ref, n_ref, m_ref, *, nb, ls):
    r = nb * ls
    rc = LANES
    dk = c0_ref.shape[2]
    dv = c0_ref.shape[3]

    def pad_rows(a):
        if r == rc:
            return a
        return jnp.concatenate([a, jnp.zeros((rc - r, a.shape[1]), a.dtype)], axis=0)

    r_i = lax.broadcasted_iota(jnp.int32, (r, rc), 0)
    c_i = lax.broadcasted_iota(jnp.int32, (r, rc), 1)
    mask = (c_i <= r_i) & ((r_i // ls) == (c_i // ls))
    ig = ig_ref[...]
    b_all = _select_sum(mask, pad_rows(lf_ref[...]))
    b_t = pad_rows(b_all).T
    ig_t = pad_rows(ig).T
    m0 = m0_ref[...]
    inter_all = b_all + _expand2(m0, nb, ls)
    lane = lax.broadcasted_iota(jnp.int32, (r, LANES), 1)
    m_t_of = {}

    def output_stages(hd):
        kcols = slice(hd * dk, (hd + 1) * dk)
        vcols = slice(hd * dv, (hd + 1) * dv)
        q = q_ref[:, kcols]
        qb = q.astype(BF16)
        qc = jnp.concatenate(
            [jnp.dot(qb[b * ls:(b + 1) * ls], c0_ref[b, hd].astype(BF16),
                     preferred_element_type=F32) for b in range(nb)], axis=0)
        d = jnp.where(mask, b_all[:, hd:hd + 1] - b_t[hd:hd + 1, :] + ig_t[hd:hd + 1, :],
                      -jnp.inf)
        inter = inter_all[:, hd:hd + 1]
        m_t = jnp.maximum(inter, jnp.max(d, axis=1, keepdims=True))
        m_t_of[hd] = m_t
        w_intra = jnp.exp(d - m_t)
        w_inter = jnp.exp(inter - m_t)
        yield
        kb = pad_rows(k_ref[:, kcols].astype(BF16))
        vb = pad_rows(v_ref[:, vcols].astype(BF16))
        s = lax.dot_general(qb, kb, (((1,), (1,)), ((), ())), preferred_element_type=F32)
        sc = s * w_intra
        intra = jnp.dot(sc.astype(BF16), vb, preferred_element_type=F32)
        den_intra = jnp.sum(sc, axis=1, keepdims=True)
        qn = jnp.sum(q * _expand2(n0_ref[hd], nb, ls), axis=1, keepdims=True)
        yield
        num = w_inter * qc + intra
        den = w_inter * qn + den_intra
        h_ref[:, vcols] = num / jnp.maximum(jnp.abs(den), jnp.exp(-m_t))
        yield

    _software_pipeline([output_stages(hd) for hd in range(H_B)], 3)
    mt_all = jnp.zeros((r, LANES), F32)
    for hd in range(H_B):
        mt_all = jnp.where(lane == hd, m_t_of[hd], mt_all)

    p_r = lax.broadcasted_iota(jnp.int32, (nb, rc), 0)
    p_c = lax.broadcasted_iota(jnp.int32, (nb, rc), 1)
    pick = p_c == p_r * ls + (ls - 1)
    bl_seq = _select_sum(pick, pad_rows(b_all))
    mn_seq = _select_sum(pick, pad_rows(mt_all))
    decay = jnp.exp(bl_seq + m0 - mn_seq)
    m_ref[...] = mn_seq
    g_all = jnp.exp(_expand2(bl_seq, nb, ls) - b_all + ig - _expand2(mn_seq, nb, ls))
    col_seq = lax.broadcasted_iota(jnp.int32, (dk, rc), 1) // ls

    def state_stages(hd):
        kg = k_ref[:, hd * dk:(hd + 1) * dk] * g_all[:, hd:hd + 1]
        n_ref[hd] = decay[:, hd:hd + 1] * n0_ref[hd] + jnp.sum(kg.reshape(nb, ls, dk), axis=1)
        kg_t = pad_rows(kg).T
        vb = pad_rows(v_ref[:, hd * dv:(hd + 1) * dv].astype(BF16))
        yield
        for b in range(nb):
            lhs = jnp.where(col_seq == b, kg_t, 0.0).astype(BF16)
            c_ref[b, hd] = decay[b:b + 1, hd:hd + 1] * c0_ref[b, hd] + jnp.dot(
                lhs, vb, preferred_element_type=F32)
        yield

    _software_pipeline([state_stages(hd) for hd in range(H_B)], 2)


def _mlstm_step(q, k, v, ig, lf, c0, n0, m0, *, bsz, nb, ls):
    t = q.shape[0]
    dk = q.shape[1] // H_B
    dv = v.shape[1] // H_B

    def rows(width):
        return pl.BlockSpec((nb * ls, width), lambda i: (i, 0))

    c_spec = pl.BlockSpec((nb, H_B, dk, dv), lambda i: (i, 0, 0, 0))
    n_spec = pl.BlockSpec((H_B, nb, dk), lambda i: (0, i, 0))
    m_spec = pl.BlockSpec((nb, LANES), lambda i: (i, 0))
    return pl.pallas_call(
        functools.partial(_mlstm_step_kernel, nb=nb, ls=ls),
        grid=(bsz // nb,),
        in_specs=[rows(H_B * dk), rows(H_B * dk), rows(H_B * dv), rows(LANES), rows(LANES),
                  c_spec, n_spec, m_spec],
        out_specs=[rows(H_B * dv), c_spec, n_spec, m_spec],
        out_shape=[jax.ShapeDtypeStruct((t, H_B * dv), F32),
                   jax.ShapeDtypeStruct((bsz, H_B, dk, dv), F32),
                   jax.ShapeDtypeStruct((H_B, bsz, dk), F32),
                   jax.ShapeDtypeStruct((bsz, LANES), F32)],
        compiler_params=_params(1),
        name="mlstm_step",
    )(q, k, v, ig, lf, c0, n0, m0)


def _mlstm_out_kernel(h_ref, o_ref, x_ref, gn_ref, wout_ref, y_ref, gate_ref):
    dv = h_ref.shape[1] // H_B
    for hd in range(H_B):
        cols = slice(hd * dv, (hd + 1) * dv)
        hh = h_ref[:, cols]
        mu = jnp.mean(hh, axis=-1, keepdims=True)
        hc = hh - mu
        var = jnp.mean(hc * hc, axis=-1, keepdims=True)
        hn = hc * lax.rsqrt(var + EPS) * gn_ref[:, cols]
        gate_ref[:, cols] = (jax.nn.sigmoid(o_ref[:, cols]) * hn).astype(BF16)
    y_ref[...] = x_ref[...] + jnp.dot(gate_ref[...], wout_ref[...], preferred_element_type=F32)


def _mlstm_out(h, o, x2, gn_g, w_out, *, tm):
    t, d = x2.shape
    d_b = h.shape[1]
    return pl.pallas_call(
        _mlstm_out_kernel,
        grid=(t // tm,),
        in_specs=[pl.BlockSpec((tm, d_b), lambda i: (i, 0)),
                  pl.BlockSpec((tm, d_b), lambda i: (i, 0)),
                  pl.BlockSpec((tm, d), lambda i: (i, 0)),
                  _const_spec(gn_g.shape), _const_spec(w_out.shape)],
        out_specs=pl.BlockSpec((tm, d), lambda i: (i, 0)),
        out_shape=jax.ShapeDtypeStruct((t, d), F32),
        scratch_shapes=[pltpu.VMEM((tm, d_b), BF16)],
        compiler_params=_params(1),
        name="mlstm_out",
    )(h, o, x2, gn_g, w_out)


def _mlstm_fused_kernel(x_ref, g_ref, wall_ref, wif_ref, cw_ref, cb_ref, bif_ref,
                        gn_ref, wout_ref, y_ref, c_ref, n_ref, m_ref, nbuf_ref, *, k_scale):
    @pl.when(pl.program_id(1) == 0)
    def _():
        c_ref[...] = jnp.zeros(c_ref.shape, F32)
        n_ref[...] = jnp.zeros(n_ref.shape, F32)
        m_ref[...] = jnp.zeros(m_ref.shape, F32)
        nbuf_ref[...] = jnp.zeros(nbuf_ref.shape, F32)

    r = x_ref.shape[0]
    dk = c_ref.shape[2]
    dv = c_ref.shape[3]

    x = x_ref[...]
    h = _rmsnorm(x, g_ref[LAYER_B:LAYER_B + 1, :]).astype(BF16)
    gates = jnp.dot(h, wif_ref[...], preferred_element_type=F32) + bif_ref[...]
    r_i = lax.broadcasted_iota(jnp.int32, (r, r), 0)
    c_i = lax.broadcasted_iota(jnp.int32, (r, r), 1)
    mask = c_i <= r_i
    b_all = _select_sum(mask, _log_sigmoid(gates))
    b_t = b_all.T
    ig_t = gates.T

    pair = 2 * dk
    n_pairs = H_B // 2
    qk_w = cw_ref.shape[1]

    def w_cols(base, cols):
        return wall_ref[0, :, base + cols.start:base + cols.stop]

    q_chunks = {}
    k_chunks = {}
    gated = {}

    def qk_chunk(c):
        cols = slice(c * pair, (c + 1) * pair)
        pre = jnp.dot(h, w_cols(0, cols), preferred_element_type=F32)
        return jax.nn.silu(_causal_conv(pre, nbuf_ref, cw_ref, cb_ref, 1, r, cols))

    def head_stages(hd):
        p, half = divmod(hd, 2)
        ig_col = gates[:, hd:hd + 1]
        i_row = ig_t[hd:hd + 1, :]
        b_col = b_all[:, H_B + hd:H_B + hd + 1]
        b_row = b_t[H_B + hd:H_B + hd + 1, :]
        m_prev = m_ref[0, hd:hd + 1, 0:1]
        vcols = slice(hd * dv, (hd + 1) * dv)
        vb = jnp.dot(h, w_cols(qk_w, vcols), preferred_element_type=F32).astype(BF16)
        o_pre = jnp.dot(h, w_cols(qk_w + H_B * dv, vcols), preferred_element_type=F32)
        d = jnp.where(mask, b_col - b_row + i_row, -jnp.inf)
        yield
        inter = b_col + m_prev
        m_t = jnp.maximum(inter, jnp.max(d, axis=1, keepdims=True))
        w_intra = jnp.exp(d - m_t)
        w_inter = jnp.exp(inter - m_t)
        yield
        q = q_chunks[p][:, half * dk:(half + 1) * dk]
        k = k_chunks[p][:, half * dk:(half + 1) * dk] * k_scale
        qb = q.astype(BF16)
        s = lax.dot_general(qb, k.astype(BF16), (((1,), (1,)), ((), ())),
                            preferred_element_type=F32)
        sc = s * w_intra
        intra = jnp.dot(sc.astype(BF16), vb, preferred_element_type=F32)
        den_intra = jnp.sum(sc, axis=1, keepdims=True)
        c_old = c_ref[0, hd]
        n_old = n_ref[0, hd:hd + 1, :]
        qc = jnp.dot(qb, c_old.astype(BF16), preferred_element_type=F32)
        qn = jnp.sum(q * n_old, axis=1, keepdims=True)
        yield
        num = w_inter * qc + intra
        den = w_inter * qn + den_intra
        hout = num / jnp.maximum(jnp.abs(den), jnp.exp(-m_t))
        mu = jnp.mean(hout, axis=-1, keepdims=True)
        hc = hout - mu
        var = jnp.mean(hc * hc, axis=-1, keepdims=True)
        hn = hc * lax.rsqrt(var + EPS) * gn_ref[:, vcols]
        yield
        gated[hd] = (jax.nn.sigmoid(o_pre) * hn).astype(BF16)
        b_last = b_col[r - 1:r, :]
        m_new = m_t[r - 1:r, :]
        g = jnp.exp(b_last - b_col + ig_col - m_new)
        g_row = jnp.exp(b_last - b_row + i_row - m_new)
        decay = jnp.exp(b_last + m_prev - m_new)
        yield
        kg_t = (k.T * g_row).astype(BF16)
        c_ref[0, hd] = decay * c_old + jnp.dot(kg_t, vb, preferred_element_type=F32)
        n_ref[0, hd:hd + 1, :] = decay * n_old + jnp.sum(k * g, axis=0, keepdims=True)
        m_ref[0, hd:hd + 1, :] = jnp.broadcast_to(m_new, (1, LANES))
        yield

    n_stages = 6
    gate_stage = 4
    q_chunks[0] = qk_chunk(0)
    k_chunks[0] = qk_chunk(n_pairs)
    heads = [head_stages(hd) for hd in range(H_B)]
    y = x
    for t in range(H_B + n_stages - 1):
        if t % 2 == 0 and t // 2 + 1 < n_pairs:
            q_chunks[t // 2 + 1] = qk_chunk(t // 2 + 1)
            k_chunks[t // 2 + 1] = qk_chunk(n_pairs + t // 2 + 1)
        for hd in reversed(range(H_B)):
            if hd <= t < hd + n_stages:
                next(heads[hd])
        p, odd = divmod(t - gate_stage, 2)
        if odd and 0 <= p < n_pairs:
            lhs = jnp.concatenate([gated[2 * p], gated[2 * p + 1]], axis=1)
            y = y + jnp.dot(lhs, wout_ref[2 * p * dv:(2 * p + 2) * dv, :],
                            preferred_element_type=F32)
    y_ref[...] = y


def _mlstm_fused(x2, g, w_all, w_if, cw, cb, b_if, gn_g, w_out, *, bsz, seq, ls, k_scale):
    t, d = x2.shape
    qk_w = cw.shape[1]
    d_b = w_out.shape[0]
    dk = qk_w // (2 * H_B)
    dv = d_b // H_B
    nj = seq // ls
    row_spec = pl.BlockSpec((ls, d), lambda i, j: (i * nj + j, 0))

    def state_spec(*tail):
        return pl.BlockSpec((1,) + tail, lambda i, j: (i,) + (0,) * len(tail))

    args = [x2, g, w_all, w_if, cw, cb, b_if, gn_g, w_out]
    return pl.pallas_call(
        functools.partial(_mlstm_fused_kernel, k_scale=k_scale),
        grid=(bsz, nj),
        in_specs=([row_spec, _const_spec(g.shape), _layer_spec(w_all.shape, 0)]
                  + [_const_spec(a.shape) for a in args[3:]]),
        out_specs=[row_spec, state_spec(H_B, dk, dv), state_spec(H_B, dk),
                   state_spec(H_B, LANES), state_spec(CONV_B - 1, qk_w)],
        out_shape=[jax.ShapeDtypeStruct((t, d), F32),
                   jax.ShapeDtypeStruct((bsz, H_B, dk, dv), F32),
                   jax.ShapeDtypeStruct((bsz, H_B, dk), F32),
                   jax.ShapeDtypeStruct((bsz, H_B, LANES), F32),
                   jax.ShapeDtypeStruct((bsz, CONV_B - 1, qk_w), F32)],
        compiler_params=_params(2),
        name="mlstm_fused",
    )(*args)


def _trunks(xp, xs, state, w):
    bp, sp, d = xp.shape
    bs_, ss, _ = xs.shape
    assert sp % PROMPT_ROWS == 0 and PROMPT_ROWS % PROMPT_MLSTM_ROWS == 0
    assert PROMPT_MLSTM_ROWS % CHUNK_A == 0 and PROMPT_MLSTM_ROWS & (PROMPT_MLSTM_ROWS - 1) == 0
    assert ss & (ss - 1) == 0 and ss % SUBLANES == 0 and ss <= CHUNK_A
    assert (bs_ * ss) % SAMPLE_ROWS == 0 and bs_ % STEP_SEQS == 0 and STEP_SEQS * ss <= LANES
    xp2 = xp.reshape(bp * sp, d)
    xs2 = xs.reshape(bs_ * ss, d)
    sample_nb = SAMPLE_ROWS // ss

    ws_s = jnp.tile(w["a_ws"][:, :ss, :ss], (1, CHUNK_A // ss, CHUNK_A // ss))
    bs_s = jnp.tile(w["a_bs_t"][:ss], (CHUNK_A // ss, 1))
    xp2, xs2, v_rows = _mixer_a(
        xp2, xs2, w["norm_mix_g"], w["a_w_in"], w["a_ln_g"], w["a_ln_b"], w["a_ws"],
        w["a_bs_t"], ws_s, bs_s, w["a_w_out"], long_tm=PROMPT_ROWS, short_tm=SAMPLE_ROWS,
        short_chunk=ss)
    ffn_tiles = dict(fresh_bsz=bp, fresh_seq=sp, fresh_ls=PROMPT_ROWS, carried_nb=sample_nb,
                     carried_ls=ss)
    xp2, xs2, fbuf0_p, fbuf0_s = _ffn(
        xp2, xs2, w["norm_ffn_g"], w["f_w_up"], w["f_conv_w"], w["f_conv_b"], w["f_w_down"],
        state["ffn_conv"], None, layer=0, **ffn_tiles)

    xp2, c_p, n_p, m_p, mconv_p = _mlstm_fused(
        xp2, w["norm_mix_g"], w["b_w_all"], w["b_w_if"], w["b_conv_w"], w["b_conv_b"],
        w["b_bias_if"], w["b_gn_g"], w["b_w_out"], bsz=bp, seq=sp, ls=PROMPT_MLSTM_ROWS,
        k_scale=w["k_scale"])
    q, k, v, o, ig, lf, mconv_s = _mlstm_proj(
        xs2, w["norm_mix_g"], w["b_w_all"], w["b_w_if"], w["b_conv_w"], w["b_conv_b"],
        w["b_bias_if"], state["mlstm_conv"],
        d_b=w["b_w_out"].shape[0], bsz=bs_, seq=ss, nb=sample_nb, ls=ss, k_scale=w["k_scale"])
    n0 = jnp.transpose(state["mlstm_n"], (1, 0, 2))
    m0 = jnp.pad(state["mlstm_m"], ((0, 0), (0, LANES - H_B)))
    h, c_s, n_s, m_s = _mlstm_step(q, k, v, ig, lf, state["mlstm_C"], n0, m0, bsz=bs_,
                                   nb=STEP_SEQS, ls=ss)
    xs2 = _mlstm_out(h, o, xs2, w["b_gn_g"], w["b_w_out"], tm=SAMPLE_ROWS)
    yp2, ys2, fbuf1_p, fbuf1_s = _ffn(
        xp2, xs2, w["norm_ffn_g"], w["f_w_up"], w["f_conv_w"], w["f_conv_b"], w["f_w_down"],
        state["ffn_conv"], w["final_norm_g"], layer=1, **ffn_tiles)
    prompt = dict(y=yp2.reshape(bp, sp, d), C=c_p[None], n=n_p[None], m=m_p[:, :, 0][None],
                  mconv=mconv_p[None], fconv=jnp.stack([fbuf0_p, fbuf1_p]))
    sample = dict(y=ys2.reshape(bs_, ss, d), v=v_rows.reshape(1, bs_, ss, -1), C=c_s[None],
                  n=jnp.transpose(n_s, (1, 0, 2))[None], m=m_s[:, :H_B][None],
                  mconv=mconv_s[None], fconv=jnp.stack([fbuf0_s, fbuf1_s]))
    return prompt, sample


def kernel(x_prompt, x_sample, state_mlstm_C, state_mlstm_n, state_mlstm_m, state_mlstm_conv, state_ffn_conv, norm_mix_g, norm_ffn_g, final_norm_g, a_w_in, a_ln_g, a_ln_b, a_w_s, a_b_s, a_w_out, b_w_in, b_conv_w, b_conv_b, b_bias_i, b_bias_f, b_gn_g, b_w_out, f_w_up, f_conv_w, f_conv_b, f_w_down):
    d_ff = f_w_down.shape[1]
    qk_w = b_conv_w.shape[2]
    d_b = b_w_out.shape[1]
    dk = qk_w // (2 * H_B)
    d_a = a_w_out.shape[1]
    dg = d_a // G_A

    w_in_b = b_w_in[0]
    w = dict(
        norm_mix_g=norm_mix_g, norm_ffn_g=norm_ffn_g, final_norm_g=final_norm_g[None, :],
        a_w_in=a_w_in[0].astype(BF16), a_ln_g=a_ln_g, a_ln_b=a_ln_b,
        a_ws=a_w_s[0].astype(BF16),
        a_bs_t=jnp.repeat(jnp.transpose(a_b_s[0]), dg, axis=1),
        a_w_out=a_w_out[0].astype(BF16),
        b_w_all=b_w_in.astype(BF16),
        b_conv_w=b_conv_w[0], b_conv_b=b_conv_b,
        b_w_if=jnp.pad(w_in_b[:, qk_w + 2 * d_b:], ((0, 0), (0, LANES - 2 * H_B))).astype(BF16),
        b_bias_if=jnp.pad(jnp.concatenate([b_bias_i, b_bias_f], axis=1),
                          ((0, 0), (0, LANES - 2 * H_B))),
        b_gn_g=b_gn_g, b_w_out=b_w_out[0].astype(BF16),
        f_w_up=f_w_up.astype(BF16),
        f_conv_w=f_conv_w, f_conv_b=f_conv_b, f_w_down=f_w_down.astype(BF16),
        k_scale=float(dk) ** -0.5,
    )
    state = dict(mlstm_C=state_mlstm_C[0], mlstm_n=state_mlstm_n[0], mlstm_m=state_mlstm_m[0],
                 mlstm_conv=state_mlstm_conv[0], ffn_conv=state_ffn_conv)
    p, s = _trunks(x_prompt, x_sample, state, w)
    return (p["y"], s["y"], p["C"], p["n"], p["m"], p["mconv"], p["fconv"],
            s["v"], s["C"], s["n"], s["m"], s["mconv"], s["fconv"])
```

```python
import functools

import jax
import jax.numpy as jnp
from jax import lax
from jax.experimental import pallas as pl
from jax.experimental.pallas import tpu as pltpu

F32 = jnp.float32
BF16 = jnp.bfloat16

EPS = 1e-6
LANES = 128
SUBLANES = 8
VMEM_LIMIT_BYTES = 56 * 1024 * 1024

CHUNK_A = 128
G_A = 8
H_B = 8
CONV_B = 4
CONV_F = 3
LAYER_A = 0
LAYER_B = 1

PROMPT_ROWS = 512
PROMPT_MLSTM_ROWS = 256
SAMPLE_ROWS = 256
STEP_SEQS = 8


def _const_spec(shape):
    nd = len(shape)
    return pl.BlockSpec(shape, lambda *_: (0,) * nd, pipeline_mode=pl.Buffered(1))


def _layer_spec(shape, layer):
    return pl.BlockSpec((1,) + tuple(shape[1:]), lambda *_: (layer, 0, 0),
                        pipeline_mode=pl.Buffered(1))


def _params(n_axes):
    return pltpu.CompilerParams(
        dimension_semantics=("arbitrary",) * n_axes,
        vmem_limit_bytes=VMEM_LIMIT_BYTES,
    )


def _software_pipeline(chains, n_stages):
    for t in range(len(chains) + n_stages - 1):
        for i in reversed(range(len(chains))):
            if i <= t < i + n_stages:
                next(chains[i])


def _rmsnorm(x, g):
    ms = jnp.mean(x * x, axis=-1, keepdims=True)
    return x * lax.rsqrt(ms + EPS) * g


def _expand3(c3, nb, ls):
    w = c3.shape[-1]
    if nb == 1:
        return jnp.broadcast_to(c3.reshape(1, w), (ls, w))
    return jnp.broadcast_to(c3, (nb, ls, w)).reshape(nb * ls, w)


def _expand2(c2, nb, ls):
    return _expand3(c2[:, None, :], nb, ls)


def _causal_conv(a, carry_ref, cw_ref, cb_ref, nb, ls, cols=slice(None)):
    r, c = a.shape
    kw = cw_ref.shape[0]
    rolled = [a] + [pltpu.roll(a, k, 0) for k in range(1, kw)]

    def taps(shifted):
        y = cb_ref[:, cols] + cw_ref[kw - 1:kw, cols] * shifted[0]
        for k in range(1, kw):
            y = y + cw_ref[kw - 1 - k:kw - k, cols] * shifted[k]
        return y

    grp = SUBLANES if nb == 1 else r
    grp_ls = SUBLANES if nb == 1 else ls
    tpos = lax.broadcasted_iota(jnp.int32, (grp, c), 0) & (grp_ls - 1)
    fixed = [a[0:grp]]
    for k in range(1, kw):
        sh = rolled[k][0:grp]
        for t in range(k):
            idx = kw - 1 + t - k
            prev = _expand3(carry_ref[:, idx:idx + 1, cols], nb, grp_ls)
            sh = jnp.where(tpos == t, prev, sh)
        fixed.append(sh)
    y = taps(fixed)
    if nb == 1:
        y = jnp.concatenate([y, taps([s[grp:] for s in rolled])], axis=0)
    last = rolled[kw - 1]
    if nb == 1:
        carry_ref[0, :, cols] = last[0:kw - 1, :]
    else:
        heads = last.reshape(nb, ls, c)[:, 0:kw - 1, :]
        carry_ref[0:nb - 1, :, cols] = heads[1:nb]
        carry_ref[nb - 1:nb, :, cols] = heads[0:1]
    return y


def _mixer_a_kernel(xl_ref, xs_ref, g_ref, win_ref, lng_ref, lnb_ref, wsl_ref, bsl_ref, wss_ref,
                    bss_ref, wout_ref, yl_ref, ys_ref, v_ref, *, long_steps, short_chunk):
    s = pl.program_id(0)
    shared = (g_ref.at[pl.ds(LAYER_A, 1)], win_ref, lng_ref, lnb_ref, wout_ref)

    @pl.when(s < long_steps)
    def _():
        _mixer_a_tile(xl_ref, yl_ref, None, wsl_ref, bsl_ref, *shared, CHUNK_A)

    @pl.when(s >= long_steps)
    def _():
        _mixer_a_tile(xs_ref, ys_ref, v_ref, wss_ref, bss_ref, *shared, short_chunk)


def _mixer_a_tile(x_ref, y_ref, v_ref, ws_ref, bs_ref, g_ref, win_ref, lng_ref, lnb_ref,
                  wout_ref, seq_chunk):
    emit_v = v_ref is not None
    tm = x_ref.shape[0]
    d_a = lng_ref.shape[1]
    dg = d_a // G_A
    x = x_ref[...]
    h = _rmsnorm(x, g_ref[...]).astype(BF16)
    v_pre = {}
    v_chunks = []
    ahead = 1
    for g in range(G_A + ahead):
        if g < G_A:
            v_pre[g] = jnp.dot(h, win_ref[:, d_a + g * dg:d_a + (g + 1) * dg],
                               preferred_element_type=F32)
        if g >= ahead:
            v_chunks.append(jax.nn.gelu(v_pre.pop(g - ahead)))
    t_i = lax.broadcasted_iota(jnp.int32, (CHUNK_A, CHUNK_A), 0)
    s_i = lax.broadcasted_iota(jnp.int32, (CHUNK_A, CHUNK_A), 1)
    mask = (s_i <= t_i) & ((t_i // seq_chunk) == (s_i // seq_chunk))
    stats = {}
    gated = {}

    def group_stages(g):
        cols = slice(g * dg, (g + 1) * dg)
        u_pre = jnp.dot(h, win_ref[:, cols], preferred_element_type=F32)
        yield
        vn = ((v_chunks[g] - stats["mu"]) * stats["rs"]) * lng_ref[:, cols] + lnb_ref[:, cols]
        if emit_v:
            v_ref[:, cols] = vn
        vb = vn.astype(BF16)
        w = jnp.where(mask, ws_ref[g], jnp.zeros((), BF16))
        mixed = [jnp.dot(w, vb[c * CHUNK_A:(c + 1) * CHUNK_A], preferred_element_type=F32)
                 + bs_ref[:, cols] for c in range(tm // CHUNK_A)]
        yield
        u = jax.nn.gelu(u_pre)
        gated[g] = jnp.concatenate(
            [(u[c * CHUNK_A:(c + 1) * CHUNK_A] * mixed[c]).astype(BF16)
             for c in range(tm // CHUNK_A)], axis=0)
        yield

    lead = 3
    times = (0, lead, lead + 1)
    groups = [group_stages(g) for g in range(G_A)]
    y = x
    for t in range(G_A + times[-1]):
        if t == lead:
            mu = sum(jnp.sum(v, axis=-1, keepdims=True) for v in v_chunks) / d_a
            var = sum(jnp.sum((v - mu) * (v - mu), axis=-1, keepdims=True)
                      for v in v_chunks) / d_a
            stats["mu"] = mu
            stats["rs"] = lax.rsqrt(var + EPS)
        for g in reversed(range(G_A)):
            if t - g in times:
                next(groups[g])
        p, odd = divmod(t - times[-1], 2)
        if odd and 0 <= p < G_A // 2:
            lhs = jnp.concatenate([gated[2 * p], gated[2 * p + 1]], axis=1)
            y = y + jnp.dot(lhs, wout_ref[2 * p * dg:(2 * p + 2) * dg, :],
                            preferred_element_type=F32)
    y_ref[...] = y


def _mixer_a(xl2, xs2, g, w_in, ln_g, ln_b, ws_l, bs_l, ws_s, bs_s, w_out, *, long_tm, short_tm,
             short_chunk):
    tl, d = xl2.shape
    ts = xs2.shape[0]
    d_a = ln_g.shape[1]
    n_l = tl // long_tm

    def l_rows(width):
        return pl.BlockSpec((long_tm, width), lambda s: (jnp.minimum(s, n_l - 1), 0))

    def s_rows(width):
        return pl.BlockSpec((short_tm, width), lambda s: (jnp.maximum(s - n_l, 0), 0))

    consts = [g, w_in, ln_g, ln_b, ws_l, bs_l, ws_s, bs_s, w_out]
    return pl.pallas_call(
        functools.partial(_mixer_a_kernel, long_steps=n_l, short_chunk=short_chunk),
        grid=(n_l + ts // short_tm,),
        in_specs=[l_rows(d), s_rows(d)] + [_const_spec(a.shape) for a in consts],
        out_specs=[l_rows(d), s_rows(d), s_rows(d_a)],
        out_shape=[jax.ShapeDtypeStruct((tl, d), F32),
                   jax.ShapeDtypeStruct((ts, d), F32),
                   jax.ShapeDtypeStruct((ts, d_a), F32)],
        compiler_params=_params(1),
        name="mixer_a",
    )(xl2, xs2, *consts)


def _ffn_tile(x_ref, y_ref, nbuf_ref, g_ref, wup_ref, cw_ref, cb_ref, wd_ref, fg_ref, nb, ls):
    x = x_ref[...]
    h = _rmsnorm(x, g_ref[...]).astype(BF16)
    d_ff = wd_ref.shape[1]
    a = jnp.dot(h, wup_ref[0, :, :d_ff], preferred_element_type=F32)
    gv = jnp.dot(h, wup_ref[0, :, d_ff:], preferred_element_type=F32)
    a_c = _causal_conv(a, nbuf_ref, cw_ref, cb_ref, nb, ls)
    act = (jax.nn.gelu(a_c) * gv).astype(BF16)
    y = x + jnp.dot(act, wd_ref[0], preferred_element_type=F32)
    if fg_ref is not None:
        y = _rmsnorm(y, fg_ref[...])
    y_ref[...] = y


def _ffn_kernel(*refs, layer, fresh_steps, fresh_nj, fresh_ls, carried_nb, carried_ls,
                final_norm):
    refs = list(refs)
    xf_ref, xc_ref, g_ref, wup_ref, cw_ref, cb_ref, wd_ref, buf_ref = refs[:8]
    pos = 8
    fg_ref = None
    if final_norm:
        fg_ref = refs[pos]
        pos += 1
    yf_ref, yc_ref, nbuff_ref, nbufc_ref = refs[pos:pos + 4]
    s = pl.program_id(0)
    shared = (g_ref.at[pl.ds(layer, 1)], wup_ref, cw_ref.at[layer], cb_ref.at[pl.ds(layer, 1)],
              wd_ref, fg_ref)

    @pl.when(s < fresh_steps)
    def _():
        @pl.when(s % fresh_nj == 0)
        def _():
            nbuff_ref[...] = jnp.zeros(nbuff_ref.shape, F32)

        _ffn_tile(xf_ref, yf_ref, nbuff_ref, *shared, 1, fresh_ls)

    @pl.when(s >= fresh_steps)
    def _():
        nbufc_ref[...] = buf_ref[0]
        _ffn_tile(xc_ref, yc_ref, nbufc_ref, *shared, carried_nb, carried_ls)


def _ffn(xf2, xc2, g, w_up, cw, cb, w_d, buf, final_g, *, layer, fresh_bsz, fresh_seq, fresh_ls,
         carried_nb, carried_ls):
    tf, d = xf2.shape
    tc = xc2.shape[0]
    d_ff = w_d.shape[1]
    nj = fresh_seq // fresh_ls
    fs = fresh_bsz * nj
    cs = tc // (carried_nb * carried_ls)
    final_norm = final_g is not None
    f_rows = pl.BlockSpec((fresh_ls, d), lambda s: (jnp.minimum(s, fs - 1), 0))
    c_rows = pl.BlockSpec((carried_nb * carried_ls, d), lambda s: (jnp.maximum(s - fs, 0), 0))
    f_buf = pl.BlockSpec((1, CONV_F - 1, d_ff), lambda s: (jnp.minimum(s, fs - 1) // nj, 0, 0))
    c_buf = pl.BlockSpec((carried_nb, CONV_F - 1, d_ff), lambda s: (jnp.maximum(s - fs, 0), 0, 0))
    c_buf_in = pl.BlockSpec((1, carried_nb, CONV_F - 1, d_ff),
                            lambda s: (layer, jnp.maximum(s - fs, 0), 0, 0))
    args = [xf2, xc2, g, w_up, cw, cb, w_d, buf]
    in_specs = [f_rows, c_rows, _const_spec(g.shape), _layer_spec(w_up.shape, layer),
                _const_spec(cw.shape), _const_spec(cb.shape), _layer_spec(w_d.shape, layer),
                c_buf_in]
    if final_norm:
        args.append(final_g)
        in_specs.append(_const_spec(final_g.shape))
    return pl.pallas_call(
        functools.partial(_ffn_kernel, layer=layer, fresh_steps=fs, fresh_nj=nj, fresh_ls=fresh_ls,
                          carried_nb=carried_nb, carried_ls=carried_ls, final_norm=final_norm),
        grid=(fs + cs,),
        in_specs=in_specs,
        out_specs=[f_rows, c_rows, f_buf, c_buf],
        out_shape=[jax.ShapeDtypeStruct((tf, d), F32),
                   jax.ShapeDtypeStruct((tc, d), F32),
                   jax.ShapeDtypeStruct((fresh_bsz, CONV_F - 1, d_ff), F32),
                   jax.ShapeDtypeStruct((tc // carried_ls, CONV_F - 1, d_ff), F32)],
        compiler_params=_params(1),
        name="conv_ffn",
    )(*args)


def _select_sum(sel, x):
    return jnp.dot(sel.astype(F32), x, precision=lax.Precision.HIGHEST,
                   preferred_element_type=F32)


def _log_sigmoid(x):
    return jnp.minimum(x, 0.0) - jnp.log1p(jnp.exp(-jnp.abs(x)))


def _mlstm_proj_kernel(*refs, nb, ls, zero_init, k_scale):
    refs = list(refs)
    x_ref, g_ref, wall_ref, wif_ref, cw_ref, cb_ref, bif_ref = refs[:7]
    pos = 7
    buf_ref = None
    if not zero_init:
        buf_ref = refs[pos]
        pos += 1
    q_ref, k_ref, v_ref, o_ref, ig_ref, lf_ref, nbuf_ref = refs[pos:pos + 7]

    @pl.when(pl.program_id(1) == 0)
    def _():
        if zero_init:
            nbuf_ref[...] = jnp.zeros(nbuf_ref.shape, F32)
        else:
            nbuf_ref[...] = buf_ref[...]

    h = _rmsnorm(x_ref[...], g_ref[LAYER_B:LAYER_B + 1, :]).astype(BF16)
    qk_w = cw_ref.shape[1]
    d_b = v_ref.shape[1]
    qk_pre = jnp.dot(h, wall_ref[0, :, 0:qk_w], preferred_element_type=F32)
    v_ref[...] = jnp.dot(h, wall_ref[0, :, qk_w:qk_w + d_b], preferred_element_type=F32)
    o_ref[...] = jnp.dot(h, wall_ref[0, :, qk_w + d_b:qk_w + 2 * d_b],
                         preferred_element_type=F32)
    gates = jnp.dot(h, wif_ref[...], preferred_element_type=F32) + bif_ref[...]
    ig_ref[...] = gates
    lf_ref[...] = pltpu.roll(_log_sigmoid(gates), LANES - H_B, 1)
    qk = jax.nn.silu(_causal_conv(qk_pre, nbuf_ref, cw_ref, cb_ref, nb, ls))
    dq = q_ref.shape[1]
    q_ref[...] = qk[:, :dq]
    k_ref[...] = qk[:, dq:] * k_scale


def _mlstm_proj(x2, g, w_all, w_if, cw, cb, b_if, buf, *, d_b, bsz, seq, nb, ls, k_scale):
    t, d = x2.shape
    qk_w = cw.shape[1]
    nj = seq // ls
    zero_init = buf is None

    def rows(width):
        return pl.BlockSpec((nb * ls, width), lambda i, j: (i * nj + j, 0))

    buf_spec = pl.BlockSpec((nb, CONV_B - 1, qk_w), lambda i, j: (i, 0, 0))
    args = [x2, g, w_all, w_if, cw, cb, b_if]
    in_specs = ([rows(d), _const_spec(g.shape), _layer_spec(w_all.shape, 0)]
                + [_const_spec(a.shape) for a in args[3:]])
    if not zero_init:
        args.append(buf)
        in_specs.append(buf_spec)
    return pl.pallas_call(
        functools.partial(_mlstm_proj_kernel, nb=nb, ls=ls, zero_init=zero_init, k_scale=k_scale),
        grid=(bsz // nb, nj),
        in_specs=in_specs,
        out_specs=[rows(qk_w // 2), rows(qk_w // 2), rows(d_b), rows(d_b), rows(LANES),
                   rows(LANES), buf_spec],
        out_shape=[jax.ShapeDtypeStruct((t, qk_w // 2), F32),
                   jax.ShapeDtypeStruct((t, qk_w // 2), F32),
                   jax.ShapeDtypeStruct((t, d_b), F32),
                   jax.ShapeDtypeStruct((t, d_b), F32),
                   jax.ShapeDtypeStruct((t, LANES), F32),
                   jax.ShapeDtypeStruct((t, LANES), F32),
                   jax.ShapeDtypeStruct((bsz, CONV_B - 1, qk_w), F32)],
        compiler_params=_params(2),
        name="mlstm_proj",
    )(*args)


def _mlstm_step_kernel(q_ref, k_ref, v_ref, ig_ref, lf_ref, c0_ref, n0_ref, m0_ref,
                       h_ref, c_ref, n_ref, m_ref, *, nb, ls):
    r = nb * ls
    rc = LANES
    dk = c0_ref.shape[2]
    dv = c0_ref.shape[3]

    def pad_rows(a):
        if r == rc:
            return a
        return jnp.concatenate([a, jnp.zeros((rc - r, a.shape[1]), a.dtype)], axis=0)

    r_i = lax.broadcasted_iota(jnp.int32, (r, rc), 0)
    c_i = lax.broadcasted_iota(jnp.int32, (r, rc), 1)
    mask = (c_i <= r_i) & ((r_i // ls) == (c_i // ls))
    ig = ig_ref[...]
    b_all = _select_sum(mask, pad_rows(lf_ref[...]))
    b_t = pad_rows(b_all).T
    ig_t = pad_rows(ig).T
    m0 = m0_ref[...]
    inter_all = b_all + _expand2(m0, nb, ls)
    lane = lax.broadcasted_iota(jnp.int32, (r, LANES), 1)
    m_t_of = {}

    def output_stages(hd):
        kcols = slice(hd * dk, (hd + 1) * dk)
        vcols = slice(hd * dv, (hd + 1) * dv)
        q = q_ref[:, kcols]
        qb = q.astype(BF16)
        qc = jnp.concatenate(
            [jnp.dot(qb[b * ls:(b + 1) * ls], c0_ref[b, hd].astype(BF16),
                     preferred_element_type=F32) for b in range(nb)], axis=0)
        d = jnp.where(mask, b_all[:, hd:hd + 1] - b_t[hd:hd + 1, :] + ig_t[hd:hd + 1, :],
                      -jnp.inf)
        inter = inter_all[:, hd:hd + 1]
        m_t = jnp.maximum(inter, jnp.max(d, axis=1, keepdims=True))
        m_t_of[hd] = m_t
        w_intra = jnp.exp(d - m_t)
        w_inter = jnp.exp(inter - m_t)
        yield
        kb = pad_rows(k_ref[:, kcols].astype(BF16))
        vb = pad_rows(v_ref[:, vcols].astype(BF16))
        s = lax.dot_general(qb, kb, (((1,), (1,)), ((), ())), preferred_element_type=F32)
        sc = s * w_intra
        intra = jnp.dot(sc.astype(BF16), vb, preferred_element_type=F32)
        den_intra = jnp.sum(sc, axis=1, keepdims=True)
        qn = jnp.sum(q * _expand2(n0_ref[hd], nb, ls), axis=1, keepdims=True)
        yield
        num = w_inter * qc + intra
        den = w_inter * qn + den_intra
        h_ref[:, vcols] = num / jnp.maximum(jnp.abs(den), jnp.exp(-m_t))
        yield

    _software_pipeline([output_stages(hd) for hd in range(H_B)], 3)
    mt_all = jnp.zeros((r, LANES), F32)
    for hd in range(H_B):
        mt_all = jnp.where(lane == hd, m_t_of[hd], mt_all)

    p_r = lax.broadcasted_iota(jnp.int32, (nb, rc), 0)
    p_c = lax.broadcasted_iota(jnp.int32, (nb, rc), 1)
    pick = p_c == p_r * ls + (ls - 1)
    bl_seq = _select_sum(pick, pad_rows(b_all))
    mn_seq = _select_sum(pick, pad_rows(mt_all))
    decay = jnp.exp(bl_seq + m0 - mn_seq)
    m_ref[...] = mn_seq
    g_all = jnp.exp(_expand2(bl_seq, nb, ls) - b_all + ig - _expand2(mn_seq, nb, ls))
    col_seq = lax.broadcasted_iota(jnp.int32, (dk, rc), 1) // ls

    def state_stages(hd):
        kg = k_ref[:, hd * dk:(hd + 1) * dk] * g_all[:, hd:hd + 1]
        n_ref[hd] = decay[:, hd:hd + 1] * n0_ref[hd] + jnp.sum(kg.reshape(nb, ls, dk), axis=1)
        kg_t = pad_rows(kg).T
        vb = pad_rows(v_ref[:, hd * dv:(hd + 1) * dv].astype(BF16))
        yield
        for b in range(nb):
            lhs = jnp.where(col_seq == b, kg_t, 0.0).astype(BF16)
            c_ref[b, hd] = decay[b:b + 1, hd:hd + 1] * c0_ref[b, hd] + jnp.dot(
                lhs, vb, preferred_element_type=F32)
        yield

    _software_pipeline([state_stages(hd) for hd in range(H_B)], 2)


def _mlstm_step(q, k, v, ig, lf, c0, n0, m0, *, bsz, nb, ls):
    t = q.shape[0]
    dk = q.shape[1] // H_B
    dv = v.shape[1] // H_B

    def rows(width):
        return pl.BlockSpec((nb * ls, width), lambda i: (i, 0))

    c_spec = pl.BlockSpec((nb, H_B, dk, dv), lambda i: (i, 0, 0, 0))
    n_spec = pl.BlockSpec((H_B, nb, dk), lambda i: (0, i, 0))
    m_spec = pl.BlockSpec((nb, LANES), lambda i: (i, 0))
    return pl.pallas_call(
        functools.partial(_mlstm_step_kernel, nb=nb, ls=ls),
        grid=(bsz // nb,),
        in_specs=[rows(H_B * dk), rows(H_B * dk), rows(H_B * dv), rows(LANES), rows(LANES),
                  c_spec, n_spec, m_spec],
        out_specs=[rows(H_B * dv), c_spec, n_spec, m_spec],
        out_shape=[jax.ShapeDtypeStruct((t, H_B * dv), F32),
                   jax.ShapeDtypeStruct((bsz, H_B, dk, dv), F32),
                   jax.ShapeDtypeStruct((H_B, bsz, dk), F32),
                   jax.ShapeDtypeStruct((bsz, LANES), F32)],
        compiler_params=_params(1),
        name="mlstm_step",
    )(q, k, v, ig, lf, c0, n0, m0)


def _mlstm_out_kernel(h_ref, o_ref, x_ref, gn_ref, wout_ref, y_ref, gate_ref):
    dv = h_ref.shape[1] // H_B
    for hd in range(H_B):
        cols = slice(hd * dv, (hd + 1) * dv)
        hh = h_ref[:, cols]
        mu = jnp.mean(hh, axis=-1, keepdims=True)
        hc = hh - mu
        var = jnp.mean(hc * hc, axis=-1, keepdims=True)
        hn = hc * lax.rsqrt(var + EPS) * gn_ref[:, cols]
        gate_ref[:, cols] = (jax.nn.sigmoid(o_ref[:, cols]) * hn).astype(BF16)
    y_ref[...] = x_ref[...] + jnp.dot(gate_ref[...], wout_ref[...], preferred_element_type=F32)


def _mlstm_out(h, o, x2, gn_g, w_out, *, tm):
    t, d = x2.shape
    d_b = h.shape[1]
    return pl.pallas_call(
        _mlstm_out_kernel,
        grid=(t // tm,),
        in_specs=[pl.BlockSpec((tm, d_b), lambda i: (i, 0)),
                  pl.BlockSpec((tm, d_b), lambda i: (i, 0)),
                  pl.BlockSpec((tm, d), lambda i: (i, 0)),
                  _const_spec(gn_g.shape), _const_spec(w_out.shape)],
        out_specs=pl.BlockSpec((tm, d), lambda i: (i, 0)),
        out_shape=jax.ShapeDtypeStruct((t, d), F32),
        scratch_shapes=[pltpu.VMEM((tm, d_b), BF16)],
        compiler_params=_params(1),
        name="mlstm_out",
    )(h, o, x2, gn_g, w_out)


def _mlstm_fused_kernel(x_ref, g_ref, wall_ref, wif_ref, cw_ref, cb_ref, bif_ref,
                        gn_ref, wout_ref, y_ref, c_ref, n_ref, m_ref, nbuf_ref, *, k_scale):
    @pl.when(pl.program_id(1) == 0)
    def _():
        c_ref[...] = jnp.zeros(c_ref.shape, F32)
        n_ref[...] = jnp.zeros(n_ref.shape, F32)
        m_ref[...] = jnp.zeros(m_ref.shape, F32)
        nbuf_ref[...] = jnp.zeros(nbuf_ref.shape, F32)

    r = x_ref.shape[0]
    dk = c_ref.shape[2]
    dv = c_ref.shape[3]

    x = x_ref[...]
    h = _rmsnorm(x, g_ref[LAYER_B:LAYER_B + 1, :]).astype(BF16)
    gates = jnp.dot(h, wif_ref[...], preferred_element_type=F32) + bif_ref[...]
    r_i = lax.broadcasted_iota(jnp.int32, (r, r), 0)
    c_i = lax.broadcasted_iota(jnp.int32, (r, r), 1)
    mask = c_i <= r_i
    b_all = _select_sum(mask, _log_sigmoid(gates))
    b_t = b_all.T
    ig_t = gates.T

    pair = 2 * dk
    n_pairs = H_B // 2
    qk_w = cw_ref.shape[1]

    def w_cols(base, cols):
        return wall_ref[0, :, base + cols.start:base + cols.stop]

    q_chunks = {}
    k_chunks = {}
    gated = {}

    def qk_chunk(c):
        cols = slice(c * pair, (c + 1) * pair)
        pre = jnp.dot(h, w_cols(0, cols), preferred_element_type=F32)
        return jax.nn.silu(_causal_conv(pre, nbuf_ref, cw_ref, cb_ref, 1, r, cols))

    def head_stages(hd):
        p, half = divmod(hd, 2)
        ig_col = gates[:, hd:hd + 1]
        i_row = ig_t[hd:hd + 1, :]
        b_col = b_all[:, H_B + hd:H_B + hd + 1]
        b_row = b_t[H_B + hd:H_B + hd + 1, :]
        m_prev = m_ref[0, hd:hd + 1, 0:1]
        vcols = slice(hd * dv, (hd + 1) * dv)
        vb = jnp.dot(h, w_cols(qk_w, vcols), preferred_element_type=F32).astype(BF16)
        o_pre = jnp.dot(h, w_cols(qk_w + H_B * dv, vcols), preferred_element_type=F32)
        d = jnp.where(mask, b_col - b_row + i_row, -jnp.inf)
        yield
        inter = b_col + m_prev
        m_t = jnp.maximum(inter, jnp.max(d, axis=1, keepdims=True))
        w_intra = jnp.exp(d - m_t)
        w_inter = jnp.exp(inter - m_t)
        yield
        q = q_chunks[p][:, half * dk:(half + 1) * dk]
        k = k_chunks[p][:, half * dk:(half + 1) * dk] * k_scale
        qb = q.astype(BF16)
        s = lax.dot_general(qb, k.astype(BF16), (((1,), (1,)), ((), ())),
                            preferred_element_type=F32)
        sc = s * w_intra
        intra = jnp.dot(sc.astype(BF16), vb, preferred_element_type=F32)
        den_intra = jnp.sum(sc, axis=1, keepdims=True)
        c_old = c_ref[0, hd]
        n_old = n_ref[0, hd:hd + 1, :]
        qc = jnp.dot(qb, c_old.astype(BF16), preferred_element_type=F32)
        qn = jnp.sum(q * n_old, axis=1, keepdims=True)
        yield
        num = w_inter * qc + intra
        den = w_inter * qn + den_intra
        hout = num / jnp.maximum(jnp.abs(den), jnp.exp(-m_t))
        mu = jnp.mean(hout, axis=-1, keepdims=True)
        hc = hout - mu
        var = jnp.mean(hc * hc, axis=-1, keepdims=True)
        hn = hc * lax.rsqrt(var + EPS) * gn_ref[:, vcols]
        yield
        gated[hd] = (jax.nn.sigmoid(o_pre) * hn).astype(BF16)
        b_last = b_col[r - 1:r, :]
        m_new = m_t[r - 1:r, :]
        g = jnp.exp(b_last - b_col + ig_col - m_new)
        g_row = jnp.exp(b_last - b_row + i_row - m_new)
        decay = jnp.exp(b_last + m_prev - m_new)
        yield
        kg_t = (k.T * g_row).astype(BF16)
        c_ref[0, hd] = decay * c_old + jnp.dot(kg_t, vb, preferred_element_type=F32)
        n_ref[0, hd:hd + 1, :] = decay * n_old + jnp.sum(k * g, axis=0, keepdims=True)
        m_ref[0, hd:hd + 1, :] = jnp.broadcast_to(m_new, (1, LANES))
        yield

    n_stages = 6
    gate_stage = 4
    q_chunks[0] = qk_chunk(0)
    k_chunks[0] = qk_chunk(n_pairs)
    heads = [head_stages(hd) for hd in range(H_B)]
    y = x
    for t in range(H_B + n_stages - 1):
        if t % 2 == 0 and t // 2 + 1 < n_pairs:
            q_chunks[t // 2 + 1] = qk_chunk(t // 2 + 1)
            k_chunks[t // 2 + 1] = qk_chunk(n_pairs + t // 2 + 1)
        for hd in reversed(range(H_B)):
            if hd <= t < hd + n_stages:
                next(heads[hd])
        p, odd = divmod(t - gate_stage, 2)
        if odd and 0 <= p < n_pairs:
            lhs = jnp.concatenate([gated[2 * p], gated[2 * p + 1]], axis=1)
            y = y + jnp.dot(lhs, wout_ref[2 * p * dv:(2 * p + 2) * dv, :],
                            preferred_element_type=F32)
    y_ref[...] = y


def _mlstm_fused(x2, g, w_all, w_if, cw, cb, b_if, gn_g, w_out, *, bsz, seq, ls, k_scale):
    t, d = x2.shape
    qk_w = cw.shape[1]
    d_b = w_out.shape[0]
    dk = qk_w // (2 * H_B)
    dv = d_b // H_B
    nj = seq // ls
    row_spec = pl.BlockSpec((ls, d), lambda i, j: (i * nj + j, 0))

    def state_spec(*tail):
        return pl.BlockSpec((1,) + tail, lambda i, j: (i,) + (0,) * len(tail))

    args = [x2, g, w_all, w_if, cw, cb, b_if, gn_g, w_out]
    return pl.pallas_call(
        functools.partial(_mlstm_fused_kernel, k_scale=k_scale),
        grid=(bsz, nj),
        in_specs=([row_spec, _const_spec(g.shape), _layer_spec(w_all.shape, 0)]
                  + [_const_spec(a.shape) for a in args[3:]]),
        out_specs=[row_spec, state_spec(H_B, dk, dv), state_spec(H_B, dk),
                   state_spec(H_B, LANES), state_spec(CONV_B - 1, qk_w)],
        out_shape=[jax.ShapeDtypeStruct((t, d), F32),
                   jax.ShapeDtypeStruct((bsz, H_B, dk, dv), F32),
                   jax.ShapeDtypeStruct((bsz, H_B, dk), F32),
                   jax.ShapeDtypeStruct((bsz, H_B, LANES), F32),
                   jax.ShapeDtypeStruct((bsz, CONV_B - 1, qk_w), F32)],
        compiler_params=_params(2),
        name="mlstm_fused",
    )(*args)


def _trunks(xp, xs, state, w):
    bp, sp, d = xp.shape
    bs_, ss, _ = xs.shape
    assert sp % PROMPT_ROWS == 0 and PROMPT_ROWS % PROMPT_MLSTM_ROWS == 0
    assert PROMPT_MLSTM_ROWS % CHUNK_A == 0 and PROMPT_MLSTM_ROWS & (PROMPT_MLSTM_ROWS - 1) == 0
    assert ss & (ss - 1) == 0 and ss % SUBLANES == 0 and ss <= CHUNK_A
    assert (bs_ * ss) % SAMPLE_ROWS == 0 and bs_ % STEP_SEQS == 0 and STEP_SEQS * ss <= LANES
    xp2 = xp.reshape(bp * sp, d)
    xs2 = xs.reshape(bs_ * ss, d)
    sample_nb = SAMPLE_ROWS // ss

    ws_s = jnp.tile(w["a_ws_f32"][:, :ss, :ss],
                    (1, CHUNK_A // ss, CHUNK_A // ss)).astype(BF16)
    bs_s = jnp.tile(w["a_bs_t"][:ss], (CHUNK_A // ss, 1))
    xp2, xs2, v_rows = _mixer_a(
        xp2, xs2, w["norm_mix_g"], w["a_w_in"], w["a_ln_g"], w["a_ln_b"], w["a_ws"],
        w["a_bs_t"], ws_s, bs_s, w["a_w_out"], long_tm=PROMPT_ROWS, short_tm=SAMPLE_ROWS,
        short_chunk=ss)
    ffn_tiles = dict(fresh_bsz=bp, fresh_seq=sp, fresh_ls=PROMPT_ROWS, carried_nb=sample_nb,
                     carried_ls=ss)
    xp2, xs2, fbuf0_p, fbuf0_s = _ffn(
        xp2, xs2, w["norm_ffn_g"], w["f_w_up"], w["f_conv_w"], w["f_conv_b"], w["f_w_down"],
        state["ffn_conv"], None, layer=0, **ffn_tiles)

    xp2, c_p, n_p, m_p, mconv_p = _mlstm_fused(
        xp2, w["norm_mix_g"], w["b_w_all"], w["b_w_if"], w["b_conv_w"], w["b_conv_b"],
        w["b_bias_if"], w["b_gn_g"], w["b_w_out"], bsz=bp, seq=sp, ls=PROMPT_MLSTM_ROWS,
        k_scale=w["k_scale"])
    q, k, v, o, ig, lf, mconv_s = _mlstm_proj(
        xs2, w["norm_mix_g"], w["b_w_all"], w["b_w_if"], w["b_conv_w"], w["b_conv_b"],
        w["b_bias_if"], state["mlstm_conv"],
        d_b=w["b_w_out"].shape[0], bsz=bs_, seq=ss, nb=sample_nb, ls=ss, k_scale=w["k_scale"])
    n0 = jnp.transpose(state["mlstm_n"], (1, 0, 2))
    m0 = jnp.pad(state["mlstm_m"], ((0, 0), (0, LANES - H_B)))
    h, c_s, n_s, m_s = _mlstm_step(q, k, v, ig, lf, state["mlstm_C"], n0, m0, bsz=bs_,
                                   nb=STEP_SEQS, ls=ss)
    xs2 = _mlstm_out(h, o, xs2, w["b_gn_g"], w["b_w_out"], tm=SAMPLE_ROWS)
    yp2, ys2, fbuf1_p, fbuf1_s = _ffn(
        xp2, xs2, w["norm_ffn_g"], w["f_w_up"], w["f_conv_w"], w["f_conv_b"], w["f_w_down"],
        state["ffn_conv"], w["final_norm_g"], layer=1, **ffn_tiles)
    prompt = dict(y=yp2.reshape(bp, sp, d), C=c_p[None], n=n_p[None], m=m_p[:, :, 0][None],
                  mconv=mconv_p[None], fconv=jnp.stack([fbuf0_p, fbuf1_p]))
    sample = dict(y=ys2.reshape(bs_, ss, d), v=v_rows.reshape(1, bs_, ss, -1), C=c_s[None],
                  n=jnp.transpose(n_s, (1, 0, 2))[None], m=m_s[:, :H_B][None],
                  mconv=mconv_s[None], fconv=jnp.stack([fbuf0_s, fbuf1_s]))
    return prompt, sample


def kernel(x_prompt, x_sample, state_mlstm_C, state_mlstm_n, state_mlstm_m, state_mlstm_conv, state_ffn_conv, norm_mix_g, norm_ffn_g, final_norm_g, a_w_in, a_ln_g, a_ln_b, a_w_s, a_b_s, a_w_out, b_w_in, b_conv_w, b_conv_b, b_bias_i, b_bias_f, b_gn_g, b_w_out, f_w_up, f_conv_w, f_conv_b, f_w_down):
    d_ff = f_w_down.shape[1]
    qk_w = b_conv_w.shape[2]
    d_b = b_w_out.shape[1]
    dk = qk_w // (2 * H_B)
    d_a = a_w_out.shape[1]
    dg = d_a // G_A

    w_in_b = b_w_in[0]
    w = dict(
        norm_mix_g=norm_mix_g, norm_ffn_g=norm_ffn_g, final_norm_g=final_norm_g[None, :],
        a_w_in=a_w_in[0].astype(BF16), a_ln_g=a_ln_g, a_ln_b=a_ln_b,
        a_ws=a_w_s[0].astype(BF16), a_ws_f32=a_w_s[0],
        a_bs_t=jnp.repeat(jnp.transpose(a_b_s[0]), dg, axis=1),
        a_w_out=a_w_out[0].astype(BF16),
        b_w_all=b_w_in.astype(BF16),
        b_conv_w=b_conv_w[0], b_conv_b=b_conv_b,
        b_w_if=jnp.pad(w_in_b[:, qk_w + 2 * d_b:], ((0, 0), (0, LANES - 2 * H_B))).astype(BF16),
        b_bias_if=jnp.pad(jnp.concatenate([b_bias_i, b_bias_f], axis=1),
                          ((0, 0), (0, LANES - 2 * H_B))),
        b_gn_g=b_gn_g, b_w_out=b_w_out[0].astype(BF16),
        f_w_up=f_w_up.astype(BF16),
        f_conv_w=f_conv_w, f_conv_b=f_conv_b, f_w_down=f_w_down.astype(BF16),
        k_scale=float(dk) ** -0.5,
    )
    state = dict(mlstm_C=state_mlstm_C[0], mlstm_n=state_mlstm_n[0], mlstm_m=state_mlstm_m[0],
                 mlstm_conv=state_mlstm_conv[0], ffn_conv=state_ffn_conv)
    p, s = _trunks(x_prompt, x_sample, state, w)
    return (p["y"], s["y"], p["C"], p["n"], p["m"], p["mconv"], p["fconv"],
            s["v"], s["C"], s["n"], s["m"], s["mconv"], s["fconv"])
```

```python
import functools

import jax
import jax.numpy as jnp
from jax import lax
from jax.experimental import pallas as pl
from jax.experimental.pallas import tpu as pltpu

F32 = jnp.float32
BF16 = jnp.bfloat16

EPS = 1e-6
LANES = 128
SUBLANES = 8
VMEM_LIMIT_BYTES = 56 * 1024 * 1024

CHUNK_A = 128
G_A = 8
H_B = 8
CONV_B = 4
CONV_F = 3
LAYER_A = 0
LAYER_B = 1

PROMPT_ROWS = 512
PROMPT_MLSTM_ROWS = 256
SAMPLE_ROWS = 256
STEP_SEQS = 8


def _const_spec(shape):
    nd = len(shape)
    return pl.BlockSpec(shape, lambda *_: (0,) * nd, pipeline_mode=pl.Buffered(1))


def _layer_spec(shape, layer):
    return pl.BlockSpec((1,) + tuple(shape[1:]), lambda *_: (layer, 0, 0),
                        pipeline_mode=pl.Buffered(1))


def _params(n_axes):
    return pltpu.CompilerParams(
        dimension_semantics=("arbitrary",) * n_axes,
        vmem_limit_bytes=VMEM_LIMIT_BYTES,
    )


def _software_pipeline(chains, n_stages):
    for t in range(len(chains) + n_stages - 1):
        for i in reversed(range(len(chains))):
            if i <= t < i + n_stages:
                next(chains[i])


def _rmsnorm(x, g):
    ms = jnp.mean(x * x, axis=-1, keepdims=True)
    return x * lax.rsqrt(ms + EPS) * g


def _expand3(c3, nb, ls):
    w = c3.shape[-1]
    if nb == 1:
        return jnp.broadcast_to(c3.reshape(1, w), (ls, w))
    return jnp.broadcast_to(c3, (nb, ls, w)).reshape(nb * ls, w)


def _expand2(c2, nb, ls):
    return _expand3(c2[:, None, :], nb, ls)


def _causal_conv(a, carry_ref, cw_ref, cb_ref, nb, ls, cols=slice(None)):
    r, c = a.shape
    kw = cw_ref.shape[0]
    rolled = [a] + [pltpu.roll(a, k, 0) for k in range(1, kw)]

    def taps(shifted):
        y = cb_ref[:, cols] + cw_ref[kw - 1:kw, cols] * shifted[0]
        for k in range(1, kw):
            y = y + cw_ref[kw - 1 - k:kw - k, cols] * shifted[k]
        return y

    grp = SUBLANES if nb == 1 else r
    grp_ls = SUBLANES if nb == 1 else ls
    tpos = lax.broadcasted_iota(jnp.int32, (grp, c), 0) & (grp_ls - 1)
    fixed = [a[0:grp]]
    for k in range(1, kw):
        sh = rolled[k][0:grp]
        for t in range(k):
            idx = kw - 1 + t - k
            prev = _expand3(carry_ref[:, idx:idx + 1, cols], nb, grp_ls)
            sh = jnp.where(tpos == t, prev, sh)
        fixed.append(sh)
    y = taps(fixed)
    if nb == 1:
        y = jnp.concatenate([y, taps([s[grp:] for s in rolled])], axis=0)
    last = rolled[kw - 1]
    if nb == 1:
        carry_ref[0, :, cols] = last[0:kw - 1, :]
    else:
        heads = last.reshape(nb, ls, c)[:, 0:kw - 1, :]
        carry_ref[0:nb - 1, :, cols] = heads[1:nb]
        carry_ref[nb - 1:nb, :, cols] = heads[0:1]
    return y


def _mixer_a_kernel(xl_ref, xs_ref, g_ref, win_ref, lng_ref, lnb_ref, wsl_ref, bsl_ref, wss_ref,
                    bss_ref, wout_ref, yl_ref, ys_ref, v_ref, *, long_steps, short_chunk):
    s = pl.program_id(0)
    shared = (g_ref.at[pl.ds(LAYER_A, 1)], win_ref, lng_ref, lnb_ref, wout_ref)

    @pl.when(s < long_steps)
    def _():
        _mixer_a_tile(xl_ref, yl_ref, None, wsl_ref, bsl_ref, *shared, CHUNK_A)

    @pl.when(s >= long_steps)
    def _():
        _mixer_a_tile(xs_ref, ys_ref, v_ref, wss_ref, bss_ref, *shared, short_chunk)


def _mixer_a_tile(x_ref, y_ref, v_ref, ws_ref, bs_ref, g_ref, win_ref, lng_ref, lnb_ref,
                  wout_ref, seq_chunk):
    emit_v = v_ref is not None
    tm = x_ref.shape[0]
    d_a = lng_ref.shape[1]
    dg = d_a // G_A
    x = x_ref[...]
    h = _rmsnorm(x, g_ref[...]).astype(BF16)
    v_pre = {}
    v_chunks = []
    ahead = 1
    for g in range(G_A + ahead):
        if g < G_A:
            v_pre[g] = jnp.dot(h, win_ref[:, d_a + g * dg:d_a + (g + 1) * dg],
                               preferred_element_type=F32)
        if g >= ahead:
            v_chunks.append(jax.nn.gelu(v_pre.pop(g - ahead)))
    t_i = lax.broadcasted_iota(jnp.int32, (CHUNK_A, CHUNK_A), 0)
    s_i = lax.broadcasted_iota(jnp.int32, (CHUNK_A, CHUNK_A), 1)
    mask = (s_i <= t_i) & ((t_i // seq_chunk) == (s_i // seq_chunk))
    stats = {}
    gated = {}

    def group_stages(g):
        cols = slice(g * dg, (g + 1) * dg)
        u_pre = jnp.dot(h, win_ref[:, cols], preferred_element_type=F32)
        yield
        vn = ((v_chunks[g] - stats["mu"]) * stats["rs"]) * lng_ref[:, cols] + lnb_ref[:, cols]
        if emit_v:
            v_ref[:, cols] = vn
        vb = vn.astype(BF16)
        w = jnp.where(mask, ws_ref[g], jnp.zeros((), BF16))
        mixed = [jnp.dot(w, vb[c * CHUNK_A:(c + 1) * CHUNK_A], preferred_element_type=F32)
                 + bs_ref[:, cols] for c in range(tm // CHUNK_A)]
        yield
        u = jax.nn.gelu(u_pre)
        gated[g] = jnp.concatenate(
            [(u[c * CHUNK_A:(c + 1) * CHUNK_A] * mixed[c]).astype(BF16)
             for c in range(tm // CHUNK_A)], axis=0)
        yield

    lead = 3
    times = (0, lead, lead + 1)
    groups = [group_stages(g) for g in range(G_A)]
    y = x
    for t in range(G_A + times[-1]):
        if t == lead:
            mu = sum(jnp.sum(v, axis=-1, keepdims=True) for v in v_chunks) / d_a
            var = sum(jnp.sum((v - mu) * (v - mu), axis=-1, keepdims=True)
                      for v in v_chunks) / d_a
            stats["mu"] = mu
            stats["rs"] = lax.rsqrt(var + EPS)
        for g in reversed(range(G_A)):
            if t - g in times:
                next(groups[g])
        p, odd = divmod(t - times[-1], 2)
        if odd and 0 <= p < G_A // 2:
            lhs = jnp.concatenate([gated[2 * p], gated[2 * p + 1]], axis=1)
            y = y + jnp.dot(lhs, wout_ref[2 * p * dg:(2 * p + 2) * dg, :],
                            preferred_element_type=F32)
    y_ref[...] = y


def _mixer_a(xl2, xs2, g, w_in, ln_g, ln_b, ws_l, bs_l, ws_s, bs_s, w_out, *, long_tm, short_tm,
             short_chunk):
    tl, d = xl2.shape
    ts = xs2.shape[0]
    d_a = ln_g.shape[1]
    n_l = tl // long_tm

    def l_rows(width):
        return pl.BlockSpec((long_tm, width), lambda s: (jnp.minimum(s, n_l - 1), 0))

    def s_rows(width):
        return pl.BlockSpec((short_tm, width), lambda s: (jnp.maximum(s - n_l, 0), 0))

    consts = [g, w_in, ln_g, ln_b, ws_l, bs_l, ws_s, bs_s, w_out]
    return pl.pallas_call(
        functools.partial(_mixer_a_kernel, long_steps=n_l, short_chunk=short_chunk),
        grid=(n_l + ts // short_tm,),
        in_specs=[l_rows(d), s_rows(d)] + [_const_spec(a.shape) for a in consts],
        out_specs=[l_rows(d), s_rows(d), s_rows(d_a)],
        out_shape=[jax.ShapeDtypeStruct((tl, d), F32),
                   jax.ShapeDtypeStruct((ts, d), F32),
                   jax.ShapeDtypeStruct((ts, d_a), F32)],
        compiler_params=_params(1),
        name="mixer_a",
    )(xl2, xs2, *consts)


def _ffn_tile(x_ref, y_ref, nbuf_ref, g_ref, wup_ref, cw_ref, cb_ref, wd_ref, fg_ref, nb, ls):
    x = x_ref[...]
    h = _rmsnorm(x, g_ref[...]).astype(BF16)
    d_ff = wd_ref.shape[1]
    a = jnp.dot(h, wup_ref[0, :, :d_ff], preferred_element_type=F32)
    gv = jnp.dot(h, wup_ref[0, :, d_ff:], preferred_element_type=F32)
    a_c = _causal_conv(a, nbuf_ref, cw_ref, cb_ref, nb, ls)
    act = (jax.nn.gelu(a_c) * gv).astype(BF16)
    y = x + jnp.dot(act, wd_ref[0], preferred_element_type=F32)
    if fg_ref is not None:
        y = _rmsnorm(y, fg_ref[...])
    y_ref[...] = y


def _ffn_kernel(*refs, layer, fresh_steps, fresh_nj, fresh_ls, carried_nb, carried_ls,
                final_norm):
    refs = list(refs)
    xf_ref, xc_ref, g_ref, wup_ref, cw_ref, cb_ref, wd_ref, buf_ref = refs[:8]
    pos = 8
    fg_ref = None
    if final_norm:
        fg_ref = refs[pos]
        pos += 1
    yf_ref, yc_ref, nbuff_ref, nbufc_ref = refs[pos:pos + 4]
    s = pl.program_id(0)
    shared = (g_ref.at[pl.ds(layer, 1)], wup_ref, cw_ref.at[layer], cb_ref.at[pl.ds(layer, 1)],
              wd_ref, fg_ref)

    @pl.when(s < fresh_steps)
    def _():
        @pl.when(s % fresh_nj == 0)
        def _():
            nbuff_ref[...] = jnp.zeros(nbuff_ref.shape, F32)

        _ffn_tile(xf_ref, yf_ref, nbuff_ref, *shared, 1, fresh_ls)

    @pl.when(s >= fresh_steps)
    def _():
        nbufc_ref[...] = buf_ref[0]
        _ffn_tile(xc_ref, yc_ref, nbufc_ref, *shared, carried_nb, carried_ls)


def _ffn(xf2, xc2, g, w_up, cw, cb, w_d, buf, final_g, *, layer, fresh_bsz, fresh_seq, fresh_ls,
         carried_nb, carried_ls):
    tf, d = xf2.shape
    tc = xc2.shape[0]
    d_ff = w_d.shape[1]
    nj = fresh_seq // fresh_ls
    fs = fresh_bsz * nj
    cs = tc // (carried_nb * carried_ls)
    final_norm = final_g is not None
    f_rows = pl.BlockSpec((fresh_ls, d), lambda s: (jnp.minimum(s, fs - 1), 0))
    c_rows = pl.BlockSpec((carried_nb * carried_ls, d), lambda s: (jnp.maximum(s - fs, 0), 0))
    f_buf = pl.BlockSpec((1, CONV_F - 1, d_ff), lambda s: (jnp.minimum(s, fs - 1) // nj, 0, 0))
    c_buf = pl.BlockSpec((carried_nb, CONV_F - 1, d_ff), lambda s: (jnp.maximum(s - fs, 0), 0, 0))
    c_buf_in = pl.BlockSpec((1, carried_nb, CONV_F - 1, d_ff),
                            lambda s: (layer, jnp.maximum(s - fs, 0), 0, 0))
    args = [xf2, xc2, g, w_up, cw, cb, w_d, buf]
    in_specs = [f_rows, c_rows, _const_spec(g.shape), _layer_spec(w_up.shape, layer),
                _const_spec(cw.shape), _const_spec(cb.shape), _layer_spec(w_d.shape, layer),
                c_buf_in]
    if final_norm:
        args.append(final_g)
        in_specs.append(_const_spec(final_g.shape))
    return pl.pallas_call(
        functools.partial(_ffn_kernel, layer=layer, fresh_steps=fs, fresh_nj=nj, fresh_ls=fresh_ls,
                          carried_nb=carried_nb, carried_ls=carried_ls, final_norm=final_norm),
        grid=(fs + cs,),
        in_specs=in_specs,
        out_specs=[f_rows, c_rows, f_buf, c_buf],
        out_shape=[jax.ShapeDtypeStruct((tf, d), F32),
                   jax.ShapeDtypeStruct((tc, d), F32),
                   jax.ShapeDtypeStruct((fresh_bsz, CONV_F - 1, d_ff), F32),
                   jax.ShapeDtypeStruct((tc // carried_ls, CONV_F - 1, d_ff), F32)],
        compiler_params=_params(1),
        name="conv_ffn",
    )(*args)


def _select_sum(sel, x):
    return jnp.dot(sel.astype(F32), x, precision=lax.Precision.HIGHEST,
                   preferred_element_type=F32)


def _log_sigmoid(x):
    return jnp.minimum(x, 0.0) - jnp.log1p(jnp.exp(-jnp.abs(x)))


def _mlstm_proj_tile(x_ref, g_ref, wall_ref, wif_ref, cw_ref, cb_ref, bif_ref, q_ref, k_ref,
                     v_ref, o_ref, ig_ref, lf_ref, nbuf_ref, nb, ls, k_scale):
    h = _rmsnorm(x_ref[...], g_ref[LAYER_B:LAYER_B + 1, :]).astype(BF16)
    qk_w = cw_ref.shape[1]
    d_b = v_ref.shape[1]
    qk_pre = jnp.dot(h, wall_ref[0, :, 0:qk_w], preferred_element_type=F32)
    v_ref[...] = jnp.dot(h, wall_ref[0, :, qk_w:qk_w + d_b], preferred_element_type=F32)
    o_ref[...] = jnp.dot(h, wall_ref[0, :, qk_w + d_b:qk_w + 2 * d_b],
                         preferred_element_type=F32)
    gates = jnp.dot(h, wif_ref[...], preferred_element_type=F32) + bif_ref[...]
    ig_ref[...] = gates
    lf_ref[...] = pltpu.roll(_log_sigmoid(gates), LANES - H_B, 1)
    qk = jax.nn.silu(_causal_conv(qk_pre, nbuf_ref, cw_ref, cb_ref, nb, ls))
    dq = q_ref.shape[1]
    q_ref[...] = qk[:, :dq]
    k_ref[...] = qk[:, dq:] * k_scale


def _mlstm_step_kernel(q_ref, k_ref, v_ref, ig_ref, lf_ref, c0_ref, n0_ref, m0_ref,
                       h_ref, c_ref, n_ref, m_ref, *, nb, ls):
    r = nb * ls
    rc = LANES
    dk = c0_ref.shape[2]
    dv = c0_ref.shape[3]

    def pad_rows(a):
        if r == rc:
            return a
        return jnp.concatenate([a, jnp.zeros((rc - r, a.shape[1]), a.dtype)], axis=0)

    r_i = lax.broadcasted_iota(jnp.int32, (r, rc), 0)
    c_i = lax.broadcasted_iota(jnp.int32, (r, rc), 1)
    mask = (c_i <= r_i) & ((r_i // ls) == (c_i // ls))
    ig = ig_ref[...]
    b_all = _select_sum(mask, pad_rows(lf_ref[...]))
    b_t = pad_rows(b_all).T
    ig_t = pad_rows(ig).T
    m0 = m0_ref[...]
    inter_all = b_all + _expand2(m0, nb, ls)
    lane = lax.broadcasted_iota(jnp.int32, (r, LANES), 1)
    m_t_of = {}

    def output_stages(hd):
        kcols = slice(hd * dk, (hd + 1) * dk)
        vcols = slice(hd * dv, (hd + 1) * dv)
        q = q_ref[:, kcols]
        qb = q.astype(BF16)
        qc = jnp.concatenate(
            [jnp.dot(qb[b * ls:(b + 1) * ls], c0_ref[b, hd].astype(BF16),
                     preferred_element_type=F32) for b in range(nb)], axis=0)
        d = jnp.where(mask, b_all[:, hd:hd + 1] - b_t[hd:hd + 1, :] + ig_t[hd:hd + 1, :],
                      -jnp.inf)
        inter = inter_all[:, hd:hd + 1]
        m_t = jnp.maximum(inter, jnp.max(d, axis=1, keepdims=True))
        m_t_of[hd] = m_t
        w_intra = jnp.exp(d - m_t)
        w_inter = jnp.exp(inter - m_t)
        yield
        kb = pad_rows(k_ref[:, kcols].astype(BF16))
        vb = pad_rows(v_ref[:, vcols].astype(BF16))
        s = lax.dot_general(qb, kb, (((1,), (1,)), ((), ())), preferred_element_type=F32)
        sc = s * w_intra
        intra = jnp.dot(sc.astype(BF16), vb, preferred_element_type=F32)
        den_intra = jnp.sum(sc, axis=1, keepdims=True)
        qn = jnp.sum(q * _expand2(n0_ref[hd], nb, ls), axis=1, keepdims=True)
        yield
        num = w_inter * qc + intra
        den = w_inter * qn + den_intra
        h_ref[:, vcols] = num / jnp.maximum(jnp.abs(den), jnp.exp(-m_t))
        yield

    _software_pipeline([output_stages(hd) for hd in range(H_B)], 3)
    mt_all = jnp.zeros((r, LANES), F32)
    for hd in range(H_B):
        mt_all = jnp.where(lane == hd, m_t_of[hd], mt_all)

    p_r = lax.broadcasted_iota(jnp.int32, (nb, rc), 0)
    p_c = lax.broadcasted_iota(jnp.int32, (nb, rc), 1)
    pick = p_c == p_r * ls + (ls - 1)
    bl_seq = _select_sum(pick, pad_rows(b_all))
    mn_seq = _select_sum(pick, pad_rows(mt_all))
    decay = jnp.exp(bl_seq + m0 - mn_seq)
    m_ref[...] = mn_seq
    g_all = jnp.exp(_expand2(bl_seq, nb, ls) - b_all + ig - _expand2(mn_seq, nb, ls))
    col_seq = lax.broadcasted_iota(jnp.int32, (dk, rc), 1) // ls

    def state_stages(hd):
        kg = k_ref[:, hd * dk:(hd + 1) * dk] * g_all[:, hd:hd + 1]
        n_ref[hd] = decay[:, hd:hd + 1] * n0_ref[hd] + jnp.sum(kg.reshape(nb, ls, dk), axis=1)
        kg_t = pad_rows(kg).T
        vb = pad_rows(v_ref[:, hd * dv:(hd + 1) * dv].astype(BF16))
        yield
        for b in range(nb):
            lhs = jnp.where(col_seq == b, kg_t, 0.0).astype(BF16)
            c_ref[b, hd] = decay[b:b + 1, hd:hd + 1] * c0_ref[b, hd] + jnp.dot(
                lhs, vb, preferred_element_type=F32)
        yield

    _software_pipeline([state_stages(hd) for hd in range(H_B)], 2)


def _mlstm_step(q, k, v, ig, lf, c0, n0, m0, *, bsz, nb, ls):
    t = q.shape[0]
    dk = q.shape[1] // H_B
    dv = v.shape[1] // H_B

    def rows(width):
        return pl.BlockSpec((nb * ls, width), lambda i: (i, 0))

    c_spec = pl.BlockSpec((nb, H_B, dk, dv), lambda i: (i, 0, 0, 0))
    n_spec = pl.BlockSpec((H_B, nb, dk), lambda i: (0, i, 0))
    m_spec = pl.BlockSpec((nb, LANES), lambda i: (i, 0))
    return pl.pallas_call(
        functools.partial(_mlstm_step_kernel, nb=nb, ls=ls),
        grid=(bsz // nb,),
        in_specs=[rows(H_B * dk), rows(H_B * dk), rows(H_B * dv), rows(LANES), rows(LANES),
                  c_spec, n_spec, m_spec],
        out_specs=[rows(H_B * dv), c_spec, n_spec, m_spec],
        out_shape=[jax.ShapeDtypeStruct((t, H_B * dv), F32),
                   jax.ShapeDtypeStruct((bsz, H_B, dk, dv), F32),
                   jax.ShapeDtypeStruct((H_B, bsz, dk), F32),
                   jax.ShapeDtypeStruct((bsz, LANES), F32)],
        compiler_params=_params(1),
        name="mlstm_step",
    )(q, k, v, ig, lf, c0, n0, m0)


def _mlstm_out_kernel(h_ref, o_ref, x_ref, gn_ref, wout_ref, y_ref, gate_ref):
    dv = h_ref.shape[1] // H_B
    for hd in range(H_B):
        cols = slice(hd * dv, (hd + 1) * dv)
        hh = h_ref[:, cols]
        mu = jnp.mean(hh, axis=-1, keepdims=True)
        hc = hh - mu
        var = jnp.mean(hc * hc, axis=-1, keepdims=True)
        hn = hc * lax.rsqrt(var + EPS) * gn_ref[:, cols]
        gate_ref[:, cols] = (jax.nn.sigmoid(o_ref[:, cols]) * hn).astype(BF16)
    y_ref[...] = x_ref[...] + jnp.dot(gate_ref[...], wout_ref[...], preferred_element_type=F32)


def _mlstm_out(h, o, x2, gn_g, w_out, *, tm):
    t, d = x2.shape
    d_b = h.shape[1]
    return pl.pallas_call(
        _mlstm_out_kernel,
        grid=(t // tm,),
        in_specs=[pl.BlockSpec((tm, d_b), lambda i: (i, 0)),
                  pl.BlockSpec((tm, d_b), lambda i: (i, 0)),
                  pl.BlockSpec((tm, d), lambda i: (i, 0)),
                  _const_spec(gn_g.shape), _const_spec(w_out.shape)],
        out_specs=pl.BlockSpec((tm, d), lambda i: (i, 0)),
        out_shape=jax.ShapeDtypeStruct((t, d), F32),
        scratch_shapes=[pltpu.VMEM((tm, d_b), BF16)],
        compiler_params=_params(1),
        name="mlstm_out",
    )(h, o, x2, gn_g, w_out)


def _mlstm_kernel(xf_ref, xc_ref, g_ref, wall_ref, wif_ref, cw_ref, cb_ref, bif_ref, gn_ref,
                  wout_ref, buf_ref, yf_ref, c_ref, n_ref, m_ref, nbuff_ref, q_ref, k_ref, v_ref,
                  o_ref, ig_ref, lf_ref, nbufc_ref, *, fresh_steps, fresh_nj, carried_nb,
                  carried_ls, k_scale):
    s = pl.program_id(0)
    proj = (g_ref, wall_ref, wif_ref, cw_ref, cb_ref, bif_ref)

    @pl.when(s < fresh_steps)
    def _():
        @pl.when(s % fresh_nj == 0)
        def _():
            c_ref[...] = jnp.zeros(c_ref.shape, F32)
            n_ref[...] = jnp.zeros(n_ref.shape, F32)
            m_ref[...] = jnp.zeros(m_ref.shape, F32)
            nbuff_ref[...] = jnp.zeros(nbuff_ref.shape, F32)

        _mlstm_fused_tile(xf_ref, *proj, gn_ref, wout_ref, yf_ref, c_ref, n_ref, m_ref, nbuff_ref,
                          k_scale)

    @pl.when(s >= fresh_steps)
    def _():
        nbufc_ref[...] = buf_ref[...]
        _mlstm_proj_tile(xc_ref, *proj, q_ref, k_ref, v_ref, o_ref, ig_ref, lf_ref, nbufc_ref,
                         carried_nb, carried_ls, k_scale)


def _mlstm_fused_tile(x_ref, g_ref, wall_ref, wif_ref, cw_ref, cb_ref, bif_ref,
                      gn_ref, wout_ref, y_ref, c_ref, n_ref, m_ref, nbuf_ref, k_scale):
    r = x_ref.shape[0]
    dk = c_ref.shape[2]
    dv = c_ref.shape[3]

    x = x_ref[...]
    h = _rmsnorm(x, g_ref[LAYER_B:LAYER_B + 1, :]).astype(BF16)
    gates = jnp.dot(h, wif_ref[...], preferred_element_type=F32) + bif_ref[...]
    r_i = lax.broadcasted_iota(jnp.int32, (r, r), 0)
    c_i = lax.broadcasted_iota(jnp.int32, (r, r), 1)
    mask = c_i <= r_i
    b_all = _select_sum(mask, _log_sigmoid(gates))
    b_t = b_all.T
    ig_t = gates.T

    pair = 2 * dk
    n_pairs = H_B // 2
    qk_w = cw_ref.shape[1]

    def w_cols(base, cols):
        return wall_ref[0, :, base + cols.start:base + cols.stop]

    q_chunks = {}
    k_chunks = {}
    gated = {}

    def qk_chunk(c):
        cols = slice(c * pair, (c + 1) * pair)
        pre = jnp.dot(h, w_cols(0, cols), preferred_element_type=F32)
        return jax.nn.silu(_causal_conv(pre, nbuf_ref, cw_ref, cb_ref, 1, r, cols))

    def head_stages(hd):
        p, half = divmod(hd, 2)
        ig_col = gates[:, hd:hd + 1]
        i_row = ig_t[hd:hd + 1, :]
        b_col = b_all[:, H_B + hd:H_B + hd + 1]
        b_row = b_t[H_B + hd:H_B + hd + 1, :]
        m_prev = m_ref[0, hd:hd + 1, 0:1]
        vcols = slice(hd * dv, (hd + 1) * dv)
        vb = jnp.dot(h, w_cols(qk_w, vcols), preferred_element_type=F32).astype(BF16)
        o_pre = jnp.dot(h, w_cols(qk_w + H_B * dv, vcols), preferred_element_type=F32)
        d = jnp.where(mask, b_col - b_row + i_row, -jnp.inf)
        yield
        inter = b_col + m_prev
        m_t = jnp.maximum(inter, jnp.max(d, axis=1, keepdims=True))
        w_intra = jnp.exp(d - m_t)
        w_inter = jnp.exp(inter - m_t)
        yield
        q = q_chunks[p][:, half * dk:(half + 1) * dk]
        k = k_chunks[p][:, half * dk:(half + 1) * dk] * k_scale
        qb = q.astype(BF16)
        s = lax.dot_general(qb, k.astype(BF16), (((1,), (1,)), ((), ())),
                            preferred_element_type=F32)
        sc = s * w_intra
        intra = jnp.dot(sc.astype(BF16), vb, preferred_element_type=F32)
        den_intra = jnp.sum(sc, axis=1, keepdims=True)
        c_old = c_ref[0, hd]
        n_old = n_ref[0, hd:hd + 1, :]
        qc = jnp.dot(qb, c_old.astype(BF16), preferred_element_type=F32)
        qn = jnp.sum(q * n_old, axis=1, keepdims=True)
        yield
        num = w_inter * qc + intra
        den = w_inter * qn + den_intra
        hout = num / jnp.maximum(jnp.abs(den), jnp.exp(-m_t))
        mu = jnp.mean(hout, axis=-1, keepdims=True)
        hc = hout - mu
        var = jnp.mean(hc * hc, axis=-1, keepdims=True)
        hn = hc * lax.rsqrt(var + EPS) * gn_ref[:, vcols]
        yield
        gated[hd] = (jax.nn.sigmoid(o_pre) * hn).astype(BF16)
        b_last = b_col[r - 1:r, :]
        m_new = m_t[r - 1:r, :]
        g = jnp.exp(b_last - b_col + ig_col - m_new)
        g_row = jnp.exp(b_last - b_row + i_row - m_new)
        decay = jnp.exp(b_last + m_prev - m_new)
        yield
        kg_t = (k.T * g_row).astype(BF16)
        c_ref[0, hd] = decay * c_old + jnp.dot(kg_t, vb, preferred_element_type=F32)
        n_ref[0, hd:hd + 1, :] = decay * n_old + jnp.sum(k * g, axis=0, keepdims=True)
        m_ref[0, hd:hd + 1, :] = jnp.broadcast_to(m_new, (1, LANES))
        yield

    n_stages = 6
    gate_stage = 4
    q_chunks[0] = qk_chunk(0)
    k_chunks[0] = qk_chunk(n_pairs)
    heads = [head_stages(hd) for hd in range(H_B)]
    y = x
    for t in range(H_B + n_stages - 1):
        if t % 2 == 0 and t // 2 + 1 < n_pairs:
            q_chunks[t // 2 + 1] = qk_chunk(t // 2 + 1)
            k_chunks[t // 2 + 1] = qk_chunk(n_pairs + t // 2 + 1)
        for hd in reversed(range(H_B)):
            if hd <= t < hd + n_stages:
                next(heads[hd])
        p, odd = divmod(t - gate_stage, 2)
        if odd and 0 <= p < n_pairs:
            lhs = jnp.concatenate([gated[2 * p], gated[2 * p + 1]], axis=1)
            y = y + jnp.dot(lhs, wout_ref[2 * p * dv:(2 * p + 2) * dv, :],
                            preferred_element_type=F32)
    y_ref[...] = y


def _mlstm(xf2, xc2, g, w_all, w_if, cw, cb, b_if, gn_g, w_out, buf, *, fresh_bsz, fresh_seq,
           fresh_ls, carried_nb, carried_ls, k_scale):
    tf, d = xf2.shape
    tc = xc2.shape[0]
    qk_w = cw.shape[1]
    d_b = w_out.shape[0]
    dk = qk_w // (2 * H_B)
    dv = d_b // H_B
    nj = fresh_seq // fresh_ls
    fs = fresh_bsz * nj
    c_rows_n = carried_nb * carried_ls
    f_rows = pl.BlockSpec((fresh_ls, d), lambda s: (jnp.minimum(s, fs - 1), 0))

    def f_state(*tail):
        return pl.BlockSpec((1,) + tail,
                            lambda s: (jnp.minimum(s, fs - 1) // nj,) + (0,) * len(tail))

    def c_rows(width):
        return pl.BlockSpec((c_rows_n, width), lambda s: (jnp.maximum(s - fs, 0), 0))

    c_buf = pl.BlockSpec((carried_nb, CONV_B - 1, qk_w), lambda s: (jnp.maximum(s - fs, 0), 0, 0))
    consts = [w_if, cw, cb, b_if, gn_g, w_out]
    return pl.pallas_call(
        functools.partial(_mlstm_kernel, fresh_steps=fs, fresh_nj=nj, carried_nb=carried_nb,
                          carried_ls=carried_ls, k_scale=k_scale),
        grid=(fs + tc // c_rows_n,),
        in_specs=([f_rows, c_rows(d), _const_spec(g.shape), _layer_spec(w_all.shape, 0)]
                  + [_const_spec(a.shape) for a in consts] + [c_buf]),
        out_specs=[f_rows, f_state(H_B, dk, dv), f_state(H_B, dk), f_state(H_B, LANES),
                   f_state(CONV_B - 1, qk_w), c_rows(qk_w // 2), c_rows(qk_w // 2), c_rows(d_b),
                   c_rows(d_b), c_rows(LANES), c_rows(LANES), c_buf],
        out_shape=[jax.ShapeDtypeStruct((tf, d), F32),
                   jax.ShapeDtypeStruct((fresh_bsz, H_B, dk, dv), F32),
                   jax.ShapeDtypeStruct((fresh_bsz, H_B, dk), F32),
                   jax.ShapeDtypeStruct((fresh_bsz, H_B, LANES), F32),
                   jax.ShapeDtypeStruct((fresh_bsz, CONV_B - 1, qk_w), F32),
                   jax.ShapeDtypeStruct((tc, qk_w // 2), F32),
                   jax.ShapeDtypeStruct((tc, qk_w // 2), F32),
                   jax.ShapeDtypeStruct((tc, d_b), F32),
                   jax.ShapeDtypeStruct((tc, d_b), F32),
                   jax.ShapeDtypeStruct((tc, LANES), F32),
                   jax.ShapeDtypeStruct((tc, LANES), F32),
                   jax.ShapeDtypeStruct((tc // carried_ls, CONV_B - 1, qk_w), F32)],
        compiler_params=_params(1),
        name="mlstm",
    )(xf2, xc2, g, w_all, *consts, buf)


def _trunks(xp, xs, state, w):
    bp, sp, d = xp.shape
    bs_, ss, _ = xs.shape
    assert sp % PROMPT_ROWS == 0 and PROMPT_ROWS % PROMPT_MLSTM_ROWS == 0
    assert PROMPT_MLSTM_ROWS % CHUNK_A == 0 and PROMPT_MLSTM_ROWS & (PROMPT_MLSTM_ROWS - 1) == 0
    assert ss & (ss - 1) == 0 and ss % SUBLANES == 0 and ss <= CHUNK_A
    assert (bs_ * ss) % SAMPLE_ROWS == 0 and bs_ % STEP_SEQS == 0 and STEP_SEQS * ss <= LANES
    xp2 = xp.reshape(bp * sp, d)
    xs2 = xs.reshape(bs_ * ss, d)
    sample_nb = SAMPLE_ROWS // ss

    ws_s = jnp.tile(w["a_ws"][:, :ss, :ss], (1, CHUNK_A // ss, CHUNK_A // ss))
    bs_s = jnp.tile(w["a_bs_t"][:ss], (CHUNK_A // ss, 1))
    xp2, xs2, v_rows = _mixer_a(
        xp2, xs2, w["norm_mix_g"], w["a_w_in"], w["a_ln_g"], w["a_ln_b"], w["a_ws"],
        w["a_bs_t"], ws_s, bs_s, w["a_w_out"], long_tm=PROMPT_ROWS, short_tm=SAMPLE_ROWS,
        short_chunk=ss)
    ffn_tiles = dict(fresh_bsz=bp, fresh_seq=sp, fresh_ls=PROMPT_ROWS, carried_nb=sample_nb,
                     carried_ls=ss)
    xp2, xs2, fbuf0_p, fbuf0_s = _ffn(
        xp2, xs2, w["norm_ffn_g"], w["f_w_up"], w["f_conv_w"], w["f_conv_b"], w["f_w_down"],
        state["ffn_conv"], None, layer=0, **ffn_tiles)

    xp2, c_p, n_p, m_p, mconv_p, q, k, v, o, ig, lf, mconv_s = _mlstm(
        xp2, xs2, w["norm_mix_g"], w["b_w_all"], w["b_w_if"], w["b_conv_w"], w["b_conv_b"],
        w["b_bias_if"], w["b_gn_g"], w["b_w_out"], state["mlstm_conv"], fresh_bsz=bp,
        fresh_seq=sp, fresh_ls=PROMPT_MLSTM_ROWS, carried_nb=sample_nb, carried_ls=ss,
        k_scale=w["k_scale"])
    n0 = jnp.transpose(state["mlstm_n"], (1, 0, 2))
    m0 = jnp.pad(state["mlstm_m"], ((0, 0), (0, LANES - H_B)))
    h, c_s, n_s, m_s = _mlstm_step(q, k, v, ig, lf, state["mlstm_C"], n0, m0, bsz=bs_,
                                   nb=STEP_SEQS, ls=ss)
    xs2 = _mlstm_out(h, o, xs2, w["b_gn_g"], w["b_w_out"], tm=SAMPLE_ROWS)
    yp2, ys2, fbuf1_p, fbuf1_s = _ffn(
        xp2, xs2, w["norm_ffn_g"], w["f_w_up"], w["f_conv_w"], w["f_conv_b"], w["f_w_down"],
        state["ffn_conv"], w["final_norm_g"], layer=1, **ffn_tiles)
    prompt = dict(y=yp2.reshape(bp, sp, d), C=c_p[None], n=n_p[None], m=m_p[:, :, 0][None],
                  mconv=mconv_p[None], fconv=jnp.stack([fbuf0_p, fbuf1_p]))
    sample = dict(y=ys2.reshape(bs_, ss, d), v=v_rows.reshape(1, bs_, ss, -1), C=c_s[None],
                  n=jnp.transpose(n_s, (1, 0, 2))[None], m=m_s[:, :H_B][None],
                  mconv=mconv_s[None], fconv=jnp.stack([fbuf0_s, fbuf1_s]))
    return prompt, sample


def kernel(x_prompt, x_sample, state_mlstm_C, state_mlstm_n, state_mlstm_m, state_mlstm_conv, state_ffn_conv, norm_mix_g, norm_ffn_g, final_norm_g, a_w_in, a_ln_g, a_ln_b, a_w_s, a_b_s, a_w_out, b_w_in, b_conv_w, b_conv_b, b_bias_i, b_bias_f, b_gn_g, b_w_out, f_w_up, f_conv_w, f_conv_b, f_w_down):
    d_ff = f_w_down.shape[1]
    qk_w = b_conv_w.shape[2]
    d_b = b_w_out.shape[1]
    dk = qk_w // (2 * H_B)
    d_a = a_w_out.shape[1]
    dg = d_a // G_A

    w_in_b = b_w_in[0]
    w = dict(
        norm_mix_g=norm_mix_g, norm_ffn_g=norm_ffn_g, final_norm_g=final_norm_g[None, :],
        a_w_in=a_w_in[0].astype(BF16), a_ln_g=a_ln_g, a_ln_b=a_ln_b,
        a_ws=a_w_s[0].astype(BF16),
        a_bs_t=jnp.repeat(jnp.transpose(a_b_s[0]), dg, axis=1),
        a_w_out=a_w_out[0].astype(BF16),
        b_w_all=b_w_in.astype(BF16),
        b_conv_w=b_conv_w[0], b_conv_b=b_conv_b,
        b_w_if=jnp.pad(w_in_b[:, qk_w + 2 * d_b:], ((0, 0), (0, LANES - 2 * H_B))).astype(BF16),
        b_bias_if=jnp.pad(jnp.concatenate([b_bias_i, b_bias_f], axis=1),
                          ((0, 0), (0, LANES - 2 * H_B))),
        b_gn_g=b_gn_g, b_w_out=b_w_out[0].astype(BF16),
        f_w_up=f_w_up.astype(BF16),
        f_conv_w=f_conv_w, f_conv_b=f_conv_b, f_w_down=f_w_down.astype(BF16),
        k_scale=float(dk) ** -0.5,
    )
    state = dict(mlstm_C=state_mlstm_C[0], mlstm_n=state_mlstm_n[0], mlstm_m=state_mlstm_m[0],
                 mlstm_conv=state_mlstm_conv[0], ffn_conv=state_ffn_conv)
    p, s = _trunks(x_prompt, x_sample, state, w)
    return (p["y"], s["y"], p["C"], p["n"], p["m"], p["mconv"], p["fconv"],
            s["v"], s["C"], s["n"], s["m"], s["mconv"], s["fconv"])
```

```python
import functools

import jax
import jax.numpy as jnp
from jax import lax
from jax.experimental import pallas as pl
from jax.experimental.pallas import tpu as pltpu

F32 = jnp.float32
BF16 = jnp.bfloat16

EPS = 1e-6
LANES = 128
SUBLANES = 8
VMEM_LIMIT_BYTES = 56 * 1024 * 1024

CHUNK_A = 128
G_A = 8
H_B = 8
CONV_B = 4
CONV_F = 3
LAYER_A = 0
LAYER_B = 1

PROMPT_ROWS = 512
PROMPT_MLSTM_ROWS = 256
SAMPLE_ROWS = 256
STEP_SEQS = 8


def _const_spec(shape):
    nd = len(shape)
    return pl.BlockSpec(shape, lambda *_: (0,) * nd, pipeline_mode=pl.Buffered(1))


def _layer_spec(shape, layer):
    return pl.BlockSpec((1,) + tuple(shape[1:]), lambda *_: (layer, 0, 0),
                        pipeline_mode=pl.Buffered(1))


def _params(n_axes):
    return pltpu.CompilerParams(
        dimension_semantics=("arbitrary",) * n_axes,
        vmem_limit_bytes=VMEM_LIMIT_BYTES,
    )


def _software_pipeline(chains, n_stages):
    for t in range(len(chains) + n_stages - 1):
        for i in reversed(range(len(chains))):
            if i <= t < i + n_stages:
                next(chains[i])


def _rmsnorm(x, g):
    ms = jnp.mean(x * x, axis=-1, keepdims=True)
    return x * lax.rsqrt(ms + EPS) * g


def _expand3(c3, nb, ls):
    w = c3.shape[-1]
    if nb == 1:
        return jnp.broadcast_to(c3.reshape(1, w), (ls, w))
    return jnp.broadcast_to(c3, (nb, ls, w)).reshape(nb * ls, w)


def _expand2(c2, nb, ls):
    return _expand3(c2[:, None, :], nb, ls)


def _causal_conv(a, carry_ref, cw_ref, cb_ref, nb, ls, cols=slice(None)):
    r, c = a.shape
    kw = cw_ref.shape[0]
    rolled = [a] + [pltpu.roll(a, k, 0) for k in range(1, kw)]

    def taps(shifted):
        y = cb_ref[:, cols] + cw_ref[kw - 1:kw, cols] * shifted[0]
        for k in range(1, kw):
            y = y + cw_ref[kw - 1 - k:kw - k, cols] * shifted[k]
        return y

    grp = SUBLANES if nb == 1 else r
    grp_ls = SUBLANES if nb == 1 else ls
    tpos = lax.broadcasted_iota(jnp.int32, (grp, c), 0) & (grp_ls - 1)
    fixed = [a[0:grp]]
    for k in range(1, kw):
        sh = rolled[k][0:grp]
        for t in range(k):
            idx = kw - 1 + t - k
            prev = _expand3(carry_ref[:, idx:idx + 1, cols], nb, grp_ls)
            sh = jnp.where(tpos == t, prev, sh)
        fixed.append(sh)
    y = taps(fixed)
    if nb == 1:
        y = jnp.concatenate([y, taps([s[grp:] for s in rolled])], axis=0)
    last = rolled[kw - 1]
    if nb == 1:
        carry_ref[0, :, cols] = last[0:kw - 1, :]
    else:
        heads = last.reshape(nb, ls, c)[:, 0:kw - 1, :]
        carry_ref[0:nb - 1, :, cols] = heads[1:nb]
        carry_ref[nb - 1:nb, :, cols] = heads[0:1]
    return y


def _mixer_a_kernel(xl_ref, xs_ref, g_ref, win_ref, lng_ref, lnb_ref, wsl_ref, bsl_ref, wss_ref,
                    bss_ref, wout_ref, yl_ref, ys_ref, v_ref, *, long_steps, short_chunk):
    s = pl.program_id(0)
    shared = (g_ref.at[pl.ds(LAYER_A, 1)], win_ref, lng_ref, lnb_ref, wout_ref)

    @pl.when(s < long_steps)
    def _():
        _mixer_a_tile(xl_ref, yl_ref, None, wsl_ref, bsl_ref, *shared, CHUNK_A)

    @pl.when(s >= long_steps)
    def _():
        _mixer_a_tile(xs_ref, ys_ref, v_ref, wss_ref, bss_ref, *shared, short_chunk)


def _mixer_a_tile(x_ref, y_ref, v_ref, ws_ref, bs_ref, g_ref, win_ref, lng_ref, lnb_ref,
                  wout_ref, seq_chunk):
    emit_v = v_ref is not None
    tm = x_ref.shape[0]
    d_a = lng_ref.shape[1]
    dg = d_a // G_A
    x = x_ref[...]
    h = _rmsnorm(x, g_ref[...]).astype(BF16)
    v_pre = {}
    v_chunks = []
    ahead = 1
    for g in range(G_A + ahead):
        if g < G_A:
            v_pre[g] = jnp.dot(h, win_ref[:, d_a + g * dg:d_a + (g + 1) * dg],
                               preferred_element_type=F32)
        if g >= ahead:
            v_chunks.append(jax.nn.gelu(v_pre.pop(g - ahead)))
    t_i = lax.broadcasted_iota(jnp.int32, (CHUNK_A, CHUNK_A), 0)
    s_i = lax.broadcasted_iota(jnp.int32, (CHUNK_A, CHUNK_A), 1)
    mask = (s_i <= t_i) & ((t_i // seq_chunk) == (s_i // seq_chunk))
    stats = {}
    gated = {}

    def group_stages(g):
        cols = slice(g * dg, (g + 1) * dg)
        u_pre = jnp.dot(h, win_ref[:, cols], preferred_element_type=F32)
        yield
        vn = ((v_chunks[g] - stats["mu"]) * stats["rs"]) * lng_ref[:, cols] + lnb_ref[:, cols]
        if emit_v:
            v_ref[:, cols] = vn
        vb = vn.astype(BF16)
        w = jnp.where(mask, ws_ref[g], jnp.zeros((), BF16))
        mixed = [jnp.dot(w, vb[c * CHUNK_A:(c + 1) * CHUNK_A], preferred_element_type=F32)
                 + bs_ref[:, cols] for c in range(tm // CHUNK_A)]
        yield
        u = jax.nn.gelu(u_pre)
        gated[g] = jnp.concatenate(
            [(u[c * CHUNK_A:(c + 1) * CHUNK_A] * mixed[c]).astype(BF16)
             for c in range(tm // CHUNK_A)], axis=0)
        yield

    lead = 3
    times = (0, lead, lead + 1)
    groups = [group_stages(g) for g in range(G_A)]
    y = x
    for t in range(G_A + times[-1]):
        if t == lead:
            mu = sum(jnp.sum(v, axis=-1, keepdims=True) for v in v_chunks) / d_a
            var = sum(jnp.sum((v - mu) * (v - mu), axis=-1, keepdims=True)
                      for v in v_chunks) / d_a
            stats["mu"] = mu
            stats["rs"] = lax.rsqrt(var + EPS)
        for g in reversed(range(G_A)):
            if t - g in times:
                next(groups[g])
        p, odd = divmod(t - times[-1], 2)
        if odd and 0 <= p < G_A // 2:
            lhs = jnp.concatenate([gated[2 * p], gated[2 * p + 1]], axis=1)
            y = y + jnp.dot(lhs, wout_ref[2 * p * dg:(2 * p + 2) * dg, :],
                            preferred_element_type=F32)
    y_ref[...] = y


def _mixer_a(xl2, xs2, g, w_in, ln_g, ln_b, ws_l, bs_l, ws_s, bs_s, w_out, *, long_tm, short_tm,
             short_chunk):
    tl, d = xl2.shape
    ts = xs2.shape[0]
    d_a = ln_g.shape[1]
    n_l = tl // long_tm

    def l_rows(width):
        return pl.BlockSpec((long_tm, width), lambda s: (jnp.minimum(s, n_l - 1), 0))

    def s_rows(width):
        return pl.BlockSpec((short_tm, width), lambda s: (jnp.maximum(s - n_l, 0), 0))

    consts = [g, w_in, ln_g, ln_b, ws_l, bs_l, ws_s, bs_s, w_out]
    return pl.pallas_call(
        functools.partial(_mixer_a_kernel, long_steps=n_l, short_chunk=short_chunk),
        grid=(n_l + ts // short_tm,),
        in_specs=[l_rows(d), s_rows(d)] + [_const_spec(a.shape) for a in consts],
        out_specs=[l_rows(d), s_rows(d), s_rows(d_a)],
        out_shape=[jax.ShapeDtypeStruct((tl, d), F32),
                   jax.ShapeDtypeStruct((ts, d), F32),
                   jax.ShapeDtypeStruct((ts, d_a), F32)],
        compiler_params=_params(1),
        name="mixer_a",
    )(xl2, xs2, *consts)


def _ffn_tile(x_ref, y_ref, nbuf_ref, g_ref, wup_ref, cw_ref, cb_ref, wd_ref, fg_ref, nb, ls):
    x = x_ref[...]
    h = _rmsnorm(x, g_ref[...]).astype(BF16)
    d_ff = wd_ref.shape[1]
    a = jnp.dot(h, wup_ref[0, :, :d_ff], preferred_element_type=F32)
    gv = jnp.dot(h, wup_ref[0, :, d_ff:], preferred_element_type=F32)
    a_c = _causal_conv(a, nbuf_ref, cw_ref, cb_ref, nb, ls)
    act = (jax.nn.gelu(a_c) * gv).astype(BF16)
    y = x + jnp.dot(act, wd_ref[0], preferred_element_type=F32)
    if fg_ref is not None:
        y = _rmsnorm(y, fg_ref[...])
    y_ref[...] = y


def _ffn_kernel(*refs, layer, fresh_steps, fresh_nj, fresh_ls, carried_nb, carried_ls,
                final_norm):
    refs = list(refs)
    xf_ref, xc_ref, g_ref, wup_ref, cw_ref, cb_ref, wd_ref, buf_ref = refs[:8]
    pos = 8
    fg_ref = None
    if final_norm:
        fg_ref = refs[pos]
        pos += 1
    yf_ref, yc_ref, nbuff_ref, nbufc_ref = refs[pos:pos + 4]
    s = pl.program_id(0)
    shared = (g_ref.at[pl.ds(layer, 1)], wup_ref, cw_ref.at[layer], cb_ref.at[pl.ds(layer, 1)],
              wd_ref, fg_ref)

    @pl.when(s < fresh_steps)
    def _():
        @pl.when(s % fresh_nj == 0)
        def _():
            nbuff_ref[...] = jnp.zeros(nbuff_ref.shape, F32)

        _ffn_tile(xf_ref, yf_ref, nbuff_ref, *shared, 1, fresh_ls)

    @pl.when(s >= fresh_steps)
    def _():
        nbufc_ref[...] = buf_ref[0]
        _ffn_tile(xc_ref, yc_ref, nbufc_ref, *shared, carried_nb, carried_ls)


def _ffn(xf2, xc2, g, w_up, cw, cb, w_d, buf, final_g, *, layer, fresh_bsz, fresh_seq, fresh_ls,
         carried_nb, carried_ls):
    tf, d = xf2.shape
    tc = xc2.shape[0]
    d_ff = w_d.shape[1]
    nj = fresh_seq // fresh_ls
    fs = fresh_bsz * nj
    cs = tc // (carried_nb * carried_ls)
    final_norm = final_g is not None
    f_rows = pl.BlockSpec((fresh_ls, d), lambda s: (jnp.minimum(s, fs - 1), 0))
    c_rows = pl.BlockSpec((carried_nb * carried_ls, d), lambda s: (jnp.maximum(s - fs, 0), 0))
    f_buf = pl.BlockSpec((1, CONV_F - 1, d_ff), lambda s: (jnp.minimum(s, fs - 1) // nj, 0, 0))
    c_buf = pl.BlockSpec((carried_nb, CONV_F - 1, d_ff), lambda s: (jnp.maximum(s - fs, 0), 0, 0))
    c_buf_in = pl.BlockSpec((1, carried_nb, CONV_F - 1, d_ff),
                            lambda s: (layer, jnp.maximum(s - fs, 0), 0, 0))
    args = [xf2, xc2, g, w_up, cw, cb, w_d, buf]
    in_specs = [f_rows, c_rows, _const_spec(g.shape), _layer_spec(w_up.shape, layer),
                _const_spec(cw.shape), _const_spec(cb.shape), _layer_spec(w_d.shape, layer),
                c_buf_in]
    if final_norm:
        args.append(final_g)
        in_specs.append(_const_spec(final_g.shape))
    return pl.pallas_call(
        functools.partial(_ffn_kernel, layer=layer, fresh_steps=fs, fresh_nj=nj, fresh_ls=fresh_ls,
                          carried_nb=carried_nb, carried_ls=carried_ls, final_norm=final_norm),
        grid=(fs + cs,),
        in_specs=in_specs,
        out_specs=[f_rows, c_rows, f_buf, c_buf],
        out_shape=[jax.ShapeDtypeStruct((tf, d), F32),
                   jax.ShapeDtypeStruct((tc, d), F32),
                   jax.ShapeDtypeStruct((fresh_bsz, CONV_F - 1, d_ff), F32),
                   jax.ShapeDtypeStruct((tc // carried_ls, CONV_F - 1, d_ff), F32)],
        compiler_params=_params(1),
        name="conv_ffn",
    )(*args)


def _select_sum(sel, x):
    return jnp.dot(sel.astype(F32), x, precision=lax.Precision.HIGHEST,
                   preferred_element_type=F32)


def _log_sigmoid(x):
    return jnp.minimum(x, 0.0) - jnp.log1p(jnp.exp(-jnp.abs(x)))


def _mlstm_proj_kernel(x_ref, g_ref, wall_ref, wif_ref, cw_ref, cb_ref, bif_ref, buf_ref,
                       q_ref, k_ref, v_ref, o_ref, ig_ref, lf_ref, nbuf_ref, *, nb, ls, k_scale):
    nbuf_ref[...] = buf_ref[...]
    h = _rmsnorm(x_ref[...], g_ref[LAYER_B:LAYER_B + 1, :]).astype(BF16)
    qk_w = cw_ref.shape[1]
    d_b = v_ref.shape[1]
    qk_pre = jnp.dot(h, wall_ref[0, :, 0:qk_w], preferred_element_type=F32)
    v_ref[...] = jnp.dot(h, wall_ref[0, :, qk_w:qk_w + d_b], preferred_element_type=F32)
    o_ref[...] = jnp.dot(h, wall_ref[0, :, qk_w + d_b:qk_w + 2 * d_b],
                         preferred_element_type=F32)
    gates = jnp.dot(h, wif_ref[...], preferred_element_type=F32) + bif_ref[...]
    ig_ref[...] = gates
    lf_ref[...] = pltpu.roll(_log_sigmoid(gates), LANES - H_B, 1)
    qk = jax.nn.silu(_causal_conv(qk_pre, nbuf_ref, cw_ref, cb_ref, nb, ls))
    dq = q_ref.shape[1]
    q_ref[...] = qk[:, :dq]
    k_ref[...] = qk[:, dq:] * k_scale


def _mlstm_proj(x2, g, w_all, w_if, cw, cb, b_if, buf, *, d_b, bsz, seq, nb, ls, k_scale):
    t, d = x2.shape
    qk_w = cw.shape[1]
    assert seq == ls

    def rows(width):
        return pl.BlockSpec((nb * ls, width), lambda i: (i, 0))

    buf_spec = pl.BlockSpec((nb, CONV_B - 1, qk_w), lambda i: (i, 0, 0))
    consts = [w_if, cw, cb, b_if]
    return pl.pallas_call(
        functools.partial(_mlstm_proj_kernel, nb=nb, ls=ls, k_scale=k_scale),
        grid=(bsz // nb,),
        in_specs=([rows(d), _const_spec(g.shape), _layer_spec(w_all.shape, 0)]
                  + [_const_spec(a.shape) for a in consts] + [buf_spec]),
        out_specs=[rows(qk_w // 2), rows(qk_w // 2), rows(d_b), rows(d_b), rows(LANES),
                   rows(LANES), buf_spec],
        out_shape=[jax.ShapeDtypeStruct((t, qk_w // 2), F32),
                   jax.ShapeDtypeStruct((t, qk_w // 2), F32),
                   jax.ShapeDtypeStruct((t, d_b), F32),
                   jax.ShapeDtypeStruct((t, d_b), F32),
                   jax.ShapeDtypeStruct((t, LANES), F32),
                   jax.ShapeDtypeStruct((t, LANES), F32),
                   jax.ShapeDtypeStruct((bsz, CONV_B - 1, qk_w), F32)],
        compiler_params=_params(1),
        name="mlstm_proj",
    )(x2, g, w_all, *consts, buf)


def _mlstm_step_kernel(q_ref, k_ref, v_ref, ig_ref, lf_ref, c0_ref, n0_ref, m0_ref,
                       h_ref, c_ref, n_ref, m_ref, *, nb, ls):
    r = nb * ls
    rc = LANES
    dk = c0_ref.shape[2]
    dv = c0_ref.shape[3]

    def pad_rows(a):
        if r == rc:
            return a
        return jnp.concatenate([a, jnp.zeros((rc - r, a.shape[1]), a.dtype)], axis=0)

    r_i = lax.broadcasted_iota(jnp.int32, (r, rc), 0)
    c_i = lax.broadcasted_iota(jnp.int32, (r, rc), 1)
    mask = (c_i <= r_i) & ((r_i // ls) == (c_i // ls))
    ig = ig_ref[...]
    b_all = _select_sum(mask, pad_rows(lf_ref[...]))
    b_t = pad_rows(b_all).T
    ig_t = pad_rows(ig).T
    m0 = m0_ref[...]
    inter_all = b_all + _expand2(m0, nb, ls)
    lane = lax.broadcasted_iota(jnp.int32, (r, LANES), 1)
    m_t_of = {}

    def output_stages(hd):
        kcols = slice(hd * dk, (hd + 1) * dk)
        vcols = slice(hd * dv, (hd + 1) * dv)
        q = q_ref[:, kcols]
        qb = q.astype(BF16)
        qc = jnp.concatenate(
            [jnp.dot(qb[b * ls:(b + 1) * ls], c0_ref[b, hd].astype(BF16),
                     preferred_element_type=F32) for b in range(nb)], axis=0)
        d = jnp.where(mask, b_all[:, hd:hd + 1] - b_t[hd:hd + 1, :] + ig_t[hd:hd + 1, :],
                      -jnp.inf)
        inter = inter_all[:, hd:hd + 1]
        m_t = jnp.maximum(inter, jnp.max(d, axis=1, keepdims=True))
        m_t_of[hd] = m_t
        w_intra = jnp.exp(d - m_t)
        w_inter = jnp.exp(inter - m_t)
        yield
        kb = pad_rows(k_ref[:, kcols].astype(BF16))
        vb = pad_rows(v_ref[:, vcols].astype(BF16))
        s = lax.dot_general(qb, kb, (((1,), (1,)), ((), ())), preferred_element_type=F32)
        sc = s * w_intra
        intra = jnp.dot(sc.astype(BF16), vb, preferred_element_type=F32)
        den_intra = jnp.sum(sc, axis=1, keepdims=True)
        qn = jnp.sum(q * _expand2(n0_ref[hd], nb, ls), axis=1, keepdims=True)
        yield
        num = w_inter * qc + intra
        den = w_inter * qn + den_intra
        h_ref[:, vcols] = num / jnp.maximum(jnp.abs(den), jnp.exp(-m_t))
        yield

    _software_pipeline([output_stages(hd) for hd in range(H_B)], 3)
    mt_all = jnp.zeros((r, LANES), F32)
    for hd in range(H_B):
        mt_all = jnp.where(lane == hd, m_t_of[hd], mt_all)

    p_r = lax.broadcasted_iota(jnp.int32, (nb, rc), 0)
    p_c = lax.broadcasted_iota(jnp.int32, (nb, rc), 1)
    pick = p_c == p_r * ls + (ls - 1)
    bl_seq = _select_sum(pick, pad_rows(b_all))
    mn_seq = _select_sum(pick, pad_rows(mt_all))
    decay = jnp.exp(bl_seq + m0 - mn_seq)
    m_ref[...] = mn_seq
    g_all = jnp.exp(_expand2(bl_seq, nb, ls) - b_all + ig - _expand2(mn_seq, nb, ls))
    col_seq = lax.broadcasted_iota(jnp.int32, (dk, rc), 1) // ls

    def state_stages(hd):
        kg = k_ref[:, hd * dk:(hd + 1) * dk] * g_all[:, hd:hd + 1]
        n_ref[hd] = decay[:, hd:hd + 1] * n0_ref[hd] + jnp.sum(kg.reshape(nb, ls, dk), axis=1)
        kg_t = pad_rows(kg).T
        vb = pad_rows(v_ref[:, hd * dv:(hd + 1) * dv].astype(BF16))
        yield
        for b in range(nb):
            lhs = jnp.where(col_seq == b, kg_t, 0.0).astype(BF16)
            c_ref[b, hd] = decay[b:b + 1, hd:hd + 1] * c0_ref[b, hd] + jnp.dot(
                lhs, vb, preferred_element_type=F32)
        yield

    _software_pipeline([state_stages(hd) for hd in range(H_B)], 2)


def _mlstm_step(q, k, v, ig, lf, c0, n0, m0, *, bsz, nb, ls):
    t = q.shape[0]
    dk = q.shape[1] // H_B
    dv = v.shape[1] // H_B

    def rows(width):
        return pl.BlockSpec((nb * ls, width), lambda i: (i, 0))

    c_spec = pl.BlockSpec((nb, H_B, dk, dv), lambda i: (i, 0, 0, 0))
    n_spec = pl.BlockSpec((H_B, nb, dk), lambda i: (0, i, 0))
    m_spec = pl.BlockSpec((nb, LANES), lambda i: (i, 0))
    return pl.pallas_call(
        functools.partial(_mlstm_step_kernel, nb=nb, ls=ls),
        grid=(bsz // nb,),
        in_specs=[rows(H_B * dk), rows(H_B * dk), rows(H_B * dv), rows(LANES), rows(LANES),
                  c_spec, n_spec, m_spec],
        out_specs=[rows(H_B * dv), c_spec, n_spec, m_spec],
        out_shape=[jax.ShapeDtypeStruct((t, H_B * dv), F32),
                   jax.ShapeDtypeStruct((bsz, H_B, dk, dv), F32),
                   jax.ShapeDtypeStruct((H_B, bsz, dk), F32),
                   jax.ShapeDtypeStruct((bsz, LANES), F32)],
        compiler_params=_params(1),
        name="mlstm_step",
    )(q, k, v, ig, lf, c0, n0, m0)


def _mlstm_out_kernel(h_ref, o_ref, x_ref, gn_ref, wout_ref, y_ref, gate_ref):
    dv = h_ref.shape[1] // H_B
    for hd in range(H_B):
        cols = slice(hd * dv, (hd + 1) * dv)
        hh = h_ref[:, cols]
        mu = jnp.mean(hh, axis=-1, keepdims=True)
        hc = hh - mu
        var = jnp.mean(hc * hc, axis=-1, keepdims=True)
        hn = hc * lax.rsqrt(var + EPS) * gn_ref[:, cols]
        gate_ref[:, cols] = (jax.nn.sigmoid(o_ref[:, cols]) * hn).astype(BF16)
    y_ref[...] = x_ref[...] + jnp.dot(gate_ref[...], wout_ref[...], preferred_element_type=F32)


def _mlstm_out(h, o, x2, gn_g, w_out, *, tm):
    t, d = x2.shape
    d_b = h.shape[1]
    return pl.pallas_call(
        _mlstm_out_kernel,
        grid=(t // tm,),
        in_specs=[pl.BlockSpec((tm, d_b), lambda i: (i, 0)),
                  pl.BlockSpec((tm, d_b), lambda i: (i, 0)),
                  pl.BlockSpec((tm, d), lambda i: (i, 0)),
                  _const_spec(gn_g.shape), _const_spec(w_out.shape)],
        out_specs=pl.BlockSpec((tm, d), lambda i: (i, 0)),
        out_shape=jax.ShapeDtypeStruct((t, d), F32),
        scratch_shapes=[pltpu.VMEM((tm, d_b), BF16)],
        compiler_params=_params(1),
        name="mlstm_out",
    )(h, o, x2, gn_g, w_out)


def _mlstm_fused_kernel(x_ref, g_ref, wall_ref, wif_ref, cw_ref, cb_ref, bif_ref,
                        gn_ref, wout_ref, y_ref, c_ref, n_ref, m_ref, nbuf_ref, *, k_scale):
    @pl.when(pl.program_id(1) == 0)
    def _():
        c_ref[...] = jnp.zeros(c_ref.shape, F32)
        n_ref[...] = jnp.zeros(n_ref.shape, F32)
        m_ref[...] = jnp.zeros(m_ref.shape, F32)
        nbuf_ref[...] = jnp.zeros(nbuf_ref.shape, F32)

    r = x_ref.shape[0]
    dk = c_ref.shape[2]
    dv = c_ref.shape[3]

    x = x_ref[...]
    h = _rmsnorm(x, g_ref[LAYER_B:LAYER_B + 1, :]).astype(BF16)
    gates = jnp.dot(h, wif_ref[...], preferred_element_type=F32) + bif_ref[...]
    r_i = lax.broadcasted_iota(jnp.int32, (r, r), 0)
    c_i = lax.broadcasted_iota(jnp.int32, (r, r), 1)
    mask = c_i <= r_i
    b_all = _select_sum(mask, _log_sigmoid(gates))
    b_t = b_all.T
    ig_t = gates.T

    pair = 2 * dk
    n_pairs = H_B // 2
    qk_w = cw_ref.shape[1]

    def w_cols(base, cols):
        return wall_ref[0, :, base + cols.start:base + cols.stop]

    q_chunks = {}
    k_chunks = {}
    gated = {}

    def qk_chunk(c):
        cols = slice(c * pair, (c + 1) * pair)
        pre = jnp.dot(h, w_cols(0, cols), preferred_element_type=F32)
        return jax.nn.silu(_causal_conv(pre, nbuf_ref, cw_ref, cb_ref, 1, r, cols))

    def head_stages(hd):
        p, half = divmod(hd, 2)
        ig_col = gates[:, hd:hd + 1]
        i_row = ig_t[hd:hd + 1, :]
        b_col = b_all[:, H_B + hd:H_B + hd + 1]
        b_row = b_t[H_B + hd:H_B + hd + 1, :]
        m_prev = m_ref[0, hd:hd + 1, 0:1]
        vcols = slice(hd * dv, (hd + 1) * dv)
        vb = jnp.dot(h, w_cols(qk_w, vcols), preferred_element_type=F32).astype(BF16)
        o_pre = jnp.dot(h, w_cols(qk_w + H_B * dv, vcols), preferred_element_type=F32)
        d = jnp.where(mask, b_col - b_row + i_row, -jnp.inf)
        yield
        inter = b_col + m_prev
        m_t = jnp.maximum(inter, jnp.max(d, axis=1, keepdims=True))
        w_intra = jnp.exp(d - m_t)
        w_inter = jnp.exp(inter - m_t)
        yield
        q = q_chunks[p][:, half * dk:(half + 1) * dk]
        k = k_chunks[p][:, half * dk:(half + 1) * dk] * k_scale
        qb = q.astype(BF16)
        s = lax.dot_general(qb, k.astype(BF16), (((1,), (1,)), ((), ())),
                            preferred_element_type=F32)
        sc = s * w_intra
        intra = jnp.dot(sc.astype(BF16), vb, preferred_element_type=F32)
        den_intra = jnp.sum(sc, axis=1, keepdims=True)
        c_old = c_ref[0, hd]
        n_old = n_ref[0, hd:hd + 1, :]
        qc = jnp.dot(qb, c_old.astype(BF16), preferred_element_type=F32)
        qn = jnp.sum(q * n_old, axis=1, keepdims=True)
        yield
        num = w_inter * qc + intra
        den = w_inter * qn + den_intra
        hout = num / jnp.maximum(jnp.abs(den), jnp.exp(-m_t))
        mu = jnp.mean(hout, axis=-1, keepdims=True)
        hc = hout - mu
        var = jnp.mean(hc * hc, axis=-1, keepdims=True)
        hn = hc * lax.rsqrt(var + EPS) * gn_ref[:, vcols]
        yield
        gated[hd] = (jax.nn.sigmoid(o_pre) * hn).astype(BF16)
        b_last = b_col[r - 1:r, :]
        m_new = m_t[r - 1:r, :]
        g = jnp.exp(b_last - b_col + ig_col - m_new)
        g_row = jnp.exp(b_last - b_row + i_row - m_new)
        decay = jnp.exp(b_last + m_prev - m_new)
        yield
        kg_t = (k.T * g_row).astype(BF16)
        c_ref[0, hd] = decay * c_old + jnp.dot(kg_t, vb, preferred_element_type=F32)
        n_ref[0, hd:hd + 1, :] = decay * n_old + jnp.sum(k * g, axis=0, keepdims=True)
        m_ref[0, hd:hd + 1, :] = jnp.broadcast_to(m_new, (1, LANES))
        yield

    n_stages = 6
    gate_stage = 4
    q_chunks[0] = qk_chunk(0)
    k_chunks[0] = qk_chunk(n_pairs)
    heads = [head_stages(hd) for hd in range(H_B)]
    y = x
    for t in range(H_B + n_stages - 1):
        if t % 2 == 0 and t // 2 + 1 < n_pairs:
            q_chunks[t // 2 + 1] = qk_chunk(t // 2 + 1)
            k_chunks[t // 2 + 1] = qk_chunk(n_pairs + t // 2 + 1)
        for hd in reversed(range(H_B)):
            if hd <= t < hd + n_stages:
                next(heads[hd])
        p, odd = divmod(t - gate_stage, 2)
        if odd and 0 <= p < n_pairs:
            lhs = jnp.concatenate([gated[2 * p], gated[2 * p + 1]], axis=1)
            y = y + jnp.dot(lhs, wout_ref[2 * p * dv:(2 * p + 2) * dv, :],
                            preferred_element_type=F32)
    y_ref[...] = y


def _mlstm_fused(x2, g, w_all, w_if, cw, cb, b_if, gn_g, w_out, *, bsz, seq, ls, k_scale):
    t, d = x2.shape
    qk_w = cw.shape[1]
    d_b = w_out.shape[0]
    dk = qk_w // (2 * H_B)
    dv = d_b // H_B
    nj = seq // ls
    row_spec = pl.BlockSpec((ls, d), lambda i, j: (i * nj + j, 0))

    def state_spec(*tail):
        return pl.BlockSpec((1,) + tail, lambda i, j: (i,) + (0,) * len(tail))

    args = [x2, g, w_all, w_if, cw, cb, b_if, gn_g, w_out]
    return pl.pallas_call(
        functools.partial(_mlstm_fused_kernel, k_scale=k_scale),
        grid=(bsz, nj),
        in_specs=([row_spec, _const_spec(g.shape), _layer_spec(w_all.shape, 0)]
                  + [_const_spec(a.shape) for a in args[3:]]),
        out_specs=[row_spec, state_spec(H_B, dk, dv), state_spec(H_B, dk),
                   state_spec(H_B, LANES), state_spec(CONV_B - 1, qk_w)],
        out_shape=[jax.ShapeDtypeStruct((t, d), F32),
                   jax.ShapeDtypeStruct((bsz, H_B, dk, dv), F32),
                   jax.ShapeDtypeStruct((bsz, H_B, dk), F32),
                   jax.ShapeDtypeStruct((bsz, H_B, LANES), F32),
                   jax.ShapeDtypeStruct((bsz, CONV_B - 1, qk_w), F32)],
        compiler_params=_params(2),
        name="mlstm_fused",
    )(*args)


def _trunks(xp, xs, state, w):
    bp, sp, d = xp.shape
    bs_, ss, _ = xs.shape
    assert sp % PROMPT_ROWS == 0 and PROMPT_ROWS % PROMPT_MLSTM_ROWS == 0
    assert PROMPT_MLSTM_ROWS % CHUNK_A == 0 and PROMPT_MLSTM_ROWS & (PROMPT_MLSTM_ROWS - 1) == 0
    assert ss & (ss - 1) == 0 and ss % SUBLANES == 0 and ss <= CHUNK_A
    assert (bs_ * ss) % SAMPLE_ROWS == 0 and bs_ % STEP_SEQS == 0 and STEP_SEQS * ss <= LANES
    xp2 = xp.reshape(bp * sp, d)
    xs2 = xs.reshape(bs_ * ss, d)
    sample_nb = SAMPLE_ROWS // ss

    ws_s = jnp.tile(w["a_ws"][:, :ss, :ss], (1, CHUNK_A // ss, CHUNK_A // ss))
    bs_s = jnp.tile(w["a_bs_t"][:ss], (CHUNK_A // ss, 1))
    xp2, xs2, v_rows = _mixer_a(
        xp2, xs2, w["norm_mix_g"], w["a_w_in"], w["a_ln_g"], w["a_ln_b"], w["a_ws"],
        w["a_bs_t"], ws_s, bs_s, w["a_w_out"], long_tm=PROMPT_ROWS, short_tm=SAMPLE_ROWS,
        short_chunk=ss)
    ffn_tiles = dict(fresh_bsz=bp, fresh_seq=sp, fresh_ls=PROMPT_ROWS, carried_nb=sample_nb,
                     carried_ls=ss)
    xp2, xs2, fbuf0_p, fbuf0_s = _ffn(
        xp2, xs2, w["norm_ffn_g"], w["f_w_up"], w["f_conv_w"], w["f_conv_b"], w["f_w_down"],
        state["ffn_conv"], None, layer=0, **ffn_tiles)

    xp2, c_p, n_p, m_p, mconv_p = _mlstm_fused(
        xp2, w["norm_mix_g"], w["b_w_all"], w["b_w_if"], w["b_conv_w"], w["b_conv_b"],
        w["b_bias_if"], w["b_gn_g"], w["b_w_out"], bsz=bp, seq=sp, ls=PROMPT_MLSTM_ROWS,
        k_scale=w["k_scale"])
    q, k, v, o, ig, lf, mconv_s = _mlstm_proj(
        xs2, w["norm_mix_g"], w["b_w_all"], w["b_w_if"], w["b_conv_w"], w["b_conv_b"],
        w["b_bias_if"], state["mlstm_conv"],
        d_b=w["b_w_out"].shape[0], bsz=bs_, seq=ss, nb=sample_nb, ls=ss, k_scale=w["k_scale"])
    n0 = jnp.transpose(state["mlstm_n"], (1, 0, 2))
    m0 = jnp.pad(state["mlstm_m"], ((0, 0), (0, LANES - H_B)))
    h, c_s, n_s, m_s = _mlstm_step(q, k, v, ig, lf, state["mlstm_C"], n0, m0, bsz=bs_,
                                   nb=STEP_SEQS, ls=ss)
    xs2 = _mlstm_out(h, o, xs2, w["b_gn_g"], w["b_w_out"], tm=SAMPLE_ROWS)
    yp2, ys2, fbuf1_p, fbuf1_s = _ffn(
        xp2, xs2, w["norm_ffn_g"], w["f_w_up"], w["f_conv_w"], w["f_conv_b"], w["f_w_down"],
        state["ffn_conv"], w["final_norm_g"], layer=1, **ffn_tiles)
    prompt = dict(y=yp2.reshape(bp, sp, d), C=c_p[None], n=n_p[None], m=m_p[:, :, 0][None],
                  mconv=mconv_p[None], fconv=jnp.stack([fbuf0_p, fbuf1_p]))
    sample = dict(y=ys2.reshape(bs_, ss, d), v=v_rows.reshape(1, bs_, ss, -1), C=c_s[None],
                  n=jnp.transpose(n_s, (1, 0, 2))[None], m=m_s[:, :H_B][None],
                  mconv=mconv_s[None], fconv=jnp.stack([fbuf0_s, fbuf1_s]))
    return prompt, sample


def kernel(x_prompt, x_sample, state_mlstm_C, state_mlstm_n, state_mlstm_m, state_mlstm_conv, state_ffn_conv, norm_mix_g, norm_ffn_g, final_norm_g, a_w_in, a_ln_g, a_ln_b, a_w_s, a_b_s, a_w_out, b_w_in, b_conv_w, b_conv_b, b_bias_i, b_bias_f, b_gn_g, b_w_out, f_w_up, f_conv_w, f_conv_b, f_w_down):
    d_ff = f_w_down.shape[1]
    qk_w = b_conv_w.shape[2]
    d_b = b_w_out.shape[1]
    dk = qk_w // (2 * H_B)
    d_a = a_w_out.shape[1]
    dg = d_a // G_A

    w_in_b = b_w_in[0]
    w = dict(
        norm_mix_g=norm_mix_g, norm_ffn_g=norm_ffn_g, final_norm_g=final_norm_g[None, :],
        a_w_in=a_w_in[0].astype(BF16), a_ln_g=a_ln_g, a_ln_b=a_ln_b,
        a_ws=a_w_s[0].astype(BF16),
        a_bs_t=jnp.repeat(jnp.transpose(a_b_s[0]), dg, axis=1),
        a_w_out=a_w_out[0].astype(BF16),
        b_w_all=b_w_in.astype(BF16),
        b_conv_w=b_conv_w[0], b_conv_b=b_conv_b,
        b_w_if=jnp.pad(w_in_b[:, qk_w + 2 * d_b:], ((0, 0), (0, LANES - 2 * H_B))).astype(BF16),
        b_bias_if=jnp.pad(jnp.concatenate([b_bias_i, b_bias_f], axis=1),
                          ((0, 0), (0, LANES - 2 * H_B))),
        b_gn_g=b_gn_g, b_w_out=b_w_out[0].astype(BF16),
        f_w_up=f_w_up.astype(BF16),
        f_conv_w=f_conv_w, f_conv_b=f_conv_b, f_w_down=f_w_down.astype(BF16),
        k_scale=float(dk) ** -0.5,
    )
    state = dict(mlstm_C=state_mlstm_C[0], mlstm_n=state_mlstm_n[0], mlstm_m=state_mlstm_m[0],
                 mlstm_conv=state_mlstm_conv[0], ffn_conv=state_ffn_conv)
    p, s = _trunks(x_prompt, x_sample, state, w)
    return (p["y"], s["y"], p["C"], p["n"], p["m"], p["mconv"], p["fconv"],
            s["v"], s["C"], s["n"], s["m"], s["mconv"], s["fconv"])
```

```python
import functools

import jax
import jax.numpy as jnp
from jax import lax
from jax.experimental import pallas as pl
from jax.experimental.pallas import tpu as pltpu

F32 = jnp.float32
BF16 = jnp.bfloat16

EPS = 1e-6
LANES = 128
SUBLANES = 8
VMEM_LIMIT_BYTES = 56 * 1024 * 1024

CHUNK_A = 128
G_A = 8
H_B = 8
CONV_B = 4
CONV_F = 3
LAYER_A = 0
LAYER_B = 1

PROMPT_ROWS = 512
PROMPT_MLSTM_ROWS = 256
SAMPLE_ROWS = 256
STEP_SEQS = 8


def _const_spec(shape):
    nd = len(shape)
    return pl.BlockSpec(shape, lambda *_: (0,) * nd, pipeline_mode=pl.Buffered(1))


def _layer_spec(shape, layer):
    return pl.BlockSpec((1,) + tuple(shape[1:]), lambda *_: (layer, 0, 0),
                        pipeline_mode=pl.Buffered(1))


def _params(n_axes):
    return pltpu.CompilerParams(
        dimension_semantics=("arbitrary",) * n_axes,
        vmem_limit_bytes=VMEM_LIMIT_BYTES,
    )


def _software_pipeline(chains, n_stages):
    for t in range(len(chains) + n_stages - 1):
        for i in reversed(range(len(chains))):
            if i <= t < i + n_stages:
                next(chains[i])


def _rmsnorm(x, g):
    ms = jnp.mean(x * x, axis=-1, keepdims=True)
    return x * lax.rsqrt(ms + EPS) * g


def _expand3(c3, nb, ls):
    w = c3.shape[-1]
    if nb == 1:
        return jnp.broadcast_to(c3.reshape(1, w), (ls, w))
    return jnp.broadcast_to(c3, (nb, ls, w)).reshape(nb * ls, w)


def _expand2(c2, nb, ls):
    return _expand3(c2[:, None, :], nb, ls)


def _causal_conv(a, carry_ref, cw_ref, cb_ref, nb, ls, cols=slice(None)):
    r, c = a.shape
    kw = cw_ref.shape[0]
    rolled = [a] + [pltpu.roll(a, k, 0) for k in range(1, kw)]

    def taps(shifted):
        y = cb_ref[:, cols] + cw_ref[kw - 1:kw, cols] * shifted[0]
        for k in range(1, kw):
            y = y + cw_ref[kw - 1 - k:kw - k, cols] * shifted[k]
        return y

    grp = SUBLANES if nb == 1 else r
    grp_ls = SUBLANES if nb == 1 else ls
    tpos = lax.broadcasted_iota(jnp.int32, (grp, c), 0) & (grp_ls - 1)
    fixed = [a[0:grp]]
    for k in range(1, kw):
        sh = rolled[k][0:grp]
        for t in range(k):
            idx = kw - 1 + t - k
            prev = _expand3(carry_ref[:, idx:idx + 1, cols], nb, grp_ls)
            sh = jnp.where(tpos == t, prev, sh)
        fixed.append(sh)
    y = taps(fixed)
    if nb == 1:
        y = jnp.concatenate([y, taps([s[grp:] for s in rolled])], axis=0)
    last = rolled[kw - 1]
    if nb == 1:
        carry_ref[0, :, cols] = last[0:kw - 1, :]
    else:
        heads = last.reshape(nb, ls, c)[:, 0:kw - 1, :]
        carry_ref[0:nb - 1, :, cols] = heads[1:nb]
        carry_ref[nb - 1:nb, :, cols] = heads[0:1]
    return y


def _mixer_a_kernel(xl_ref, xs_ref, g_ref, win_ref, lng_ref, lnb_ref, wsl_ref, bsl_ref, wss_ref,
                    bss_ref, wout_ref, yl_ref, ys_ref, v_ref, *, long_steps, short_chunk):
    s = pl.program_id(0)
    shared = (g_ref.at[pl.ds(LAYER_A, 1)], win_ref, lng_ref, lnb_ref, wout_ref)

    @pl.when(s < long_steps)
    def _():
        _mixer_a_tile(xl_ref, yl_ref, None, wsl_ref, bsl_ref, *shared, CHUNK_A)

    @pl.when(s >= long_steps)
    def _():
        _mixer_a_tile(xs_ref, ys_ref, v_ref, wss_ref, bss_ref, *shared, short_chunk)


def _mixer_a_tile(x_ref, y_ref, v_ref, ws_ref, bs_ref, g_ref, win_ref, lng_ref, lnb_ref,
                  wout_ref, seq_chunk):
    emit_v = v_ref is not None
    tm = x_ref.shape[0]
    d_a = lng_ref.shape[1]
    dg = d_a // G_A
    x = x_ref[...]
    h = _rmsnorm(x, g_ref[...]).astype(BF16)
    v_pre = {}
    v_chunks = []
    ahead = 1
    for g in range(G_A + ahead):
        if g < G_A:
            v_pre[g] = jnp.dot(h, win_ref[:, d_a + g * dg:d_a + (g + 1) * dg],
                               preferred_element_type=F32)
        if g >= ahead:
            v_chunks.append(jax.nn.gelu(v_pre.pop(g - ahead)))
    t_i = lax.broadcasted_iota(jnp.int32, (CHUNK_A, CHUNK_A), 0)
    s_i = lax.broadcasted_iota(jnp.int32, (CHUNK_A, CHUNK_A), 1)
    mask = (s_i <= t_i) & ((t_i // seq_chunk) == (s_i // seq_chunk))
    stats = {}
    gated = {}

    def group_stages(g):
        cols = slice(g * dg, (g + 1) * dg)
        u_pre = jnp.dot(h, win_ref[:, cols], preferred_element_type=F32)
        yield
        vn = ((v_chunks[g] - stats["mu"]) * stats["rs"]) * lng_ref[:, cols] + lnb_ref[:, cols]
        if emit_v:
            v_ref[:, cols] = vn
        vb = vn.astype(BF16)
        w = jnp.where(mask, ws_ref[g], jnp.zeros((), BF16))
        mixed = [jnp.dot(w, vb[c * CHUNK_A:(c + 1) * CHUNK_A], preferred_element_type=F32)
                 + bs_ref[:, cols] for c in range(tm // CHUNK_A)]
        yield
        u = jax.nn.gelu(u_pre)
        gated[g] = jnp.concatenate(
            [(u[c * CHUNK_A:(c + 1) * CHUNK_A] * mixed[c]).astype(BF16)
             for c in range(tm // CHUNK_A)], axis=0)
        yield

    lead = 3
    times = (0, lead, lead + 1)
    groups = [group_stages(g) for g in range(G_A)]
    y = x
    for t in range(G_A + times[-1]):
        if t == lead:
            mu = sum(jnp.sum(v, axis=-1, keepdims=True) for v in v_chunks) / d_a
            var = sum(jnp.sum((v - mu) * (v - mu), axis=-1, keepdims=True)
                      for v in v_chunks) / d_a
            stats["mu"] = mu
            stats["rs"] = lax.rsqrt(var + EPS)
        for g in reversed(range(G_A)):
            if t - g in times:
                next(groups[g])
        p, odd = divmod(t - times[-1], 2)
        if odd and 0 <= p < G_A // 2:
            lhs = jnp.concatenate([gated[2 * p], gated[2 * p + 1]], axis=1)
            y = y + jnp.dot(lhs, wout_ref[2 * p * dg:(2 * p + 2) * dg, :],
                            preferred_element_type=F32)
    y_ref[...] = y


def _mixer_a(xl2, xs2, g, w_in, ln_g, ln_b, ws_l, bs_l, ws_s, bs_s, w_out, *, long_tm, short_tm,
             short_chunk):
    tl, d = xl2.shape
    ts = xs2.shape[0]
    d_a = ln_g.shape[1]
    n_l = tl // long_tm

    def l_rows(width):
        return pl.BlockSpec((long_tm, width), lambda s: (jnp.minimum(s, n_l - 1), 0))

    def s_rows(width):
        return pl.BlockSpec((short_tm, width), lambda s: (jnp.maximum(s - n_l, 0), 0))

    consts = [g, w_in, ln_g, ln_b, ws_l, bs_l, ws_s, bs_s, w_out]
    return pl.pallas_call(
        functools.partial(_mixer_a_kernel, long_steps=n_l, short_chunk=short_chunk),
        grid=(n_l + ts // short_tm,),
        in_specs=[l_rows(d), s_rows(d)] + [_const_spec(a.shape) for a in consts],
        out_specs=[l_rows(d), s_rows(d), s_rows(d_a)],
        out_shape=[jax.ShapeDtypeStruct((tl, d), F32),
                   jax.ShapeDtypeStruct((ts, d), F32),
                   jax.ShapeDtypeStruct((ts, d_a), F32)],
        compiler_params=_params(1),
        name="mixer_a",
    )(xl2, xs2, *consts)


def _ffn_tile(x_ref, y_ref, nbuf_ref, g_ref, wup_ref, cw_ref, cb_ref, wd_ref, fg_ref, nb, ls):
    x = x_ref[...]
    h = _rmsnorm(x, g_ref[...]).astype(BF16)
    d_ff = wd_ref.shape[1]
    a = jnp.dot(h, wup_ref[0, :, :d_ff], preferred_element_type=F32)
    gv = jnp.dot(h, wup_ref[0, :, d_ff:], preferred_element_type=F32)
    a_c = _causal_conv(a, nbuf_ref, cw_ref, cb_ref, nb, ls)
    act = (jax.nn.gelu(a_c) * gv).astype(BF16)
    y = x + jnp.dot(act, wd_ref[0], preferred_element_type=F32)
    if fg_ref is not None:
        y = _rmsnorm(y, fg_ref[...])
    y_ref[...] = y


def _ffn_kernel(*refs, layer, fresh_steps, fresh_nj, fresh_ls, carried_nb, carried_ls,
                final_norm):
    refs = list(refs)
    xf_ref, xc_ref, g_ref, wup_ref, cw_ref, cb_ref, wd_ref, buf_ref = refs[:8]
    pos = 8
    fg_ref = None
    if final_norm:
        fg_ref = refs[pos]
        pos += 1
    yf_ref, yc_ref, nbuff_ref, nbufc_ref = refs[pos:pos + 4]
    s = pl.program_id(0)
    shared = (g_ref.at[pl.ds(layer, 1)], wup_ref, cw_ref.at[layer], cb_ref.at[pl.ds(layer, 1)],
              wd_ref, fg_ref)

    @pl.when(s < fresh_steps)
    def _():
        @pl.when(s % fresh_nj == 0)
        def _():
            nbuff_ref[...] = jnp.zeros(nbuff_ref.shape, F32)

        _ffn_tile(xf_ref, yf_ref, nbuff_ref, *shared, 1, fresh_ls)

    @pl.when(s >= fresh_steps)
    def _():
        nbufc_ref[...] = buf_ref[0]
        _ffn_tile(xc_ref, yc_ref, nbufc_ref, *shared, carried_nb, carried_ls)


def _ffn(xf2, xc2, g, w_up, cw, cb, w_d, buf, final_g, *, layer, fresh_bsz, fresh_seq, fresh_ls,
         carried_nb, carried_ls):
    tf, d = xf2.shape
    tc = xc2.shape[0]
    d_ff = w_d.shape[1]
    nj = fresh_seq // fresh_ls
    fs = fresh_bsz * nj
    cs = tc // (carried_nb * carried_ls)
    final_norm = final_g is not None
    f_rows = pl.BlockSpec((fresh_ls, d), lambda s: (jnp.minimum(s, fs - 1), 0))
    c_rows = pl.BlockSpec((carried_nb * carried_ls, d), lambda s: (jnp.maximum(s - fs, 0), 0))
    f_buf = pl.BlockSpec((1, CONV_F - 1, d_ff), lambda s: (jnp.minimum(s, fs - 1) // nj, 0, 0))
    c_buf = pl.BlockSpec((carried_nb, CONV_F - 1, d_ff), lambda s: (jnp.maximum(s - fs, 0), 0, 0))
    c_buf_in = pl.BlockSpec((1, carried_nb, CONV_F - 1, d_ff),
                            lambda s: (layer, jnp.maximum(s - fs, 0), 0, 0))
    args = [xf2, xc2, g, w_up, cw, cb, w_d, buf]
    in_specs = [f_rows, c_rows, _const_spec(g.shape), _layer_spec(w_up.shape, layer),
                _const_spec(cw.shape), _const_spec(cb.shape), _layer_spec(w_d.shape, layer),
                c_buf_in]
    if final_norm:
        args.append(final_g)
        in_specs.append(_const_spec(final_g.shape))
    return pl.pallas_call(
        functools.partial(_ffn_kernel, layer=layer, fresh_steps=fs, fresh_nj=nj, fresh_ls=fresh_ls,
                          carried_nb=carried_nb, carried_ls=carried_ls, final_norm=final_norm),
        grid=(fs + cs,),
        in_specs=in_specs,
        out_specs=[f_rows, c_rows, f_buf, c_buf],
        out_shape=[jax.ShapeDtypeStruct((tf, d), F32),
                   jax.ShapeDtypeStruct((tc, d), F32),
                   jax.ShapeDtypeStruct((fresh_bsz, CONV_F - 1, d_ff), F32),
                   jax.ShapeDtypeStruct((tc // carried_ls, CONV_F - 1, d_ff), F32)],
        compiler_params=_params(1),
        name="conv_ffn",
    )(*args)


def _select_sum(sel, x):
    return jnp.dot(sel.astype(F32), x, precision=lax.Precision.HIGHEST,
                   preferred_element_type=F32)


def _log_sigmoid(x):
    return jnp.minimum(x, 0.0) - jnp.log1p(jnp.exp(-jnp.abs(x)))


def _mlstm_proj_kernel(x_ref, g_ref, wall_ref, wif_ref, cw_ref, cb_ref, bif_ref, buf_ref,
                       q_ref, k_ref, v_ref, o_ref, ig_ref, lf_ref, nbuf_ref, *, nb, ls, k_scale):
    nbuf_ref[...] = buf_ref[...]
    h = _rmsnorm(x_ref[...], g_ref[LAYER_B:LAYER_B + 1, :]).astype(BF16)
    qk_w = cw_ref.shape[1]
    d_b = v_ref.shape[1]
    qk_pre = jnp.dot(h, wall_ref[0, :, 0:qk_w], preferred_element_type=F32)
    v_ref[...] = jnp.dot(h, wall_ref[0, :, qk_w:qk_w + d_b], preferred_element_type=F32)
    o_ref[...] = jnp.dot(h, wall_ref[0, :, qk_w + d_b:qk_w + 2 * d_b],
                         preferred_element_type=F32)
    gates = jnp.dot(h, wif_ref[...], preferred_element_type=F32) + bif_ref[...]
    ig_ref[...] = gates
    lf_ref[...] = pltpu.roll(_log_sigmoid(gates), LANES - H_B, 1)
    qk = jax.nn.silu(_causal_conv(qk_pre, nbuf_ref, cw_ref, cb_ref, nb, ls))
    dq = q_ref.shape[1]
    q_ref[...] = qk[:, :dq]
    k_ref[...] = qk[:, dq:] * k_scale


def _mlstm_proj(x2, g, w_all, w_if, cw, cb, b_if, buf, *, d_b, bsz, seq, nb, ls, k_scale):
    t, d = x2.shape
    qk_w = cw.shape[1]
    assert seq == ls

    def rows(width):
        return pl.BlockSpec((nb * ls, width), lambda i: (i, 0))

    buf_spec = pl.BlockSpec((nb, CONV_B - 1, qk_w), lambda i: (i, 0, 0))
    consts = [w_if, cw, cb, b_if]
    return pl.pallas_call(
        functools.partial(_mlstm_proj_kernel, nb=nb, ls=ls, k_scale=k_scale),
        grid=(bsz // nb,),
        in_specs=([rows(d), _const_spec(g.shape), _layer_spec(w_all.shape, 0)]
                  + [_const_spec(a.shape) for a in consts] + [buf_spec]),
        out_specs=[rows(qk_w // 2), rows(qk_w // 2), rows(d_b), rows(d_b), rows(LANES),
                   rows(LANES), buf_spec],
        out_shape=[jax.ShapeDtypeStruct((t, qk_w // 2), F32),
                   jax.ShapeDtypeStruct((t, qk_w // 2), F32),
                   jax.ShapeDtypeStruct((t, d_b), F32),
                   jax.ShapeDtypeStruct((t, d_b), F32),
                   jax.ShapeDtypeStruct((t, LANES), F32),
                   jax.ShapeDtypeStruct((t, LANES), F32),
                   jax.ShapeDtypeStruct((bsz, CONV_B - 1, qk_w), F32)],
        compiler_params=_params(1),
        name="mlstm_proj",
    )(x2, g, w_all, *consts, buf)


def _mlstm_step_kernel(q_ref, k_ref, v_ref, ig_ref, lf_ref, c0_ref, n0_ref, m0_ref,
                       h_ref, c_ref, n_ref, m_ref, *, nb, ls):
    r = nb * ls
    rc = LANES
    dk = c0_ref.shape[2]
    dv = c0_ref.shape[3]

    def pad_rows(a):
        if r == rc:
            return a
        return jnp.concatenate([a, jnp.zeros((rc - r, a.shape[1]), a.dtype)], axis=0)

    r_i = lax.broadcasted_iota(jnp.int32, (r, rc), 0)
    c_i = lax.broadcasted_iota(jnp.int32, (r, rc), 1)
    mask = (c_i <= r_i) & ((r_i // ls) == (c_i // ls))
    ig = ig_ref[...]
    b_all = _select_sum(mask, pad_rows(lf_ref[...]))
    b_t = pad_rows(b_all).T
    ig_t = pad_rows(ig).T
    m0 = m0_ref[...]
    inter_all = b_all + _expand2(m0, nb, ls)
    lane = lax.broadcasted_iota(jnp.int32, (r, LANES), 1)
    m_t_of = {}

    def output_stages(hd):
        kcols = slice(hd * dk, (hd + 1) * dk)
        vcols = slice(hd * dv, (hd + 1) * dv)
        q = q_ref[:, kcols]
        qb = q.astype(BF16)
        qc = jnp.concatenate(
            [jnp.dot(qb[b * ls:(b + 1) * ls], c0_ref[b, hd].astype(BF16),
                     preferred_element_type=F32) for b in range(nb)], axis=0)
        d = jnp.where(mask, b_all[:, hd:hd + 1] - b_t[hd:hd + 1, :] + ig_t[hd:hd + 1, :],
                      -jnp.inf)
        inter = inter_all[:, hd:hd + 1]
        m_t = jnp.maximum(inter, jnp.max(d, axis=1, keepdims=True))
        m_t_of[hd] = m_t
        w_intra = jnp.exp(d - m_t)
        w_inter = jnp.exp(inter - m_t)
        yield
        kb = pad_rows(k_ref[:, kcols].astype(BF16))
        vb = pad_rows(v_ref[:, vcols].astype(BF16))
        s = lax.dot_general(qb, kb, (((1,), (1,)), ((), ())), preferred_element_type=F32)
        sc = s * w_intra
        intra = jnp.dot(sc.astype(BF16), vb, preferred_element_type=F32)
        den_intra = jnp.sum(sc, axis=1, keepdims=True)
        qn = jnp.sum(q * _expand2(n0_ref[hd], nb, ls), axis=1, keepdims=True)
        yield
        num = w_inter * qc + intra
        den = w_inter * qn + den_intra
        h_ref[:, vcols] = num / jnp.maximum(jnp.abs(den), jnp.exp(-m_t))
        yield

    _software_pipeline([output_stages(hd) for hd in range(H_B)], 3)
    mt_all = jnp.zeros((r, LANES), F32)
    for hd in range(H_B):
        mt_all = jnp.where(lane == hd, m_t_of[hd], mt_all)

    p_r = lax.broadcasted_iota(jnp.int32, (nb, rc), 0)
    p_c = lax.broadcasted_iota(jnp.int32, (nb, rc), 1)
    pick = p_c == p_r * ls + (ls - 1)
    bl_seq = _select_sum(pick, pad_rows(b_all))
    mn_seq = _select_sum(pick, pad_rows(mt_all))
    decay = jnp.exp(bl_seq + m0 - mn_seq)
    m_ref[...] = mn_seq
    g_all = jnp.exp(_expand2(bl_seq, nb, ls) - b_all + ig - _expand2(mn_seq, nb, ls))
    col_seq = lax.broadcasted_iota(jnp.int32, (dk, rc), 1) // ls

    def state_stages(hd):
        kg = k_ref[:, hd * dk:(hd + 1) * dk] * g_all[:, hd:hd + 1]
        n_ref[hd] = decay[:, hd:hd + 1] * n0_ref[hd] + jnp.sum(kg.reshape(nb, ls, dk), axis=1)
        kg_t = pad_rows(kg).T
        vb = pad_rows(v_ref[:, hd * dv:(hd + 1) * dv].astype(BF16))
        yield
        for b in range(nb):
            lhs = jnp.where(col_seq == b, kg_t, 0.0).astype(BF16)
            c_ref[b, hd] = decay[b:b + 1, hd:hd + 1] * c0_ref[b, hd] + jnp.dot(
                lhs, vb, preferred_element_type=F32)
        yield

    _software_pipeline([state_stages(hd) for hd in range(H_B)], 2)


def _mlstm_step(q, k, v, ig, lf, c0, n0, m0, *, bsz, nb, ls):
    t = q.shape[0]
    dk = q.shape[1] // H_B
    dv = v.shape[1] // H_B

    def rows(width):
        return pl.BlockSpec((nb * ls, width), lambda i: (i, 0))

    c_spec = pl.BlockSpec((nb, H_B, dk, dv), lambda i: (i, 0, 0, 0))
    n_spec = pl.BlockSpec((H_B, nb, dk), lambda i: (0, i, 0))
    m_spec = pl.BlockSpec((nb, LANES), lambda i: (i, 0))
    return pl.pallas_call(
        functools.partial(_mlstm_step_kernel, nb=nb, ls=ls),
        grid=(bsz // nb,),
        in_specs=[rows(H_B * dk), rows(H_B * dk), rows(H_B * dv), rows(LANES), rows(LANES),
                  c_spec, n_spec, m_spec],
        out_specs=[rows(H_B * dv), c_spec, n_spec, m_spec],
        out_shape=[jax.ShapeDtypeStruct((t, H_B * dv), F32),
                   jax.ShapeDtypeStruct((bsz, H_B, dk, dv), F32),
                   jax.ShapeDtypeStruct((H_B, bsz, dk), F32),
                   jax.ShapeDtypeStruct((bsz, LANES), F32)],
        compiler_params=_params(1),
        name="mlstm_step",
    )(q, k, v, ig, lf, c0, n0, m0)


def _mlstm_out_kernel(h_ref, o_ref, x_ref, gn_ref, wout_ref, y_ref, gate_ref):
    dv = h_ref.shape[1] // H_B
    for hd in range(H_B):
        cols = slice(hd * dv, (hd + 1) * dv)
        hh = h_ref[:, cols]
        mu = jnp.mean(hh, axis=-1, keepdims=True)
        hc = hh - mu
        var = jnp.mean(hc * hc, axis=-1, keepdims=True)
        hn = hc * lax.rsqrt(var + EPS) * gn_ref[:, cols]
        gate_ref[:, cols] = (jax.nn.sigmoid(o_ref[:, cols]) * hn).astype(BF16)
    y_ref[...] = x_ref[...] + jnp.dot(gate_ref[...], wout_ref[...], preferred_element_type=F32)


def _mlstm_out(h, o, x2, gn_g, w_out, *, tm):
    t, d = x2.shape
    d_b = h.shape[1]
    return pl.pallas_call(
        _mlstm_out_kernel,
        grid=(t // tm,),
        in_specs=[pl.BlockSpec((tm, d_b), lambda i: (i, 0)),
                  pl.BlockSpec((tm, d_b), lambda i: (i, 0)),
                  pl.BlockSpec((tm, d), lambda i: (i, 0)),
                  _const_spec(gn_g.shape), _const_spec(w_out.shape)],
        out_specs=pl.BlockSpec((tm, d), lambda i: (i, 0)),
        out_shape=jax.ShapeDtypeStruct((t, d), F32),
        scratch_shapes=[pltpu.VMEM((tm, d_b), BF16)],
        compiler_params=_params(1),
        name="mlstm_out",
    )(h, o, x2, gn_g, w_out)


def _mlstm_fused_kernel(x_ref, g_ref, wall_ref, wif_ref, cw_ref, cb_ref, bif_ref,
                        gn_ref, wout_ref, y_ref, c_ref, n_ref, m_ref, nbuf_ref, *, k_scale):
    @pl.when(pl.program_id(1) == 0)
    def _():
        c_ref[...] = jnp.zeros(c_ref.shape, F32)
        n_ref[...] = jnp.zeros(n_ref.shape, F32)
        m_ref[...] = jnp.zeros(m_ref.shape, F32)
        nbuf_ref[...] = jnp.zeros(nbuf_ref.shape, F32)

    r = x_ref.shape[0]
    dk = c_ref.shape[2]
    dv = c_ref.shape[3]

    x = x_ref[...]
    h = _rmsnorm(x, g_ref[LAYER_B:LAYER_B + 1, :]).astype(BF16)
    gates = jnp.dot(h, wif_ref[...], preferred_element_type=F32) + bif_ref[...]
    r_i = lax.broadcasted_iota(jnp.int32, (r, r), 0)
    c_i = lax.broadcasted_iota(jnp.int32, (r, r), 1)
    mask = c_i <= r_i
    b_all = _select_sum(mask, _log_sigmoid(gates))
    b_t = b_all.T
    ig_t = gates.T

    pair = 2 * dk
    n_pairs = H_B // 2
    qk_w = cw_ref.shape[1]

    def w_cols(base, cols):
        return wall_ref[0, :, base + cols.start:base + cols.stop]

    q_chunks = {}
    k_chunks = {}
    gated = {}

    def qk_chunk(c):
        cols = slice(c * pair, (c + 1) * pair)
        pre = jnp.dot(h, w_cols(0, cols), preferred_element_type=F32)
        return jax.nn.silu(_causal_conv(pre, nbuf_ref, cw_ref, cb_ref, 1, r, cols))

    def head_stages(hd):
        p, half = divmod(hd, 2)
        ig_col = gates[:, hd:hd + 1]
        i_row = ig_t[hd:hd + 1, :]
        b_col = b_all[:, H_B + hd:H_B + hd + 1]
        b_row = b_t[H_B + hd:H_B + hd + 1, :]
        m_prev = m_ref[0, hd:hd + 1, 0:1]
        vcols = slice(hd * dv, (hd + 1) * dv)
        vb = jnp.dot(h, w_cols(qk_w, vcols), preferred_element_type=F32).astype(BF16)
        d = jnp.where(mask, b_col - b_row + i_row, -jnp.inf)
        yield
        o_pre = jnp.dot(h, w_cols(qk_w + H_B * dv, vcols), preferred_element_type=F32)
        inter = b_col + m_prev
        m_t = jnp.maximum(inter, jnp.max(d, axis=1, keepdims=True))
        w_intra = jnp.exp(d - m_t)
        w_inter = jnp.exp(inter - m_t)
        yield
        q = q_chunks[p][:, half * dk:(half + 1) * dk]
        k = k_chunks[p][:, half * dk:(half + 1) * dk] * k_scale
        qb = q.astype(BF16)
        s = lax.dot_general(qb, k.astype(BF16), (((1,), (1,)), ((), ())),
                            preferred_element_type=F32)
        sc = s * w_intra
        intra = jnp.dot(sc.astype(BF16), vb, preferred_element_type=F32)
        den_intra = jnp.sum(sc, axis=1, keepdims=True)
        c_old = c_ref[0, hd]
        n_old = n_ref[0, hd:hd + 1, :]
        qc = jnp.dot(qb, c_old.astype(BF16), preferred_element_type=F32)
        qn = jnp.sum(q * n_old, axis=1, keepdims=True)
        yield
        num = w_inter * qc + intra
        den = w_inter * qn + den_intra
        hout = num / jnp.maximum(jnp.abs(den), jnp.exp(-m_t))
        mu = jnp.mean(hout, axis=-1, keepdims=True)
        hc = hout - mu
        var = jnp.mean(hc * hc, axis=-1, keepdims=True)
        hn = hc * lax.rsqrt(var + EPS) * gn_ref[:, vcols]
        yield
        gated[hd] = (jax.nn.sigmoid(o_pre) * hn).astype(BF16)
        b_last = b_col[r - 1:r, :]
        m_new = m_t[r - 1:r, :]
        g = jnp.exp(b_last - b_col + ig_col - m_new)
        g_row = jnp.exp(b_last - b_row + i_row - m_new)
        decay = jnp.exp(b_last + m_prev - m_new)
        yield
        kg_t = (k.T * g_row).astype(BF16)
        c_ref[0, hd] = decay * c_old + jnp.dot(kg_t, vb, preferred_element_type=F32)
        n_ref[0, hd:hd + 1, :] = decay * n_old + jnp.sum(k * g, axis=0, keepdims=True)
        m_ref[0, hd:hd + 1, :] = jnp.broadcast_to(m_new, (1, LANES))
        yield

    n_stages = 6
    gate_stage = 4
    q_chunks[0] = qk_chunk(0)
    k_chunks[0] = qk_chunk(n_pairs)
    heads = [head_stages(hd) for hd in range(H_B)]
    y = x
    for t in range(H_B + n_stages - 1):
        if t // 2 + 1 < n_pairs:
            if t % 2 == 0:
                q_chunks[t // 2 + 1] = qk_chunk(t // 2 + 1)
            else:
                k_chunks[t // 2 + 1] = qk_chunk(n_pairs + t // 2 + 1)
        for hd in reversed(range(H_B)):
            if hd <= t < hd + n_stages:
                next(heads[hd])
        p, odd = divmod(t - gate_stage, 2)
        if odd and 0 <= p < n_pairs:
            lhs = jnp.concatenate([gated[2 * p], gated[2 * p + 1]], axis=1)
            y = y + jnp.dot(lhs, wout_ref[2 * p * dv:(2 * p + 2) * dv, :],
                            preferred_element_type=F32)
    y_ref[...] = y


def _mlstm_fused(x2, g, w_all, w_if, cw, cb, b_if, gn_g, w_out, *, bsz, seq, ls, k_scale):
    t, d = x2.shape
    qk_w = cw.shape[1]
    d_b = w_out.shape[0]
    dk = qk_w // (2 * H_B)
    dv = d_b // H_B
    nj = seq // ls
    row_spec = pl.BlockSpec((ls, d), lambda i, j: (i * nj + j, 0))

    def state_spec(*tail):
        return pl.BlockSpec((1,) + tail, lambda i, j: (i,) + (0,) * len(tail))

    args = [x2, g, w_all, w_if, cw, cb, b_if, gn_g, w_out]
    return pl.pallas_call(
        functools.partial(_mlstm_fused_kernel, k_scale=k_scale),
        grid=(bsz, nj),
        in_specs=([row_spec, _const_spec(g.shape), _layer_spec(w_all.shape, 0)]
                  + [_const_spec(a.shape) for a in args[3:]]),
        out_specs=[row_spec, state_spec(H_B, dk, dv), state_spec(H_B, dk),
                   state_spec(H_B, LANES), state_spec(CONV_B - 1, qk_w)],
        out_shape=[jax.ShapeDtypeStruct((t, d), F32),
                   jax.ShapeDtypeStruct((bsz, H_B, dk, dv), F32),
                   jax.ShapeDtypeStruct((bsz, H_B, dk), F32),
                   jax.ShapeDtypeStruct((bsz, H_B, LANES), F32),
                   jax.ShapeDtypeStruct((bsz, CONV_B - 1, qk_w), F32)],
        compiler_params=_params(2),
        name="mlstm_fused",
    )(*args)


def _trunks(xp, xs, state, w):
    bp, sp, d = xp.shape
    bs_, ss, _ = xs.shape
    assert sp % PROMPT_ROWS == 0 and PROMPT_ROWS % PROMPT_MLSTM_ROWS == 0
    assert PROMPT_MLSTM_ROWS % CHUNK_A == 0 and PROMPT_MLSTM_ROWS & (PROMPT_MLSTM_ROWS - 1) == 0
    assert ss & (ss - 1) == 0 and ss % SUBLANES == 0 and ss <= CHUNK_A
    assert (bs_ * ss) % SAMPLE_ROWS == 0 and bs_ % STEP_SEQS == 0 and STEP_SEQS * ss <= LANES
    xp2 = xp.reshape(bp * sp, d)
    xs2 = xs.reshape(bs_ * ss, d)
    sample_nb = SAMPLE_ROWS // ss

    ws_s = jnp.tile(w["a_ws"][:, :ss, :ss], (1, CHUNK_A // ss, CHUNK_A // ss))
    bs_s = jnp.tile(w["a_bs_t"][:ss], (CHUNK_A // ss, 1))
    xp2, xs2, v_rows = _mixer_a(
        xp2, xs2, w["norm_mix_g"], w["a_w_in"], w["a_ln_g"], w["a_ln_b"], w["a_ws"],
        w["a_bs_t"], ws_s, bs_s, w["a_w_out"], long_tm=PROMPT_ROWS, short_tm=SAMPLE_ROWS,
        short_chunk=ss)
    ffn_tiles = dict(fresh_bsz=bp, fresh_seq=sp, fresh_ls=PROMPT_ROWS, carried_nb=sample_nb,
                     carried_ls=ss)
    xp2, xs2, fbuf0_p, fbuf0_s = _ffn(
        xp2, xs2, w["norm_ffn_g"], w["f_w_up"], w["f_conv_w"], w["f_conv_b"], w["f_w_down"],
        state["ffn_conv"], None, layer=0, **ffn_tiles)

    xp2, c_p, n_p, m_p, mconv_p = _mlstm_fused(
        xp2, w["norm_mix_g"], w["b_w_all"], w["b_w_if"], w["b_conv_w"], w["b_conv_b"],
        w["b_bias_if"], w["b_gn_g"], w["b_w_out"], bsz=bp, seq=sp, ls=PROMPT_MLSTM_ROWS,
        k_scale=w["k_scale"])
    q, k, v, o, ig, lf, mconv_s = _mlstm_proj(
        xs2, w["norm_mix_g"], w["b_w_all"], w["b_w_if"], w["b_conv_w"], w["b_conv_b"],
        w["b_bias_if"], state["mlstm_conv"],
        d_b=w["b_w_out"].shape[0], bsz=bs_, seq=ss, nb=sample_nb, ls=ss, k_scale=w["k_scale"])
    n0 = jnp.transpose(state["mlstm_n"], (1, 0, 2))
    m0 = jnp.pad(state["mlstm_m"], ((0, 0), (0, LANES - H_B)))
    h, c_s, n_s, m_s = _mlstm_step(q, k, v, ig, lf, state["mlstm_C"], n0, m0, bsz=bs_,
                                   nb=STEP_SEQS, ls=ss)
    xs2 = _mlstm_out(h, o, xs2, w["b_gn_g"], w["b_w_out"], tm=SAMPLE_ROWS)
    yp2, ys2, fbuf1_p, fbuf1_s = _ffn(
        xp2, xs2, w["norm_ffn_g"], w["f_w_up"], w["f_conv_w"], w["f_conv_b"], w["f_w_down"],
        state["ffn_conv"], w["final_norm_g"], layer=1, **ffn_tiles)
    prompt = dict(y=yp2.reshape(bp, sp, d), C=c_p[None], n=n_p[None], m=m_p[:, :, 0][None],
                  mconv=mconv_p[None], fconv=jnp.stack([fbuf0_p, fbuf1_p]))
    sample = dict(y=ys2.reshape(bs_, ss, d), v=v_rows.reshape(1, bs_, ss, -1), C=c_s[None],
                  n=jnp.transpose(n_s, (1, 0, 2))[None], m=m_s[:, :H_B][None],
                  mconv=mconv_s[None], fconv=jnp.stack([fbuf0_s, fbuf1_s]))
    return prompt, sample


def kernel(x_prompt, x_sample, state_mlstm_C, state_mlstm_n, state_mlstm_m, state_mlstm_conv, state_ffn_conv, norm_mix_g, norm_ffn_g, final_norm_g, a_w_in, a_ln_g, a_ln_b, a_w_s, a_b_s, a_w_out, b_w_in, b_conv_w, b_conv_b, b_bias_i, b_bias_f, b_gn_g, b_w_out, f_w_up, f_conv_w, f_conv_b, f_w_down):
    d_ff = f_w_down.shape[1]
    qk_w = b_conv_w.shape[2]
    d_b = b_w_out.shape[1]
    dk = qk_w // (2 * H_B)
    d_a = a_w_out.shape[1]
    dg = d_a // G_A

    w_in_b = b_w_in[0]
    w = dict(
        norm_mix_g=norm_mix_g, norm_ffn_g=norm_ffn_g, final_norm_g=final_norm_g[None, :],
        a_w_in=a_w_in[0].astype(BF16), a_ln_g=a_ln_g, a_ln_b=a_ln_b,
        a_ws=a_w_s[0].astype(BF16),
        a_bs_t=jnp.repeat(jnp.transpose(a_b_s[0]), dg, axis=1),
        a_w_out=a_w_out[0].astype(BF16),
        b_w_all=b_w_in.astype(BF16),
        b_conv_w=b_conv_w[0], b_conv_b=b_conv_b,
        b_w_if=jnp.pad(w_in_b[:, qk_w + 2 * d_b:], ((0, 0), (0, LANES - 2 * H_B))).astype(BF16),
        b_bias_if=jnp.pad(jnp.concatenate([b_bias_i, b_bias_f], axis=1),
                          ((0, 0), (0, LANES - 2 * H_B))),
        b_gn_g=b_gn_g, b_w_out=b_w_out[0].astype(BF16),
        f_w_up=f_w_up.astype(BF16),
        f_conv_w=f_conv_w, f_conv_b=f_conv_b, f_w_down=f_w_down.astype(BF16),
        k_scale=float(dk) ** -0.5,
    )
    state = dict(mlstm_C=state_mlstm_C[0], mlstm_n=state_mlstm_n[0], mlstm_m=state_mlstm_m[0],
                 mlstm_conv=state_mlstm_conv[0], ffn_conv=state_ffn_conv)
    p, s = _trunks(x_prompt, x_sample, state, w)
    return (p["y"], s["y"], p["C"], p["n"], p["m"], p["mconv"], p["fconv"],
            s["v"], s["C"], s["n"], s["m"], s["mconv"], s["fconv"])
```

```python
import functools

import jax
import jax.numpy as jnp
from jax import lax
from jax.experimental import pallas as pl
from jax.experimental.pallas import tpu as pltpu

F32 = jnp.float32
BF16 = jnp.bfloat16

EPS = 1e-6
LANES = 128
SUBLANES = 8
VMEM_LIMIT_BYTES = 56 * 1024 * 1024

CHUNK_A = 128
G_A = 8
H_B = 8
CONV_B = 4
CONV_F = 3
LAYER_A = 0
LAYER_B = 1

PROMPT_ROWS = 512
PROMPT_MLSTM_ROWS = 256
SAMPLE_ROWS = 256
STEP_SEQS = 8


def _const_spec(shape):
    nd = len(shape)
    return pl.BlockSpec(shape, lambda *_: (0,) * nd, pipeline_mode=pl.Buffered(1))


def _layer_spec(shape, layer):
    return pl.BlockSpec((1,) + tuple(shape[1:]), lambda *_: (layer, 0, 0),
                        pipeline_mode=pl.Buffered(1))


def _params(n_axes):
    return pltpu.CompilerParams(
        dimension_semantics=("arbitrary",) * n_axes,
        vmem_limit_bytes=VMEM_LIMIT_BYTES,
    )


def _software_pipeline(chains, n_stages):
    for t in range(len(chains) + n_stages - 1):
        for i in reversed(range(len(chains))):
            if i <= t < i + n_stages:
                next(chains[i])


def _rmsnorm(x, g):
    ms = jnp.mean(x * x, axis=-1, keepdims=True)
    return x * lax.rsqrt(ms + EPS) * g


def _expand3(c3, nb, ls):
    w = c3.shape[-1]
    if nb == 1:
        return jnp.broadcast_to(c3.reshape(1, w), (ls, w))
    return jnp.broadcast_to(c3, (nb, ls, w)).reshape(nb * ls, w)


def _expand2(c2, nb, ls):
    return _expand3(c2[:, None, :], nb, ls)


def _causal_conv(a, carry_ref, cw_ref, cb_ref, nb, ls, cols=slice(None)):
    r, c = a.shape
    kw = cw_ref.shape[0]
    rolled = [a] + [pltpu.roll(a, k, 0) for k in range(1, kw)]

    def taps(shifted):
        y = cb_ref[:, cols] + cw_ref[kw - 1:kw, cols] * shifted[0]
        for k in range(1, kw):
            y = y + cw_ref[kw - 1 - k:kw - k, cols] * shifted[k]
        return y

    grp = SUBLANES if nb == 1 else r
    grp_ls = SUBLANES if nb == 1 else ls
    tpos = lax.broadcasted_iota(jnp.int32, (grp, c), 0) & (grp_ls - 1)
    fixed = [a[0:grp]]
    for k in range(1, kw):
        sh = rolled[k][0:grp]
        for t in range(k):
            idx = kw - 1 + t - k
            prev = _expand3(carry_ref[:, idx:idx + 1, cols], nb, grp_ls)
            sh = jnp.where(tpos == t, prev, sh)
        fixed.append(sh)
    y = taps(fixed)
    if nb == 1:
        y = jnp.concatenate([y, taps([s[grp:] for s in rolled])], axis=0)
    last = rolled[kw - 1]
    if nb == 1:
        carry_ref[0, :, cols] = last[0:kw - 1, :]
    else:
        heads = last.reshape(nb, ls, c)[:, 0:kw - 1, :]
        carry_ref[0:nb - 1, :, cols] = heads[1:nb]
        carry_ref[nb - 1:nb, :, cols] = heads[0:1]
    return y


def _mixer_a_kernel(xl_ref, xs_ref, g_ref, win_ref, lng_ref, lnb_ref, wsl_ref, bsl_ref, wss_ref,
                    bss_ref, wout_ref, yl_ref, ys_ref, v_ref, *, long_steps, short_chunk):
    s = pl.program_id(0)
    shared = (g_ref.at[pl.ds(LAYER_A, 1)], win_ref, lng_ref, lnb_ref, wout_ref)

    @pl.when(s < long_steps)
    def _():
        _mixer_a_tile(xl_ref, yl_ref, None, wsl_ref, bsl_ref, *shared, CHUNK_A)

    @pl.when(s >= long_steps)
    def _():
        _mixer_a_tile(xs_ref, ys_ref, v_ref, wss_ref, bss_ref, *shared, short_chunk)


def _mixer_a_tile(x_ref, y_ref, v_ref, ws_ref, bs_ref, g_ref, win_ref, lng_ref, lnb_ref,
                  wout_ref, seq_chunk):
    emit_v = v_ref is not None
    tm = x_ref.shape[0]
    d_a = lng_ref.shape[1]
    dg = d_a // G_A
    x = x_ref[...]
    h = _rmsnorm(x, g_ref[...]).astype(BF16)
    v_pre = {}
    v_chunks = []
    ahead = 1
    for g in range(G_A + ahead):
        if g < G_A:
            v_pre[g] = jnp.dot(h, win_ref[:, d_a + g * dg:d_a + (g + 1) * dg],
                               preferred_element_type=F32)
        if g >= ahead:
            v_chunks.append(jax.nn.gelu(v_pre.pop(g - ahead)))
    t_i = lax.broadcasted_iota(jnp.int32, (CHUNK_A, CHUNK_A), 0)
    s_i = lax.broadcasted_iota(jnp.int32, (CHUNK_A, CHUNK_A), 1)
    mask = (s_i <= t_i) & ((t_i // seq_chunk) == (s_i // seq_chunk))
    stats = {}
    gated = {}

    def group_stages(g):
        cols = slice(g * dg, (g + 1) * dg)
        u_pre = jnp.dot(h, win_ref[:, cols], preferred_element_type=F32)
        yield
        vn = ((v_chunks[g] - stats["mu"]) * stats["rs"]) * lng_ref[:, cols] + lnb_ref[:, cols]
        if emit_v:
            v_ref[:, cols] = vn
        vb = vn.astype(BF16)
        w = jnp.where(mask, ws_ref[g], jnp.zeros((), BF16))
        mixed = [jnp.dot(w, vb[c * CHUNK_A:(c + 1) * CHUNK_A], preferred_element_type=F32)
                 + bs_ref[:, cols] for c in range(tm // CHUNK_A)]
        yield
        u = jax.nn.gelu(u_pre)
        gated[g] = jnp.concatenate(
            [(u[c * CHUNK_A:(c + 1) * CHUNK_A] * mixed[c]).astype(BF16)
             for c in range(tm // CHUNK_A)], axis=0)
        yield

    lead = 3
    times = (0, lead, lead + 1)
    groups = [group_stages(g) for g in range(G_A)]
    y = x
    for t in range(G_A + times[-1]):
        if t == lead:
            mu = sum(jnp.sum(v, axis=-1, keepdims=True) for v in v_chunks) / d_a
            var = sum(jnp.sum((v - mu) * (v - mu), axis=-1, keepdims=True)
                      for v in v_chunks) / d_a
            stats["mu"] = mu
            stats["rs"] = lax.rsqrt(var + EPS)
        for g in reversed(range(G_A)):
            if t - g in times:
                next(groups[g])
        p, odd = divmod(t - times[-1], 2)
        if odd and 0 <= p < G_A // 2:
            lhs = jnp.concatenate([gated[2 * p], gated[2 * p + 1]], axis=1)
            y = y + jnp.dot(lhs, wout_ref[2 * p * dg:(2 * p + 2) * dg, :],
                            preferred_element_type=F32)
    y_ref[...] = y


def _mixer_a(xl2, xs2, g, w_in, ln_g, ln_b, ws_l, bs_l, ws_s, bs_s, w_out, *, long_tm, short_tm,
             short_chunk):
    tl, d = xl2.shape
    ts = xs2.shape[0]
    d_a = ln_g.shape[1]
    n_l = tl // long_tm

    def l_rows(width):
        return pl.BlockSpec((long_tm, width), lambda s: (jnp.minimum(s, n_l - 1), 0))

    def s_rows(width):
        return pl.BlockSpec((short_tm, width), lambda s: (jnp.maximum(s - n_l, 0), 0))

    consts = [g, w_in, ln_g, ln_b, ws_l, bs_l, ws_s, bs_s, w_out]
    return pl.pallas_call(
        functools.partial(_mixer_a_kernel, long_steps=n_l, short_chunk=short_chunk),
        grid=(n_l + ts // short_tm,),
        in_specs=[l_rows(d), s_rows(d)] + [_const_spec(a.shape) for a in consts],
        out_specs=[l_rows(d), s_rows(d), s_rows(d_a)],
        out_shape=[jax.ShapeDtypeStruct((tl, d), F32),
                   jax.ShapeDtypeStruct((ts, d), F32),
                   jax.ShapeDtypeStruct((ts, d_a), F32)],
        compiler_params=_params(1),
        name="mixer_a",
    )(xl2, xs2, *consts)


def _ffn_tile(x_ref, y_ref, nbuf_ref, g_ref, wup_ref, cw_ref, cb_ref, wd_ref, fg_ref, nb, ls):
    x = x_ref[...]
    h = _rmsnorm(x, g_ref[...]).astype(BF16)
    d_ff = wd_ref.shape[1]
    a = jnp.dot(h, wup_ref[0, :, :d_ff], preferred_element_type=F32)
    gv = jnp.dot(h, wup_ref[0, :, d_ff:], preferred_element_type=F32)
    a_c = _causal_conv(a, nbuf_ref, cw_ref, cb_ref, nb, ls)
    act = (jax.nn.gelu(a_c) * gv).astype(BF16)
    y = x + jnp.dot(act, wd_ref[0], preferred_element_type=F32)
    if fg_ref is not None:
        y = _rmsnorm(y, fg_ref[...])
    y_ref[...] = y


def _ffn_kernel(*refs, layer, fresh_steps, fresh_nj, fresh_ls, carried_nb, carried_ls,
                final_norm, stack_prev):
    refs = list(refs)
    xf_ref, xc_ref, g_ref, wup_ref, cw_ref, cb_ref, wd_ref, buf_ref = refs[:8]
    pos = 8
    if stack_prev:
        prevf_ref, prevc_ref = refs[pos:pos + 2]
        pos += 2
    fg_ref = None
    if final_norm:
        fg_ref = refs[pos]
        pos += 1
    yf_ref, yc_ref, nbuff_ref, nbufc_ref = refs[pos:pos + 4]
    carry_f = nbuff_ref.at[1] if stack_prev else nbuff_ref
    carry_c = nbufc_ref.at[1] if stack_prev else nbufc_ref
    s = pl.program_id(0)
    shared = (g_ref.at[pl.ds(layer, 1)], wup_ref, cw_ref.at[layer], cb_ref.at[pl.ds(layer, 1)],
              wd_ref, fg_ref)

    @pl.when(s < fresh_steps)
    def _():
        @pl.when(s % fresh_nj == 0)
        def _():
            carry_f[...] = jnp.zeros(carry_f.shape, F32)
            if stack_prev:
                nbuff_ref[0] = prevf_ref[...]

        _ffn_tile(xf_ref, yf_ref, carry_f, *shared, 1, fresh_ls)

    @pl.when(s >= fresh_steps)
    def _():
        carry_c[...] = buf_ref[0]
        if stack_prev:
            nbufc_ref[0] = prevc_ref[...]
        _ffn_tile(xc_ref, yc_ref, carry_c, *shared, carried_nb, carried_ls)


def _ffn(xf2, xc2, g, w_up, cw, cb, w_d, buf, final_g, prev, *, layer, fresh_bsz, fresh_seq,
         fresh_ls, carried_nb, carried_ls):
    tf, d = xf2.shape
    tc = xc2.shape[0]
    d_ff = w_d.shape[1]
    nj = fresh_seq // fresh_ls
    fs = fresh_bsz * nj
    cs = tc // (carried_nb * carried_ls)
    final_norm = final_g is not None
    f_rows = pl.BlockSpec((fresh_ls, d), lambda s: (jnp.minimum(s, fs - 1), 0))
    c_rows = pl.BlockSpec((carried_nb * carried_ls, d), lambda s: (jnp.maximum(s - fs, 0), 0))
    f_buf = pl.BlockSpec((1, CONV_F - 1, d_ff), lambda s: (jnp.minimum(s, fs - 1) // nj, 0, 0))
    c_buf = pl.BlockSpec((carried_nb, CONV_F - 1, d_ff), lambda s: (jnp.maximum(s - fs, 0), 0, 0))
    c_buf_in = pl.BlockSpec((1, carried_nb, CONV_F - 1, d_ff),
                            lambda s: (layer, jnp.maximum(s - fs, 0), 0, 0))
    args = [xf2, xc2, g, w_up, cw, cb, w_d, buf]
    in_specs = [f_rows, c_rows, _const_spec(g.shape), _layer_spec(w_up.shape, layer),
                _const_spec(cw.shape), _const_spec(cb.shape), _layer_spec(w_d.shape, layer),
                c_buf_in]
    out_bufs = [f_buf, c_buf]
    buf_shapes = [(fresh_bsz, CONV_F - 1, d_ff), (tc // carried_ls, CONV_F - 1, d_ff)]
    if prev is not None:
        args += list(prev)
        in_specs += [f_buf, c_buf]
        out_bufs = [pl.BlockSpec((2,) + spec.block_shape,
                                 lambda s, m=spec.index_map: (0,) + tuple(m(s))) for spec in out_bufs]
        buf_shapes = [(2,) + shp for shp in buf_shapes]
    if final_norm:
        args.append(final_g)
        in_specs.append(_const_spec(final_g.shape))
    return pl.pallas_call(
        functools.partial(_ffn_kernel, layer=layer, stack_prev=prev is not None, fresh_steps=fs, fresh_nj=nj, fresh_ls=fresh_ls,
                          carried_nb=carried_nb, carried_ls=carried_ls, final_norm=final_norm),
        grid=(fs + cs,),
        in_specs=in_specs,
        out_specs=[f_rows, c_rows] + out_bufs,
        out_shape=[jax.ShapeDtypeStruct((tf, d), F32), jax.ShapeDtypeStruct((tc, d), F32)]
        + [jax.ShapeDtypeStruct(shp, F32) for shp in buf_shapes],
        compiler_params=_params(1),
        name="conv_ffn",
    )(*args)


def _select_sum(sel, x):
    return jnp.dot(sel.astype(F32), x, precision=lax.Precision.HIGHEST,
                   preferred_element_type=F32)


def _log_sigmoid(x):
    return jnp.minimum(x, 0.0) - jnp.log1p(jnp.exp(-jnp.abs(x)))


def _mlstm_proj_kernel(x_ref, g_ref, wall_ref, wif_ref, cw_ref, cb_ref, bif_ref, buf_ref,
                       q_ref, k_ref, v_ref, o_ref, ig_ref, lf_ref, nbuf_ref, *, nb, ls, k_scale):
    nbuf_ref[...] = buf_ref[...]
    h = _rmsnorm(x_ref[...], g_ref[LAYER_B:LAYER_B + 1, :]).astype(BF16)
    qk_w = cw_ref.shape[1]
    d_b = v_ref.shape[1]
    qk_pre = jnp.dot(h, wall_ref[0, :, 0:qk_w], preferred_element_type=F32)
    v_ref[...] = jnp.dot(h, wall_ref[0, :, qk_w:qk_w + d_b], preferred_element_type=F32)
    o_ref[...] = jnp.dot(h, wall_ref[0, :, qk_w + d_b:qk_w + 2 * d_b],
                         preferred_element_type=F32)
    gates = jnp.dot(h, wif_ref[...], preferred_element_type=F32) + bif_ref[...]
    ig_ref[...] = gates
    lf_ref[...] = pltpu.roll(_log_sigmoid(gates), LANES - H_B, 1)
    qk = jax.nn.silu(_causal_conv(qk_pre, nbuf_ref, cw_ref, cb_ref, nb, ls))
    dq = q_ref.shape[1]
    q_ref[...] = qk[:, :dq]
    k_ref[...] = qk[:, dq:] * k_scale


def _mlstm_proj(x2, g, w_all, w_if, cw, cb, b_if, buf, *, d_b, bsz, seq, nb, ls, k_scale):
    t, d = x2.shape
    qk_w = cw.shape[1]
    assert seq == ls

    def rows(width):
        return pl.BlockSpec((nb * ls, width), lambda i: (i, 0))

    buf_spec = pl.BlockSpec((nb, CONV_B - 1, qk_w), lambda i: (i, 0, 0))
    consts = [w_if, cw, cb, b_if]
    return pl.pallas_call(
        functools.partial(_mlstm_proj_kernel, nb=nb, ls=ls, k_scale=k_scale),
        grid=(bsz // nb,),
        in_specs=([rows(d), _const_spec(g.shape), _layer_spec(w_all.shape, 0)]
                  + [_const_spec(a.shape) for a in consts] + [buf_spec]),
        out_specs=[rows(qk_w // 2), rows(qk_w // 2), rows(d_b), rows(d_b), rows(LANES),
                   rows(LANES), buf_spec],
        out_shape=[jax.ShapeDtypeStruct((t, qk_w // 2), F32),
                   jax.ShapeDtypeStruct((t, qk_w // 2), F32),
                   jax.ShapeDtypeStruct((t, d_b), F32),
                   jax.ShapeDtypeStruct((t, d_b), F32),
                   jax.ShapeDtypeStruct((t, LANES), F32),
                   jax.ShapeDtypeStruct((t, LANES), F32),
                   jax.ShapeDtypeStruct((bsz, CONV_B - 1, qk_w), F32)],
        compiler_params=_params(1),
        name="mlstm_proj",
    )(x2, g, w_all, *consts, buf)


def _mlstm_step_kernel(q_ref, k_ref, v_ref, ig_ref, lf_ref, c0_ref, n0_ref, m0_ref,
                       h_ref, c_ref, n_ref, m_ref, *, nb, ls):
    r = nb * ls
    rc = LANES
    dk = c0_ref.shape[2]
    dv = c0_ref.shape[3]

    def pad_rows(a):
        if r == rc:
            return a
        return jnp.concatenate([a, jnp.zeros((rc - r, a.shape[1]), a.dtype)], axis=0)

    r_i = lax.broadcasted_iota(jnp.int32, (r, rc), 0)
    c_i = lax.broadcasted_iota(jnp.int32, (r, rc), 1)
    mask = (c_i <= r_i) & ((r_i // ls) == (c_i // ls))
    ig = ig_ref[...]
    b_all = _select_sum(mask, pad_rows(lf_ref[...]))
    b_t = pad_rows(b_all).T
    ig_t = pad_rows(ig).T
    m0 = m0_ref[...]
    inter_all = b_all + _expand2(m0, nb, ls)
    lane = lax.broadcasted_iota(jnp.int32, (r, LANES), 1)
    m_t_of = {}

    def output_stages(hd):
        kcols = slice(hd * dk, (hd + 1) * dk)
        vcols = slice(hd * dv, (hd + 1) * dv)
        q = q_ref[:, kcols]
        qb = q.astype(BF16)
        qc = jnp.concatenate(
            [jnp.dot(qb[b * ls:(b + 1) * ls], c0_ref[b, hd].astype(BF16),
                     preferred_element_type=F32) for b in range(nb)], axis=0)
        d = jnp.where(mask, b_all[:, hd:hd + 1] - b_t[hd:hd + 1, :] + ig_t[hd:hd + 1, :],
                      -jnp.inf)
        inter = inter_all[:, hd:hd + 1]
        m_t = jnp.maximum(inter, jnp.max(d, axis=1, keepdims=True))
        m_t_of[hd] = m_t
        w_intra = jnp.exp(d - m_t)
        w_inter = jnp.exp(inter - m_t)
        yield
        kb = pad_rows(k_ref[:, kcols].astype(BF16))
        vb = pad_rows(v_ref[:, vcols].astype(BF16))
        s = lax.dot_general(qb, kb, (((1,), (1,)), ((), ())), preferred_element_type=F32)
        sc = s * w_intra
        intra = jnp.dot(sc.astype(BF16), vb, preferred_element_type=F32)
        den_intra = jnp.sum(sc, axis=1, keepdims=True)
        qn = jnp.sum(q * _expand2(n0_ref[hd], nb, ls), axis=1, keepdims=True)
        yield
        num = w_inter * qc + intra
        den = w_inter * qn + den_intra
        h_ref[:, vcols] = num / jnp.maximum(jnp.abs(den), jnp.exp(-m_t))
        yield

    _software_pipeline([output_stages(hd) for hd in range(H_B)], 3)
    mt_all = jnp.zeros((r, LANES), F32)
    for hd in range(H_B):
        mt_all = jnp.where(lane == hd, m_t_of[hd], mt_all)

    p_r = lax.broadcasted_iota(jnp.int32, (nb, rc), 0)
    p_c = lax.broadcasted_iota(jnp.int32, (nb, rc), 1)
    pick = p_c == p_r * ls + (ls - 1)
    bl_seq = _select_sum(pick, pad_rows(b_all))
    mn_seq = _select_sum(pick, pad_rows(mt_all))
    decay = jnp.exp(bl_seq + m0 - mn_seq)
    m_ref[...] = mn_seq
    g_all = jnp.exp(_expand2(bl_seq, nb, ls) - b_all + ig - _expand2(mn_seq, nb, ls))
    col_seq = lax.broadcasted_iota(jnp.int32, (dk, rc), 1) // ls

    def state_stages(hd):
        kg = k_ref[:, hd * dk:(hd + 1) * dk] * g_all[:, hd:hd + 1]
        n_ref[hd] = decay[:, hd:hd + 1] * n0_ref[hd] + jnp.sum(kg.reshape(nb, ls, dk), axis=1)
        kg_t = pad_rows(kg).T
        vb = pad_rows(v_ref[:, hd * dv:(hd + 1) * dv].astype(BF16))
        yield
        for b in range(nb):
            lhs = jnp.where(col_seq == b, kg_t, 0.0).astype(BF16)
            c_ref[b, hd] = decay[b:b + 1, hd:hd + 1] * c0_ref[b, hd] + jnp.dot(
                lhs, vb, preferred_element_type=F32)
        yield

    _software_pipeline([state_stages(hd) for hd in range(H_B)], 2)


def _mlstm_step(q, k, v, ig, lf, c0, n0, m0, *, bsz, nb, ls):
    t = q.shape[0]
    dk = q.shape[1] // H_B
    dv = v.shape[1] // H_B

    def rows(width):
        return pl.BlockSpec((nb * ls, width), lambda i: (i, 0))

    c_spec = pl.BlockSpec((nb, H_B, dk, dv), lambda i: (i, 0, 0, 0))
    n_spec = pl.BlockSpec((H_B, nb, dk), lambda i: (0, i, 0))
    m_spec = pl.BlockSpec((nb, LANES), lambda i: (i, 0))
    return pl.pallas_call(
        functools.partial(_mlstm_step_kernel, nb=nb, ls=ls),
        grid=(bsz // nb,),
        in_specs=[rows(H_B * dk), rows(H_B * dk), rows(H_B * dv), rows(LANES), rows(LANES),
                  c_spec, n_spec, m_spec],
        out_specs=[rows(H_B * dv), c_spec, n_spec, m_spec],
        out_shape=[jax.ShapeDtypeStruct((t, H_B * dv), F32),
                   jax.ShapeDtypeStruct((bsz, H_B, dk, dv), F32),
                   jax.ShapeDtypeStruct((H_B, bsz, dk), F32),
                   jax.ShapeDtypeStruct((bsz, LANES), F32)],
        compiler_params=_params(1),
        name="mlstm_step",
    )(q, k, v, ig, lf, c0, n0, m0)


def _mlstm_out_kernel(h_ref, o_ref, x_ref, gn_ref, wout_ref, y_ref, gate_ref):
    dv = h_ref.shape[1] // H_B
    for hd in range(H_B):
        cols = slice(hd * dv, (hd + 1) * dv)
        hh = h_ref[:, cols]
        mu = jnp.mean(hh, axis=-1, keepdims=True)
        hc = hh - mu
        var = jnp.mean(hc * hc, axis=-1, keepdims=True)
        hn = hc * lax.rsqrt(var + EPS) * gn_ref[:, cols]
        gate_ref[:, cols] = (jax.nn.sigmoid(o_ref[:, cols]) * hn).astype(BF16)
    y_ref[...] = x_ref[...] + jnp.dot(gate_ref[...], wout_ref[...], preferred_element_type=F32)


def _mlstm_out(h, o, x2, gn_g, w_out, *, tm):
    t, d = x2.shape
    d_b = h.shape[1]
    return pl.pallas_call(
        _mlstm_out_kernel,
        grid=(t // tm,),
        in_specs=[pl.BlockSpec((tm, d_b), lambda i: (i, 0)),
                  pl.BlockSpec((tm, d_b), lambda i: (i, 0)),
                  pl.BlockSpec((tm, d), lambda i: (i, 0)),
                  _const_spec(gn_g.shape), _const_spec(w_out.shape)],
        out_specs=pl.BlockSpec((tm, d), lambda i: (i, 0)),
        out_shape=jax.ShapeDtypeStruct((t, d), F32),
        scratch_shapes=[pltpu.VMEM((tm, d_b), BF16)],
        compiler_params=_params(1),
        name="mlstm_out",
    )(h, o, x2, gn_g, w_out)


def _mlstm_fused_kernel(x_ref, g_ref, wall_ref, wif_ref, cw_ref, cb_ref, bif_ref,
                        gn_ref, wout_ref, y_ref, c_ref, n_ref, m_ref, nbuf_ref, *, k_scale):
    @pl.when(pl.program_id(1) == 0)
    def _():
        c_ref[...] = jnp.zeros(c_ref.shape, F32)
        n_ref[...] = jnp.zeros(n_ref.shape, F32)
        m_ref[...] = jnp.zeros(m_ref.shape, F32)
        nbuf_ref[...] = jnp.zeros(nbuf_ref.shape, F32)

    r = x_ref.shape[0]
    dk = c_ref.shape[2]
    dv = c_ref.shape[3]

    x = x_ref[...]
    h = _rmsnorm(x, g_ref[LAYER_B:LAYER_B + 1, :]).astype(BF16)
    gates = jnp.dot(h, wif_ref[...], preferred_element_type=F32) + bif_ref[...]
    r_i = lax.broadcasted_iota(jnp.int32, (r, r), 0)
    c_i = lax.broadcasted_iota(jnp.int32, (r, r), 1)
    mask = c_i <= r_i
    b_all = _select_sum(mask, _log_sigmoid(gates))
    b_t = b_all.T
    ig_t = gates.T

    pair = 2 * dk
    n_pairs = H_B // 2
    qk_w = cw_ref.shape[1]

    def w_cols(base, cols):
        return wall_ref[0, :, base + cols.start:base + cols.stop]

    q_chunks = {}
    k_chunks = {}
    gated = {}

    def qk_chunk(c):
        cols = slice(c * pair, (c + 1) * pair)
        pre = jnp.dot(h, w_cols(0, cols), preferred_element_type=F32)
        return jax.nn.silu(_causal_conv(pre, nbuf_ref, cw_ref, cb_ref, 1, r, cols))

    def head_stages(hd):
        p, half = divmod(hd, 2)
        ig_col = gates[:, hd:hd + 1]
        i_row = ig_t[hd:hd + 1, :]
        b_col = b_all[:, H_B + hd:H_B + hd + 1]
        b_row = b_t[H_B + hd:H_B + hd + 1, :]
        m_prev = m_ref[0, hd:hd + 1, 0:1]
        vcols = slice(hd * dv, (hd + 1) * dv)
        vb = jnp.dot(h, w_cols(qk_w, vcols), preferred_element_type=F32).astype(BF16)
        o_pre = jnp.dot(h, w_cols(qk_w + H_B * dv, vcols), preferred_element_type=F32)
        d = jnp.where(mask, b_col - b_row + i_row, -jnp.inf)
        yield
        inter = b_col + m_prev
        m_t = jnp.maximum(inter, jnp.max(d, axis=1, keepdims=True))
        w_intra = jnp.exp(d - m_t)
        w_inter = jnp.exp(inter - m_t)
        yield
        q = q_chunks[p][:, half * dk:(half + 1) * dk]
        k = k_chunks[p][:, half * dk:(half + 1) * dk] * k_scale
        qb = q.astype(BF16)
        s = lax.dot_general(qb, k.astype(BF16), (((1,), (1,)), ((), ())),
                            preferred_element_type=F32)
        sc = s * w_intra
        intra = jnp.dot(sc.astype(BF16), vb, preferred_element_type=F32)
        den_intra = jnp.sum(sc, axis=1, keepdims=True)
        c_old = c_ref[0, hd]
        n_old = n_ref[0, hd:hd + 1, :]
        qc = jnp.dot(qb, c_old.astype(BF16), preferred_element_type=F32)
        qn = jnp.sum(q * n_old, axis=1, keepdims=True)
        yield
        num = w_inter * qc + intra
        den = w_inter * qn + den_intra
        hout = num / jnp.maximum(jnp.abs(den), jnp.exp(-m_t))
        mu = jnp.mean(hout, axis=-1, keepdims=True)
        hc = hout - mu
        var = jnp.mean(hc * hc, axis=-1, keepdims=True)
        hn = hc * lax.rsqrt(var + EPS) * gn_ref[:, vcols]
        yield
        gated[hd] = (jax.nn.sigmoid(o_pre) * hn).astype(BF16)
        b_last = b_col[r - 1:r, :]
        m_new = m_t[r - 1:r, :]
        g = jnp.exp(b_last - b_col + ig_col - m_new)
        g_row = jnp.exp(b_last - b_row + i_row - m_new)
        decay = jnp.exp(b_last + m_prev - m_new)
        yield
        kg_t = (k.T * g_row).astype(BF16)
        c_ref[0, hd] = decay * c_old + jnp.dot(kg_t, vb, preferred_element_type=F32)
        n_ref[0, hd:hd + 1, :] = decay * n_old + jnp.sum(k * g, axis=0, keepdims=True)
        m_ref[0, hd:hd + 1, :] = jnp.broadcast_to(m_new, (1, LANES))
        yield

    n_stages = 6
    gate_stage = 4
    q_chunks[0] = qk_chunk(0)
    k_chunks[0] = qk_chunk(n_pairs)
    heads = [head_stages(hd) for hd in range(H_B)]
    y = x
    for t in range(H_B + n_stages - 1):
        if t % 2 == 0 and t // 2 + 1 < n_pairs:
            q_chunks[t // 2 + 1] = qk_chunk(t // 2 + 1)
            k_chunks[t // 2 + 1] = qk_chunk(n_pairs + t // 2 + 1)
        for hd in reversed(range(H_B)):
            if hd <= t < hd + n_stages:
                next(heads[hd])
        p, odd = divmod(t - gate_stage, 2)
        if odd and 0 <= p < n_pairs:
            lhs = jnp.concatenate([gated[2 * p], gated[2 * p + 1]], axis=1)
            y = y + jnp.dot(lhs, wout_ref[2 * p * dv:(2 * p + 2) * dv, :],
                            preferred_element_type=F32)
    y_ref[...] = y


def _mlstm_fused(x2, g, w_all, w_if, cw, cb, b_if, gn_g, w_out, *, bsz, seq, ls, k_scale):
    t, d = x2.shape
    qk_w = cw.shape[1]
    d_b = w_out.shape[0]
    dk = qk_w // (2 * H_B)
    dv = d_b // H_B
    nj = seq // ls
    row_spec = pl.BlockSpec((ls, d), lambda i, j: (i * nj + j, 0))

    def state_spec(*tail):
        return pl.BlockSpec((1,) + tail, lambda i, j: (i,) + (0,) * len(tail))

    args = [x2, g, w_all, w_if, cw, cb, b_if, gn_g, w_out]
    return pl.pallas_call(
        functools.partial(_mlstm_fused_kernel, k_scale=k_scale),
        grid=(bsz, nj),
        in_specs=([row_spec, _const_spec(g.shape), _layer_spec(w_all.shape, 0)]
                  + [_const_spec(a.shape) for a in args[3:]]),
        out_specs=[row_spec, state_spec(H_B, dk, dv), state_spec(H_B, dk),
                   state_spec(H_B, LANES), state_spec(CONV_B - 1, qk_w)],
        out_shape=[jax.ShapeDtypeStruct((t, d), F32),
                   jax.ShapeDtypeStruct((bsz, H_B, dk, dv), F32),
                   jax.ShapeDtypeStruct((bsz, H_B, dk), F32),
                   jax.ShapeDtypeStruct((bsz, H_B, LANES), F32),
                   jax.ShapeDtypeStruct((bsz, CONV_B - 1, qk_w), F32)],
        compiler_params=_params(2),
        name="mlstm_fused",
    )(*args)


def _trunks(xp, xs, state, w):
    bp, sp, d = xp.shape
    bs_, ss, _ = xs.shape
    assert sp % PROMPT_ROWS == 0 and PROMPT_ROWS % PROMPT_MLSTM_ROWS == 0
    assert PROMPT_MLSTM_ROWS % CHUNK_A == 0 and PROMPT_MLSTM_ROWS & (PROMPT_MLSTM_ROWS - 1) == 0
    assert ss & (ss - 1) == 0 and ss % SUBLANES == 0 and ss <= CHUNK_A
    assert (bs_ * ss) % SAMPLE_ROWS == 0 and bs_ % STEP_SEQS == 0 and STEP_SEQS * ss <= LANES
    xp2 = xp.reshape(bp * sp, d)
    xs2 = xs.reshape(bs_ * ss, d)
    sample_nb = SAMPLE_ROWS // ss

    ws_s = jnp.tile(w["a_ws"][:, :ss, :ss], (1, CHUNK_A // ss, CHUNK_A // ss))
    bs_s = jnp.tile(w["a_bs_t"][:ss], (CHUNK_A // ss, 1))
    xp2, xs2, v_rows = _mixer_a(
        xp2, xs2, w["norm_mix_g"], w["a_w_in"], w["a_ln_g"], w["a_ln_b"], w["a_ws"],
        w["a_bs_t"], ws_s, bs_s, w["a_w_out"], long_tm=PROMPT_ROWS, short_tm=SAMPLE_ROWS,
        short_chunk=ss)
    ffn_tiles = dict(fresh_bsz=bp, fresh_seq=sp, fresh_ls=PROMPT_ROWS, carried_nb=sample_nb,
                     carried_ls=ss)
    xp2, xs2, fbuf0_p, fbuf0_s = _ffn(
        xp2, xs2, w["norm_ffn_g"], w["f_w_up"], w["f_conv_w"], w["f_conv_b"], w["f_w_down"],
        state["ffn_conv"], None, None, layer=0, **ffn_tiles)

    xp2, c_p, n_p, m_p, mconv_p = _mlstm_fused(
        xp2, w["norm_mix_g"], w["b_w_all"], w["b_w_if"], w["b_conv_w"], w["b_conv_b"],
        w["b_bias_if"], w["b_gn_g"], w["b_w_out"], bsz=bp, seq=sp, ls=PROMPT_MLSTM_ROWS,
        k_scale=w["k_scale"])
    q, k, v, o, ig, lf, mconv_s = _mlstm_proj(
        xs2, w["norm_mix_g"], w["b_w_all"], w["b_w_if"], w["b_conv_w"], w["b_conv_b"],
        w["b_bias_if"], state["mlstm_conv"],
        d_b=w["b_w_out"].shape[0], bsz=bs_, seq=ss, nb=sample_nb, ls=ss, k_scale=w["k_scale"])
    n0 = jnp.transpose(state["mlstm_n"], (1, 0, 2))
    m0 = jnp.pad(state["mlstm_m"], ((0, 0), (0, LANES - H_B)))
    h, c_s, n_s, m_s = _mlstm_step(q, k, v, ig, lf, state["mlstm_C"], n0, m0, bsz=bs_,
                                   nb=STEP_SEQS, ls=ss)
    xs2 = _mlstm_out(h, o, xs2, w["b_gn_g"], w["b_w_out"], tm=SAMPLE_ROWS)
    yp2, ys2, fconv_p, fconv_s = _ffn(
        xp2, xs2, w["norm_ffn_g"], w["f_w_up"], w["f_conv_w"], w["f_conv_b"], w["f_w_down"],
        state["ffn_conv"], w["final_norm_g"], (fbuf0_p, fbuf0_s), layer=1, **ffn_tiles)
    prompt = dict(y=yp2.reshape(bp, sp, d), C=c_p[None], n=n_p[None], m=m_p[:, :, 0][None],
                  mconv=mconv_p[None], fconv=fconv_p)
    sample = dict(y=ys2.reshape(bs_, ss, d), v=v_rows.reshape(1, bs_, ss, -1), C=c_s[None],
                  n=jnp.transpose(n_s, (1, 0, 2))[None], m=m_s[:, :H_B][None],
                  mconv=mconv_s[None], fconv=fconv_s)
    return prompt, sample


def kernel(x_prompt, x_sample, state_mlstm_C, state_mlstm_n, state_mlstm_m, state_mlstm_conv, state_ffn_conv, norm_mix_g, norm_ffn_g, final_norm_g, a_w_in, a_ln_g, a_ln_b, a_w_s, a_b_s, a_w_out, b_w_in, b_conv_w, b_conv_b, b_bias_i, b_bias_f, b_gn_g, b_w_out, f_w_up, f_conv_w, f_conv_b, f_w_down):
    d_ff = f_w_down.shape[1]
    qk_w = b_conv_w.shape[2]
    d_b = b_w_out.shape[1]
    dk = qk_w // (2 * H_B)
    d_a = a_w_out.shape[1]
    dg = d_a // G_A

    w_in_b = b_w_in[0]
    w = dict(
        norm_mix_g=norm_mix_g, norm_ffn_g=norm_ffn_g, final_norm_g=final_norm_g[None, :],
        a_w_in=a_w_in[0].astype(BF16), a_ln_g=a_ln_g, a_ln_b=a_ln_b,
        a_ws=a_w_s[0].astype(BF16),
        a_bs_t=jnp.repeat(jnp.transpose(a_b_s[0]), dg, axis=1),
        a_w_out=a_w_out[0].astype(BF16),
        b_w_all=b_w_in.astype(BF16),
        b_conv_w=b_conv_w[0], b_conv_b=b_conv_b,
        b_w_if=jnp.pad(w_in_b[:, qk_w + 2 * d_b:], ((0, 0), (0, LANES - 2 * H_B))).astype(BF16),
        b_bias_if=jnp.pad(jnp.concatenate([b_bias_i, b_bias_f], axis=1),
                          ((0, 0), (0, LANES - 2 * H_B))),
        b_gn_g=b_gn_g, b_w_out=b_w_out[0].astype(BF16),
        f_w_up=f_w_up.astype(BF16),
        f_conv_w=f_conv_w, f_conv_b=f_conv_b, f_w_down=f_w_down.astype(BF16),
        k_scale=float(dk) ** -0.5,
    )
    state = dict(mlstm_C=state_mlstm_C[0], mlstm_n=state_mlstm_n[0], mlstm_m=state_mlstm_m[0],
                 mlstm_conv=state_mlstm_conv[0], ffn_conv=state_ffn_conv)
    p, s = _trunks(x_prompt, x_sample, state, w)
    return (p["y"], s["y"], p["C"], p["n"], p["m"], p["mconv"], p["fconv"],
            s["v"], s["C"], s["n"], s["m"], s["mconv"], s["fconv"])
```

```python
import functools

import jax
import jax.numpy as jnp
from jax import lax
from jax.experimental import pallas as pl
from jax.experimental.pallas import tpu as pltpu

F32 = jnp.float32
BF16 = jnp.bfloat16

EPS = 1e-6
LANES = 128
SUBLANES = 8
VMEM_LIMIT_BYTES = 56 * 1024 * 1024

CHUNK_A = 128
G_A = 8
H_B = 8
CONV_B = 4
CONV_F = 3
LAYER_A = 0
LAYER_B = 1

PROMPT_ROWS = 512
PROMPT_MLSTM_ROWS = 256
SAMPLE_ROWS = 256
STEP_SEQS = 8


def _const_spec(shape):
    nd = len(shape)
    return pl.BlockSpec(shape, lambda *_: (0,) * nd, pipeline_mode=pl.Buffered(1))


def _layer_spec(shape, layer):
    return pl.BlockSpec((1,) + tuple(shape[1:]), lambda *_: (layer, 0, 0),
                        pipeline_mode=pl.Buffered(1))


def _params(n_axes):
    return pltpu.CompilerParams(
        dimension_semantics=("arbitrary",) * n_axes,
        vmem_limit_bytes=VMEM_LIMIT_BYTES,
    )


def _software_pipeline(chains, n_stages):
    for t in range(len(chains) + n_stages - 1):
        for i in reversed(range(len(chains))):
            if i <= t < i + n_stages:
                next(chains[i])


def _rmsnorm(x, g):
    ms = jnp.mean(x * x, axis=-1, keepdims=True)
    return x * lax.rsqrt(ms + EPS) * g


def _expand3(c3, nb, ls):
    w = c3.shape[-1]
    if nb == 1:
        return jnp.broadcast_to(c3.reshape(1, w), (ls, w))
    return jnp.broadcast_to(c3, (nb, ls, w)).reshape(nb * ls, w)


def _expand2(c2, nb, ls):
    return _expand3(c2[:, None, :], nb, ls)


def _causal_conv(a, carry_ref, cw_ref, cb_ref, nb, ls, cols=slice(None)):
    r, c = a.shape
    kw = cw_ref.shape[0]
    rolled = [a] + [pltpu.roll(a, k, 0) for k in range(1, kw)]

    def taps(shifted):
        y = cb_ref[:, cols] + cw_ref[kw - 1:kw, cols] * shifted[0]
        for k in range(1, kw):
            y = y + cw_ref[kw - 1 - k:kw - k, cols] * shifted[k]
        return y

    grp = SUBLANES if nb == 1 else r
    grp_ls = SUBLANES if nb == 1 else ls
    tpos = lax.broadcasted_iota(jnp.int32, (grp, c), 0) & (grp_ls - 1)
    fixed = [a[0:grp]]
    for k in range(1, kw):
        sh = rolled[k][0:grp]
        for t in range(k):
            idx = kw - 1 + t - k
            prev = _expand3(carry_ref[:, idx:idx + 1, cols], nb, grp_ls)
            sh = jnp.where(tpos == t, prev, sh)
        fixed.append(sh)
    y = taps(fixed)
    if nb == 1:
        y = jnp.concatenate([y, taps([s[grp:] for s in rolled])], axis=0)
    last = rolled[kw - 1]
    if nb == 1:
        carry_ref[0, :, cols] = last[0:kw - 1, :]
    else:
        heads = last.reshape(nb, ls, c)[:, 0:kw - 1, :]
        carry_ref[0:nb - 1, :, cols] = heads[1:nb]
        carry_ref[nb - 1:nb, :, cols] = heads[0:1]
    return y


def _mixer_a_kernel(xl_ref, xs_ref, g_ref, win_ref, lng_ref, lnb_ref, wsl_ref, bsl_ref, wss_ref,
                    bss_ref, wout_ref, yl_ref, ys_ref, v_ref, *, long_steps, short_chunk):
    s = pl.program_id(0)
    shared = (g_ref.at[pl.ds(LAYER_A, 1)], win_ref, lng_ref, lnb_ref, wout_ref)

    @pl.when(s < long_steps)
    def _():
        _mixer_a_tile(xl_ref, yl_ref, None, wsl_ref, bsl_ref, *shared, CHUNK_A)

    @pl.when(s >= long_steps)
    def _():
        _mixer_a_tile(xs_ref, ys_ref, v_ref, wss_ref, bss_ref, *shared, short_chunk)


def _mixer_a_tile(x_ref, y_ref, v_ref, ws_ref, bs_ref, g_ref, win_ref, lng_ref, lnb_ref,
                  wout_ref, seq_chunk):
    emit_v = v_ref is not None
    tm = x_ref.shape[0]
    d_a = lng_ref.shape[1]
    dg = d_a // G_A
    x = x_ref[...]
    h = _rmsnorm(x, g_ref[...]).astype(BF16)
    v_pre = {}
    v_chunks = []
    ahead = 1
    for g in range(G_A + ahead):
        if g < G_A:
            v_pre[g] = jnp.dot(h, win_ref[:, d_a + g * dg:d_a + (g + 1) * dg],
                               preferred_element_type=F32)
        if g >= ahead:
            v_chunks.append(jax.nn.gelu(v_pre.pop(g - ahead)))
    t_i = lax.broadcasted_iota(jnp.int32, (CHUNK_A, CHUNK_A), 0)
    s_i = lax.broadcasted_iota(jnp.int32, (CHUNK_A, CHUNK_A), 1)
    mask = (s_i <= t_i) & ((t_i // seq_chunk) == (s_i // seq_chunk))
    stats = {}
    gated = {}

    def group_stages(g):
        cols = slice(g * dg, (g + 1) * dg)
        u_pre = jnp.dot(h, win_ref[:, cols], preferred_element_type=F32)
        yield
        vn = ((v_chunks[g] - stats["mu"]) * stats["rs"]) * lng_ref[:, cols] + lnb_ref[:, cols]
        if emit_v:
            v_ref[:, cols] = vn
        vb = vn.astype(BF16)
        w = jnp.where(mask, ws_ref[g], jnp.zeros((), BF16))
        mixed = [jnp.dot(w, vb[c * CHUNK_A:(c + 1) * CHUNK_A], preferred_element_type=F32)
                 + bs_ref[:, cols] for c in range(tm // CHUNK_A)]
        yield
        u = jax.nn.gelu(u_pre)
        gated[g] = jnp.concatenate(
            [(u[c * CHUNK_A:(c + 1) * CHUNK_A] * mixed[c]).astype(BF16)
             for c in range(tm // CHUNK_A)], axis=0)
        yield

    lead = 3
    times = (0, lead, lead + 1)
    groups = [group_stages(g) for g in range(G_A)]
    y = x
    for t in range(G_A + times[-1]):
        if t == lead:
            mu = sum(jnp.sum(v, axis=-1, keepdims=True) for v in v_chunks) / d_a
            var = sum(jnp.sum((v - mu) * (v - mu), axis=-1, keepdims=True)
                      for v in v_chunks) / d_a
            stats["mu"] = mu
            stats["rs"] = lax.rsqrt(var + EPS)
        for g in reversed(range(G_A)):
            if t - g in times:
                next(groups[g])
        p, odd = divmod(t - times[-1], 2)
        if odd and 0 <= p < G_A // 2:
            lhs = jnp.concatenate([gated[2 * p], gated[2 * p + 1]], axis=1)
            y = y + jnp.dot(lhs, wout_ref[2 * p * dg:(2 * p + 2) * dg, :],
                            preferred_element_type=F32)
    y_ref[...] = y


def _mixer_a(xl2, xs2, g, w_in, ln_g, ln_b, ws_l, bs_l, ws_s, bs_s, w_out, *, long_tm, short_tm,
             short_chunk):
    tl, d = xl2.shape
    ts = xs2.shape[0]
    d_a = ln_g.shape[1]
    n_l = tl // long_tm

    def l_rows(width):
        return pl.BlockSpec((long_tm, width), lambda s: (jnp.minimum(s, n_l - 1), 0))

    def s_rows(width):
        return pl.BlockSpec((short_tm, width), lambda s: (jnp.maximum(s - n_l, 0), 0))

    consts = [g, w_in, ln_g, ln_b, ws_l, bs_l, ws_s, bs_s, w_out]
    return pl.pallas_call(
        functools.partial(_mixer_a_kernel, long_steps=n_l, short_chunk=short_chunk),
        grid=(n_l + ts // short_tm,),
        in_specs=[l_rows(d), s_rows(d)] + [_const_spec(a.shape) for a in consts],
        out_specs=[l_rows(d), s_rows(d), s_rows(d_a)],
        out_shape=[jax.ShapeDtypeStruct((tl, d), F32),
                   jax.ShapeDtypeStruct((ts, d), F32),
                   jax.ShapeDtypeStruct((ts, d_a), F32)],
        compiler_params=_params(1),
        name="mixer_a",
    )(xl2, xs2, *consts)


def _ffn_tile(x_ref, y_ref, nbuf_ref, g_ref, wup_ref, cw_ref, cb_ref, wd_ref, fg_ref, nb, ls):
    x = x_ref[...]
    h = _rmsnorm(x, g_ref[...]).astype(BF16)
    d_ff = wd_ref.shape[1]
    a = jnp.dot(h, wup_ref[0, :, :d_ff], preferred_element_type=F32)
    gv = jnp.dot(h, wup_ref[0, :, d_ff:], preferred_element_type=F32)
    a_c = _causal_conv(a, nbuf_ref, cw_ref, cb_ref, nb, ls)
    act = (jax.nn.gelu(a_c) * gv).astype(BF16)
    y = x + jnp.dot(act, wd_ref[0], preferred_element_type=F32)
    if fg_ref is not None:
        y = _rmsnorm(y, fg_ref[...])
    y_ref[...] = y


def _ffn_kernel(*refs, layer, fresh_steps, fresh_nj, fresh_ls, carried_nb, carried_ls,
                final_norm, stack_prev):
    refs = list(refs)
    xf_ref, xc_ref, g_ref, wup_ref, cw_ref, cb_ref, wd_ref, buf_ref = refs[:8]
    pos = 8
    if stack_prev:
        prevf_ref, prevc_ref = refs[pos:pos + 2]
        pos += 2
    fg_ref = None
    if final_norm:
        fg_ref = refs[pos]
        pos += 1
    yf_ref, yc_ref, nbuff_ref, nbufc_ref = refs[pos:pos + 4]
    carry_f = nbuff_ref.at[1] if stack_prev else nbuff_ref
    carry_c = nbufc_ref.at[1] if stack_prev else nbufc_ref
    s = pl.program_id(0)
    shared = (g_ref.at[pl.ds(layer, 1)], wup_ref, cw_ref.at[layer], cb_ref.at[pl.ds(layer, 1)],
              wd_ref, fg_ref)

    @pl.when(s < fresh_steps)
    def _():
        @pl.when(s % fresh_nj == 0)
        def _():
            carry_f[...] = jnp.zeros(carry_f.shape, F32)
            if stack_prev:
                nbuff_ref[0] = prevf_ref[...]

        _ffn_tile(xf_ref, yf_ref, carry_f, *shared, 1, fresh_ls)

    @pl.when(s >= fresh_steps)
    def _():
        carry_c[...] = buf_ref[0]
        if stack_prev:
            nbufc_ref[0] = prevc_ref[...]
        _ffn_tile(xc_ref, yc_ref, carry_c, *shared, carried_nb, carried_ls)


def _ffn(xf2, xc2, g, w_up, cw, cb, w_d, buf, final_g, prev, *, layer, fresh_bsz, fresh_seq,
         fresh_ls, carried_nb, carried_ls):
    tf, d = xf2.shape
    tc = xc2.shape[0]
    d_ff = w_d.shape[1]
    nj = fresh_seq // fresh_ls
    fs = fresh_bsz * nj
    cs = tc // (carried_nb * carried_ls)
    final_norm = final_g is not None
    f_rows = pl.BlockSpec((fresh_ls, d), lambda s: (jnp.minimum(s, fs - 1), 0))
    c_rows = pl.BlockSpec((carried_nb * carried_ls, d), lambda s: (jnp.maximum(s - fs, 0), 0))
    f_buf = pl.BlockSpec((1, CONV_F - 1, d_ff), lambda s: (jnp.minimum(s, fs - 1) // nj, 0, 0))
    c_buf = pl.BlockSpec((carried_nb, CONV_F - 1, d_ff), lambda s: (jnp.maximum(s - fs, 0), 0, 0))
    c_buf_in = pl.BlockSpec((1, carried_nb, CONV_F - 1, d_ff),
                            lambda s: (layer, jnp.maximum(s - fs, 0), 0, 0))
    args = [xf2, xc2, g, w_up, cw, cb, w_d, buf]
    in_specs = [f_rows, c_rows, _const_spec(g.shape), _layer_spec(w_up.shape, layer),
                _const_spec(cw.shape), _const_spec(cb.shape), _layer_spec(w_d.shape, layer),
                c_buf_in]
    out_bufs = [f_buf, c_buf]
    buf_shapes = [(fresh_bsz, CONV_F - 1, d_ff), (tc // carried_ls, CONV_F - 1, d_ff)]
    if prev is not None:
        args += list(prev)
        in_specs += [f_buf, c_buf]
        out_bufs = [pl.BlockSpec((2,) + spec.block_shape,
                                 lambda s, m=spec.index_map: (0,) + tuple(m(s))) for spec in out_bufs]
        buf_shapes = [(2,) + shp for shp in buf_shapes]
    if final_norm:
        args.append(final_g)
        in_specs.append(_const_spec(final_g.shape))
    return pl.pallas_call(
        functools.partial(_ffn_kernel, layer=layer, stack_prev=prev is not None, fresh_steps=fs, fresh_nj=nj, fresh_ls=fresh_ls,
                          carried_nb=carried_nb, carried_ls=carried_ls, final_norm=final_norm),
        grid=(fs + cs,),
        in_specs=in_specs,
        out_specs=[f_rows, c_rows] + out_bufs,
        out_shape=[jax.ShapeDtypeStruct((tf, d), F32), jax.ShapeDtypeStruct((tc, d), F32)]
        + [jax.ShapeDtypeStruct(shp, F32) for shp in buf_shapes],
        compiler_params=_params(1),
        name="conv_ffn",
    )(*args)


def _select_sum(sel, x):
    return jnp.dot(sel.astype(F32), x, precision=lax.Precision.HIGHEST,
                   preferred_element_type=F32)


def _log_sigmoid(x):
    return jnp.minimum(x, 0.0) - jnp.log1p(jnp.exp(-jnp.abs(x)))


def _mlstm_proj_kernel(x_ref, g_ref, wall_ref, wif_ref, cw_ref, cb_ref, bif_ref, buf_ref,
                       q_ref, k_ref, v_ref, o_ref, ig_ref, lf_ref, nbuf_ref, *, nb, ls, k_scale):
    nbuf_ref[...] = buf_ref[...]
    h = _rmsnorm(x_ref[...], g_ref[LAYER_B:LAYER_B + 1, :]).astype(BF16)
    qk_w = cw_ref.shape[1]
    d_b = v_ref.shape[1]
    qk_pre = jnp.dot(h, wall_ref[0, :, 0:qk_w], preferred_element_type=F32)
    v_ref[...] = jnp.dot(h, wall_ref[0, :, qk_w:qk_w + d_b], preferred_element_type=F32)
    o_ref[...] = jnp.dot(h, wall_ref[0, :, qk_w + d_b:qk_w + 2 * d_b],
                         preferred_element_type=F32)
    gates = jnp.dot(h, wif_ref[...], preferred_element_type=F32) + bif_ref[...]
    ig_ref[...] = gates
    lf_ref[...] = pltpu.roll(_log_sigmoid(gates), LANES - H_B, 1)
    qk = jax.nn.silu(_causal_conv(qk_pre, nbuf_ref, cw_ref, cb_ref, nb, ls))
    dq = q_ref.shape[1]
    q_ref[...] = qk[:, :dq]
    k_ref[...] = qk[:, dq:] * k_scale


def _mlstm_proj(x2, g, w_all, w_if, cw, cb, b_if, buf, *, d_b, bsz, seq, nb, ls, k_scale):
    t, d = x2.shape
    qk_w = cw.shape[1]
    assert seq == ls

    def rows(width):
        return pl.BlockSpec((nb * ls, width), lambda i: (i, 0))

    buf_spec = pl.BlockSpec((nb, CONV_B - 1, qk_w), lambda i: (i, 0, 0))
    consts = [w_if, cw, cb, b_if]
    return pl.pallas_call(
        functools.partial(_mlstm_proj_kernel, nb=nb, ls=ls, k_scale=k_scale),
        grid=(bsz // nb,),
        in_specs=([rows(d), _const_spec(g.shape), _layer_spec(w_all.shape, 0)]
                  + [_const_spec(a.shape) for a in consts] + [buf_spec]),
        out_specs=[rows(qk_w // 2), rows(qk_w // 2), rows(d_b), rows(d_b), rows(LANES),
                   rows(LANES), buf_spec],
        out_shape=[jax.ShapeDtypeStruct((t, qk_w // 2), F32),
                   jax.ShapeDtypeStruct((t, qk_w // 2), F32),
                   jax.ShapeDtypeStruct((t, d_b), F32),
                   jax.ShapeDtypeStruct((t, d_b), F32),
                   jax.ShapeDtypeStruct((t, LANES), F32),
                   jax.ShapeDtypeStruct((t, LANES), F32),
                   jax.ShapeDtypeStruct((bsz, CONV_B - 1, qk_w), F32)],
        compiler_params=_params(1),
        name="mlstm_proj",
    )(x2, g, w_all, *consts, buf)


def _mlstm_step_kernel(q_ref, k_ref, v_ref, ig_ref, lf_ref, c0_ref, n0_ref, m0_ref,
                       h_ref, c_ref, n_ref, m_ref, *, nb, ls):
    r = nb * ls
    rc = LANES
    dk = c0_ref.shape[2]
    dv = c0_ref.shape[3]

    def pad_rows(a):
        if r == rc:
            return a
        return jnp.concatenate([a, jnp.zeros((rc - r, a.shape[1]), a.dtype)], axis=0)

    r_i = lax.broadcasted_iota(jnp.int32, (r, rc), 0)
    c_i = lax.broadcasted_iota(jnp.int32, (r, rc), 1)
    mask = (c_i <= r_i) & ((r_i // ls) == (c_i // ls))
    ig = ig_ref[...]
    b_all = _select_sum(mask, pad_rows(lf_ref[...]))
    b_t = pad_rows(b_all).T
    ig_t = pad_rows(ig).T
    m0 = m0_ref[...]
    inter_all = b_all + _expand2(m0, nb, ls)
    lane = lax.broadcasted_iota(jnp.int32, (r, LANES), 1)
    m_t_of = {}

    def output_stages(hd):
        kcols = slice(hd * dk, (hd + 1) * dk)
        vcols = slice(hd * dv, (hd + 1) * dv)
        q = q_ref[:, kcols]
        qb = q.astype(BF16)
        qc = jnp.concatenate(
            [jnp.dot(qb[b * ls:(b + 1) * ls], c0_ref[b, hd].astype(BF16),
                     preferred_element_type=F32) for b in range(nb)], axis=0)
        d = jnp.where(mask, b_all[:, hd:hd + 1] - b_t[hd:hd + 1, :] + ig_t[hd:hd + 1, :],
                      -jnp.inf)
        inter = inter_all[:, hd:hd + 1]
        m_t = jnp.maximum(inter, jnp.max(d, axis=1, keepdims=True))
        m_t_of[hd] = m_t
        w_intra = jnp.exp(d - m_t)
        w_inter = jnp.exp(inter - m_t)
        yield
        kb = pad_rows(k_ref[:, kcols].astype(BF16))
        vb = pad_rows(v_ref[:, vcols].astype(BF16))
        s = lax.dot_general(qb, kb, (((1,), (1,)), ((), ())), preferred_element_type=F32)
        sc = s * w_intra
        intra = jnp.dot(sc.astype(BF16), vb, preferred_element_type=F32)
        den_intra = jnp.sum(sc, axis=1, keepdims=True)
        qn = jnp.sum(q * _expand2(n0_ref[:, hd, :], nb, ls), axis=1, keepdims=True)
        yield
        num = w_inter * qc + intra
        den = w_inter * qn + den_intra
        h_ref[:, vcols] = num / jnp.maximum(jnp.abs(den), jnp.exp(-m_t))
        yield

    _software_pipeline([output_stages(hd) for hd in range(H_B)], 3)
    mt_all = jnp.zeros((r, LANES), F32)
    for hd in range(H_B):
        mt_all = jnp.where(lane == hd, m_t_of[hd], mt_all)

    p_r = lax.broadcasted_iota(jnp.int32, (nb, rc), 0)
    p_c = lax.broadcasted_iota(jnp.int32, (nb, rc), 1)
    pick = p_c == p_r * ls + (ls - 1)
    bl_seq = _select_sum(pick, pad_rows(b_all))
    mn_seq = _select_sum(pick, pad_rows(mt_all))
    decay = jnp.exp(bl_seq + m0 - mn_seq)
    m_ref[...] = mn_seq
    g_all = jnp.exp(_expand2(bl_seq, nb, ls) - b_all + ig - _expand2(mn_seq, nb, ls))
    col_seq = lax.broadcasted_iota(jnp.int32, (dk, rc), 1) // ls

    def state_stages(hd):
        kg = k_ref[:, hd * dk:(hd + 1) * dk] * g_all[:, hd:hd + 1]
        n_ref[:, hd, :] = (decay[:, hd:hd + 1] * n0_ref[:, hd, :]
                           + jnp.sum(kg.reshape(nb, ls, dk), axis=1))
        kg_t = pad_rows(kg).T
        vb = pad_rows(v_ref[:, hd * dv:(hd + 1) * dv].astype(BF16))
        yield
        for b in range(nb):
            lhs = jnp.where(col_seq == b, kg_t, 0.0).astype(BF16)
            c_ref[b, hd] = decay[b:b + 1, hd:hd + 1] * c0_ref[b, hd] + jnp.dot(
                lhs, vb, preferred_element_type=F32)
        yield

    _software_pipeline([state_stages(hd) for hd in range(H_B)], 2)


def _mlstm_step(q, k, v, ig, lf, c0, n0, m0, *, bsz, nb, ls):
    t = q.shape[0]
    dk = q.shape[1] // H_B
    dv = v.shape[1] // H_B

    def rows(width):
        return pl.BlockSpec((nb * ls, width), lambda i: (i, 0))

    c_spec = pl.BlockSpec((nb, H_B, dk, dv), lambda i: (i, 0, 0, 0))
    n_spec = pl.BlockSpec((nb, H_B, dk), lambda i: (i, 0, 0))
    m_spec = pl.BlockSpec((nb, LANES), lambda i: (i, 0))
    return pl.pallas_call(
        functools.partial(_mlstm_step_kernel, nb=nb, ls=ls),
        grid=(bsz // nb,),
        in_specs=[rows(H_B * dk), rows(H_B * dk), rows(H_B * dv), rows(LANES), rows(LANES),
                  c_spec, n_spec, m_spec],
        out_specs=[rows(H_B * dv), c_spec, n_spec, m_spec],
        out_shape=[jax.ShapeDtypeStruct((t, H_B * dv), F32),
                   jax.ShapeDtypeStruct((bsz, H_B, dk, dv), F32),
                   jax.ShapeDtypeStruct((bsz, H_B, dk), F32),
                   jax.ShapeDtypeStruct((bsz, LANES), F32)],
        compiler_params=_params(1),
        name="mlstm_step",
    )(q, k, v, ig, lf, c0, n0, m0)


def _mlstm_out_kernel(h_ref, o_ref, x_ref, gn_ref, wout_ref, y_ref, gate_ref):
    dv = h_ref.shape[1] // H_B
    for hd in range(H_B):
        cols = slice(hd * dv, (hd + 1) * dv)
        hh = h_ref[:, cols]
        mu = jnp.mean(hh, axis=-1, keepdims=True)
        hc = hh - mu
        var = jnp.mean(hc * hc, axis=-1, keepdims=True)
        hn = hc * lax.rsqrt(var + EPS) * gn_ref[:, cols]
        gate_ref[:, cols] = (jax.nn.sigmoid(o_ref[:, cols]) * hn).astype(BF16)
    y_ref[...] = x_ref[...] + jnp.dot(gate_ref[...], wout_ref[...], preferred_element_type=F32)


def _mlstm_out(h, o, x2, gn_g, w_out, *, tm):
    t, d = x2.shape
    d_b = h.shape[1]
    return pl.pallas_call(
        _mlstm_out_kernel,
        grid=(t // tm,),
        in_specs=[pl.BlockSpec((tm, d_b), lambda i: (i, 0)),
                  pl.BlockSpec((tm, d_b), lambda i: (i, 0)),
                  pl.BlockSpec((tm, d), lambda i: (i, 0)),
                  _const_spec(gn_g.shape), _const_spec(w_out.shape)],
        out_specs=pl.BlockSpec((tm, d), lambda i: (i, 0)),
        out_shape=jax.ShapeDtypeStruct((t, d), F32),
        scratch_shapes=[pltpu.VMEM((tm, d_b), BF16)],
        compiler_params=_params(1),
        name="mlstm_out",
    )(h, o, x2, gn_g, w_out)


def _mlstm_fused_kernel(x_ref, g_ref, wall_ref, wif_ref, cw_ref, cb_ref, bif_ref,
                        gn_ref, wout_ref, y_ref, c_ref, n_ref, m_ref, nbuf_ref, *, k_scale):
    @pl.when(pl.program_id(1) == 0)
    def _():
        c_ref[...] = jnp.zeros(c_ref.shape, F32)
        n_ref[...] = jnp.zeros(n_ref.shape, F32)
        m_ref[...] = jnp.zeros(m_ref.shape, F32)
        nbuf_ref[...] = jnp.zeros(nbuf_ref.shape, F32)

    r = x_ref.shape[0]
    dk = c_ref.shape[2]
    dv = c_ref.shape[3]

    x = x_ref[...]
    h = _rmsnorm(x, g_ref[LAYER_B:LAYER_B + 1, :]).astype(BF16)
    gates = jnp.dot(h, wif_ref[...], preferred_element_type=F32) + bif_ref[...]
    r_i = lax.broadcasted_iota(jnp.int32, (r, r), 0)
    c_i = lax.broadcasted_iota(jnp.int32, (r, r), 1)
    mask = c_i <= r_i
    b_all = _select_sum(mask, _log_sigmoid(gates))
    b_t = b_all.T
    ig_t = gates.T

    pair = 2 * dk
    n_pairs = H_B // 2
    qk_w = cw_ref.shape[1]

    def w_cols(base, cols):
        return wall_ref[0, :, base + cols.start:base + cols.stop]

    q_chunks = {}
    k_chunks = {}
    gated = {}

    def qk_chunk(c):
        cols = slice(c * pair, (c + 1) * pair)
        pre = jnp.dot(h, w_cols(0, cols), preferred_element_type=F32)
        return jax.nn.silu(_causal_conv(pre, nbuf_ref, cw_ref, cb_ref, 1, r, cols))

    def head_stages(hd):
        p, half = divmod(hd, 2)
        ig_col = gates[:, hd:hd + 1]
        i_row = ig_t[hd:hd + 1, :]
        b_col = b_all[:, H_B + hd:H_B + hd + 1]
        b_row = b_t[H_B + hd:H_B + hd + 1, :]
        m_prev = m_ref[0, hd:hd + 1, 0:1]
        vcols = slice(hd * dv, (hd + 1) * dv)
        vb = jnp.dot(h, w_cols(qk_w, vcols), preferred_element_type=F32).astype(BF16)
        o_pre = jnp.dot(h, w_cols(qk_w + H_B * dv, vcols), preferred_element_type=F32)
        d = jnp.where(mask, b_col - b_row + i_row, -jnp.inf)
        yield
        inter = b_col + m_prev
        m_t = jnp.maximum(inter, jnp.max(d, axis=1, keepdims=True))
        w_intra = jnp.exp(d - m_t)
        w_inter = jnp.exp(inter - m_t)
        yield
        q = q_chunks[p][:, half * dk:(half + 1) * dk]
        k = k_chunks[p][:, half * dk:(half + 1) * dk] * k_scale
        qb = q.astype(BF16)
        s = lax.dot_general(qb, k.astype(BF16), (((1,), (1,)), ((), ())),
                            preferred_element_type=F32)
        sc = s * w_intra
        intra = jnp.dot(sc.astype(BF16), vb, preferred_element_type=F32)
        den_intra = jnp.sum(sc, axis=1, keepdims=True)
        c_old = c_ref[0, hd]
        n_old = n_ref[0, hd:hd + 1, :]
        qc = jnp.dot(qb, c_old.astype(BF16), preferred_element_type=F32)
        qn = jnp.sum(q * n_old, axis=1, keepdims=True)
        yield
        num = w_inter * qc + intra
        den = w_inter * qn + den_intra
        hout = num / jnp.maximum(jnp.abs(den), jnp.exp(-m_t))
        mu = jnp.mean(hout, axis=-1, keepdims=True)
        hc = hout - mu
        var = jnp.mean(hc * hc, axis=-1, keepdims=True)
        hn = hc * lax.rsqrt(var + EPS) * gn_ref[:, vcols]
        yield
        gated[hd] = (jax.nn.sigmoid(o_pre) * hn).astype(BF16)
        b_last = b_col[r - 1:r, :]
        m_new = m_t[r - 1:r, :]
        g = jnp.exp(b_last - b_col + ig_col - m_new)
        g_row = jnp.exp(b_last - b_row + i_row - m_new)
        decay = jnp.exp(b_last + m_prev - m_new)
        yield
        kg_t = (k.T * g_row).astype(BF16)
        c_ref[0, hd] = decay * c_old + jnp.dot(kg_t, vb, preferred_element_type=F32)
        n_ref[0, hd:hd + 1, :] = decay * n_old + jnp.sum(k * g, axis=0, keepdims=True)
        m_ref[0, hd:hd + 1, :] = jnp.broadcast_to(m_new, (1, LANES))
        yield

    n_stages = 6
    gate_stage = 4
    q_chunks[0] = qk_chunk(0)
    k_chunks[0] = qk_chunk(n_pairs)
    heads = [head_stages(hd) for hd in range(H_B)]
    y = x
    for t in range(H_B + n_stages - 1):
        if t % 2 == 0 and t // 2 + 1 < n_pairs:
            q_chunks[t // 2 + 1] = qk_chunk(t // 2 + 1)
            k_chunks[t // 2 + 1] = qk_chunk(n_pairs + t // 2 + 1)
        for hd in reversed(range(H_B)):
            if hd <= t < hd + n_stages:
                next(heads[hd])
        p, odd = divmod(t - gate_stage, 2)
        if odd and 0 <= p < n_pairs:
            lhs = jnp.concatenate([gated[2 * p], gated[2 * p + 1]], axis=1)
            y = y + jnp.dot(lhs, wout_ref[2 * p * dv:(2 * p + 2) * dv, :],
                            preferred_element_type=F32)
    y_ref[...] = y


def _mlstm_fused(x2, g, w_all, w_if, cw, cb, b_if, gn_g, w_out, *, bsz, seq, ls, k_scale):
    t, d = x2.shape
    qk_w = cw.shape[1]
    d_b = w_out.shape[0]
    dk = qk_w // (2 * H_B)
    dv = d_b // H_B
    nj = seq // ls
    row_spec = pl.BlockSpec((ls, d), lambda i, j: (i * nj + j, 0))

    def state_spec(*tail):
        return pl.BlockSpec((1,) + tail, lambda i, j: (i,) + (0,) * len(tail))

    args = [x2, g, w_all, w_if, cw, cb, b_if, gn_g, w_out]
    return pl.pallas_call(
        functools.partial(_mlstm_fused_kernel, k_scale=k_scale),
        grid=(bsz, nj),
        in_specs=([row_spec, _const_spec(g.shape), _layer_spec(w_all.shape, 0)]
                  + [_const_spec(a.shape) for a in args[3:]]),
        out_specs=[row_spec, state_spec(H_B, dk, dv), state_spec(H_B, dk),
                   state_spec(H_B, LANES), state_spec(CONV_B - 1, qk_w)],
        out_shape=[jax.ShapeDtypeStruct((t, d), F32),
                   jax.ShapeDtypeStruct((bsz, H_B, dk, dv), F32),
                   jax.ShapeDtypeStruct((bsz, H_B, dk), F32),
                   jax.ShapeDtypeStruct((bsz, H_B, LANES), F32),
                   jax.ShapeDtypeStruct((bsz, CONV_B - 1, qk_w), F32)],
        compiler_params=_params(2),
        name="mlstm_fused",
    )(*args)


def _trunks(xp, xs, state, w):
    bp, sp, d = xp.shape
    bs_, ss, _ = xs.shape
    assert sp % PROMPT_ROWS == 0 and PROMPT_ROWS % PROMPT_MLSTM_ROWS == 0
    assert PROMPT_MLSTM_ROWS % CHUNK_A == 0 and PROMPT_MLSTM_ROWS & (PROMPT_MLSTM_ROWS - 1) == 0
    assert ss & (ss - 1) == 0 and ss % SUBLANES == 0 and ss <= CHUNK_A
    assert (bs_ * ss) % SAMPLE_ROWS == 0 and bs_ % STEP_SEQS == 0 and STEP_SEQS * ss <= LANES
    xp2 = xp.reshape(bp * sp, d)
    xs2 = xs.reshape(bs_ * ss, d)
    sample_nb = SAMPLE_ROWS // ss

    ws_s = jnp.tile(w["a_ws"][:, :ss, :ss], (1, CHUNK_A // ss, CHUNK_A // ss))
    bs_s = jnp.tile(w["a_bs_t"][:ss], (CHUNK_A // ss, 1))
    xp2, xs2, v_rows = _mixer_a(
        xp2, xs2, w["norm_mix_g"], w["a_w_in"], w["a_ln_g"], w["a_ln_b"], w["a_ws"],
        w["a_bs_t"], ws_s, bs_s, w["a_w_out"], long_tm=PROMPT_ROWS, short_tm=SAMPLE_ROWS,
        short_chunk=ss)
    ffn_tiles = dict(fresh_bsz=bp, fresh_seq=sp, fresh_ls=PROMPT_ROWS, carried_nb=sample_nb,
                     carried_ls=ss)
    xp2, xs2, fbuf0_p, fbuf0_s = _ffn(
        xp2, xs2, w["norm_ffn_g"], w["f_w_up"], w["f_conv_w"], w["f_conv_b"], w["f_w_down"],
        state["ffn_conv"], None, None, layer=0, **ffn_tiles)

    xp2, c_p, n_p, m_p, mconv_p = _mlstm_fused(
        xp2, w["norm_mix_g"], w["b_w_all"], w["b_w_if"], w["b_conv_w"], w["b_conv_b"],
        w["b_bias_if"], w["b_gn_g"], w["b_w_out"], bsz=bp, seq=sp, ls=PROMPT_MLSTM_ROWS,
        k_scale=w["k_scale"])
    q, k, v, o, ig, lf, mconv_s = _mlstm_proj(
        xs2, w["norm_mix_g"], w["b_w_all"], w["b_w_if"], w["b_conv_w"], w["b_conv_b"],
        w["b_bias_if"], state["mlstm_conv"],
        d_b=w["b_w_out"].shape[0], bsz=bs_, seq=ss, nb=sample_nb, ls=ss, k_scale=w["k_scale"])
    n0 = state["mlstm_n"]
    m0 = jnp.pad(state["mlstm_m"], ((0, 0), (0, LANES - H_B)))
    h, c_s, n_s, m_s = _mlstm_step(q, k, v, ig, lf, state["mlstm_C"], n0, m0, bsz=bs_,
                                   nb=STEP_SEQS, ls=ss)
    xs2 = _mlstm_out(h, o, xs2, w["b_gn_g"], w["b_w_out"], tm=SAMPLE_ROWS)
    yp2, ys2, fconv_p, fconv_s = _ffn(
        xp2, xs2, w["norm_ffn_g"], w["f_w_up"], w["f_conv_w"], w["f_conv_b"], w["f_w_down"],
        state["ffn_conv"], w["final_norm_g"], (fbuf0_p, fbuf0_s), layer=1, **ffn_tiles)
    prompt = dict(y=yp2.reshape(bp, sp, d), C=c_p[None], n=n_p[None], m=m_p[:, :, 0][None],
                  mconv=mconv_p[None], fconv=fconv_p)
    sample = dict(y=ys2.reshape(bs_, ss, d), v=v_rows.reshape(1, bs_, ss, -1), C=c_s[None],
                  n=n_s[None], m=m_s[:, :H_B][None],
                  mconv=mconv_s[None], fconv=fconv_s)
    return prompt, sample


def kernel(x_prompt, x_sample, state_mlstm_C, state_mlstm_n, state_mlstm_m, state_mlstm_conv, state_ffn_conv, norm_mix_g, norm_ffn_g, final_norm_g, a_w_in, a_ln_g, a_ln_b, a_w_s, a_b_s, a_w_out, b_w_in, b_conv_w, b_conv_b, b_bias_i, b_bias_f, b_gn_g, b_w_out, f_w_up, f_conv_w, f_conv_b, f_w_down):
    d_ff = f_w_down.shape[1]
    qk_w = b_conv_w.shape[2]
    d_b = b_w_out.shape[1]
    dk = qk_w // (2 * H_B)
    d_a = a_w_out.shape[1]
    dg = d_a // G_A

    w_in_b = b_w_in[0]
    w = dict(
        norm_mix_g=norm_mix_g, norm_ffn_g=norm_ffn_g, final_norm_g=final_norm_g[None, :],
        a_w_in=a_w_in[0].astype(BF16), a_ln_g=a_ln_g, a_ln_b=a_ln_b,
        a_ws=a_w_s[0].astype(BF16),
        a_bs_t=jnp.repeat(jnp.transpose(a_b_s[0]), dg, axis=1),
        a_w_out=a_w_out[0].astype(BF16),
        b_w_all=b_w_in.astype(BF16),
        b_conv_w=b_conv_w[0], b_conv_b=b_conv_b,
        b_w_if=jnp.pad(w_in_b[:, qk_w + 2 * d_b:], ((0, 0), (0, LANES - 2 * H_B))).astype(BF16),
        b_bias_if=jnp.pad(jnp.concatenate([b_bias_i, b_bias_f], axis=1),
                          ((0, 0), (0, LANES - 2 * H_B))),
        b_gn_g=b_gn_g, b_w_out=b_w_out[0].astype(BF16),
        f_w_up=f_w_up.astype(BF16),
        f_conv_w=f_conv_w, f_conv_b=f_conv_b, f_w_down=f_w_down.astype(BF16),
        k_scale=float(dk) ** -0.5,
    )
    state = dict(mlstm_C=state_mlstm_C[0], mlstm_n=state_mlstm_n[0], mlstm_m=state_mlstm_m[0],
                 mlstm_conv=state_mlstm_conv[0], ffn_conv=state_ffn_conv)
    p, s = _trunks(x_prompt, x_sample, state, w)
    return (p["y"], s["y"], p["C"], p["n"], p["m"], p["mconv"], p["fconv"],
            s["v"], s["C"], s["n"], s["m"], s["mconv"], s["fconv"])
```

```python
import functools

import jax
import jax.numpy as jnp
from jax import lax
from jax.experimental import pallas as pl
from jax.experimental.pallas import tpu as pltpu

F32 = jnp.float32
BF16 = jnp.bfloat16

EPS = 1e-6
LANES = 128
SUBLANES = 8
VMEM_LIMIT_BYTES = 56 * 1024 * 1024

CHUNK_A = 128
G_A = 8
H_B = 8
CONV_B = 4
CONV_F = 3
LAYER_A = 0
LAYER_B = 1

PROMPT_ROWS = 512
PROMPT_MLSTM_ROWS = 256
SAMPLE_ROWS = 256
STEP_SEQS = 8


def _const_spec(shape):
    nd = len(shape)
    return pl.BlockSpec(shape, lambda *_: (0,) * nd, pipeline_mode=pl.Buffered(1))


def _layer_spec(shape, layer):
    return pl.BlockSpec((1,) + tuple(shape[1:]), lambda *_: (layer, 0, 0),
                        pipeline_mode=pl.Buffered(1))


def _params(n_axes):
    return pltpu.CompilerParams(
        dimension_semantics=("arbitrary",) * n_axes,
        vmem_limit_bytes=VMEM_LIMIT_BYTES,
    )


def _software_pipeline(chains, n_stages):
    for t in range(len(chains) + n_stages - 1):
        for i in reversed(range(len(chains))):
            if i <= t < i + n_stages:
                next(chains[i])


def _rmsnorm(x, g):
    ms = jnp.mean(x * x, axis=-1, keepdims=True)
    return x * lax.rsqrt(ms + EPS) * g


def _expand3(c3, nb, ls):
    w = c3.shape[-1]
    if nb == 1:
        return jnp.broadcast_to(c3.reshape(1, w), (ls, w))
    return jnp.broadcast_to(c3, (nb, ls, w)).reshape(nb * ls, w)


def _expand2(c2, nb, ls):
    return _expand3(c2[:, None, :], nb, ls)


def _causal_conv(a, carry_ref, cw_ref, cb_ref, nb, ls, cols=slice(None)):
    r, c = a.shape
    kw = cw_ref.shape[0]
    rolled = [a] + [pltpu.roll(a, k, 0) for k in range(1, kw)]

    def taps(shifted):
        y = cb_ref[:, cols] + cw_ref[kw - 1:kw, cols] * shifted[0]
        for k in range(1, kw):
            y = y + cw_ref[kw - 1 - k:kw - k, cols] * shifted[k]
        return y

    grp = SUBLANES if nb == 1 else r
    grp_ls = SUBLANES if nb == 1 else ls
    tpos = lax.broadcasted_iota(jnp.int32, (grp, c), 0) & (grp_ls - 1)
    fixed = [a[0:grp]]
    for k in range(1, kw):
        sh = rolled[k][0:grp]
        for t in range(k):
            idx = kw - 1 + t - k
            prev = _expand3(carry_ref[:, idx:idx + 1, cols], nb, grp_ls)
            sh = jnp.where(tpos == t, prev, sh)
        fixed.append(sh)
    y = taps(fixed)
    if nb == 1:
        y = jnp.concatenate([y, taps([s[grp:] for s in rolled])], axis=0)
    last = rolled[kw - 1]
    if nb == 1:
        carry_ref[0, :, cols] = last[0:kw - 1, :]
    else:
        heads = last.reshape(nb, ls, c)[:, 0:kw - 1, :]
        carry_ref[0:nb - 1, :, cols] = heads[1:nb]
        carry_ref[nb - 1:nb, :, cols] = heads[0:1]
    return y


def _mixer_a_kernel(xl_ref, xs_ref, g_ref, win_ref, lng_ref, lnb_ref, wsl_ref, bsl_ref, wss_ref,
                    bss_ref, wout_ref, yl_ref, ys_ref, v_ref, *, long_steps, short_chunk):
    s = pl.program_id(0)
    shared = (g_ref.at[pl.ds(LAYER_A, 1)], win_ref, lng_ref, lnb_ref, wout_ref)

    @pl.when(s < long_steps)
    def _():
        _mixer_a_tile(xl_ref, yl_ref, None, wsl_ref, bsl_ref, *shared, CHUNK_A)

    @pl.when(s >= long_steps)
    def _():
        _mixer_a_tile(xs_ref, ys_ref, v_ref, wss_ref, bss_ref, *shared, short_chunk)


def _mixer_a_tile(x_ref, y_ref, v_ref, ws_ref, bs_ref, g_ref, win_ref, lng_ref, lnb_ref,
                  wout_ref, seq_chunk):
    emit_v = v_ref is not None
    tm = x_ref.shape[0]
    d_a = lng_ref.shape[1]
    dg = d_a // G_A
    x = x_ref[...]
    h = _rmsnorm(x, g_ref[...]).astype(BF16)
    v_pre = {}
    v_chunks = []
    ahead = 1
    for g in range(G_A + ahead):
        if g < G_A:
            v_pre[g] = jnp.dot(h, win_ref[:, d_a + g * dg:d_a + (g + 1) * dg],
                               preferred_element_type=F32)
        if g >= ahead:
            v_chunks.append(jax.nn.gelu(v_pre.pop(g - ahead)))
    t_i = lax.broadcasted_iota(jnp.int32, (CHUNK_A, CHUNK_A), 0)
    s_i = lax.broadcasted_iota(jnp.int32, (CHUNK_A, CHUNK_A), 1)
    mask = (s_i <= t_i) & ((t_i // seq_chunk) == (s_i // seq_chunk))
    stats = {}
    gated = {}

    def group_stages(g):
        cols = slice(g * dg, (g + 1) * dg)
        u_pre = jnp.dot(h, win_ref[:, cols], preferred_element_type=F32)
        yield
        vn = ((v_chunks[g] - stats["mu"]) * stats["rs"]) * lng_ref[:, cols] + lnb_ref[:, cols]
        if emit_v:
            v_ref[:, cols] = vn
        vb = vn.astype(BF16)
        w = jnp.where(mask, ws_ref[g], jnp.zeros((), BF16))
        mixed = [jnp.dot(w, vb[c * CHUNK_A:(c + 1) * CHUNK_A], preferred_element_type=F32)
                 + bs_ref[:, cols] for c in range(tm // CHUNK_A)]
        yield
        u = jax.nn.gelu(u_pre)
        gated[g] = jnp.concatenate(
            [(u[c * CHUNK_A:(c + 1) * CHUNK_A] * mixed[c]).astype(BF16)
             for c in range(tm // CHUNK_A)], axis=0)
        yield

    lead = 3
    times = (0, lead, lead + 1)
    groups = [group_stages(g) for g in range(G_A)]
    y = x
    for t in range(G_A + times[-1]):
        if t == lead:
            mu = sum(jnp.sum(v, axis=-1, keepdims=True) for v in v_chunks) / d_a
            var = sum(jnp.sum((v - mu) * (v - mu), axis=-1, keepdims=True)
                      for v in v_chunks) / d_a
            stats["mu"] = mu
            stats["rs"] = lax.rsqrt(var + EPS)
        for g in reversed(range(G_A)):
            if t - g in times:
                next(groups[g])
        p, odd = divmod(t - times[-1], 2)
        if odd and 0 <= p < G_A // 2:
            lhs = jnp.concatenate([gated[2 * p], gated[2 * p + 1]], axis=1)
            y = y + jnp.dot(lhs, wout_ref[2 * p * dg:(2 * p + 2) * dg, :],
                            preferred_element_type=F32)
    y_ref[...] = y


def _mixer_a(xl2, xs2, g, w_in, ln_g, ln_b, ws_l, bs_l, ws_s, bs_s, w_out, *, long_tm, short_tm,
             short_chunk):
    tl, d = xl2.shape
    ts = xs2.shape[0]
    d_a = ln_g.shape[1]
    n_l = tl // long_tm

    def l_rows(width):
        return pl.BlockSpec((long_tm, width), lambda s: (jnp.minimum(s, n_l - 1), 0))

    def s_rows(width):
        return pl.BlockSpec((short_tm, width), lambda s: (jnp.maximum(s - n_l, 0), 0))

    consts = [g, w_in, ln_g, ln_b, ws_l, bs_l, ws_s, bs_s, w_out]
    return pl.pallas_call(
        functools.partial(_mixer_a_kernel, long_steps=n_l, short_chunk=short_chunk),
        grid=(n_l + ts // short_tm,),
        in_specs=[l_rows(d), s_rows(d)] + [_const_spec(a.shape) for a in consts],
        out_specs=[l_rows(d), s_rows(d), s_rows(d_a)],
        out_shape=[jax.ShapeDtypeStruct((tl, d), F32),
                   jax.ShapeDtypeStruct((ts, d), F32),
                   jax.ShapeDtypeStruct((ts, d_a), F32)],
        compiler_params=_params(1),
        name="mixer_a",
    )(xl2, xs2, *consts)


def _ffn_tile(x_ref, y_ref, nbuf_ref, g_ref, wup_ref, cw_ref, cb_ref, wd_ref, fg_ref, nb, ls):
    x = x_ref[...]
    h = _rmsnorm(x, g_ref[...]).astype(BF16)
    d_ff = wd_ref.shape[1]
    a = jnp.dot(h, wup_ref[0, :, :d_ff], preferred_element_type=F32)
    gv = jnp.dot(h, wup_ref[0, :, d_ff:], preferred_element_type=F32)
    a_c = _causal_conv(a, nbuf_ref, cw_ref, cb_ref, nb, ls)
    act = (jax.nn.gelu(a_c) * gv).astype(BF16)
    y = x + jnp.dot(act, wd_ref[0], preferred_element_type=F32)
    if fg_ref is not None:
        y = _rmsnorm(y, fg_ref[...])
    y_ref[...] = y


def _ffn_kernel(*refs, layer, fresh_steps, fresh_nj, fresh_ls, carried_nb, carried_ls,
                final_norm, stack_prev):
    refs = list(refs)
    xf_ref, xc_ref, g_ref, wup_ref, cw_ref, cb_ref, wd_ref, buf_ref = refs[:8]
    pos = 8
    if stack_prev:
        prevf_ref, prevc_ref = refs[pos:pos + 2]
        pos += 2
    fg_ref = None
    if final_norm:
        fg_ref = refs[pos]
        pos += 1
    yf_ref, yc_ref, nbuff_ref, nbufc_ref = refs[pos:pos + 4]
    carry_f = nbuff_ref.at[1] if stack_prev else nbuff_ref
    carry_c = nbufc_ref.at[1] if stack_prev else nbufc_ref
    s = pl.program_id(0)
    shared = (g_ref.at[pl.ds(layer, 1)], wup_ref, cw_ref.at[layer], cb_ref.at[pl.ds(layer, 1)],
              wd_ref, fg_ref)

    @pl.when(s < fresh_steps)
    def _():
        @pl.when(s % fresh_nj == 0)
        def _():
            carry_f[...] = jnp.zeros(carry_f.shape, F32)
            if stack_prev:
                nbuff_ref[0] = prevf_ref[...]

        _ffn_tile(xf_ref, yf_ref, carry_f, *shared, 1, fresh_ls)

    @pl.when(s >= fresh_steps)
    def _():
        carry_c[...] = buf_ref[0]
        if stack_prev:
            nbufc_ref[0] = prevc_ref[...]
        _ffn_tile(xc_ref, yc_ref, carry_c, *shared, carried_nb, carried_ls)


def _ffn(xf2, xc2, g, w_up, cw, cb, w_d, buf, final_g, prev, *, layer, fresh_bsz, fresh_seq,
         fresh_ls, carried_nb, carried_ls):
    tf, d = xf2.shape
    tc = xc2.shape[0]
    d_ff = w_d.shape[1]
    nj = fresh_seq // fresh_ls
    fs = fresh_bsz * nj
    cs = tc // (carried_nb * carried_ls)
    final_norm = final_g is not None
    f_rows = pl.BlockSpec((fresh_ls, d), lambda s: (jnp.minimum(s, fs - 1), 0))
    c_rows = pl.BlockSpec((carried_nb * carried_ls, d), lambda s: (jnp.maximum(s - fs, 0), 0))
    f_buf = pl.BlockSpec((1, CONV_F - 1, d_ff), lambda s: (jnp.minimum(s, fs - 1) // nj, 0, 0))
    c_buf = pl.BlockSpec((carried_nb, CONV_F - 1, d_ff), lambda s: (jnp.maximum(s - fs, 0), 0, 0))
    c_buf_in = pl.BlockSpec((1, carried_nb, CONV_F - 1, d_ff),
                            lambda s: (layer, jnp.maximum(s - fs, 0), 0, 0))
    args = [xf2, xc2, g, w_up, cw, cb, w_d, buf]
    in_specs = [f_rows, c_rows, _const_spec(g.shape), _layer_spec(w_up.shape, layer),
                _const_spec(cw.shape), _const_spec(cb.shape), _layer_spec(w_d.shape, layer),
                c_buf_in]
    out_bufs = [f_buf, c_buf]
    buf_shapes = [(fresh_bsz, CONV_F - 1, d_ff), (tc // carried_ls, CONV_F - 1, d_ff)]
    if prev is not None:
        args += list(prev)
        in_specs += [f_buf, c_buf]
        out_bufs = [pl.BlockSpec((2,) + spec.block_shape,
                                 lambda s, m=spec.index_map: (0,) + tuple(m(s))) for spec in out_bufs]
        buf_shapes = [(2,) + shp for shp in buf_shapes]
    if final_norm:
        args.append(final_g)
        in_specs.append(_const_spec(final_g.shape))
    return pl.pallas_call(
        functools.partial(_ffn_kernel, layer=layer, stack_prev=prev is not None, fresh_steps=fs, fresh_nj=nj, fresh_ls=fresh_ls,
                          carried_nb=carried_nb, carried_ls=carried_ls, final_norm=final_norm),
        grid=(fs + cs,),
        in_specs=in_specs,
        out_specs=[f_rows, c_rows] + out_bufs,
        out_shape=[jax.ShapeDtypeStruct((tf, d), F32), jax.ShapeDtypeStruct((tc, d), F32)]
        + [jax.ShapeDtypeStruct(shp, F32) for shp in buf_shapes],
        compiler_params=_params(1),
        name="conv_ffn",
    )(*args)


def _select_sum(sel, x):
    return jnp.dot(sel.astype(F32), x, precision=lax.Precision.HIGHEST,
                   preferred_element_type=F32)


def _log_sigmoid(x):
    return jnp.minimum(x, 0.0) - jnp.log1p(jnp.exp(-jnp.abs(x)))


def _mlstm_proj_kernel(x_ref, g_ref, wall_ref, wif_ref, cw_ref, cb_ref, bif_ref, buf_ref,
                       q_ref, k_ref, v_ref, o_ref, ig_ref, lf_ref, nbuf_ref, *, nb, ls, k_scale):
    nbuf_ref[...] = buf_ref[...]
    h = _rmsnorm(x_ref[...], g_ref[LAYER_B:LAYER_B + 1, :]).astype(BF16)
    qk_w = cw_ref.shape[1]
    d_b = v_ref.shape[1]
    qk_pre = jnp.dot(h, wall_ref[0, :, 0:qk_w], preferred_element_type=F32)
    v_ref[...] = jnp.dot(h, wall_ref[0, :, qk_w:qk_w + d_b], preferred_element_type=F32)
    o_ref[...] = jnp.dot(h, wall_ref[0, :, qk_w + d_b:qk_w + 2 * d_b],
                         preferred_element_type=F32)
    gates = jnp.dot(h, wif_ref[...], preferred_element_type=F32) + bif_ref[...]
    ig_ref[...] = gates
    lf_ref[...] = pltpu.roll(_log_sigmoid(gates), LANES - H_B, 1)
    qk = jax.nn.silu(_causal_conv(qk_pre, nbuf_ref, cw_ref, cb_ref, nb, ls))
    dq = q_ref.shape[1]
    q_ref[...] = qk[:, :dq]
    k_ref[...] = qk[:, dq:] * k_scale


def _mlstm_proj(x2, g, w_all, w_if, cw, cb, b_if, buf, *, d_b, bsz, seq, nb, ls, k_scale):
    t, d = x2.shape
    qk_w = cw.shape[1]
    assert seq == ls

    def rows(width):
        return pl.BlockSpec((nb * ls, width), lambda i: (i, 0))

    buf_spec = pl.BlockSpec((nb, CONV_B - 1, qk_w), lambda i: (i, 0, 0))
    consts = [w_if, cw, cb, b_if]
    return pl.pallas_call(
        functools.partial(_mlstm_proj_kernel, nb=nb, ls=ls, k_scale=k_scale),
        grid=(bsz // nb,),
        in_specs=([rows(d), _const_spec(g.shape), _layer_spec(w_all.shape, 0)]
                  + [_const_spec(a.shape) for a in consts] + [buf_spec]),
        out_specs=[rows(qk_w // 2), rows(qk_w // 2), rows(d_b), rows(d_b), rows(LANES),
                   rows(LANES), buf_spec],
        out_shape=[jax.ShapeDtypeStruct((t, qk_w // 2), F32),
                   jax.ShapeDtypeStruct((t, qk_w // 2), F32),
                   jax.ShapeDtypeStruct((t, d_b), F32),
                   jax.ShapeDtypeStruct((t, d_b), F32),
                   jax.ShapeDtypeStruct((t, LANES), F32),
                   jax.ShapeDtypeStruct((t, LANES), F32),
                   jax.ShapeDtypeStruct((bsz, CONV_B - 1, qk_w), F32)],
        compiler_params=_params(1),
        name="mlstm_proj",
    )(x2, g, w_all, *consts, buf)


def _mlstm_step_kernel(q_ref, k_ref, v_ref, ig_ref, lf_ref, c0_ref, n0_ref, m0_ref,
                       h_ref, c_ref, n_ref, m_ref, *, nb, ls):
    r = nb * ls
    rc = LANES
    dk = c0_ref.shape[2]
    dv = c0_ref.shape[3]

    def pad_rows(a):
        if r == rc:
            return a
        return jnp.concatenate([a, jnp.zeros((rc - r, a.shape[1]), a.dtype)], axis=0)

    r_i = lax.broadcasted_iota(jnp.int32, (r, rc), 0)
    c_i = lax.broadcasted_iota(jnp.int32, (r, rc), 1)
    mask = (c_i <= r_i) & ((r_i // ls) == (c_i // ls))
    ig = ig_ref[...]
    b_all = _select_sum(mask, pad_rows(lf_ref[...]))
    b_t = pad_rows(b_all).T
    ig_t = pad_rows(ig).T
    m0 = m0_ref[...]
    inter_all = b_all + _expand2(m0, nb, ls)
    lane = lax.broadcasted_iota(jnp.int32, (r, LANES), 1)
    m_t_of = {}

    def output_stages(hd):
        kcols = slice(hd * dk, (hd + 1) * dk)
        vcols = slice(hd * dv, (hd + 1) * dv)
        q = q_ref[:, kcols]
        qb = q.astype(BF16)
        qc = jnp.concatenate(
            [jnp.dot(qb[b * ls:(b + 1) * ls], c0_ref[b, hd].astype(BF16),
                     preferred_element_type=F32) for b in range(nb)], axis=0)
        d = jnp.where(mask, b_all[:, hd:hd + 1] - b_t[hd:hd + 1, :] + ig_t[hd:hd + 1, :],
                      -jnp.inf)
        inter = inter_all[:, hd:hd + 1]
        m_t = jnp.maximum(inter, jnp.max(d, axis=1, keepdims=True))
        m_t_of[hd] = m_t
        w_intra = jnp.exp(d - m_t)
        w_inter = jnp.exp(inter - m_t)
        yield
        kb = pad_rows(k_ref[:, kcols].astype(BF16))
        vb = pad_rows(v_ref[:, vcols].astype(BF16))
        s = lax.dot_general(qb, kb, (((1,), (1,)), ((), ())), preferred_element_type=F32)
        sc = s * w_intra
        intra = jnp.dot(sc.astype(BF16), vb, preferred_element_type=F32)
        den_intra = jnp.sum(sc, axis=1, keepdims=True)
        qn = jnp.sum(q * _expand2(n0_ref[:, hd, :], nb, ls), axis=1, keepdims=True)
        yield
        num = w_inter * qc + intra
        den = w_inter * qn + den_intra
        h_ref[:, vcols] = num / jnp.maximum(jnp.abs(den), jnp.exp(-m_t))
        yield

    _software_pipeline([output_stages(hd) for hd in range(H_B)], 3)
    mt_all = jnp.zeros((r, LANES), F32)
    for hd in range(H_B):
        mt_all = jnp.where(lane == hd, m_t_of[hd], mt_all)

    p_r = lax.broadcasted_iota(jnp.int32, (nb, rc), 0)
    p_c = lax.broadcasted_iota(jnp.int32, (nb, rc), 1)
    pick = p_c == p_r * ls + (ls - 1)
    bl_seq = _select_sum(pick, pad_rows(b_all))
    mn_seq = _select_sum(pick, pad_rows(mt_all))
    decay = jnp.exp(bl_seq + m0 - mn_seq)
    m_ref[...] = mn_seq
    g_all = jnp.exp(_expand2(bl_seq, nb, ls) - b_all + ig - _expand2(mn_seq, nb, ls))
    col_seq = lax.broadcasted_iota(jnp.int32, (dk, rc), 1) // ls

    def state_stages(hd):
        kg = k_ref[:, hd * dk:(hd + 1) * dk] * g_all[:, hd:hd + 1]
        n_ref[:, hd, :] = (decay[:, hd:hd + 1] * n0_ref[:, hd, :]
                           + jnp.sum(kg.reshape(nb, ls, dk), axis=1))
        kg_t = pad_rows(kg).T
        vb = pad_rows(v_ref[:, hd * dv:(hd + 1) * dv].astype(BF16))
        yield
        for b in range(nb):
            lhs = jnp.where(col_seq == b, kg_t, 0.0).astype(BF16)
            c_ref[b, hd] = decay[b:b + 1, hd:hd + 1] * c0_ref[b, hd] + jnp.dot(
                lhs, vb, preferred_element_type=F32)
        yield

    _software_pipeline([state_stages(hd) for hd in range(H_B)], 2)


def _mlstm_step(q, k, v, ig, lf, c0, n0, m0, *, bsz, nb, ls):
    t = q.shape[0]
    dk = q.shape[1] // H_B
    dv = v.shape[1] // H_B

    def rows(width):
        return pl.BlockSpec((nb * ls, width), lambda i: (i, 0))

    c_spec = pl.BlockSpec((nb, H_B, dk, dv), lambda i: (i, 0, 0, 0))
    c_in_spec = pl.BlockSpec((nb, H_B, dk, dv), lambda i: (i, 0, 0, 0),
                             pipeline_mode=pl.Buffered(3))
    n_spec = pl.BlockSpec((nb, H_B, dk), lambda i: (i, 0, 0))
    m_spec = pl.BlockSpec((nb, LANES), lambda i: (i, 0))
    n_in = 8

    def outer(*refs):
        pltpu.emit_pipeline(
            functools.partial(_mlstm_step_kernel, nb=nb, ls=ls),
            grid=(bsz // nb,),
            in_specs=[rows(H_B * dk), rows(H_B * dk), rows(H_B * dv), rows(LANES), rows(LANES),
                      c_in_spec, n_spec, m_spec],
            out_specs=[rows(H_B * dv), c_spec, n_spec, m_spec],
        )(*refs[:n_in], *refs[n_in:])

    any_spec = pl.BlockSpec(memory_space=pl.ANY)
    return pl.pallas_call(
        outer,
        in_specs=[any_spec] * n_in,
        out_specs=[any_spec] * 4,
        out_shape=[jax.ShapeDtypeStruct((t, H_B * dv), F32),
                   jax.ShapeDtypeStruct((bsz, H_B, dk, dv), F32),
                   jax.ShapeDtypeStruct((bsz, H_B, dk), F32),
                   jax.ShapeDtypeStruct((bsz, LANES), F32)],
        compiler_params=pltpu.CompilerParams(vmem_limit_bytes=VMEM_LIMIT_BYTES),
        name="mlstm_step",
    )(q, k, v, ig, lf, c0, n0, m0)


def _mlstm_out_kernel(h_ref, o_ref, x_ref, gn_ref, wout_ref, y_ref, gate_ref):
    dv = h_ref.shape[1] // H_B
    for hd in range(H_B):
        cols = slice(hd * dv, (hd + 1) * dv)
        hh = h_ref[:, cols]
        mu = jnp.mean(hh, axis=-1, keepdims=True)
        hc = hh - mu
        var = jnp.mean(hc * hc, axis=-1, keepdims=True)
        hn = hc * lax.rsqrt(var + EPS) * gn_ref[:, cols]
        gate_ref[:, cols] = (jax.nn.sigmoid(o_ref[:, cols]) * hn).astype(BF16)
    y_ref[...] = x_ref[...] + jnp.dot(gate_ref[...], wout_ref[...], preferred_element_type=F32)


def _mlstm_out(h, o, x2, gn_g, w_out, *, tm):
    t, d = x2.shape
    d_b = h.shape[1]
    return pl.pallas_call(
        _mlstm_out_kernel,
        grid=(t // tm,),
        in_specs=[pl.BlockSpec((tm, d_b), lambda i: (i, 0)),
                  pl.BlockSpec((tm, d_b), lambda i: (i, 0)),
                  pl.BlockSpec((tm, d), lambda i: (i, 0)),
                  _const_spec(gn_g.shape), _const_spec(w_out.shape)],
        out_specs=pl.BlockSpec((tm, d), lambda i: (i, 0)),
        out_shape=jax.ShapeDtypeStruct((t, d), F32),
        scratch_shapes=[pltpu.VMEM((tm, d_b), BF16)],
        compiler_params=_params(1),
        name="mlstm_out",
    )(h, o, x2, gn_g, w_out)


def _mlstm_fused_kernel(x_ref, g_ref, wall_ref, wif_ref, cw_ref, cb_ref, bif_ref,
                        gn_ref, wout_ref, y_ref, c_ref, n_ref, m_ref, nbuf_ref, *, k_scale):
    @pl.when(pl.program_id(1) == 0)
    def _():
        c_ref[...] = jnp.zeros(c_ref.shape, F32)
        n_ref[...] = jnp.zeros(n_ref.shape, F32)
        m_ref[...] = jnp.zeros(m_ref.shape, F32)
        nbuf_ref[...] = jnp.zeros(nbuf_ref.shape, F32)

    r = x_ref.shape[0]
    dk = c_ref.shape[2]
    dv = c_ref.shape[3]

    x = x_ref[...]
    h = _rmsnorm(x, g_ref[LAYER_B:LAYER_B + 1, :]).astype(BF16)
    gates = jnp.dot(h, wif_ref[...], preferred_element_type=F32) + bif_ref[...]
    r_i = lax.broadcasted_iota(jnp.int32, (r, r), 0)
    c_i = lax.broadcasted_iota(jnp.int32, (r, r), 1)
    mask = c_i <= r_i
    b_all = _select_sum(mask, _log_sigmoid(gates))
    b_t = b_all.T
    ig_t = gates.T

    pair = 2 * dk
    n_pairs = H_B // 2
    qk_w = cw_ref.shape[1]

    def w_cols(base, cols):
        return wall_ref[0, :, base + cols.start:base + cols.stop]

    q_chunks = {}
    k_chunks = {}
    gated = {}

    def qk_chunk(c):
        cols = slice(c * pair, (c + 1) * pair)
        pre = jnp.dot(h, w_cols(0, cols), preferred_element_type=F32)
        return jax.nn.silu(_causal_conv(pre, nbuf_ref, cw_ref, cb_ref, 1, r, cols))

    def head_stages(hd):
        p, half = divmod(hd, 2)
        ig_col = gates[:, hd:hd + 1]
        i_row = ig_t[hd:hd + 1, :]
        b_col = b_all[:, H_B + hd:H_B + hd + 1]
        b_row = b_t[H_B + hd:H_B + hd + 1, :]
        m_prev = m_ref[0, hd:hd + 1, 0:1]
        vcols = slice(hd * dv, (hd + 1) * dv)
        vb = jnp.dot(h, w_cols(qk_w, vcols), preferred_element_type=F32).astype(BF16)
        o_pre = jnp.dot(h, w_cols(qk_w + H_B * dv, vcols), preferred_element_type=F32)
        d = jnp.where(mask, b_col - b_row + i_row, -jnp.inf)
        yield
        inter = b_col + m_prev
        m_t = jnp.maximum(inter, jnp.max(d, axis=1, keepdims=True))
        w_intra = jnp.exp(d - m_t)
        w_inter = jnp.exp(inter - m_t)
        yield
        q = q_chunks[p][:, half * dk:(half + 1) * dk]
        k = k_chunks[p][:, half * dk:(half + 1) * dk] * k_scale
        qb = q.astype(BF16)
        s = lax.dot_general(qb, k.astype(BF16), (((1,), (1,)), ((), ())),
                            preferred_element_type=F32)
        sc = s * w_intra
        intra = jnp.dot(sc.astype(BF16), vb, preferred_element_type=F32)
        den_intra = jnp.sum(sc, axis=1, keepdims=True)
        c_old = c_ref[0, hd]
        n_old = n_ref[0, hd:hd + 1, :]
        qc = jnp.dot(qb, c_old.astype(BF16), preferred_element_type=F32)
        qn = jnp.sum(q * n_old, axis=1, keepdims=True)
        yield
        num = w_inter * qc + intra
        den = w_inter * qn + den_intra
        hout = num / jnp.maximum(jnp.abs(den), jnp.exp(-m_t))
        mu = jnp.mean(hout, axis=-1, keepdims=True)
        hc = hout - mu
        var = jnp.mean(hc * hc, axis=-1, keepdims=True)
        hn = hc * lax.rsqrt(var + EPS) * gn_ref[:, vcols]
        yield
        gated[hd] = (jax.nn.sigmoid(o_pre) * hn).astype(BF16)
        b_last = b_col[r - 1:r, :]
        m_new = m_t[r - 1:r, :]
        g = jnp.exp(b_last - b_col + ig_col - m_new)
        g_row = jnp.exp(b_last - b_row + i_row - m_new)
        decay = jnp.exp(b_last + m_prev - m_new)
        yield
        kg_t = (k.T * g_row).astype(BF16)
        c_ref[0, hd] = decay * c_old + jnp.dot(kg_t, vb, preferred_element_type=F32)
        n_ref[0, hd:hd + 1, :] = decay * n_old + jnp.sum(k * g, axis=0, keepdims=True)
        m_ref[0, hd:hd + 1, :] = jnp.broadcast_to(m_new, (1, LANES))
        yield

    n_stages = 6
    gate_stage = 4
    q_chunks[0] = qk_chunk(0)
    k_chunks[0] = qk_chunk(n_pairs)
    heads = [head_stages(hd) for hd in range(H_B)]
    y = x
    for t in range(H_B + n_stages - 1):
        if t % 2 == 0 and t // 2 + 1 < n_pairs:
            q_chunks[t // 2 + 1] = qk_chunk(t // 2 + 1)
            k_chunks[t // 2 + 1] = qk_chunk(n_pairs + t // 2 + 1)
        for hd in reversed(range(H_B)):
            if hd <= t < hd + n_stages:
                next(heads[hd])
        p, odd = divmod(t - gate_stage, 2)
        if odd and 0 <= p < n_pairs:
            lhs = jnp.concatenate([gated[2 * p], gated[2 * p + 1]], axis=1)
            y = y + jnp.dot(lhs, wout_ref[2 * p * dv:(2 * p + 2) * dv, :],
                            preferred_element_type=F32)
    y_ref[...] = y


def _mlstm_fused(x2, g, w_all, w_if, cw, cb, b_if, gn_g, w_out, *, bsz, seq, ls, k_scale):
    t, d = x2.shape
    qk_w = cw.shape[1]
    d_b = w_out.shape[0]
    dk = qk_w // (2 * H_B)
    dv = d_b // H_B
    nj = seq // ls
    row_spec = pl.BlockSpec((ls, d), lambda i, j: (i * nj + j, 0))

    def state_spec(*tail):
        return pl.BlockSpec((1,) + tail, lambda i, j: (i,) + (0,) * len(tail))

    args = [x2, g, w_all, w_if, cw, cb, b_if, gn_g, w_out]
    return pl.pallas_call(
        functools.partial(_mlstm_fused_kernel, k_scale=k_scale),
        grid=(bsz, nj),
        in_specs=([row_spec, _const_spec(g.shape), _layer_spec(w_all.shape, 0)]
                  + [_const_spec(a.shape) for a in args[3:]]),
        out_specs=[row_spec, state_spec(H_B, dk, dv), state_spec(H_B, dk),
                   state_spec(H_B, LANES), state_spec(CONV_B - 1, qk_w)],
        out_shape=[jax.ShapeDtypeStruct((t, d), F32),
                   jax.ShapeDtypeStruct((bsz, H_B, dk, dv), F32),
                   jax.ShapeDtypeStruct((bsz, H_B, dk), F32),
                   jax.ShapeDtypeStruct((bsz, H_B, LANES), F32),
                   jax.ShapeDtypeStruct((bsz, CONV_B - 1, qk_w), F32)],
        compiler_params=_params(2),
        name="mlstm_fused",
    )(*args)


def _trunks(xp, xs, state, w):
    bp, sp, d = xp.shape
    bs_, ss, _ = xs.shape
    assert sp % PROMPT_ROWS == 0 and PROMPT_ROWS % PROMPT_MLSTM_ROWS == 0
    assert PROMPT_MLSTM_ROWS % CHUNK_A == 0 and PROMPT_MLSTM_ROWS & (PROMPT_MLSTM_ROWS - 1) == 0
    assert ss & (ss - 1) == 0 and ss % SUBLANES == 0 and ss <= CHUNK_A
    assert (bs_ * ss) % SAMPLE_ROWS == 0 and bs_ % STEP_SEQS == 0 and STEP_SEQS * ss <= LANES
    xp2 = xp.reshape(bp * sp, d)
    xs2 = xs.reshape(bs_ * ss, d)
    sample_nb = SAMPLE_ROWS // ss

    ws_s = jnp.tile(w["a_ws"][:, :ss, :ss], (1, CHUNK_A // ss, CHUNK_A // ss))
    bs_s = jnp.tile(w["a_bs_t"][:ss], (CHUNK_A // ss, 1))
    xp2, xs2, v_rows = _mixer_a(
        xp2, xs2, w["norm_mix_g"], w["a_w_in"], w["a_ln_g"], w["a_ln_b"], w["a_ws"],
        w["a_bs_t"], ws_s, bs_s, w["a_w_out"], long_tm=PROMPT_ROWS, short_tm=SAMPLE_ROWS,
        short_chunk=ss)
    ffn_tiles = dict(fresh_bsz=bp, fresh_seq=sp, fresh_ls=PROMPT_ROWS, carried_nb=sample_nb,
                     carried_ls=ss)
    xp2, xs2, fbuf0_p, fbuf0_s = _ffn(
        xp2, xs2, w["norm_ffn_g"], w["f_w_up"], w["f_conv_w"], w["f_conv_b"], w["f_w_down"],
        state["ffn_conv"], None, None, layer=0, **ffn_tiles)

    xp2, c_p, n_p, m_p, mconv_p = _mlstm_fused(
        xp2, w["norm_mix_g"], w["b_w_all"], w["b_w_if"], w["b_conv_w"], w["b_conv_b"],
        w["b_bias_if"], w["b_gn_g"], w["b_w_out"], bsz=bp, seq=sp, ls=PROMPT_MLSTM_ROWS,
        k_scale=w["k_scale"])
    q, k, v, o, ig, lf, mconv_s = _mlstm_proj(
        xs2, w["norm_mix_g"], w["b_w_all"], w["b_w_if"], w["b_conv_w"], w["b_conv_b"],
        w["b_bias_if"], state["mlstm_conv"],
        d_b=w["b_w_out"].shape[0], bsz=bs_, seq=ss, nb=sample_nb, ls=ss, k_scale=w["k_scale"])
    n0 = state["mlstm_n"]
    m0 = jnp.pad(state["mlstm_m"], ((0, 0), (0, LANES - H_B)))
    h, c_s, n_s, m_s = _mlstm_step(q, k, v, ig, lf, state["mlstm_C"], n0, m0, bsz=bs_,
                                   nb=STEP_SEQS, ls=ss)
    xs2 = _mlstm_out(h, o, xs2, w["b_gn_g"], w["b_w_out"], tm=SAMPLE_ROWS)
    yp2, ys2, fconv_p, fconv_s = _ffn(
        xp2, xs2, w["norm_ffn_g"], w["f_w_up"], w["f_conv_w"], w["f_conv_b"], w["f_w_down"],
        state["ffn_conv"], w["final_norm_g"], (fbuf0_p, fbuf0_s), layer=1, **ffn_tiles)
    prompt = dict(y=yp2.reshape(bp, sp, d), C=c_p[None], n=n_p[None], m=m_p[:, :, 0][None],
                  mconv=mconv_p[None], fconv=fconv_p)
    sample = dict(y=ys2.reshape(bs_, ss, d), v=v_rows.reshape(1, bs_, ss, -1), C=c_s[None],
                  n=n_s[None], m=m_s[:, :H_B][None],
                  mconv=mconv_s[None], fconv=fconv_s)
    return prompt, sample


def kernel(x_prompt, x_sample, state_mlstm_C, state_mlstm_n, state_mlstm_m, state_mlstm_conv, state_ffn_conv, norm_mix_g, norm_ffn_g, final_norm_g, a_w_in, a_ln_g, a_ln_b, a_w_s, a_b_s, a_w_out, b_w_in, b_conv_w, b_conv_b, b_bias_i, b_bias_f, b_gn_g, b_w_out, f_w_up, f_conv_w, f_conv_b, f_w_down):
    d_ff = f_w_down.shape[1]
    qk_w = b_conv_w.shape[2]
    d_b = b_w_out.shape[1]
    dk = qk_w // (2 * H_B)
    d_a = a_w_out.shape[1]
    dg = d_a // G_A

    w_in_b = b_w_in[0]
    w = dict(
        norm_mix_g=norm_mix_g, norm_ffn_g=norm_ffn_g, final_norm_g=final_norm_g[None, :],
        a_w_in=a_w_in[0].astype(BF16), a_ln_g=a_ln_g, a_ln_b=a_ln_b,
        a_ws=a_w_s[0].astype(BF16),
        a_bs_t=jnp.repeat(jnp.transpose(a_b_s[0]), dg, axis=1),
        a_w_out=a_w_out[0].astype(BF16),
        b_w_all=b_w_in.astype(BF16),
        b_conv_w=b_conv_w[0], b_conv_b=b_conv_b,
        b_w_if=jnp.pad(w_in_b[:, qk_w + 2 * d_b:], ((0, 0), (0, LANES - 2 * H_B))).astype(BF16),
        b_bias_if=jnp.pad(jnp.concatenate([b_bias_i, b_bias_f], axis=1),
                          ((0, 0), (0, LANES - 2 * H_B))),
        b_gn_g=b_gn_g, b_w_out=b_w_out[0].astype(BF16),
        f_w_up=f_w_up.astype(BF16),
        f_conv_w=f_conv_w, f_conv_b=f_conv_b, f_w_down=f_w_down.astype(BF16),
        k_scale=float(dk) ** -0.5,
    )
    state = dict(mlstm_C=state_mlstm_C[0], mlstm_n=state_mlstm_n[0], mlstm_m=state_mlstm_m[0],
                 mlstm_conv=state_mlstm_conv[0], ffn_conv=state_ffn_conv)
    p, s = _trunks(x_prompt, x_sample, state, w)
    return (p["y"], s["y"], p["C"], p["n"], p["m"], p["mconv"], p["fconv"],
            s["v"], s["C"], s["n"], s["m"], s["mconv"], s["fconv"])
```
